```python
import jax, jax.numpy as jnp
from jax import lax
import numpy as np

D_MODEL = 1024
BATCH = 8
SEQ = 2048
DEPTH = 1
DEC_BATCH = 128
DEC_SEQ = 1
PAST_LEN = 16384
PAGE_SIZE = 128

MIX_WIDTH = D_MODEL
A_WIDTH = MIX_WIDTH // 2
B_WIDTH = MIX_WIDTH - A_WIDTH
A_HEADS = 4
A_HEAD_DIM = A_WIDTH // A_HEADS
CHUNK = 128
B_GROUPS = 8
CONV_W = 3
N_EXPERTS = 32
TOP_K = 4
D_EXPERT = D_MODEL
SWIGLU_LIMIT = 7.0
SWIGLU_ALPHA = 1.702
ALPHA_RES = (2.0 * DEPTH) ** 0.25
BETA_INIT = (8.0 * DEPTH) ** -0.25
LN_EPS = 1e-5
N_MOD = 6
PROJ_COLS = 2 * A_WIDTH + 3 * B_WIDTH
PROJ_SPLITS = [A_WIDTH, 2 * A_WIDTH, 2 * A_WIDTH + B_WIDTH, 2 * A_WIDTH + 2 * B_WIDTH]
MOE_MIN_BLOCK = 8
MOE_MAX_BLOCK = 256

kernel_name = 'hybrid_sgu_shortconv_moe_decoder_step'


def layer_norm(x, g, b):
    xf = x.astype(jnp.float32)
    mu = jnp.mean(xf, axis=-1, keepdims=True)
    var = jnp.mean(jnp.square(xf - mu), axis=-1, keepdims=True)
    y = (xf - mu) * lax.rsqrt(var + LN_EPS)
    return (y * g.astype(jnp.float32) + b.astype(jnp.float32)).astype(x.dtype)


def ada_modulation(c, ada_w, ada_b):
    m = jax.nn.silu(c) @ ada_w + ada_b
    return [p[:, None, :] for p in jnp.split(m, N_MOD, axis=-1)]


def spatial_mix(vn, sgu_w, sgu_b):
    lc = vn.shape[2]
    mask = jnp.tril(jnp.ones((lc, lc), dtype=sgu_w.dtype))
    w = sgu_w[:, :lc, :lc] * mask
    s = jnp.einsum('hts,bnshd->bnthd', w, vn)
    return s + jnp.transpose(sgu_b[:, :lc])[None, None, :, :, None]


def causal_conv(q_ext, conv_w, conv_b):
    length = q_ext.shape[1] - (CONV_W - 1)
    out = conv_b
    for k in range(CONV_W):
        out = out + q_ext[:, k:k + length] * conv_w[k]
    return out


def token_mixers(h, conv_prev, w_in, sgu_ln_g, sgu_ln_b, sgu_w, sgu_b, conv_w, conv_b, w_out):
    bt, L, _ = h.shape
    z = h @ w_in
    u, v, gate_b, gate_c, hb = jnp.split(z, PROJ_SPLITS, axis=-1)
    u = u.reshape(bt, L, A_HEADS, A_HEAD_DIM)
    vn = layer_norm(v.reshape(bt, L, A_HEADS, A_HEAD_DIM), sgu_ln_g, sgu_ln_b)
    lc = min(L, CHUNK)
    n_chunks = L // lc
    s = spatial_mix(vn.reshape(bt, n_chunks, lc, A_HEADS, A_HEAD_DIM), sgu_w, sgu_b)
    a_out = (u * s.reshape(bt, L, A_HEADS, A_HEAD_DIM)).reshape(bt, L, A_WIDTH)
    q_ext = jnp.concatenate([conv_prev.astype(h.dtype), gate_c * hb], axis=1)
    b_out = gate_b * causal_conv(q_ext, conv_w, conv_b)
    out = jnp.concatenate([a_out, b_out], axis=-1) @ w_out
    return out, q_ext[:, -(CONV_W - 1):], vn


def moe_block_size(n_assign):
    per_expert = max(1, n_assign // N_EXPERTS)
    return int(min(MOE_MAX_BLOCK, max(MOE_MIN_BLOCK, 1 << (per_expert - 1).bit_length())))


def moe(h, router_w, router_b, w_gu, b_gu, w_down, b_down):
    bt, L, d = h.shape
    T = bt * L
    xt = h.reshape(T, d)
    logits = (xt @ router_w + router_b).astype(jnp.float32)
    top_v, top_i = lax.top_k(logits, TOP_K)
    gates = jax.nn.softmax(top_v, axis=-1).astype(h.dtype)
    n_assign = T * TOP_K
    blk = moe_block_size(n_assign)
    n_blocks = -(-n_assign // blk) + N_EXPERTS
    cap = n_blocks * blk
    flat_e = top_i.reshape(n_assign)
    flat_tok = jnp.arange(n_assign, dtype=jnp.int32) // TOP_K
    flat_g = gates.reshape(n_assign)
    order = jnp.argsort(flat_e)
    sorted_e = flat_e[order]
    counts = jnp.bincount(flat_e, length=N_EXPERTS)
    padded = ((counts + blk - 1) // blk) * blk
    start = jnp.cumsum(counts) - counts
    pend = jnp.cumsum(padded)
    pstart = pend - padded
    dest = pstart[sorted_e] + (jnp.arange(n_assign, dtype=jnp.int32) - start[sorted_e])
    tok_buf = jnp.full((cap,), T, dtype=jnp.int32).at[dest].set(flat_tok[order])
    gate_buf = jnp.zeros((cap,), dtype=h.dtype).at[dest].set(flat_g[order])
    block_start = jnp.arange(n_blocks, dtype=jnp.int32) * blk
    block_e = jnp.minimum(jnp.searchsorted(pend, block_start, side='right'), N_EXPERTS - 1)
    x_pad = jnp.concatenate([xt, jnp.zeros((1, d), dtype=xt.dtype)], axis=0)
    xb = x_pad[tok_buf].reshape(n_blocks, blk, d)

    def expert_block(args):
        xblk, e = args
        gu = xblk @ w_gu[e] + b_gu[e]
        gate = jnp.minimum(gu[:, :D_EXPERT], SWIGLU_LIMIT)
        up = jnp.clip(gu[:, D_EXPERT:], -SWIGLU_LIMIT, SWIGLU_LIMIT)
        act = (up + 1.0) * gate * jax.nn.sigmoid(SWIGLU_ALPHA * gate)
        return act @ w_down[e] + b_down[e]

    yb = lax.map(expert_block, (xb, block_e)).reshape(cap, d)
    y = jax.ops.segment_sum(yb * gate_buf[:, None], tok_buf, num_segments=T + 1)[:T]
    return y.reshape(bt, L, d)


def decoder_layer(x, c, conv_prev, ada_w, ada_b, w_in, sgu_ln_g, sgu_ln_b, sgu_w, sgu_b,
                  conv_w, conv_b, w_out, ln1_g, ln1_b, router_w, router_b, w_gu, b_gu,
                  w_down, b_down, ln2_g, ln2_b):
    sh1, sc1, g1, sh2, sc2, g2 = ada_modulation(c, ada_w, ada_b)
    mix, conv_state, vn = token_mixers(x * (1.0 + sc1) + sh1, conv_prev, w_in, sgu_ln_g,
                                       sgu_ln_b, sgu_w, sgu_b, conv_w, conv_b, w_out)
    x = layer_norm(ALPHA_RES * x + g1 * mix, ln1_g, ln1_b)
    ffn = moe(x * (1.0 + sc2) + sh2, router_w, router_b, w_gu, b_gu, w_down, b_down)
    x = layer_norm(ALPHA_RES * x + g2 * ffn, ln2_g, ln2_b)
    return x, conv_state, vn


def setup_inputs(seed: int = 0) -> dict:
    key = jax.random.key(seed)
    ks = jax.random.split(key, 32)

    def nrm(k, shape, scale):
        return scale * jax.random.normal(k, shape, jnp.float32)

    D = D_MODEL
    return {
        'x_prompt': nrm(ks[0], (BATCH, SEQ, D), 1.0),
        'x_sample': nrm(ks[1], (DEC_BATCH, DEC_SEQ, D), 1.0),
        'state_conv': nrm(ks[2], (DEPTH, DEC_BATCH, CONV_W - 1, B_WIDTH), 1.0),
        'c_prompt': nrm(ks[3], (BATCH, D), 1.0),
        'c_sample': nrm(ks[4], (DEC_BATCH, D), 1.0),
        'ada_w': nrm(ks[5], (DEPTH, D, N_MOD * D), 0.5 * D ** -0.5),
        'ada_b': nrm(ks[6], (DEPTH, N_MOD * D), 0.02),
        'w_in': nrm(ks[7], (DEPTH, D, PROJ_COLS), D ** -0.5),
        'sgu_ln_g': 1.0 + nrm(ks[8], (DEPTH, A_HEADS, A_HEAD_DIM), 0.02),
        'sgu_ln_b': nrm(ks[9], (DEPTH, A_HEADS, A_HEAD_DIM), 0.02),
        'sgu_w': nrm(ks[10], (DEPTH, A_HEADS, CHUNK, CHUNK), CHUNK ** -0.5),
        'sgu_b': 1.0 + nrm(ks[11], (DEPTH, A_HEADS, CHUNK), 0.02),
        'conv_w': nrm(ks[12], (DEPTH, CONV_W, B_WIDTH), CONV_W ** -0.5),
        'conv_b': nrm(ks[13], (DEPTH, B_WIDTH), 0.02),
        'w_out': nrm(ks[14], (DEPTH, MIX_WIDTH, D), BETA_INIT * MIX_WIDTH ** -0.5),
        'ln1_g': 1.0 + nrm(ks[15], (DEPTH, D), 0.02),
        'ln1_b': nrm(ks[16], (DEPTH, D), 0.02),
        'router_w': nrm(ks[17], (DEPTH, D, N_EXPERTS), D ** -0.5),
        'router_b': nrm(ks[18], (DEPTH, N_EXPERTS), 0.01),
        'w_gu': nrm(ks[19], (DEPTH, N_EXPERTS, D, 2 * D_EXPERT), D ** -0.5),
        'b_gu': nrm(ks[20], (DEPTH, N_EXPERTS, 2 * D_EXPERT), 0.01),
        'w_down': nrm(ks[21], (DEPTH, N_EXPERTS, D_EXPERT, D), BETA_INIT * D_EXPERT ** -0.5),
        'b_down': nrm(ks[22], (DEPTH, N_EXPERTS, D), 0.01),
        'ln2_g': 1.0 + nrm(ks[23], (DEPTH, D), 0.02),
        'ln2_b': nrm(ks[24], (DEPTH, D), 0.02),
    }


def reference(x_prompt, x_sample, state_conv, c_prompt, c_sample, ada_w, ada_b, w_in,
              sgu_ln_g, sgu_ln_b, sgu_w, sgu_b, conv_w, conv_b, w_out, ln1_g, ln1_b,
              router_w, router_b, w_gu, b_gu, w_down, b_down, ln2_g, ln2_b):
    weights = (ada_w, ada_b, w_in, sgu_ln_g, sgu_ln_b, sgu_w, sgu_b, conv_w, conv_b, w_out,
               ln1_g, ln1_b, router_w, router_b, w_gu, b_gu, w_down, b_down, ln2_g, ln2_b)
    y_p, y_s = x_prompt, x_sample
    zero_prev = jnp.zeros((x_prompt.shape[0], CONV_W - 1, B_WIDTH), dtype=x_prompt.dtype)
    conv_p, conv_s, v_s = [], [], []
    for l in range(DEPTH):
        per = [w[l] for w in weights]
        y_p, cp, _ = decoder_layer(y_p, c_prompt, zero_prev, *per)
        y_s, cs, vs = decoder_layer(y_s, c_sample, state_conv[l], *per)
        conv_p.append(cp)
        conv_s.append(cs)
        v_s.append(vs)
    conv_state_prompt = jnp.stack(conv_p)
    conv_state_sample = jnp.stack(conv_s)
    sgu_v_sample = jnp.stack(v_s)
    return (y_p, y_s, conv_state_prompt, conv_state_sample, sgu_v_sample)
```

```python
import functools

import jax
import jax.numpy as jnp
from jax import lax
from jax.experimental import pallas as pl
from jax.experimental.pallas import tpu as pltpu

F32 = jnp.float32
BF16 = jnp.bfloat16
HIGHEST = lax.Precision.HIGHEST

D_MODEL = 1024
A_WIDTH = 512
B_WIDTH = 512
A_HEADS = 4
A_HEAD_DIM = 128
CHUNK = 128
PROJ_COLS = 2 * A_WIDTH + 3 * B_WIDTH
N_EXPERTS = 32
TOP_K = 4
D_EXPERT = 1024
SWIGLU_LIMIT = 7.0
SWIGLU_ALPHA = 1.702
LN_EPS = 1e-5
N_MOD = 6

LANES = 128
SUBLANES = 8
ROW_SLABS = D_MODEL // LANES
TOK_TILE = 256
ROW_TILE = 256
STRIP_BITS = tuple(1 << b for b in range(8, -1, -1))
NEG_BIG = -1e30
VMEM_LIMIT = 56 * 1024 * 1024


def _layer_norm(x, g, b):
    mu = jnp.mean(x, axis=-1, keepdims=True)
    xc = x - mu
    var = jnp.mean(xc * xc, axis=-1, keepdims=True)
    return xc * lax.rsqrt(var + LN_EPS) * g + b


def _dot(a, b):
    return jnp.dot(a, b, preferred_element_type=F32)


def _ada_kernel(c_ref, w_ref, b_ref, o_ref):
    c = c_ref[...]
    s = c * jax.nn.sigmoid(c)
    o_ref[...] = jnp.dot(s, w_ref[...], precision=HIGHEST, preferred_element_type=F32) + b_ref[...]


def _ada(c_all, ada_w, ada_b):
    rows = c_all.shape[0]
    return pl.pallas_call(
        _ada_kernel,
        grid=(N_MOD,),
        in_specs=[
            pl.BlockSpec((rows, D_MODEL), lambda n: (0, 0)),
            pl.BlockSpec((D_MODEL, D_MODEL), lambda n: (0, n)),
            pl.BlockSpec((1, D_MODEL), lambda n: (0, n)),
        ],
        out_specs=pl.BlockSpec((rows, D_MODEL), lambda n: (0, n)),
        out_shape=jax.ShapeDtypeStruct((rows, N_MOD * D_MODEL), F32),
        name="ada",
    )(c_all, ada_w, ada_b.reshape(1, -1))


def _route(h2, router_w, router_b):
    n = h2.shape[0]
    logits = jnp.dot(h2, router_w, precision=HIGHEST, preferred_element_type=F32) + router_b
    lane = lax.broadcasted_iota(jnp.int32, (n, LANES), 1)
    work = logits
    vals, idxs, hots = [], [], []
    for _ in range(TOP_K):
        m = jnp.max(work, axis=-1, keepdims=True)
        idx = jnp.min(jnp.where(work == m, lane, LANES), axis=-1, keepdims=True)
        hot = lane == idx
        work = jnp.where(hot, -jnp.inf, work)
        vals.append(m)
        idxs.append(idx)
        hots.append(hot)
    exps = [jnp.exp(v - vals[0]) for v in vals]
    denom = exps[0] + exps[1] + exps[2] + exps[3]
    gates = [e / denom for e in exps]

    onehot = jnp.zeros((n, LANES), F32)
    for hot in hots:
        onehot = onehot + hot.astype(F32)
    onehot_bf = onehot.astype(BF16)
    r_i = lax.broadcasted_iota(jnp.int32, (n, n), 0)
    c_i = lax.broadcasted_iota(jnp.int32, (n, n), 1)
    tri = (c_i < r_i).astype(BF16)
    rank = _dot(tri, onehot_bf)
    cnt = _dot(jnp.ones((SUBLANES, n), BF16), onehot_bf)
    e_r = lax.broadcasted_iota(jnp.int32, (LANES, LANES), 0)
    e_c = lax.broadcasted_iota(jnp.int32, (LANES, LANES), 1)
    upper = (e_r < e_c).astype(BF16)
    off = _dot(cnt.astype(BF16), upper)
    base = rank + off[0:1, :]

    meta = jnp.zeros((n, LANES), F32)
    for k in range(TOP_K):
        pos_k = jnp.sum(jnp.where(hots[k], base, 0.0), axis=-1, keepdims=True)
        meta = jnp.where(lane == k, pos_k, meta)
        meta = jnp.where(lane == TOP_K + k, gates[k], meta)
        meta = jnp.where(lane == 2 * TOP_K + k, idxs[k].astype(F32), meta)
    return meta, cnt


def _mix_prompt_kernel(alpha_res, x_ref, mod_ref, w_in_ref, sgu_g_ref, sgu_bln_ref, sgu_w_ref,
                       sgu_bias_ref, conv_w_ref, conv_b_ref, w_out_ref, ln1_g_ref, ln1_b_ref,
                       router_w_ref, router_b_ref,
                       x1_ref, h2_ref, meta_ref, cnt_ref, convst_ref, carry_ref):
    t = pl.program_id(1)

    @pl.when(t == 0)
    def _():
        carry_ref[...] = jnp.zeros_like(carry_ref)

    x = x_ref[...]
    mod = mod_ref[...]
    sh1, sc1, g1, sh2, sc2 = mod[0:1], mod[1:2], mod[2:3], mod[3:4], mod[4:5]
    h = (x * (1.0 + sc1) + sh1).astype(BF16)
    z = _dot(h, w_in_ref[...])
    u = z[:, 0:A_WIDTH]
    v = z[:, A_WIDTH:2 * A_WIDTH]
    gate_b = z[:, 2 * A_WIDTH:2 * A_WIDTH + B_WIDTH]
    gate_c = z[:, 2 * A_WIDTH + B_WIDTH:2 * A_WIDTH + 2 * B_WIDTH]
    hb = z[:, 2 * A_WIDTH + 2 * B_WIDTH:]

    tm = x.shape[0]
    r_i = lax.broadcasted_iota(jnp.int32, (CHUNK, CHUNK), 0)
    c_i = lax.broadcasted_iota(jnp.int32, (CHUNK, CHUNK), 1)
    tril = c_i <= r_i
    a_parts = []
    for hd in range(A_HEADS):
        sl = slice(hd * A_HEAD_DIM, (hd + 1) * A_HEAD_DIM)
        vn = _layer_norm(v[:, sl], sgu_g_ref[hd:hd + 1, :], sgu_bln_ref[hd:hd + 1, :]).astype(BF16)
        wm = jnp.where(tril, sgu_w_ref[hd], 0.0).astype(BF16)
        bias = sgu_bias_ref[:, hd:hd + 1]
        s_parts = []
        for c in range(tm // CHUNK):
            s_parts.append(_dot(wm, vn[c * CHUNK:(c + 1) * CHUNK, :]) + bias)
        a_parts.append(u[:, sl] * jnp.concatenate(s_parts, axis=0))

    q = gate_c * hb
    row = lax.broadcasted_iota(jnp.int32, q.shape, 0)
    prev2 = carry_ref[0:1, :]
    prev1 = carry_ref[1:2, :]
    q_m1 = jnp.where(row == 0, prev1, pltpu.roll(q, 1, 0))
    q_m2 = jnp.where(row == 0, prev2, jnp.where(row == 1, prev1, pltpu.roll(q, 2, 0)))
    conv = conv_b_ref[...] + q_m2 * conv_w_ref[0:1, :] + q_m1 * conv_w_ref[1:2, :] + q * conv_w_ref[2:3, :]
    b_out = gate_b * conv
    carry_ref[0:2, :] = q[tm - 2:tm, :]
    convst_ref[...] = q[tm - 2:tm, :]

    mix_in = jnp.concatenate(a_parts + [b_out], axis=-1).astype(BF16)
    mix = _dot(mix_in, w_out_ref[...])
    x1 = _layer_norm(alpha_res * x + g1 * mix, ln1_g_ref[...], ln1_b_ref[...])
    x1_ref[...] = x1
    h2 = x1 * (1.0 + sc2) + sh2
    h2_ref[...] = h2.astype(BF16)
    meta, cnt = _route(h2, router_w_ref[...], router_b_ref[...])
    meta_ref[...] = meta
    cnt_ref[...] = cnt.astype(jnp.int32)


def _mix_prompt(x, mod, w_in_bf, sgu_g, sgu_bln, sgu_w, sgu_bias_t, conv_w, conv_b, w_out_bf,
                ln1_g, ln1_b, router_w_pad, router_b_pad, alpha_res):
    bsz, seq, _ = x.shape
    tiles = seq // TOK_TILE
    n_tok = bsz * seq
    const2 = lambda b, t: (0, 0)
    tok = lambda b, t: (b * tiles + t, 0)
    return pl.pallas_call(
        functools.partial(_mix_prompt_kernel, alpha_res),
        grid=(bsz, tiles),
        in_specs=[
            pl.BlockSpec((None, TOK_TILE, D_MODEL), lambda b, t: (b, t, 0)),
            pl.BlockSpec((None, SUBLANES, D_MODEL), lambda b, t: (b, 0, 0)),
            pl.BlockSpec((D_MODEL, PROJ_COLS), const2),
            pl.BlockSpec((A_HEADS, A_HEAD_DIM), const2),
            pl.BlockSpec((A_HEADS, A_HEAD_DIM), const2),
            pl.BlockSpec((A_HEADS, CHUNK, CHUNK), lambda b, t: (0, 0, 0)),
            pl.BlockSpec((CHUNK, A_HEADS), const2),
            pl.BlockSpec((3, B_WIDTH), const2),
            pl.BlockSpec((1, B_WIDTH), const2),
            pl.BlockSpec((D_MODEL, D_MODEL), const2),
            pl.BlockSpec((1, D_MODEL), const2),
            pl.BlockSpec((1, D_MODEL), const2),
            pl.BlockSpec((D_MODEL, LANES), const2),
            pl.BlockSpec((1, LANES), const2),
        ],
        out_specs=[
            pl.BlockSpec((TOK_TILE, D_MODEL), tok),
            pl.BlockSpec((TOK_TILE, D_MODEL), tok),
            pl.BlockSpec((TOK_TILE, LANES), tok),
            pl.BlockSpec((None, SUBLANES, LANES), lambda b, t: (b * tiles + t, 0, 0)),
            pl.BlockSpec((None, 2, B_WIDTH), lambda b, t: (b, 0, 0)),
        ],
        out_shape=[
            jax.ShapeDtypeStruct((n_tok, D_MODEL), F32),
            jax.ShapeDtypeStruct((n_tok, D_MODEL), BF16),
            jax.ShapeDtypeStruct((n_tok, LANES), F32),
            jax.ShapeDtypeStruct((bsz * tiles, SUBLANES, LANES), jnp.int32),
            jax.ShapeDtypeStruct((bsz, 2, B_WIDTH), F32),
        ],
        scratch_shapes=[pltpu.VMEM((SUBLANES, B_WIDTH), F32)],
        compiler_params=pltpu.CompilerParams(
            dimension_semantics=("arbitrary", "arbitrary"), vmem_limit_bytes=VMEM_LIMIT),
        name="mix_prompt",
    )(x, mod, w_in_bf, sgu_g, sgu_bln, sgu_w, sgu_bias_t, conv_w, conv_b, w_out_bf,
      ln1_g, ln1_b, router_w_pad, router_b_pad)


def _mix_sample_kernel(alpha_res, x_ref, mod_ref, prev0_ref, prev1_ref, w_in_ref, sgu_g_ref,
                       sgu_bln_ref, sgu_w00_ref, sgu_b0_ref, conv_w_ref, conv_b_ref, w_out_ref,
                       ln1_g_ref, ln1_b_ref, router_w_ref, router_b_ref,
                       x1_ref, h2_ref, meta_ref, cnt_ref, q_ref, vn_ref):
    x = x_ref[...]
    sh1 = mod_ref[:, 0:D_MODEL]
    sc1 = mod_ref[:, D_MODEL:2 * D_MODEL]
    g1 = mod_ref[:, 2 * D_MODEL:3 * D_MODEL]
    sh2 = mod_ref[:, 3 * D_MODEL:4 * D_MODEL]
    sc2 = mod_ref[:, 4 * D_MODEL:5 * D_MODEL]
    h = (x * (1.0 + sc1) + sh1).astype(BF16)
    z = _dot(h, w_in_ref[...])
    u = z[:, 0:A_WIDTH]
    v = z[:, A_WIDTH:2 * A_WIDTH]
    gate_b = z[:, 2 * A_WIDTH:2 * A_WIDTH + B_WIDTH]
    gate_c = z[:, 2 * A_WIDTH + B_WIDTH:2 * A_WIDTH + 2 * B_WIDTH]
    hb = z[:, 2 * A_WIDTH + 2 * B_WIDTH:]

    vn_parts = []
    for hd in range(A_HEADS):
        sl = slice(hd * A_HEAD_DIM, (hd + 1) * A_HEAD_DIM)
        vn_parts.append(_layer_norm(v[:, sl], sgu_g_ref[hd:hd + 1, :], sgu_bln_ref[hd:hd + 1, :]))
    vn = jnp.concatenate(vn_parts, axis=-1)
    vn_ref[...] = vn
    a_out = u * (vn * sgu_w00_ref[...] + sgu_b0_ref[...])

    q = gate_c * hb
    q_ref[...] = q
    conv = (conv_b_ref[...] + prev0_ref[...] * conv_w_ref[0:1, :] + prev1_ref[...] * conv_w_ref[1:2, :]
            + q * conv_w_ref[2:3, :])
    b_out = gate_b * conv

    mix_in = jnp.concatenate([a_out, b_out], axis=-1).astype(BF16)
    mix = _dot(mix_in, w_out_ref[...])
    x1 = _layer_norm(alpha_res * x + g1 * mix, ln1_g_ref[...], ln1_b_ref[...])
    x1_ref[...] = x1
    h2 = x1 * (1.0 + sc2) + sh2
    h2_ref[...] = h2.astype(BF16)
    meta, cnt = _route(h2, router_w_ref[...], router_b_ref[...])
    meta_ref[...] = meta
    cnt_ref[...] = cnt.astype(jnp.int32)


def _mix_sample(x, mod, prev0, prev1, w_in_bf, sgu_g, sgu_bln, sgu_w00, sgu_b0, conv_w, conv_b,
                w_out_bf, ln1_g, ln1_b, router_w_pad, router_b_pad, alpha_res):
    n = x.shape[0]
    return pl.pallas_call(
        functools.partial(_mix_sample_kernel, alpha_res),
        out_shape=[
            jax.ShapeDtypeStruct((n, D_MODEL), F32),
            jax.ShapeDtypeStruct((n, D_MODEL), BF16),
            jax.ShapeDtypeStruct((n, LANES), F32),
            jax.ShapeDtypeStruct((SUBLANES, LANES), jnp.int32),
            jax.ShapeDtypeStruct((n, B_WIDTH), F32),
            jax.ShapeDtypeStruct((n, A_WIDTH), F32),
        ],
        compiler_params=pltpu.CompilerParams(vmem_limit_bytes=VMEM_LIMIT),
        name="mix_sample",
    )(x, mod, prev0, prev1, w_in_bf, sgu_g, sgu_bln, sgu_w00, sgu_b0, conv_w, conv_b, w_out_bf,
      ln1_g, ln1_b, router_w_pad, router_b_pad)


def _strip_copies(src_ref, dst_ref, src_row, dst_row, n_rows, sem, wait):
    for bit in STRIP_BITS:
        @pl.when((n_rows & bit) != 0)
        def _(src_row=src_row, dst_row=dst_row, bit=bit):
            s = 0 if src_row is None else pl.multiple_of(src_row * ROW_SLABS, ROW_SLABS)
            d = pl.multiple_of(dst_row * ROW_SLABS, ROW_SLABS)
            cp = pltpu.make_async_copy(src_ref.at[pl.ds(s, bit * ROW_SLABS)],
                                       dst_ref.at[pl.ds(d, bit * ROW_SLABS)], sem)
            if wait:
                cp.wait()
            else:
                cp.start()
        if src_row is not None:
            src_row = src_row + (n_rows & bit)
        dst_row = dst_row + (n_rows & bit)


def _sort_kernel(n_tok, tile0, first, *refs):
    if first:
        (sorted_row_ref, off_ref, cnt_ref, pad_row_ref, pad_n_ref, h2_ref, meta_ref,
         xs_ref, stage_ref, zero_ref, sem) = refs
    else:
        (sorted_row_ref, off_ref, cnt_ref, pad_row_ref, pad_n_ref, h2_ref, meta_ref, _,
         xs_ref, stage_ref, sem) = refs
    j = pl.program_id(0)
    n_sorted = n_tok * TOP_K

    if first:
        @pl.when(j == 0)
        def _():
            zero_ref[...] = jnp.zeros_like(zero_ref)
            for wait in (False, True):
                def body(e, carry, wait=wait):
                    _strip_copies(zero_ref, xs_ref, None, pad_row_ref[e], pad_n_ref[e], sem, wait)
                    return carry
                lax.fori_loop(0, N_EXPERTS, body, 0)

    meta = meta_ref[...]
    lane_r = lax.broadcasted_iota(jnp.int32, (n_tok, n_sorted), 1)
    sel = jnp.zeros((n_tok, n_sorted), jnp.bool_)
    for k in range(TOP_K):
        sel = jnp.logical_or(sel, lane_r == meta[:, k:k + 1].astype(jnp.int32))
    perm_t = jnp.where(sel, 1.0, 0.0).astype(BF16)
    rows = lax.dot_general(perm_t, h2_ref[...], (((0,), (0,)), ((), ())),
                           preferred_element_type=F32)
    for c in range(ROW_SLABS):
        stage_ref[pl.ds(c, n_sorted, stride=ROW_SLABS), :] = rows[:, c * LANES:(c + 1) * LANES]

    base = (tile0 + j) * N_EXPERTS
    for wait in (False, True):
        def body(e, carry, wait=wait):
            _strip_copies(stage_ref, xs_ref, off_ref[base + e], sorted_row_ref[base + e],
                          cnt_ref[base + e], sem, wait)
            return carry
        lax.fori_loop(0, N_EXPERTS, body, 0)


def _sort(tables, pad_row, pad_n, h2, meta, xs, n_sorted_rows, n_tok, tile0):
    sorted_row, off, cnt = tables
    first = xs is None
    n_tiles = h2.shape[0] // n_tok
    in_specs = [
        pl.BlockSpec((n_tok, D_MODEL), lambda j, *_: (j, 0)),
        pl.BlockSpec((n_tok, LANES), lambda j, *_: (j, 0)),
    ]
    operands = [sorted_row, off, cnt, pad_row, pad_n, h2, meta]
    scratch = [pltpu.VMEM((n_tok * TOP_K * ROW_SLABS, LANES), F32)]
    if first:
        scratch.append(pltpu.VMEM((ROW_TILE * ROW_SLABS, LANES), F32))
        aliases = {}
    else:
        in_specs.append(pl.BlockSpec(memory_space=pl.ANY))
        operands.append(xs)
        aliases = {len(operands) - 1: 0}
    scratch.append(pltpu.SemaphoreType.DMA(()))
    grid_spec = pltpu.PrefetchScalarGridSpec(
        num_scalar_prefetch=5,
        grid=(n_tiles,),
        in_specs=in_specs,
        out_specs=pl.BlockSpec(memory_space=pl.ANY),
        scratch_shapes=scratch,
    )
    return pl.pallas_call(
        functools.partial(_sort_kernel, n_tok, tile0, first),
        grid_spec=grid_spec,
        out_shape=jax.ShapeDtypeStruct((n_sorted_rows * ROW_SLABS, LANES), F32),
        input_output_aliases=aliases,
        compiler_params=pltpu.CompilerParams(
            dimension_semantics=("arbitrary",), vmem_limit_bytes=VMEM_LIMIT),
        name="sort_first" if first else "sort_more",
    )(*operands)


def _experts_kernel(tile_e_ref, tile_first_ref, n_used_ref, xs_ref, w_gu_ref, b_gu_ref, w_down_ref,
                    b_down_ref, ys_ref, w_gu_bf_ref, w_down_bf_ref):
    i = pl.program_id(0)

    @pl.when(tile_first_ref[i] == 1)
    def _():
        w_gu_bf_ref[...] = w_gu_ref[...].astype(BF16)
        w_down_bf_ref[...] = w_down_ref[...].astype(BF16)

    @pl.when(i < n_used_ref[0])
    def _():
        x = jnp.concatenate(
            [xs_ref[pl.ds(c, ROW_TILE, stride=ROW_SLABS), :] for c in range(ROW_SLABS)],
            axis=-1).astype(BF16)
        gu = _dot(x, w_gu_bf_ref[...]) + b_gu_ref[...]
        gate = jnp.minimum(gu[:, :D_EXPERT], SWIGLU_LIMIT)
        up = jnp.clip(gu[:, D_EXPERT:], -SWIGLU_LIMIT, SWIGLU_LIMIT)
        act = (up + 1.0) * gate * jax.nn.sigmoid(SWIGLU_ALPHA * gate)
        y = _dot(act.astype(BF16), w_down_bf_ref[...]) + b_down_ref[...]
        for c in range(ROW_SLABS):
            ys_ref[pl.ds(c, ROW_TILE, stride=ROW_SLABS), :] = y[:, c * LANES:(c + 1) * LANES]


def _experts(tile_e, tile_first, n_used, xs, w_gu, b_gu, w_down, b_down):
    n_tiles = xs.shape[0] // (ROW_TILE * ROW_SLABS)
    row_blk = lambda i, te, tf, nu: (jnp.minimum(i, nu[0] - 1), 0)
    w_blk = lambda i, te, tf, nu: (te[i], 0, 0)
    grid_spec = pltpu.PrefetchScalarGridSpec(
        num_scalar_prefetch=3,
        grid=(n_tiles,),
        in_specs=[
            pl.BlockSpec((ROW_TILE * ROW_SLABS, LANES), row_blk),
            pl.BlockSpec((None, D_MODEL, 2 * D_EXPERT), w_blk),
            pl.BlockSpec((None, 1, 2 * D_EXPERT), w_blk),
            pl.BlockSpec((None, D_EXPERT, D_MODEL), w_blk),
            pl.BlockSpec((None, 1, D_MODEL), w_blk),
        ],
        out_specs=pl.BlockSpec((ROW_TILE * ROW_SLABS, LANES), row_blk),
        scratch_shapes=[pltpu.VMEM((D_MODEL, 2 * D_EXPERT), BF16),
                        pltpu.VMEM((D_EXPERT, D_MODEL), BF16)],
    )
    return pl.pallas_call(
        _experts_kernel,
        grid_spec=grid_spec,
        out_shape=jax.ShapeDtypeStruct(xs.shape, F32),
        compiler_params=pltpu.CompilerParams(
            dimension_semantics=("arbitrary",), vmem_limit_bytes=VMEM_LIMIT),
        name="experts",
    )(tile_e, tile_first, n_used, xs, w_gu, b_gu.reshape(N_EXPERTS, 1, -1), w_down,
      b_down.reshape(N_EXPERTS, 1, -1))


def _combine_kernel(n_tok, tile0, mod_rows, alpha_res, sorted_row_ref, off_ref, cnt_ref,
                    ys_ref, x1_ref, meta_ref, mod_ref, ln2_g_ref, ln2_b_ref, out_ref,
                    stage_ref, sem):
    j = pl.program_id(0)
    n_sorted = n_tok * TOP_K

    @pl.when(j == 0)
    def _():
        stage_ref[...] = jnp.zeros_like(stage_ref)

    base = (tile0 + j) * N_EXPERTS
    for wait in (False, True):
        def body(e, carry, wait=wait):
            _strip_copies(ys_ref, stage_ref, sorted_row_ref[base + e], off_ref[base + e],
                          cnt_ref[base + e], sem, wait)
            return carry
        lax.fori_loop(0, N_EXPERTS, body, 0)

    meta = meta_ref[...]
    lane_r = lax.broadcasted_iota(jnp.int32, (n_tok, n_sorted), 1)
    comb = jnp.zeros((n_tok, n_sorted), F32)
    for k in range(TOP_K):
        comb = jnp.where(lane_r == meta[:, k:k + 1].astype(jnp.int32),
                         meta[:, TOP_K + k:TOP_K + k + 1], comb)
    ys = jnp.concatenate(
        [stage_ref[pl.ds(c, n_sorted, stride=ROW_SLABS), :] for c in range(ROW_SLABS)],
        axis=-1).astype(BF16)
    ffn = _dot(comb.astype(BF16), ys)
    if mod_rows:
        g2 = mod_ref[:, 5 * D_MODEL:6 * D_MODEL]
    else:
        g2 = mod_ref[5:6, :]
    out_ref[...] = _layer_norm(alpha_res * x1_ref[...] + g2 * ffn, ln2_g_ref[...], ln2_b_ref[...])


def _combine(tables, ys, x1, meta, mod, ln2_g, ln2_b, n_tok, tile0, alpha_res):
    sorted_row, off, cnt = tables
    n_tiles = x1.shape[0] // n_tok
    mod_rows = mod.ndim == 2
    if mod_rows:
        mod_spec = pl.BlockSpec((n_tok, N_MOD * D_MODEL), lambda j, *_: (j, 0))
    else:
        tiles_per_seq = n_tiles // mod.shape[0]
        mod_spec = pl.BlockSpec((None, SUBLANES, D_MODEL), lambda j, *_: (j // tiles_per_seq, 0, 0))
    grid_spec = pltpu.PrefetchScalarGridSpec(
        num_scalar_prefetch=3,
        grid=(n_tiles,),
        in_specs=[
            pl.BlockSpec(memory_space=pl.ANY),
            pl.BlockSpec((n_tok, D_MODEL), lambda j, *_: (j, 0)),
            pl.BlockSpec((n_tok, LANES), lambda j, *_: (j, 0)),
            mod_spec,
            pl.BlockSpec((1, D_MODEL), lambda j, *_: (0, 0)),
            pl.BlockSpec((1, D_MODEL), lambda j, *_: (0, 0)),
        ],
        out_specs=pl.BlockSpec((n_tok, D_MODEL), lambda j, *_: (j, 0)),
        scratch_shapes=[pltpu.VMEM((n_tok * TOP_K * ROW_SLABS, LANES), F32),
                        pltpu.SemaphoreType.DMA(())],
    )
    return pl.pallas_call(
        functools.partial(_combine_kernel, n_tok, tile0, mod_rows, alpha_res),
        grid_spec=grid_spec,
        out_shape=jax.ShapeDtypeStruct(x1.shape, F32),
        compiler_params=pltpu.CompilerParams(
            dimension_semantics=("arbitrary",), vmem_limit_bytes=VMEM_LIMIT),
        name="combine",
    )(sorted_row, off, cnt, ys, x1, meta, mod, ln2_g, ln2_b)


def _routing_tables(cnt_all, n_row_tiles):
    total = jnp.sum(cnt_all, axis=0)
    n_tile_e = (total + ROW_TILE - 1) // ROW_TILE
    tile_end = jnp.cumsum(n_tile_e)
    tile_start = tile_end - n_tile_e
    row_start = tile_start * ROW_TILE
    cum = jnp.cumsum(cnt_all, axis=0) - cnt_all
    off = jnp.cumsum(cnt_all, axis=1) - cnt_all
    sorted_row = row_start[None, :] + cum
    n_used = tile_end[-1]
    tile_id = jnp.minimum(jnp.arange(n_row_tiles, dtype=jnp.int32), n_used - 1)
    tile_e = jnp.sum((tile_id[:, None] >= tile_end[None, :]).astype(jnp.int32), axis=1)
    tile_first = jnp.logical_and(jnp.arange(n_row_tiles) < n_used,
                                 tile_id == tile_start[tile_e]).astype(jnp.int32)
    pad_row = row_start + total
    pad_n = n_tile_e * ROW_TILE - total
    i32 = lambda a: a.astype(jnp.int32)
    return ((i32(sorted_row).reshape(-1), i32(off).reshape(-1), i32(cnt_all).reshape(-1)),
            i32(pad_row), i32(pad_n), i32(tile_e), tile_first, i32(n_used).reshape(1))


def kernel(x_prompt, x_sample, state_conv, c_prompt, c_sample, ada_w, ada_b, w_in, sgu_ln_g, sgu_ln_b,
           sgu_w, sgu_b, conv_w, conv_b, w_out, ln1_g, ln1_b, router_w, router_b, w_gu, b_gu,
           w_down, b_down, ln2_g, ln2_b):
    depth = ada_w.shape[0]
    assert depth == 1
    bsz, seq, _ = x_prompt.shape
    n_dec = x_sample.shape[0]
    assert x_sample.shape[1] == 1 and seq % TOK_TILE == 0
    alpha_res = (2.0 * depth) ** 0.25
    l = 0

    mod = _ada(jnp.concatenate([c_prompt, c_sample], axis=0), ada_w[l], ada_b[l])
    mod_p = jnp.pad(mod[:bsz].reshape(bsz, N_MOD, D_MODEL), ((0, 0), (0, SUBLANES - N_MOD), (0, 0)))
    mod_s = mod[bsz:]

    w_in_bf = w_in[l].astype(BF16)
    w_out_bf = w_out[l].astype(BF16)
    router_w_pad = jnp.pad(router_w[l], ((0, 0), (0, LANES - N_EXPERTS)))
    router_b_pad = jnp.pad(router_b[l], (0, LANES - N_EXPERTS), constant_values=NEG_BIG).reshape(1, LANES)
    row = lambda a: a.reshape(1, -1)

    x1_p, h2_p, meta_p, cnt_p, convst_p = _mix_prompt(
        x_prompt, mod_p, w_in_bf, sgu_ln_g[l], sgu_ln_b[l], sgu_w[l], jnp.transpose(sgu_b[l]),
        conv_w[l], row(conv_b[l]), w_out_bf, row(ln1_g[l]), row(ln1_b[l]), router_w_pad,
        router_b_pad, alpha_res)
    x1_s, h2_s, meta_s, cnt_s, q_s, vn_s = _mix_sample(
        x_sample.reshape(n_dec, D_MODEL), mod_s, state_conv[l, :, 0, :], state_conv[l, :, 1, :],
        w_in_bf, sgu_ln_g[l], sgu_ln_b[l], row(jnp.repeat(sgu_w[l, :, 0, 0], A_HEAD_DIM)),
        row(jnp.repeat(sgu_b[l, :, 0], A_HEAD_DIM)), conv_w[l], row(conv_b[l]), w_out_bf,
        row(ln1_g[l]), row(ln1_b[l]), router_w_pad, router_b_pad, alpha_res)

    n_ptiles = bsz * seq // TOK_TILE
    cnt_all = jnp.concatenate([cnt_p[:, 0, :N_EXPERTS], cnt_s[0:1, :N_EXPERTS]], axis=0)
    n_assign = (bsz * seq + n_dec) * TOP_K
    n_row_tiles = -(-n_assign // ROW_TILE) + N_EXPERTS
    tables, pad_row, pad_n, tile_e, tile_first, n_used = _routing_tables(cnt_all, n_row_tiles)

    n_sorted_rows = n_row_tiles * ROW_TILE
    xs = _sort(tables, pad_row, pad_n, h2_p, meta_p, None, n_sorted_rows, TOK_TILE, 0)
    xs = _sort(tables, pad_row, pad_n, h2_s, meta_s, xs, n_sorted_rows, n_dec, n_ptiles)
    ys = _experts(tile_e, tile_first, n_used, xs, w_gu[l], b_gu[l], w_down[l], b_down[l])
    y_p = _combine(tables, ys, x1_p, meta_p, mod_p, row(ln2_g[l]), row(ln2_b[l]), TOK_TILE, 0, alpha_res)
    y_s = _combine(tables, ys, x1_s, meta_s, mod_s, row(ln2_g[l]), row(ln2_b[l]), n_dec, n_ptiles,
                   alpha_res)

    conv_state_sample = jnp.stack([state_conv[l, :, 1, :], q_s], axis=1)[None]
    return (y_p.reshape(bsz, seq, D_MODEL),
            y_s.reshape(n_dec, 1, D_MODEL),
            convst_p[None],
            conv_state_sample,
            vn_s.reshape(1, n_dec, 1, A_HEADS, A_HEAD_DIM))
```

```python
import functools

import jax
import jax.numpy as jnp
from jax import lax
from jax.experimental import pallas as pl
from jax.experimental.pallas import tpu as pltpu

F32 = jnp.float32
BF16 = jnp.bfloat16
HIGHEST = lax.Precision.HIGHEST

D_MODEL = 1024
A_WIDTH = 512
B_WIDTH = 512
A_HEADS = 4
A_HEAD_DIM = 128
CHUNK = 128
PROJ_COLS = 2 * A_WIDTH + 3 * B_WIDTH
N_EXPERTS = 32
TOP_K = 4
D_EXPERT = 1024
SWIGLU_LIMIT = 7.0
SWIGLU_ALPHA = 1.702
LN_EPS = 1e-5
N_MOD = 6

LANES = 128
SUBLANES = 8
ROW_SLABS = D_MODEL // LANES
TOK_TILE = 256
ROW_TILE = 256
VMEM_LIMIT = 56 * 1024 * 1024


def _layer_norm(x, g, b):
    mu = jnp.mean(x, axis=-1, keepdims=True)
    xc = x - mu
    var = jnp.mean(xc * xc, axis=-1, keepdims=True)
    return xc * lax.rsqrt(var + LN_EPS) * g + b


def _dot(a, b):
    return jnp.dot(a, b, preferred_element_type=F32)


def _dot_nt(a, b):
    return lax.dot_general(a, b, (((1,), (1,)), ((), ())), preferred_element_type=F32)


def _store_rows(ref, base, rows):
    n = rows.shape[0]
    for c in range(ROW_SLABS):
        ref[pl.ds(base + c, n, stride=ROW_SLABS), :] = rows[:, c * LANES:(c + 1) * LANES]


def _load_rows(ref, base, n):
    return jnp.concatenate(
        [ref[pl.ds(base + c, n, stride=ROW_SLABS), :] for c in range(ROW_SLABS)], axis=-1)


def _ada_kernel(c_ref, w_ref, b_ref, o_ref):
    c = c_ref[...]
    s = c * jax.nn.sigmoid(c)
    o_ref[...] = jnp.dot(s, w_ref[...], precision=HIGHEST, preferred_element_type=F32) + b_ref[...]


def _ada(c_all, ada_w, ada_b):
    rows = c_all.shape[0]
    return pl.pallas_call(
        _ada_kernel,
        grid=(N_MOD,),
        in_specs=[
            pl.BlockSpec((rows, D_MODEL), lambda n: (0, 0)),
            pl.BlockSpec((D_MODEL, D_MODEL), lambda n: (0, n)),
            pl.BlockSpec((1, D_MODEL), lambda n: (0, n)),
        ],
        out_specs=pl.BlockSpec((rows, D_MODEL), lambda n: (0, n)),
        out_shape=jax.ShapeDtypeStruct((rows, N_MOD * D_MODEL), F32),
        name="ada",
    )(c_all, ada_w, ada_b.reshape(1, -1))


def _route(h2, router_wt2, router_b):
    n = h2.shape[0]
    h_hi = h2.astype(BF16)
    h_lo = (h2 - h_hi.astype(F32)).astype(BF16)
    l1 = _dot_nt(router_wt2, h_hi)
    l2 = _dot_nt(router_wt2[:N_EXPERTS], h_lo)
    logits = l1[:N_EXPERTS] + l1[N_EXPERTS:] + l2 + router_b

    sub = lax.broadcasted_iota(jnp.int32, (N_EXPERTS, n), 0)
    work = logits
    vals, hots = [], []
    for _ in range(TOP_K):
        m = jnp.max(work, axis=0, keepdims=True)
        idx = jnp.min(jnp.where(work == m, sub, N_EXPERTS), axis=0, keepdims=True)
        hot = sub == idx
        work = jnp.where(hot, -jnp.inf, work)
        vals.append(m)
        hots.append(hot)
    exps = [jnp.exp(v - vals[0]) for v in vals]
    denom = exps[0] + exps[1] + exps[2] + exps[3]
    gates = [e / denom for e in exps]

    onehot = jnp.zeros((N_EXPERTS, n), F32)
    for hot in hots:
        onehot = onehot + hot.astype(F32)
    onehot_bf = onehot.astype(BF16)
    t_r = lax.broadcasted_iota(jnp.int32, (n, n), 0)
    t_c = lax.broadcasted_iota(jnp.int32, (n, n), 1)
    rank = _dot(onehot_bf, (t_r < t_c).astype(BF16))
    cnt = jnp.sum(onehot, axis=1, keepdims=True)
    e_r = lax.broadcasted_iota(jnp.int32, (N_EXPERTS, N_EXPERTS), 0)
    e_c = lax.broadcasted_iota(jnp.int32, (N_EXPERTS, N_EXPERTS), 1)
    below = _dot((e_c < e_r).astype(BF16), onehot_bf)
    off = jnp.sum(below, axis=1, keepdims=True)
    base = rank + off

    row8 = lax.broadcasted_iota(jnp.int32, (SUBLANES, n), 0)
    meta = jnp.zeros((SUBLANES, n), F32)
    for k in range(TOP_K):
        pos_k = jnp.sum(jnp.where(hots[k], base, 0.0), axis=0, keepdims=True)
        meta = jnp.where(row8 == k, pos_k, meta)
        meta = jnp.where(row8 == TOP_K + k, gates[k], meta)
    return meta, jnp.broadcast_to(cnt, (N_EXPERTS, LANES))


def _mix_prompt_kernel(alpha_res, x_ref, mod_ref, w_in_ref, sgu_g_ref, sgu_bln_ref, sgu_w_ref,
                       sgu_bias_ref, conv_w_ref, conv_b_ref, w_out_ref, ln1_g_ref, ln1_b_ref,
                       router_w_ref, router_b_ref,
                       x1_ref, h2_ref, meta_ref, cnt_ref, convst_ref, carry_ref):
    t = pl.program_id(1)

    @pl.when(t == 0)
    def _():
        carry_ref[...] = jnp.zeros_like(carry_ref)

    x = x_ref[...]
    mod = mod_ref[...]
    sh1, sc1, g1, sh2, sc2 = mod[0:1], mod[1:2], mod[2:3], mod[3:4], mod[4:5]
    h = (x * (1.0 + sc1) + sh1).astype(BF16)
    z = _dot(h, w_in_ref[...])
    u = z[:, 0:A_WIDTH]
    v = z[:, A_WIDTH:2 * A_WIDTH]
    gate_b = z[:, 2 * A_WIDTH:2 * A_WIDTH + B_WIDTH]
    gate_c = z[:, 2 * A_WIDTH + B_WIDTH:2 * A_WIDTH + 2 * B_WIDTH]
    hb = z[:, 2 * A_WIDTH + 2 * B_WIDTH:]

    tm = x.shape[0]
    r_i = lax.broadcasted_iota(jnp.int32, (CHUNK, CHUNK), 0)
    c_i = lax.broadcasted_iota(jnp.int32, (CHUNK, CHUNK), 1)
    tril = c_i <= r_i
    a_parts = []
    for hd in range(A_HEADS):
        sl = slice(hd * A_HEAD_DIM, (hd + 1) * A_HEAD_DIM)
        vn = _layer_norm(v[:, sl], sgu_g_ref[hd:hd + 1, :], sgu_bln_ref[hd:hd + 1, :]).astype(BF16)
        wm = jnp.where(tril, sgu_w_ref[hd], 0.0).astype(BF16)
        bias = sgu_bias_ref[:, hd:hd + 1]
        s_parts = []
        for c in range(tm // CHUNK):
            s_parts.append(_dot(wm, vn[c * CHUNK:(c + 1) * CHUNK, :]) + bias)
        a_parts.append(u[:, sl] * jnp.concatenate(s_parts, axis=0))

    q = gate_c * hb
    row = lax.broadcasted_iota(jnp.int32, q.shape, 0)
    prev2 = carry_ref[0:1, :]
    prev1 = carry_ref[1:2, :]
    q_m1 = jnp.where(row == 0, prev1, pltpu.roll(q, 1, 0))
    q_m2 = jnp.where(row == 0, prev2, jnp.where(row == 1, prev1, pltpu.roll(q, 2, 0)))
    conv = conv_b_ref[...] + q_m2 * conv_w_ref[0:1, :] + q_m1 * conv_w_ref[1:2, :] + q * conv_w_ref[2:3, :]
    b_out = gate_b * conv
    carry_ref[0:2, :] = q[tm - 2:tm, :]
    convst_ref[...] = q[tm - 2:tm, :]

    mix_in = jnp.concatenate(a_parts + [b_out], axis=-1).astype(BF16)
    mix = _dot(mix_in, w_out_ref[...])
    x1 = _layer_norm(alpha_res * x + g1 * mix, ln1_g_ref[...], ln1_b_ref[...])
    x1_ref[...] = x1
    h2 = x1 * (1.0 + sc2) + sh2
    h2_ref[...] = h2.astype(BF16)
    meta, cnt = _route(h2, router_w_ref[...], router_b_ref[...])
    meta_ref[...] = meta
    cnt_ref[...] = cnt.astype(jnp.int32)


def _mix_prompt(x, mod, w_in_bf, sgu_g, sgu_bln, sgu_w, sgu_bias_t, conv_w, conv_b, w_out_bf,
                ln1_g, ln1_b, router_wt2, router_b_col, alpha_res):
    bsz, seq, _ = x.shape
    tiles = seq // TOK_TILE
    n_tok = bsz * seq
    const2 = lambda b, t: (0, 0)
    tok = lambda b, t: (b * tiles + t, 0)
    tile3 = lambda b, t: (b * tiles + t, 0, 0)
    return pl.pallas_call(
        functools.partial(_mix_prompt_kernel, alpha_res),
        grid=(bsz, tiles),
        in_specs=[
            pl.BlockSpec((None, TOK_TILE, D_MODEL), lambda b, t: (b, t, 0)),
            pl.BlockSpec((None, SUBLANES, D_MODEL), lambda b, t: (b, 0, 0)),
            pl.BlockSpec((D_MODEL, PROJ_COLS), const2),
            pl.BlockSpec((A_HEADS, A_HEAD_DIM), const2),
            pl.BlockSpec((A_HEADS, A_HEAD_DIM), const2),
            pl.BlockSpec((A_HEADS, CHUNK, CHUNK), lambda b, t: (0, 0, 0)),
            pl.BlockSpec((CHUNK, A_HEADS), const2),
            pl.BlockSpec((3, B_WIDTH), const2),
            pl.BlockSpec((1, B_WIDTH), const2),
            pl.BlockSpec((D_MODEL, D_MODEL), const2),
            pl.BlockSpec((1, D_MODEL), const2),
            pl.BlockSpec((1, D_MODEL), const2),
            pl.BlockSpec((2 * N_EXPERTS, D_MODEL), const2),
            pl.BlockSpec((N_EXPERTS, 1), const2),
        ],
        out_specs=[
            pl.BlockSpec((TOK_TILE, D_MODEL), tok),
            pl.BlockSpec((TOK_TILE, D_MODEL), tok),
            pl.BlockSpec((None, SUBLANES, TOK_TILE), tile3),
            pl.BlockSpec((None, N_EXPERTS, LANES), tile3),
            pl.BlockSpec((None, 2, B_WIDTH), lambda b, t: (b, 0, 0)),
        ],
        out_shape=[
            jax.ShapeDtypeStruct((n_tok, D_MODEL), F32),
            jax.ShapeDtypeStruct((n_tok, D_MODEL), BF16),
            jax.ShapeDtypeStruct((bsz * tiles, SUBLANES, TOK_TILE), F32),
            jax.ShapeDtypeStruct((bsz * tiles, N_EXPERTS, LANES), jnp.int32),
            jax.ShapeDtypeStruct((bsz, 2, B_WIDTH), F32),
        ],
        scratch_shapes=[pltpu.VMEM((SUBLANES, B_WIDTH), F32)],
        compiler_params=pltpu.CompilerParams(
            dimension_semantics=("arbitrary", "arbitrary"), vmem_limit_bytes=VMEM_LIMIT),
        name="mix_prompt",
    )(x, mod, w_in_bf, sgu_g, sgu_bln, sgu_w, sgu_bias_t, conv_w, conv_b, w_out_bf,
      ln1_g, ln1_b, router_wt2, router_b_col)


def _mix_sample_kernel(alpha_res, x_ref, mod_ref, prev0_ref, prev1_ref, w_in_ref, sgu_g_ref,
                       sgu_bln_ref, sgu_w00_ref, sgu_b0_ref, conv_w_ref, conv_b_ref, w_out_ref,
                       ln1_g_ref, ln1_b_ref, router_w_ref, router_b_ref,
                       x1_ref, h2_ref, meta_ref, cnt_ref, q_ref, vn_ref):
    x = x_ref[...]
    sh1 = mod_ref[:, 0:D_MODEL]
    sc1 = mod_ref[:, D_MODEL:2 * D_MODEL]
    g1 = mod_ref[:, 2 * D_MODEL:3 * D_MODEL]
    sh2 = mod_ref[:, 3 * D_MODEL:4 * D_MODEL]
    sc2 = mod_ref[:, 4 * D_MODEL:5 * D_MODEL]
    h = (x * (1.0 + sc1) + sh1).astype(BF16)
    z = _dot(h, w_in_ref[...])
    u = z[:, 0:A_WIDTH]
    v = z[:, A_WIDTH:2 * A_WIDTH]
    gate_b = z[:, 2 * A_WIDTH:2 * A_WIDTH + B_WIDTH]
    gate_c = z[:, 2 * A_WIDTH + B_WIDTH:2 * A_WIDTH + 2 * B_WIDTH]
    hb = z[:, 2 * A_WIDTH + 2 * B_WIDTH:]

    vn_parts = []
    for hd in range(A_HEADS):
        sl = slice(hd * A_HEAD_DIM, (hd + 1) * A_HEAD_DIM)
        vn_parts.append(_layer_norm(v[:, sl], sgu_g_ref[hd:hd + 1, :], sgu_bln_ref[hd:hd + 1, :]))
    vn = jnp.concatenate(vn_parts, axis=-1)
    vn_ref[...] = vn
    a_out = u * (vn * sgu_w00_ref[...] + sgu_b0_ref[...])

    q = gate_c * hb
    q_ref[...] = q
    conv = (conv_b_ref[...] + prev0_ref[...] * conv_w_ref[0:1, :] + prev1_ref[...] * conv_w_ref[1:2, :]
            + q * conv_w_ref[2:3, :])
    b_out = gate_b * conv

    mix_in = jnp.concatenate([a_out, b_out], axis=-1).astype(BF16)
    mix = _dot(mix_in, w_out_ref[...])
    x1 = _layer_norm(alpha_res * x + g1 * mix, ln1_g_ref[...], ln1_b_ref[...])
    x1_ref[...] = x1
    h2 = x1 * (1.0 + sc2) + sh2
    h2_ref[...] = h2.astype(BF16)
    meta, cnt = _route(h2, router_w_ref[...], router_b_ref[...])
    meta_ref[...] = meta
    cnt_ref[...] = cnt.astype(jnp.int32)


def _mix_sample(x, mod, prev0, prev1, w_in_bf, sgu_g, sgu_bln, sgu_w00, sgu_b0, conv_w, conv_b,
                w_out_bf, ln1_g, ln1_b, router_wt2, router_b_col, alpha_res):
    n = x.shape[0]
    return pl.pallas_call(
        functools.partial(_mix_sample_kernel, alpha_res),
        out_shape=[
            jax.ShapeDtypeStruct((n, D_MODEL), F32),
            jax.ShapeDtypeStruct((n, D_MODEL), BF16),
            jax.ShapeDtypeStruct((SUBLANES, n), F32),
            jax.ShapeDtypeStruct((N_EXPERTS, LANES), jnp.int32),
            jax.ShapeDtypeStruct((n, B_WIDTH), F32),
            jax.ShapeDtypeStruct((n, A_WIDTH), F32),
        ],
        compiler_params=pltpu.CompilerParams(vmem_limit_bytes=VMEM_LIMIT),
        name="mix_sample",
    )(x, mod, prev0, prev1, w_in_bf, sgu_g, sgu_bln, sgu_w00, sgu_b0, conv_w, conv_b, w_out_bf,
      ln1_g, ln1_b, router_wt2, router_b_col)


def _strip_copy(src_ref, src_row, dst_ref, dst_row, n_rows, sem):
    @pl.when(n_rows > 0)
    def _():
        s = src_row * ROW_SLABS
        if not isinstance(s, int):
            s = pl.multiple_of(s, ROW_SLABS)
        d = pl.multiple_of(dst_row * ROW_SLABS, ROW_SLABS)
        pltpu.make_async_copy(src_ref.at[pl.ds(s, n_rows * ROW_SLABS)],
                              dst_ref.at[pl.ds(d, n_rows * ROW_SLABS)], sem).start()


def _wait_rows(hbm_ref, vmem_ref, n_rows, sem):
    pltpu.make_async_copy(hbm_ref.at[pl.ds(0, n_rows * ROW_SLABS)],
                          vmem_ref.at[pl.ds(0, n_rows * ROW_SLABS)], sem).wait()


def _sort_kernel(n_tok, tile0, n_tiles, first, *refs):
    if first:
        (sorted_row_ref, off_ref, cnt_ref, pad_row_ref, pad_n_ref, h2_ref, meta_ref,
         xs_ref, stage_ref, zero_ref, sem, pad_sem) = refs
    else:
        (sorted_row_ref, off_ref, cnt_ref, pad_row_ref, pad_n_ref, h2_ref, meta_ref, _,
         xs_ref, stage_ref, sem) = refs
    j = pl.program_id(0)
    n_sorted = n_tok * TOP_K
    slot = j % 2

    if first:
        @pl.when(j == 0)
        def _():
            zero_ref[...] = jnp.zeros_like(zero_ref)

            def start(e, carry):
                _strip_copy(zero_ref, 0, xs_ref, pad_row_ref[e], pad_n_ref[e], pad_sem)
                return carry
            lax.fori_loop(0, N_EXPERTS, start, 0)

            def wait(e, carry):
                @pl.when(pad_n_ref[e] > 0)
                def _():
                    pltpu.make_async_copy(zero_ref.at[pl.ds(0, pad_n_ref[e] * ROW_SLABS)],
                                          xs_ref.at[pl.ds(0, pad_n_ref[e] * ROW_SLABS)], pad_sem).wait()
                return carry
            lax.fori_loop(0, N_EXPERTS, wait, 0)

    meta = meta_ref[...]
    sub_r = lax.broadcasted_iota(jnp.int32, (n_sorted, n_tok), 0)
    sel = sub_r == meta[0:1, :].astype(jnp.int32)
    for k in range(1, TOP_K):
        sel = jnp.logical_or(sel, sub_r == meta[k:k + 1, :].astype(jnp.int32))
    perm = jnp.where(sel, 1.0, 0.0).astype(BF16)
    rows = _dot(perm, h2_ref[...])
    stage_base = slot * (n_sorted * ROW_SLABS)
    _store_rows(stage_ref, stage_base, rows)

    base = (tile0 + j) * N_EXPERTS

    def start(e, carry):
        _strip_copy(stage_ref, slot * n_sorted + off_ref[base + e], xs_ref, sorted_row_ref[base + e],
                    cnt_ref[base + e], sem.at[slot])
        return carry
    lax.fori_loop(0, N_EXPERTS, start, 0)

    @pl.when(j > 0)
    def _():
        _wait_rows(xs_ref, stage_ref, n_sorted, sem.at[1 - slot])

    @pl.when(j == n_tiles - 1)
    def _():
        _wait_rows(xs_ref, stage_ref, n_sorted, sem.at[slot])


def _sort(tables, pad_row, pad_n, h2, meta, xs, n_sorted_rows, n_tok, tile0):
    sorted_row, off, cnt = tables
    first = xs is None
    n_tiles = h2.shape[0] // n_tok
    if meta.ndim == 2:
        meta_spec = pl.BlockSpec((SUBLANES, n_tok), lambda j, *_: (0, j))
    else:
        meta_spec = pl.BlockSpec((None, SUBLANES, n_tok), lambda j, *_: (j, 0, 0))
    in_specs = [pl.BlockSpec((n_tok, D_MODEL), lambda j, *_: (j, 0)), meta_spec]
    operands = [sorted_row, off, cnt, pad_row, pad_n, h2, meta]
    scratch = [pltpu.VMEM((2 * n_tok * TOP_K * ROW_SLABS, LANES), F32)]
    if first:
        scratch.append(pltpu.VMEM((ROW_TILE * ROW_SLABS, LANES), F32))
        aliases = {}
    else:
        in_specs.append(pl.BlockSpec(memory_space=pl.ANY))
        operands.append(xs)
        aliases = {len(operands) - 1: 0}
    scratch.append(pltpu.SemaphoreType.DMA((2,)))
    if first:
        scratch.append(pltpu.SemaphoreType.DMA(()))
    grid_spec = pltpu.PrefetchScalarGridSpec(
        num_scalar_prefetch=5,
        grid=(n_tiles,),
        in_specs=in_specs,
        out_specs=pl.BlockSpec(memory_space=pl.ANY),
        scratch_shapes=scratch,
    )
    return pl.pallas_call(
        functools.partial(_sort_kernel, n_tok, tile0, n_tiles, first),
        grid_spec=grid_spec,
        out_shape=jax.ShapeDtypeStruct((n_sorted_rows * ROW_SLABS, LANES), F32),
        input_output_aliases=aliases,
        compiler_params=pltpu.CompilerParams(
            dimension_semantics=("arbitrary",), vmem_limit_bytes=VMEM_LIMIT),
        name="sort_first" if first else "sort_more",
    )(*operands)


def _experts_kernel(tile_e_ref, tile_first_ref, n_used_ref, xs_ref, w_gu_ref, b_gu_ref, w_down_ref,
                    b_down_ref, ys_ref, w_gu_bf_ref, w_down_bf_ref):
    i = pl.program_id(0)

    @pl.when(tile_first_ref[i] == 1)
    def _():
        w_gu_bf_ref[...] = w_gu_ref[...].astype(BF16)
        w_down_bf_ref[...] = w_down_ref[...].astype(BF16)

    @pl.when(i < n_used_ref[0])
    def _():
        x = _load_rows(xs_ref, 0, ROW_TILE).astype(BF16)
        gu = _dot(x, w_gu_bf_ref[...]) + b_gu_ref[...]
        gate = jnp.minimum(gu[:, :D_EXPERT], SWIGLU_LIMIT)
        up = jnp.clip(gu[:, D_EXPERT:], -SWIGLU_LIMIT, SWIGLU_LIMIT)
        act = (up + 1.0) * gate * jax.nn.sigmoid(SWIGLU_ALPHA * gate)
        y = _dot(act.astype(BF16), w_down_bf_ref[...]) + b_down_ref[...]
        _store_rows(ys_ref, 0, y)


def _experts(tile_e, tile_first, n_used, xs, w_gu, b_gu, w_down, b_down):
    n_tiles = xs.shape[0] // (ROW_TILE * ROW_SLABS)
    row_blk = lambda i, te, tf, nu: (jnp.minimum(i, jnp.maximum(nu[0], 1) - 1), 0)
    w_blk = lambda i, te, tf, nu: (te[i], 0, 0)
    grid_spec = pltpu.PrefetchScalarGridSpec(
        num_scalar_prefetch=3,
        grid=(n_tiles,),
        in_specs=[
            pl.BlockSpec((ROW_TILE * ROW_SLABS, LANES), row_blk),
            pl.BlockSpec((None, D_MODEL, 2 * D_EXPERT), w_blk),
            pl.BlockSpec((None, 1, 2 * D_EXPERT), w_blk),
            pl.BlockSpec((None, D_EXPERT, D_MODEL), w_blk),
            pl.BlockSpec((None, 1, D_MODEL), w_blk),
        ],
        out_specs=pl.BlockSpec((ROW_TILE * ROW_SLABS, LANES), row_blk),
        scratch_shapes=[pltpu.VMEM((D_MODEL, 2 * D_EXPERT), BF16),
                        pltpu.VMEM((D_EXPERT, D_MODEL), BF16)],
    )
    return pl.pallas_call(
        _experts_kernel,
        grid_spec=grid_spec,
        out_shape=jax.ShapeDtypeStruct(xs.shape, F32),
        compiler_params=pltpu.CompilerParams(
            dimension_semantics=("arbitrary",), vmem_limit_bytes=VMEM_LIMIT),
        name="experts",
    )(tile_e, tile_first, n_used, xs, w_gu, b_gu.reshape(N_EXPERTS, 1, -1), w_down,
      b_down.reshape(N_EXPERTS, 1, -1))


def _combine_kernel(n_tok, tile0, n_tiles, mod_rows, alpha_res, sorted_row_ref, off_ref, cnt_ref,
                    ys_ref, x1_ref, meta_ref, mod_ref, ln2_g_ref, ln2_b_ref, out_ref,
                    stage_ref, sem):
    j = pl.program_id(0)
    n_sorted = n_tok * TOP_K
    slot = j % 2

    def start_tile(tile, to_slot):
        base = (tile0 + tile) * N_EXPERTS

        def start(e, carry):
            _strip_copy(ys_ref, sorted_row_ref[base + e], stage_ref,
                        to_slot * n_sorted + off_ref[base + e], cnt_ref[base + e], sem.at[to_slot])
            return carry
        lax.fori_loop(0, N_EXPERTS, start, 0)

    @pl.when(j == 0)
    def _():
        start_tile(j, slot)

    @pl.when(j + 1 < n_tiles)
    def _():
        start_tile(j + 1, 1 - slot)

    meta = meta_ref[...]
    sub_r = lax.broadcasted_iota(jnp.int32, (n_sorted, n_tok), 0)
    comb_t = jnp.zeros((n_sorted, n_tok), F32)
    for k in range(TOP_K):
        comb_t = jnp.where(sub_r == meta[k:k + 1, :].astype(jnp.int32),
                           meta[TOP_K + k:TOP_K + k + 1, :], comb_t)
    comb_t = comb_t.astype(BF16)

    _wait_rows(ys_ref, stage_ref, n_sorted, sem.at[slot])
    ys = _load_rows(stage_ref, slot * (n_sorted * ROW_SLABS), n_sorted).astype(BF16)
    ffn = lax.dot_general(comb_t, ys, (((0,), (0,)), ((), ())), preferred_element_type=F32)
    if mod_rows:
        g2 = mod_ref[:, 5 * D_MODEL:6 * D_MODEL]
    else:
        g2 = mod_ref[5:6, :]
    out_ref[...] = _layer_norm(alpha_res * x1_ref[...] + g2 * ffn, ln2_g_ref[...], ln2_b_ref[...])


def _combine(tables, ys, x1, meta, mod, ln2_g, ln2_b, n_tok, tile0, alpha_res):
    sorted_row, off, cnt = tables
    n_tiles = x1.shape[0] // n_tok
    mod_rows = mod.ndim == 2
    if mod_rows:
        mod_spec = pl.BlockSpec((n_tok, N_MOD * D_MODEL), lambda j, *_: (j, 0))
    else:
        tiles_per_seq = n_tiles // mod.shape[0]
        mod_spec = pl.BlockSpec((None, SUBLANES, D_MODEL), lambda j, *_: (j // tiles_per_seq, 0, 0))
    if meta.ndim == 2:
        meta_spec = pl.BlockSpec((SUBLANES, n_tok), lambda j, *_: (0, j))
    else:
        meta_spec = pl.BlockSpec((None, SUBLANES, n_tok), lambda j, *_: (j, 0, 0))
    grid_spec = pltpu.PrefetchScalarGridSpec(
        num_scalar_prefetch=3,
        grid=(n_tiles,),
        in_specs=[
            pl.BlockSpec(memory_space=pl.ANY),
            pl.BlockSpec((n_tok, D_MODEL), lambda j, *_: (j, 0)),
            meta_spec,
            mod_spec,
            pl.BlockSpec((1, D_MODEL), lambda j, *_: (0, 0)),
            pl.BlockSpec((1, D_MODEL), lambda j, *_: (0, 0)),
        ],
        out_specs=pl.BlockSpec((n_tok, D_MODEL), lambda j, *_: (j, 0)),
        scratch_shapes=[pltpu.VMEM((2 * n_tok * TOP_K * ROW_SLABS, LANES), F32),
                        pltpu.SemaphoreType.DMA((2,))],
    )
    return pl.pallas_call(
        functools.partial(_combine_kernel, n_tok, tile0, n_tiles, mod_rows, alpha_res),
        grid_spec=grid_spec,
        out_shape=jax.ShapeDtypeStruct(x1.shape, F32),
        compiler_params=pltpu.CompilerParams(
            dimension_semantics=("arbitrary",), vmem_limit_bytes=VMEM_LIMIT),
        name="combine",
    )(sorted_row, off, cnt, ys, x1, meta, mod, ln2_g, ln2_b)


def _routing_tables(cnt_all, n_row_tiles):
    total = jnp.sum(cnt_all, axis=0)
    n_tile_e = (total + ROW_TILE - 1) // ROW_TILE
    tile_end = jnp.cumsum(n_tile_e)
    tile_start = tile_end - n_tile_e
    row_start = tile_start * ROW_TILE
    cum = jnp.cumsum(cnt_all, axis=0) - cnt_all
    off = jnp.cumsum(cnt_all, axis=1) - cnt_all
    sorted_row = row_start[None, :] + cum
    n_used = tile_end[-1]
    tile_id = jnp.minimum(jnp.arange(n_row_tiles, dtype=jnp.int32), n_used - 1)
    tile_e = jnp.sum((tile_id[:, None] >= tile_end[None, :]).astype(jnp.int32), axis=1)
    tile_first = jnp.logical_and(jnp.arange(n_row_tiles) < n_used,
                                 tile_id == tile_start[tile_e]).astype(jnp.int32)
    pad_row = row_start + total
    pad_n = n_tile_e * ROW_TILE - total
    i32 = lambda a: a.astype(jnp.int32)
    return ((i32(sorted_row).reshape(-1), i32(off).reshape(-1), i32(cnt_all).reshape(-1)),
            i32(pad_row), i32(pad_n), i32(tile_e), tile_first, i32(n_used).reshape(1))


def kernel(x_prompt, x_sample, state_conv, c_prompt, c_sample, ada_w, ada_b, w_in, sgu_ln_g, sgu_ln_b,
           sgu_w, sgu_b, conv_w, conv_b, w_out, ln1_g, ln1_b, router_w, router_b, w_gu, b_gu,
           w_down, b_down, ln2_g, ln2_b):
    depth = ada_w.shape[0]
    assert depth == 1
    bsz, seq, _ = x_prompt.shape
    n_dec = x_sample.shape[0]
    assert x_sample.shape[1] == 1 and seq % TOK_TILE == 0
    alpha_res = (2.0 * depth) ** 0.25
    l = 0

    mod = _ada(jnp.concatenate([c_prompt, c_sample], axis=0), ada_w[l], ada_b[l])
    mod_p = jnp.pad(mod[:bsz].reshape(bsz, N_MOD, D_MODEL), ((0, 0), (0, SUBLANES - N_MOD), (0, 0)))
    mod_s = mod[bsz:]

    w_in_bf = w_in[l].astype(BF16)
    w_out_bf = w_out[l].astype(BF16)
    router_wt = jnp.transpose(router_w[l])
    router_wt_hi = router_wt.astype(BF16)
    router_wt_lo = (router_wt - router_wt_hi.astype(F32)).astype(BF16)
    router_wt2 = jnp.concatenate([router_wt_hi, router_wt_lo], axis=0)
    router_b_col = router_b[l].reshape(N_EXPERTS, 1)
    row = lambda a: a.reshape(1, -1)

    x1_p, h2_p, meta_p, cnt_p, convst_p = _mix_prompt(
        x_prompt, mod_p, w_in_bf, sgu_ln_g[l], sgu_ln_b[l], sgu_w[l], jnp.transpose(sgu_b[l]),
        conv_w[l], row(conv_b[l]), w_out_bf, row(ln1_g[l]), row(ln1_b[l]), router_wt2,
        router_b_col, alpha_res)
    x1_s, h2_s, meta_s, cnt_s, q_s, vn_s = _mix_sample(
        x_sample.reshape(n_dec, D_MODEL), mod_s, state_conv[l, :, 0, :], state_conv[l, :, 1, :],
        w_in_bf, sgu_ln_g[l], sgu_ln_b[l], row(jnp.repeat(sgu_w[l, :, 0, 0], A_HEAD_DIM)),
        row(jnp.repeat(sgu_b[l, :, 0], A_HEAD_DIM)), conv_w[l], row(conv_b[l]), w_out_bf,
        row(ln1_g[l]), row(ln1_b[l]), router_wt2, router_b_col, alpha_res)

    n_ptiles = bsz * seq // TOK_TILE
    cnt_all = jnp.concatenate([cnt_p[:, :, 0], cnt_s[None, :, 0]], axis=0)
    n_assign = (bsz * seq + n_dec) * TOP_K
    n_row_tiles = -(-n_assign // ROW_TILE) + N_EXPERTS
    tables, pad_row, pad_n, tile_e, tile_first, n_used = _routing_tables(cnt_all, n_row_tiles)

    n_sorted_rows = n_row_tiles * ROW_TILE
    xs = _sort(tables, pad_row, pad_n, h2_p, meta_p, None, n_sorted_rows, TOK_TILE, 0)
    xs = _sort(tables, pad_row, pad_n, h2_s, meta_s, xs, n_sorted_rows, n_dec, n_ptiles)
    ys = _experts(tile_e, tile_first, n_used, xs, w_gu[l], b_gu[l], w_down[l], b_down[l])
    y_p = _combine(tables, ys, x1_p, meta_p, mod_p, row(ln2_g[l]), row(ln2_b[l]), TOK_TILE, 0, alpha_res)
    y_s = _combine(tables, ys, x1_s, meta_s, mod_s, row(ln2_g[l]), row(ln2_b[l]), n_dec, n_ptiles,
                   alpha_res)

    conv_state_sample = jnp.stack([state_conv[l, :, 1, :], q_s], axis=1)[None]
    return (y_p.reshape(bsz, seq, D_MODEL),
            y_s.reshape(n_dec, 1, D_MODEL),
            convst_p[None],
            conv_state_sample,
            vn_s.reshape(1, n_dec, 1, A_HEADS, A_HEAD_DIM))
```

```python
import functools

import jax
import jax.numpy as jnp
from jax import lax
from jax.experimental import pallas as pl
from jax.experimental.pallas import tpu as pltpu

F32 = jnp.float32
BF16 = jnp.bfloat16

D_MODEL = 1024
A_WIDTH = 512
B_WIDTH = 512
A_HEADS = 4
A_HEAD_DIM = 128
CHUNK = 128
PROJ_COLS = 2 * A_WIDTH + 3 * B_WIDTH
N_EXPERTS = 32
TOP_K = 4
D_EXPERT = 1024
SWIGLU_LIMIT = 7.0
SWIGLU_ALPHA = 1.702
LN_EPS = 1e-5
N_MOD = 6

LANES = 128
SUBLANES = 8
ROW_SLABS = D_MODEL // LANES
TOK_TILE = 256
ROW_TILE = 256
VMEM_LIMIT = 56 * 1024 * 1024


def _layer_norm(x, g, b):
    mu = jnp.mean(x, axis=-1, keepdims=True)
    xc = x - mu
    var = jnp.mean(xc * xc, axis=-1, keepdims=True)
    return xc * lax.rsqrt(var + LN_EPS) * g + b


def _dot(a, b):
    return jnp.dot(a, b, preferred_element_type=F32)


def _dot_nt(a, b):
    return lax.dot_general(a, b, (((1,), (1,)), ((), ())), preferred_element_type=F32)


def _store_rows(ref, base, rows):
    n = rows.shape[0]
    for c in range(ROW_SLABS):
        ref[pl.ds(base + c, n, stride=ROW_SLABS), :] = rows[:, c * LANES:(c + 1) * LANES]


def _load_rows(ref, base, n):
    return jnp.concatenate(
        [ref[pl.ds(base + c, n, stride=ROW_SLABS), :] for c in range(ROW_SLABS)], axis=-1)


def _split_bf16(a):
    hi = a.astype(BF16)
    lo = (a - hi.astype(F32)).astype(BF16)
    return hi, lo


def _ada_kernel(n_first, c_ref, w_ref, b_ref, o_first_ref, o_rest_ref):
    c = c_ref[...]
    s_hi, s_lo = _split_bf16(c * jax.nn.sigmoid(c))
    w_hi, w_lo = _split_bf16(w_ref[...])
    m = _dot(s_hi, w_hi) + _dot(s_hi, w_lo) + _dot(s_lo, w_hi) + b_ref[...]
    o_first_ref[...] = m[:n_first]
    o_rest_ref[...] = m[n_first:]


def _ada(c_all, n_first, ada_w, ada_b):
    rows = c_all.shape[0]
    return pl.pallas_call(
        functools.partial(_ada_kernel, n_first),
        grid=(N_MOD,),
        in_specs=[
            pl.BlockSpec((rows, D_MODEL), lambda n: (0, 0)),
            pl.BlockSpec((D_MODEL, D_MODEL), lambda n: (0, n)),
            pl.BlockSpec((1, D_MODEL), lambda n: (0, n)),
        ],
        out_specs=[pl.BlockSpec((n_first, D_MODEL), lambda n: (0, n)),
                   pl.BlockSpec((rows - n_first, D_MODEL), lambda n: (0, n))],
        out_shape=[jax.ShapeDtypeStruct((n_first, N_MOD * D_MODEL), F32),
                   jax.ShapeDtypeStruct((rows - n_first, N_MOD * D_MODEL), F32)],
        name="ada",
    )(c_all, ada_w, ada_b.reshape(1, -1))


def _route(h2, router_wt2, router_b):
    n = h2.shape[0]
    h_hi = h2.astype(BF16)
    h_lo = (h2 - h_hi.astype(F32)).astype(BF16)
    l1 = _dot_nt(router_wt2, h_hi)
    l2 = _dot_nt(router_wt2[:N_EXPERTS], h_lo)
    logits = l1[:N_EXPERTS] + l1[N_EXPERTS:] + l2 + router_b

    sub = lax.broadcasted_iota(jnp.int32, (N_EXPERTS, n), 0)
    work = logits
    vals, hots = [], []
    for _ in range(TOP_K):
        m = jnp.max(work, axis=0, keepdims=True)
        idx = jnp.min(jnp.where(work == m, sub, N_EXPERTS), axis=0, keepdims=True)
        hot = sub == idx
        work = jnp.where(hot, -jnp.inf, work)
        vals.append(m)
        hots.append(hot)
    exps = [jnp.exp(v - vals[0]) for v in vals]
    denom = exps[0] + exps[1] + exps[2] + exps[3]
    gates = [e / denom for e in exps]

    onehot = jnp.zeros((N_EXPERTS, n), F32)
    for hot in hots:
        onehot = onehot + hot.astype(F32)
    onehot_bf = onehot.astype(BF16)
    t_r = lax.broadcasted_iota(jnp.int32, (n, n), 0)
    t_c = lax.broadcasted_iota(jnp.int32, (n, n), 1)
    rank = _dot(onehot_bf, (t_r < t_c).astype(BF16))
    cnt = jnp.sum(onehot, axis=1, keepdims=True)
    e_r = lax.broadcasted_iota(jnp.int32, (N_EXPERTS, N_EXPERTS), 0)
    e_c = lax.broadcasted_iota(jnp.int32, (N_EXPERTS, N_EXPERTS), 1)
    below = _dot((e_c < e_r).astype(BF16), onehot_bf)
    off = jnp.sum(below, axis=1, keepdims=True)
    base = rank + off

    row8 = lax.broadcasted_iota(jnp.int32, (SUBLANES, n), 0)
    meta = jnp.zeros((SUBLANES, n), F32)
    for k in range(TOP_K):
        pos_k = jnp.sum(jnp.where(hots[k], base, 0.0), axis=0, keepdims=True)
        meta = jnp.where(row8 == k, pos_k, meta)
        meta = jnp.where(row8 == TOP_K + k, gates[k], meta)
    return meta, jnp.broadcast_to(cnt, (N_EXPERTS, LANES))


def _mix_prompt_kernel(alpha_res, x_ref, mod_ref, w_in_ref, sgu_g_ref, sgu_bln_ref, sgu_w_ref,
                       sgu_bias_ref, conv_w_ref, conv_b_ref, w_out_ref, ln1_g_ref, ln1_b_ref,
                       router_w_ref, router_b_ref,
                       x1_ref, h2_ref, meta_ref, cnt_ref, convst_ref, carry_ref):
    t = pl.program_id(1)

    @pl.when(t == 0)
    def _():
        carry_ref[...] = jnp.zeros_like(carry_ref)

    x = x_ref[...]
    mod = mod_ref[...]
    sh1, sc1, g1, sh2, sc2 = mod[0:1], mod[1:2], mod[2:3], mod[3:4], mod[4:5]
    h = (x * (1.0 + sc1) + sh1).astype(BF16)
    z = _dot(h, w_in_ref[...])
    u = z[:, 0:A_WIDTH]
    v = z[:, A_WIDTH:2 * A_WIDTH]
    gate_b = z[:, 2 * A_WIDTH:2 * A_WIDTH + B_WIDTH]
    gate_c = z[:, 2 * A_WIDTH + B_WIDTH:2 * A_WIDTH + 2 * B_WIDTH]
    hb = z[:, 2 * A_WIDTH + 2 * B_WIDTH:]

    tm = x.shape[0]
    r_i = lax.broadcasted_iota(jnp.int32, (CHUNK, CHUNK), 0)
    c_i = lax.broadcasted_iota(jnp.int32, (CHUNK, CHUNK), 1)
    tril = c_i <= r_i
    a_parts = []
    for hd in range(A_HEADS):
        sl = slice(hd * A_HEAD_DIM, (hd + 1) * A_HEAD_DIM)
        vn = _layer_norm(v[:, sl], sgu_g_ref[hd:hd + 1, :], sgu_bln_ref[hd:hd + 1, :]).astype(BF16)
        wm = jnp.where(tril, sgu_w_ref[hd], 0.0).astype(BF16)
        bias = sgu_bias_ref[:, hd:hd + 1]
        s_parts = []
        for c in range(tm // CHUNK):
            s_parts.append(_dot(wm, vn[c * CHUNK:(c + 1) * CHUNK, :]) + bias)
        a_parts.append(u[:, sl] * jnp.concatenate(s_parts, axis=0))

    q = gate_c * hb
    row = lax.broadcasted_iota(jnp.int32, q.shape, 0)
    prev2 = carry_ref[0:1, :]
    prev1 = carry_ref[1:2, :]
    q_m1 = jnp.where(row == 0, prev1, pltpu.roll(q, 1, 0))
    q_m2 = jnp.where(row == 0, prev2, jnp.where(row == 1, prev1, pltpu.roll(q, 2, 0)))
    conv = conv_b_ref[...] + q_m2 * conv_w_ref[0:1, :] + q_m1 * conv_w_ref[1:2, :] + q * conv_w_ref[2:3, :]
    b_out = gate_b * conv
    carry_ref[0:2, :] = q[tm - 2:tm, :]
    convst_ref[...] = q[tm - 2:tm, :]

    mix_in = jnp.concatenate(a_parts + [b_out], axis=-1).astype(BF16)
    mix = _dot(mix_in, w_out_ref[...])
    x1 = _layer_norm(alpha_res * x + g1 * mix, ln1_g_ref[...], ln1_b_ref[...])
    x1_ref[...] = x1
    h2 = x1 * (1.0 + sc2) + sh2
    h2_ref[...] = h2.astype(BF16)
    meta, cnt = _route(h2, router_w_ref[...], router_b_ref[...])
    meta_ref[...] = meta
    cnt_ref[...] = cnt.astype(jnp.int32)


def _mix_prompt(x, mod, w_in_bf, sgu_g, sgu_bln, sgu_w, sgu_bias_t, conv_w, conv_b, w_out_bf,
                ln1_g, ln1_b, router_wt2, router_b_col, alpha_res):
    bsz, seq, _ = x.shape
    tiles = seq // TOK_TILE
    n_tok = bsz * seq
    const2 = lambda b, t: (0, 0)
    tok = lambda b, t: (b * tiles + t, 0)
    tile3 = lambda b, t: (b * tiles + t, 0, 0)
    return pl.pallas_call(
        functools.partial(_mix_prompt_kernel, alpha_res),
        grid=(bsz, tiles),
        in_specs=[
            pl.BlockSpec((None, TOK_TILE, D_MODEL), lambda b, t: (b, t, 0)),
            pl.BlockSpec((None, SUBLANES, D_MODEL), lambda b, t: (b, 0, 0)),
            pl.BlockSpec((D_MODEL, PROJ_COLS), const2),
            pl.BlockSpec((A_HEADS, A_HEAD_DIM), const2),
            pl.BlockSpec((A_HEADS, A_HEAD_DIM), const2),
            pl.BlockSpec((A_HEADS, CHUNK, CHUNK), lambda b, t: (0, 0, 0)),
            pl.BlockSpec((CHUNK, A_HEADS), const2),
            pl.BlockSpec((3, B_WIDTH), const2),
            pl.BlockSpec((1, B_WIDTH), const2),
            pl.BlockSpec((D_MODEL, D_MODEL), const2),
            pl.BlockSpec((1, D_MODEL), const2),
            pl.BlockSpec((1, D_MODEL), const2),
            pl.BlockSpec((2 * N_EXPERTS, D_MODEL), const2),
            pl.BlockSpec((N_EXPERTS, 1), const2),
        ],
        out_specs=[
            pl.BlockSpec((TOK_TILE, D_MODEL), tok),
            pl.BlockSpec((TOK_TILE, D_MODEL), tok),
            pl.BlockSpec((None, SUBLANES, TOK_TILE), tile3),
            pl.BlockSpec((None, N_EXPERTS, LANES), tile3),
            pl.BlockSpec((None, 2, B_WIDTH), lambda b, t: (b, 0, 0)),
        ],
        out_shape=[
            jax.ShapeDtypeStruct((n_tok, D_MODEL), F32),
            jax.ShapeDtypeStruct((n_tok, D_MODEL), BF16),
            jax.ShapeDtypeStruct((bsz * tiles, SUBLANES, TOK_TILE), F32),
            jax.ShapeDtypeStruct((bsz * tiles, N_EXPERTS, LANES), jnp.int32),
            jax.ShapeDtypeStruct((bsz, 2, B_WIDTH), F32),
        ],
        scratch_shapes=[pltpu.VMEM((SUBLANES, B_WIDTH), F32)],
        compiler_params=pltpu.CompilerParams(
            dimension_semantics=("arbitrary", "arbitrary"), vmem_limit_bytes=VMEM_LIMIT),
        name="mix_prompt",
    )(x, mod, w_in_bf, sgu_g, sgu_bln, sgu_w, sgu_bias_t, conv_w, conv_b, w_out_bf,
      ln1_g, ln1_b, router_wt2, router_b_col)


def _mix_sample_kernel(alpha_res, x_ref, mod_ref, prev0_ref, prev1_ref, w_in_ref, sgu_g_ref,
                       sgu_bln_ref, sgu_w00_ref, sgu_b0_ref, conv_w_ref, conv_b_ref, w_out_ref,
                       ln1_g_ref, ln1_b_ref, router_w_ref, router_b_ref,
                       x1_ref, h2_ref, meta_ref, cnt_ref, q_ref, vn_ref):
    x = x_ref[...]
    sh1 = mod_ref[:, 0:D_MODEL]
    sc1 = mod_ref[:, D_MODEL:2 * D_MODEL]
    g1 = mod_ref[:, 2 * D_MODEL:3 * D_MODEL]
    sh2 = mod_ref[:, 3 * D_MODEL:4 * D_MODEL]
    sc2 = mod_ref[:, 4 * D_MODEL:5 * D_MODEL]
    h = (x * (1.0 + sc1) + sh1).astype(BF16)
    z = _dot(h, w_in_ref[...])
    u = z[:, 0:A_WIDTH]
    v = z[:, A_WIDTH:2 * A_WIDTH]
    gate_b = z[:, 2 * A_WIDTH:2 * A_WIDTH + B_WIDTH]
    gate_c = z[:, 2 * A_WIDTH + B_WIDTH:2 * A_WIDTH + 2 * B_WIDTH]
    hb = z[:, 2 * A_WIDTH + 2 * B_WIDTH:]

    vn_parts = []
    for hd in range(A_HEADS):
        sl = slice(hd * A_HEAD_DIM, (hd + 1) * A_HEAD_DIM)
        vn_parts.append(_layer_norm(v[:, sl], sgu_g_ref[hd:hd + 1, :], sgu_bln_ref[hd:hd + 1, :]))
    vn = jnp.concatenate(vn_parts, axis=-1)
    vn_ref[...] = vn
    a_out = u * (vn * sgu_w00_ref[...] + sgu_b0_ref[...])

    q = gate_c * hb
    q_ref[...] = q
    conv = (conv_b_ref[...] + prev0_ref[...] * conv_w_ref[0:1, :] + prev1_ref[...] * conv_w_ref[1:2, :]
            + q * conv_w_ref[2:3, :])
    b_out = gate_b * conv

    mix_in = jnp.concatenate([a_out, b_out], axis=-1).astype(BF16)
    mix = _dot(mix_in, w_out_ref[...])
    x1 = _layer_norm(alpha_res * x + g1 * mix, ln1_g_ref[...], ln1_b_ref[...])
    x1_ref[...] = x1
    h2 = x1 * (1.0 + sc2) + sh2
    h2_ref[...] = h2.astype(BF16)
    meta, cnt = _route(h2, router_w_ref[...], router_b_ref[...])
    meta_ref[...] = meta
    cnt_ref[...] = cnt.astype(jnp.int32)


def _mix_sample(x, mod, prev0, prev1, w_in_bf, sgu_g, sgu_bln, sgu_w00, sgu_b0, conv_w, conv_b,
                w_out_bf, ln1_g, ln1_b, router_wt2, router_b_col, alpha_res):
    n = x.shape[0]
    return pl.pallas_call(
        functools.partial(_mix_sample_kernel, alpha_res),
        out_shape=[
            jax.ShapeDtypeStruct((n, D_MODEL), F32),
            jax.ShapeDtypeStruct((n, D_MODEL), BF16),
            jax.ShapeDtypeStruct((SUBLANES, n), F32),
            jax.ShapeDtypeStruct((N_EXPERTS, LANES), jnp.int32),
            jax.ShapeDtypeStruct((n, B_WIDTH), F32),
            jax.ShapeDtypeStruct((n, A_WIDTH), F32),
        ],
        compiler_params=pltpu.CompilerParams(vmem_limit_bytes=VMEM_LIMIT),
        name="mix_sample",
    )(x, mod, prev0, prev1, w_in_bf, sgu_g, sgu_bln, sgu_w00, sgu_b0, conv_w, conv_b, w_out_bf,
      ln1_g, ln1_b, router_wt2, router_b_col)


def _strip_copy(src_ref, src_row, dst_ref, dst_row, n_rows, sem):
    @pl.when(n_rows > 0)
    def _():
        s = src_row * ROW_SLABS
        if not isinstance(s, int):
            s = pl.multiple_of(s, ROW_SLABS)
        d = pl.multiple_of(dst_row * ROW_SLABS, ROW_SLABS)
        pltpu.make_async_copy(src_ref.at[pl.ds(s, n_rows * ROW_SLABS)],
                              dst_ref.at[pl.ds(d, n_rows * ROW_SLABS)], sem).start()


def _wait_rows(hbm_ref, vmem_ref, n_rows, sem):
    pltpu.make_async_copy(hbm_ref.at[pl.ds(0, n_rows * ROW_SLABS)],
                          vmem_ref.at[pl.ds(0, n_rows * ROW_SLABS)], sem).wait()


def _sort_kernel(n_tok, tile0, n_tiles, first, *refs):
    if first:
        (sorted_row_ref, off_ref, cnt_ref, pad_row_ref, pad_n_ref, h2_ref, meta_ref,
         xs_ref, stage_ref, zero_ref, sem, pad_sem) = refs
    else:
        (sorted_row_ref, off_ref, cnt_ref, pad_row_ref, pad_n_ref, h2_ref, meta_ref, _,
         xs_ref, stage_ref, sem) = refs
    j = pl.program_id(0)
    n_sorted = n_tok * TOP_K
    slot = j % 2

    if first:
        @pl.when(j == 0)
        def _():
            zero_ref[...] = jnp.zeros_like(zero_ref)

            def start(e, carry):
                _strip_copy(zero_ref, 0, xs_ref, pad_row_ref[e], pad_n_ref[e], pad_sem)
                return carry
            lax.fori_loop(0, N_EXPERTS, start, 0)

            def wait(e, carry):
                @pl.when(pad_n_ref[e] > 0)
                def _():
                    pltpu.make_async_copy(zero_ref.at[pl.ds(0, pad_n_ref[e] * ROW_SLABS)],
                                          xs_ref.at[pl.ds(0, pad_n_ref[e] * ROW_SLABS)], pad_sem).wait()
                return carry
            lax.fori_loop(0, N_EXPERTS, wait, 0)

    meta = meta_ref[...]
    sub_r = lax.broadcasted_iota(jnp.int32, (n_sorted, n_tok), 0)
    sel = sub_r == meta[0:1, :].astype(jnp.int32)
    for k in range(1, TOP_K):
        sel = jnp.logical_or(sel, sub_r == meta[k:k + 1, :].astype(jnp.int32))
    perm = jnp.where(sel, 1.0, 0.0).astype(BF16)
    rows = _dot(perm, h2_ref[...])
    stage_base = slot * (n_sorted * ROW_SLABS)
    _store_rows(stage_ref, stage_base, rows)

    base = (tile0 + j) * N_EXPERTS

    def start(e, carry):
        _strip_copy(stage_ref, slot * n_sorted + off_ref[base + e], xs_ref, sorted_row_ref[base + e],
                    cnt_ref[base + e], sem.at[slot])
        return carry
    lax.fori_loop(0, N_EXPERTS, start, 0)

    @pl.when(j > 0)
    def _():
        _wait_rows(xs_ref, stage_ref, n_sorted, sem.at[1 - slot])

    @pl.when(j == n_tiles - 1)
    def _():
        _wait_rows(xs_ref, stage_ref, n_sorted, sem.at[slot])


def _sort(tables, pad_row, pad_n, h2, meta, xs, n_sorted_rows, n_tok, tile0):
    sorted_row, off, cnt = tables
    first = xs is None
    n_tiles = h2.shape[0] // n_tok
    if meta.ndim == 2:
        meta_spec = pl.BlockSpec((SUBLANES, n_tok), lambda j, *_: (0, j))
    else:
        meta_spec = pl.BlockSpec((None, SUBLANES, n_tok), lambda j, *_: (j, 0, 0))
    in_specs = [pl.BlockSpec((n_tok, D_MODEL), lambda j, *_: (j, 0)), meta_spec]
    operands = [sorted_row, off, cnt, pad_row, pad_n, h2, meta]
    scratch = [pltpu.VMEM((2 * n_tok * TOP_K * ROW_SLABS, LANES), F32)]
    if first:
        scratch.append(pltpu.VMEM((ROW_TILE * ROW_SLABS, LANES), F32))
        aliases = {}
    else:
        in_specs.append(pl.BlockSpec(memory_space=pl.ANY))
        operands.append(xs)
        aliases = {len(operands) - 1: 0}
    scratch.append(pltpu.SemaphoreType.DMA((2,)))
    if first:
        scratch.append(pltpu.SemaphoreType.DMA(()))
    grid_spec = pltpu.PrefetchScalarGridSpec(
        num_scalar_prefetch=5,
        grid=(n_tiles,),
        in_specs=in_specs,
        out_specs=pl.BlockSpec(memory_space=pl.ANY),
        scratch_shapes=scratch,
    )
    return pl.pallas_call(
        functools.partial(_sort_kernel, n_tok, tile0, n_tiles, first),
        grid_spec=grid_spec,
        out_shape=jax.ShapeDtypeStruct((n_sorted_rows * ROW_SLABS, LANES), F32),
        input_output_aliases=aliases,
        compiler_params=pltpu.CompilerParams(
            dimension_semantics=("arbitrary",), vmem_limit_bytes=VMEM_LIMIT),
        name="sort_first" if first else "sort_more",
    )(*operands)


TILE_SLAB_ROWS = ROW_TILE * ROW_SLABS


def _experts_kernel(tile_start_ref, n_tile_ref, n_used_ref, xs_ref, w_gu_ref, b_gu_ref, w_down_ref,
                    b_down_ref, ys_ref, w_gu_bf_ref, w_down_bf_ref, x_buf, y_buf, x_sem, y_sem):
    e = pl.program_id(0)
    n_used = n_used_ref[0]
    first_tile = tile_start_ref[e]
    n_tile = n_tile_ref[e]

    def x_copy(g, slot):
        return pltpu.make_async_copy(
            xs_ref.at[pl.ds(pl.multiple_of(g * TILE_SLAB_ROWS, TILE_SLAB_ROWS), TILE_SLAB_ROWS)],
            x_buf.at[pl.ds(pl.multiple_of(slot * TILE_SLAB_ROWS, TILE_SLAB_ROWS), TILE_SLAB_ROWS)],
            x_sem.at[slot])

    def y_copy(g, slot):
        return pltpu.make_async_copy(
            y_buf.at[pl.ds(pl.multiple_of(slot * TILE_SLAB_ROWS, TILE_SLAB_ROWS), TILE_SLAB_ROWS)],
            ys_ref.at[pl.ds(pl.multiple_of(g * TILE_SLAB_ROWS, TILE_SLAB_ROWS), TILE_SLAB_ROWS)],
            y_sem.at[slot])

    @pl.when(e == 0)
    def _():
        x_copy(0, 0).start()

    @pl.when(n_tile > 0)
    def _():
        w_gu_bf_ref[...] = w_gu_ref[...].astype(BF16)
        w_down_bf_ref[...] = w_down_ref[...].astype(BF16)

    def tile_body(t, carry):
        g = first_tile + t
        slot = g % 2
        x_copy(g, slot).wait()

        @pl.when(g + 1 < n_used)
        def _():
            x_copy(g + 1, 1 - slot).start()

        x = _load_rows(x_buf, slot * TILE_SLAB_ROWS, ROW_TILE).astype(BF16)
        gu = _dot(x, w_gu_bf_ref[...]) + b_gu_ref[...]
        gate = jnp.minimum(gu[:, :D_EXPERT], SWIGLU_LIMIT)
        up = jnp.clip(gu[:, D_EXPERT:], -SWIGLU_LIMIT, SWIGLU_LIMIT)
        act = (up + 1.0) * gate * jax.nn.sigmoid(SWIGLU_ALPHA * gate)
        y = _dot(act.astype(BF16), w_down_bf_ref[...]) + b_down_ref[...]

        @pl.when(g >= 2)
        def _():
            y_copy(g - 2, slot).wait()

        _store_rows(y_buf, slot * TILE_SLAB_ROWS, y)
        y_copy(g, slot).start()
        return carry

    lax.fori_loop(0, n_tile, tile_body, 0)

    @pl.when(e == N_EXPERTS - 1)
    def _():
        @pl.when(n_used >= 2)
        def _():
            y_copy(n_used - 2, n_used % 2).wait()
        y_copy(n_used - 1, (n_used - 1) % 2).wait()


def _experts(tile_start, n_tile_e, n_used, xs, w_gu, b_gu, w_down, b_down):
    w_blk = lambda e, *_: (e, 0, 0)
    grid_spec = pltpu.PrefetchScalarGridSpec(
        num_scalar_prefetch=3,
        grid=(N_EXPERTS,),
        in_specs=[
            pl.BlockSpec(memory_space=pl.ANY),
            pl.BlockSpec((None, D_MODEL, 2 * D_EXPERT), w_blk),
            pl.BlockSpec((None, 1, 2 * D_EXPERT), w_blk),
            pl.BlockSpec((None, D_EXPERT, D_MODEL), w_blk),
            pl.BlockSpec((None, 1, D_MODEL), w_blk),
        ],
        out_specs=pl.BlockSpec(memory_space=pl.ANY),
        scratch_shapes=[pltpu.VMEM((D_MODEL, 2 * D_EXPERT), BF16),
                        pltpu.VMEM((D_EXPERT, D_MODEL), BF16),
                        pltpu.VMEM((2 * TILE_SLAB_ROWS, LANES), F32),
                        pltpu.VMEM((2 * TILE_SLAB_ROWS, LANES), F32),
                        pltpu.SemaphoreType.DMA((2,)),
                        pltpu.SemaphoreType.DMA((2,))],
    )
    return pl.pallas_call(
        _experts_kernel,
        grid_spec=grid_spec,
        out_shape=jax.ShapeDtypeStruct(xs.shape, F32),
        compiler_params=pltpu.CompilerParams(
            dimension_semantics=("arbitrary",), vmem_limit_bytes=VMEM_LIMIT),
        name="experts",
    )(tile_start, n_tile_e, n_used, xs, w_gu, b_gu.reshape(N_EXPERTS, 1, -1), w_down,
      b_down.reshape(N_EXPERTS, 1, -1))


def _combine_kernel(n_tok, tile0, n_tiles, mod_rows, alpha_res, sorted_row_ref, off_ref, cnt_ref,
                    ys_ref, x1_ref, meta_ref, mod_ref, ln2_g_ref, ln2_b_ref, out_ref,
                    stage_ref, sem):
    j = pl.program_id(0)
    n_sorted = n_tok * TOP_K
    slot = j % 2

    def start_tile(tile, to_slot):
        base = (tile0 + tile) * N_EXPERTS

        def start(e, carry):
            _strip_copy(ys_ref, sorted_row_ref[base + e], stage_ref,
                        to_slot * n_sorted + off_ref[base + e], cnt_ref[base + e], sem.at[to_slot])
            return carry
        lax.fori_loop(0, N_EXPERTS, start, 0)

    @pl.when(j == 0)
    def _():
        start_tile(j, slot)

    @pl.when(j + 1 < n_tiles)
    def _():
        start_tile(j + 1, 1 - slot)

    meta = meta_ref[...]
    sub_r = lax.broadcasted_iota(jnp.int32, (n_sorted, n_tok), 0)
    comb_t = jnp.zeros((n_sorted, n_tok), F32)
    for k in range(TOP_K):
        comb_t = jnp.where(sub_r == meta[k:k + 1, :].astype(jnp.int32),
                           meta[TOP_K + k:TOP_K + k + 1, :], comb_t)
    comb_t = comb_t.astype(BF16)

    _wait_rows(ys_ref, stage_ref, n_sorted, sem.at[slot])
    ys = _load_rows(stage_ref, slot * (n_sorted * ROW_SLABS), n_sorted).astype(BF16)
    ffn = lax.dot_general(comb_t, ys, (((0,), (0,)), ((), ())), preferred_element_type=F32)
    if mod_rows:
        g2 = mod_ref[:, 5 * D_MODEL:6 * D_MODEL]
    else:
        g2 = mod_ref[5:6, :]
    out_ref[...] = _layer_norm(alpha_res * x1_ref[...] + g2 * ffn, ln2_g_ref[...], ln2_b_ref[...])


def _combine(tables, ys, x1, meta, mod, ln2_g, ln2_b, n_tok, tile0, alpha_res):
    sorted_row, off, cnt = tables
    n_tiles = x1.shape[0] // n_tok
    mod_rows = mod.ndim == 2
    if mod_rows:
        mod_spec = pl.BlockSpec((n_tok, N_MOD * D_MODEL), lambda j, *_: (j, 0))
    else:
        tiles_per_seq = n_tiles // mod.shape[0]
        mod_spec = pl.BlockSpec((None, SUBLANES, D_MODEL), lambda j, *_: (j // tiles_per_seq, 0, 0))
    if meta.ndim == 2:
        meta_spec = pl.BlockSpec((SUBLANES, n_tok), lambda j, *_: (0, j))
    else:
        meta_spec = pl.BlockSpec((None, SUBLANES, n_tok), lambda j, *_: (j, 0, 0))
    grid_spec = pltpu.PrefetchScalarGridSpec(
        num_scalar_prefetch=3,
        grid=(n_tiles,),
        in_specs=[
            pl.BlockSpec(memory_space=pl.ANY),
            pl.BlockSpec((n_tok, D_MODEL), lambda j, *_: (j, 0)),
            meta_spec,
            mod_spec,
            pl.BlockSpec((1, D_MODEL), lambda j, *_: (0, 0)),
            pl.BlockSpec((1, D_MODEL), lambda j, *_: (0, 0)),
        ],
        out_specs=pl.BlockSpec((n_tok, D_MODEL), lambda j, *_: (j, 0)),
        scratch_shapes=[pltpu.VMEM((2 * n_tok * TOP_K * ROW_SLABS, LANES), F32),
                        pltpu.SemaphoreType.DMA((2,))],
    )
    return pl.pallas_call(
        functools.partial(_combine_kernel, n_tok, tile0, n_tiles, mod_rows, alpha_res),
        grid_spec=grid_spec,
        out_shape=jax.ShapeDtypeStruct(x1.shape, F32),
        compiler_params=pltpu.CompilerParams(
            dimension_semantics=("arbitrary",), vmem_limit_bytes=VMEM_LIMIT),
        name="combine",
    )(sorted_row, off, cnt, ys, x1, meta, mod, ln2_g, ln2_b)


def _routing_tables(cnt_all):
    total = jnp.sum(cnt_all, axis=0)
    n_tile_e = (total + ROW_TILE - 1) // ROW_TILE
    tile_end = jnp.cumsum(n_tile_e)
    tile_start = tile_end - n_tile_e
    row_start = tile_start * ROW_TILE
    cum = jnp.cumsum(cnt_all, axis=0) - cnt_all
    off = jnp.cumsum(cnt_all, axis=1) - cnt_all
    sorted_row = row_start[None, :] + cum
    n_used = tile_end[-1]
    pad_row = row_start + total
    pad_n = n_tile_e * ROW_TILE - total
    i32 = lambda a: a.astype(jnp.int32)
    return ((i32(sorted_row).reshape(-1), i32(off).reshape(-1), i32(cnt_all).reshape(-1)),
            i32(pad_row), i32(pad_n), i32(tile_start), i32(n_tile_e), i32(n_used).reshape(1))


def kernel(x_prompt, x_sample, state_conv, c_prompt, c_sample, ada_w, ada_b, w_in, sgu_ln_g, sgu_ln_b,
           sgu_w, sgu_b, conv_w, conv_b, w_out, ln1_g, ln1_b, router_w, router_b, w_gu, b_gu,
           w_down, b_down, ln2_g, ln2_b):
    depth = ada_w.shape[0]
    assert depth == 1
    bsz, seq, _ = x_prompt.shape
    n_dec = x_sample.shape[0]
    assert x_sample.shape[1] == 1 and seq % TOK_TILE == 0
    alpha_res = (2.0 * depth) ** 0.25
    l = 0

    mod_s, mod_p = _ada(jnp.concatenate([c_sample, c_prompt], axis=0), n_dec, ada_w[l], ada_b[l])
    mod_p = jnp.pad(mod_p.reshape(bsz, N_MOD, D_MODEL), ((0, 0), (0, SUBLANES - N_MOD), (0, 0)))

    w_in_bf = w_in[l].astype(BF16)
    w_out_bf = w_out[l].astype(BF16)
    router_wt = jnp.transpose(router_w[l])
    router_wt_hi = router_wt.astype(BF16)
    router_wt_lo = (router_wt - router_wt_hi.astype(F32)).astype(BF16)
    router_wt2 = jnp.concatenate([router_wt_hi, router_wt_lo], axis=0)
    router_b_col = router_b[l].reshape(N_EXPERTS, 1)
    row = lambda a: a.reshape(1, -1)

    x1_p, h2_p, meta_p, cnt_p, convst_p = _mix_prompt(
        x_prompt, mod_p, w_in_bf, sgu_ln_g[l], sgu_ln_b[l], sgu_w[l], jnp.transpose(sgu_b[l]),
        conv_w[l], row(conv_b[l]), w_out_bf, row(ln1_g[l]), row(ln1_b[l]), router_wt2,
        router_b_col, alpha_res)
    x1_s, h2_s, meta_s, cnt_s, q_s, vn_s = _mix_sample(
        x_sample.reshape(n_dec, D_MODEL), mod_s, state_conv[l, :, 0, :], state_conv[l, :, 1, :],
        w_in_bf, sgu_ln_g[l], sgu_ln_b[l], row(jnp.repeat(sgu_w[l, :, 0, 0], A_HEAD_DIM)),
        row(jnp.repeat(sgu_b[l, :, 0], A_HEAD_DIM)), conv_w[l], row(conv_b[l]), w_out_bf,
        row(ln1_g[l]), row(ln1_b[l]), router_wt2, router_b_col, alpha_res)

    n_ptiles = bsz * seq // TOK_TILE
    cnt_all = jnp.concatenate([cnt_p[:, :, 0], cnt_s[None, :, 0]], axis=0)
    n_assign = (bsz * seq + n_dec) * TOP_K
    n_row_tiles = -(-n_assign // ROW_TILE) + N_EXPERTS
    tables, pad_row, pad_n, tile_start, n_tile_e, n_used = _routing_tables(cnt_all)

    n_sorted_rows = n_row_tiles * ROW_TILE
    xs = _sort(tables, pad_row, pad_n, h2_p, meta_p, None, n_sorted_rows, TOK_TILE, 0)
    xs = _sort(tables, pad_row, pad_n, h2_s, meta_s, xs, n_sorted_rows, n_dec, n_ptiles)
    ys = _experts(tile_start, n_tile_e, n_used, xs, w_gu[l], b_gu[l], w_down[l], b_down[l])
    y_p = _combine(tables, ys, x1_p, meta_p, mod_p, row(ln2_g[l]), row(ln2_b[l]), TOK_TILE, 0, alpha_res)
    y_s = _combine(tables, ys, x1_s, meta_s, mod_s, row(ln2_g[l]), row(ln2_b[l]), n_dec, n_ptiles,
                   alpha_res)

    conv_state_sample = jnp.stack([state_conv[l, :, 1, :], q_s], axis=1)[None]
    return (y_p.reshape(bsz, seq, D_MODEL),
            y_s.reshape(n_dec, 1, D_MODEL),
            convst_p[None],
            conv_state_sample,
            vn_s.reshape(1, n_dec, 1, A_HEADS, A_HEAD_DIM))
```

```python
import functools

import jax
import jax.numpy as jnp
from jax import lax
from jax.experimental import pallas as pl
from jax.experimental.pallas import tpu as pltpu

F32 = jnp.float32
BF16 = jnp.bfloat16

D_MODEL = 1024
A_WIDTH = 512
B_WIDTH = 512
A_HEADS = 4
A_HEAD_DIM = 128
CHUNK = 128
PROJ_COLS = 2 * A_WIDTH + 3 * B_WIDTH
N_EXPERTS = 32
TOP_K = 4
D_EXPERT = 1024
SWIGLU_LIMIT = 7.0
SWIGLU_ALPHA = 1.702
LN_EPS = 1e-5
N_MOD = 6

LANES = 128
SUBLANES = 8
ROW_SLABS = D_MODEL // LANES
TOK_TILE = 256
ROW_TILE = 256
TILE_SLOTS = 4
TILE_AHEAD = 2
VMEM_LIMIT = 56 * 1024 * 1024


def _layer_norm(x, g, b):
    mu = jnp.mean(x, axis=-1, keepdims=True)
    xc = x - mu
    var = jnp.mean(xc * xc, axis=-1, keepdims=True)
    return xc * lax.rsqrt(var + LN_EPS) * g + b


def _dot(a, b):
    return jnp.dot(a, b, preferred_element_type=F32)


def _dot_nt(a, b):
    return lax.dot_general(a, b, (((1,), (1,)), ((), ())), preferred_element_type=F32)


def _store_rows(ref, slot, rows):
    n = rows.shape[0]
    ref[slot] = rows.astype(BF16).reshape(n, ROW_SLABS, LANES)


def _load_rows(ref, slot):
    return ref[slot].reshape(ref.shape[1], D_MODEL)


def _split_bf16(a):
    hi = a.astype(BF16)
    lo = (a - hi.astype(F32)).astype(BF16)
    return hi, lo


def _ada_kernel(n_first, c_ref, w_ref, b_ref, o_first_ref, o_rest_ref):
    c = c_ref[...]
    s_hi, s_lo = _split_bf16(c * jax.nn.sigmoid(c))
    w_hi, w_lo = _split_bf16(w_ref[...])
    m = _dot(s_hi, w_hi) + _dot(s_hi, w_lo) + _dot(s_lo, w_hi) + b_ref[...]
    o_first_ref[...] = m[:n_first]
    o_rest_ref[...] = m[n_first:]


def _ada(c_all, n_first, ada_w, ada_b):
    rows = c_all.shape[0]
    return pl.pallas_call(
        functools.partial(_ada_kernel, n_first),
        grid=(N_MOD,),
        in_specs=[
            pl.BlockSpec((rows, D_MODEL), lambda n: (0, 0)),
            pl.BlockSpec((D_MODEL, D_MODEL), lambda n: (0, n)),
            pl.BlockSpec((1, D_MODEL), lambda n: (0, n)),
        ],
        out_specs=[pl.BlockSpec((n_first, D_MODEL), lambda n: (0, n)),
                   pl.BlockSpec((rows - n_first, D_MODEL), lambda n: (0, n))],
        out_shape=[jax.ShapeDtypeStruct((n_first, N_MOD * D_MODEL), F32),
                   jax.ShapeDtypeStruct((rows - n_first, N_MOD * D_MODEL), F32)],
        name="ada",
    )(c_all, ada_w, ada_b.reshape(1, -1))


def _route(h2, router_wt2, router_b):
    n = h2.shape[0]
    h_hi = h2.astype(BF16)
    h_lo = (h2 - h_hi.astype(F32)).astype(BF16)
    l1 = _dot_nt(router_wt2, h_hi)
    l2 = _dot_nt(router_wt2[:N_EXPERTS], h_lo)
    logits = l1[:N_EXPERTS] + l1[N_EXPERTS:] + l2 + router_b

    sub = lax.broadcasted_iota(jnp.int32, (N_EXPERTS, n), 0)
    work = logits
    vals, hots = [], []
    for _ in range(TOP_K):
        m = jnp.max(work, axis=0, keepdims=True)
        idx = jnp.min(jnp.where(work == m, sub, N_EXPERTS), axis=0, keepdims=True)
        hot = sub == idx
        work = jnp.where(hot, -jnp.inf, work)
        vals.append(m)
        hots.append(hot)
    exps = [jnp.exp(v - vals[0]) for v in vals]
    denom = exps[0] + exps[1] + exps[2] + exps[3]
    gates = [e / denom for e in exps]

    onehot = jnp.zeros((N_EXPERTS, n), F32)
    for hot in hots:
        onehot = onehot + hot.astype(F32)
    onehot_bf = onehot.astype(BF16)
    t_r = lax.broadcasted_iota(jnp.int32, (n, n), 0)
    t_c = lax.broadcasted_iota(jnp.int32, (n, n), 1)
    rank = _dot(onehot_bf, (t_r < t_c).astype(BF16))
    cnt = jnp.sum(onehot, axis=1, keepdims=True)
    e_r = lax.broadcasted_iota(jnp.int32, (N_EXPERTS, N_EXPERTS), 0)
    e_c = lax.broadcasted_iota(jnp.int32, (N_EXPERTS, N_EXPERTS), 1)
    below = _dot((e_c < e_r).astype(BF16), onehot_bf)
    off = jnp.sum(below, axis=1, keepdims=True)
    base = rank + off

    row8 = lax.broadcasted_iota(jnp.int32, (SUBLANES, n), 0)
    meta = jnp.zeros((SUBLANES, n), F32)
    for k in range(TOP_K):
        pos_k = jnp.sum(jnp.where(hots[k], base, 0.0), axis=0, keepdims=True)
        meta = jnp.where(row8 == k, pos_k, meta)
        meta = jnp.where(row8 == TOP_K + k, gates[k], meta)
    return meta, jnp.broadcast_to(cnt, (N_EXPERTS, LANES))


def _mix_prompt_kernel(alpha_res, x_ref, mod_ref, w_in_ref, sgu_g_ref, sgu_bln_ref, sgu_w_ref,
                       sgu_bias_ref, conv_w_ref, conv_b_ref, w_out_ref, ln1_g_ref, ln1_b_ref,
                       router_w_ref, router_b_ref,
                       x1_ref, h2_ref, meta_ref, cnt_ref, convst_ref, carry_ref):
    t = pl.program_id(1)

    @pl.when(t == 0)
    def _():
        carry_ref[...] = jnp.zeros_like(carry_ref)

    x = x_ref[...]
    mod = mod_ref[...]
    sh1, sc1, g1, sh2, sc2 = mod[0:1], mod[1:2], mod[2:3], mod[3:4], mod[4:5]
    h = (x * (1.0 + sc1) + sh1).astype(BF16)
    z = _dot(h, w_in_ref[...])
    u = z[:, 0:A_WIDTH]
    v = z[:, A_WIDTH:2 * A_WIDTH]
    gate_b = z[:, 2 * A_WIDTH:2 * A_WIDTH + B_WIDTH]
    gate_c = z[:, 2 * A_WIDTH + B_WIDTH:2 * A_WIDTH + 2 * B_WIDTH]
    hb = z[:, 2 * A_WIDTH + 2 * B_WIDTH:]

    tm = x.shape[0]
    r_i = lax.broadcasted_iota(jnp.int32, (CHUNK, CHUNK), 0)
    c_i = lax.broadcasted_iota(jnp.int32, (CHUNK, CHUNK), 1)
    tril = c_i <= r_i
    a_parts = []
    for hd in range(A_HEADS):
        sl = slice(hd * A_HEAD_DIM, (hd + 1) * A_HEAD_DIM)
        vn = _layer_norm(v[:, sl], sgu_g_ref[hd:hd + 1, :], sgu_bln_ref[hd:hd + 1, :]).astype(BF16)
        wm = jnp.where(tril, sgu_w_ref[hd], 0.0).astype(BF16)
        bias = sgu_bias_ref[:, hd:hd + 1]
        s_parts = []
        for c in range(tm // CHUNK):
            s_parts.append(_dot(wm, vn[c * CHUNK:(c + 1) * CHUNK, :]) + bias)
        a_parts.append(u[:, sl] * jnp.concatenate(s_parts, axis=0))

    q = gate_c * hb
    row = lax.broadcasted_iota(jnp.int32, q.shape, 0)
    prev2 = carry_ref[0:1, :]
    prev1 = carry_ref[1:2, :]
    q_m1 = jnp.where(row == 0, prev1, pltpu.roll(q, 1, 0))
    q_m2 = jnp.where(row == 0, prev2, jnp.where(row == 1, prev1, pltpu.roll(q, 2, 0)))
    conv = conv_b_ref[...] + q_m2 * conv_w_ref[0:1, :] + q_m1 * conv_w_ref[1:2, :] + q * conv_w_ref[2:3, :]
    b_out = gate_b * conv
    carry_ref[0:2, :] = q[tm - 2:tm, :]
    convst_ref[...] = q[tm - 2:tm, :]

    mix_in = jnp.concatenate(a_parts + [b_out], axis=-1).astype(BF16)
    mix = _dot(mix_in, w_out_ref[...])
    x1 = _layer_norm(alpha_res * x + g1 * mix, ln1_g_ref[...], ln1_b_ref[...])
    x1_ref[...] = x1
    h2 = x1 * (1.0 + sc2) + sh2
    h2_ref[...] = h2.astype(BF16)
    meta, cnt = _route(h2, router_w_ref[...], router_b_ref[...])
    meta_ref[...] = meta
    cnt_ref[...] = cnt.astype(jnp.int32)


def _mix_prompt(x, mod, w_in_bf, sgu_g, sgu_bln, sgu_w, sgu_bias_t, conv_w, conv_b, w_out_bf,
                ln1_g, ln1_b, router_wt2, router_b_col, alpha_res):
    bsz, seq, _ = x.shape
    tiles = seq // TOK_TILE
    n_tok = bsz * seq
    const2 = lambda b, t: (0, 0)
    tok = lambda b, t: (b * tiles + t, 0)
    tile3 = lambda b, t: (b * tiles + t, 0, 0)
    return pl.pallas_call(
        functools.partial(_mix_prompt_kernel, alpha_res),
        grid=(bsz, tiles),
        in_specs=[
            pl.BlockSpec((None, TOK_TILE, D_MODEL), lambda b, t: (b, t, 0)),
            pl.BlockSpec((None, SUBLANES, D_MODEL), lambda b, t: (b, 0, 0)),
            pl.BlockSpec((D_MODEL, PROJ_COLS), const2),
            pl.BlockSpec((A_HEADS, A_HEAD_DIM), const2),
            pl.BlockSpec((A_HEADS, A_HEAD_DIM), const2),
            pl.BlockSpec((A_HEADS, CHUNK, CHUNK), lambda b, t: (0, 0, 0)),
            pl.BlockSpec((CHUNK, A_HEADS), const2),
            pl.BlockSpec((3, B_WIDTH), const2),
            pl.BlockSpec((1, B_WIDTH), const2),
            pl.BlockSpec((D_MODEL, D_MODEL), const2),
            pl.BlockSpec((1, D_MODEL), const2),
            pl.BlockSpec((1, D_MODEL), const2),
            pl.BlockSpec((2 * N_EXPERTS, D_MODEL), const2),
            pl.BlockSpec((N_EXPERTS, 1), const2),
        ],
        out_specs=[
            pl.BlockSpec((TOK_TILE, D_MODEL), tok),
            pl.BlockSpec((TOK_TILE, D_MODEL), tok),
            pl.BlockSpec((None, SUBLANES, TOK_TILE), tile3),
            pl.BlockSpec((None, N_EXPERTS, LANES), tile3),
            pl.BlockSpec((None, 2, B_WIDTH), lambda b, t: (b, 0, 0)),
        ],
        out_shape=[
            jax.ShapeDtypeStruct((n_tok, D_MODEL), F32),
            jax.ShapeDtypeStruct((n_tok, D_MODEL), BF16),
            jax.ShapeDtypeStruct((bsz * tiles, SUBLANES, TOK_TILE), F32),
            jax.ShapeDtypeStruct((bsz * tiles, N_EXPERTS, LANES), jnp.int32),
            jax.ShapeDtypeStruct((bsz, 2, B_WIDTH), F32),
        ],
        scratch_shapes=[pltpu.VMEM((SUBLANES, B_WIDTH), F32)],
        compiler_params=pltpu.CompilerParams(
            dimension_semantics=("arbitrary", "arbitrary"), vmem_limit_bytes=VMEM_LIMIT),
        name="mix_prompt",
    )(x, mod, w_in_bf, sgu_g, sgu_bln, sgu_w, sgu_bias_t, conv_w, conv_b, w_out_bf,
      ln1_g, ln1_b, router_wt2, router_b_col)


def _mix_sample_kernel(alpha_res, x_ref, mod_ref, prev0_ref, prev1_ref, w_in_ref, sgu_g_ref,
                       sgu_bln_ref, sgu_w00_ref, sgu_b0_ref, conv_w_ref, conv_b_ref, w_out_ref,
                       ln1_g_ref, ln1_b_ref, router_w_ref, router_b_ref,
                       x1_ref, h2_ref, meta_ref, cnt_ref, q_ref, vn_ref):
    x = x_ref[...]
    sh1 = mod_ref[:, 0:D_MODEL]
    sc1 = mod_ref[:, D_MODEL:2 * D_MODEL]
    g1 = mod_ref[:, 2 * D_MODEL:3 * D_MODEL]
    sh2 = mod_ref[:, 3 * D_MODEL:4 * D_MODEL]
    sc2 = mod_ref[:, 4 * D_MODEL:5 * D_MODEL]
    h = (x * (1.0 + sc1) + sh1).astype(BF16)
    z = _dot(h, w_in_ref[...])
    u = z[:, 0:A_WIDTH]
    v = z[:, A_WIDTH:2 * A_WIDTH]
    gate_b = z[:, 2 * A_WIDTH:2 * A_WIDTH + B_WIDTH]
    gate_c = z[:, 2 * A_WIDTH + B_WIDTH:2 * A_WIDTH + 2 * B_WIDTH]
    hb = z[:, 2 * A_WIDTH + 2 * B_WIDTH:]

    vn_parts = []
    for hd in range(A_HEADS):
        sl = slice(hd * A_HEAD_DIM, (hd + 1) * A_HEAD_DIM)
        vn_parts.append(_layer_norm(v[:, sl], sgu_g_ref[hd:hd + 1, :], sgu_bln_ref[hd:hd + 1, :]))
    vn = jnp.concatenate(vn_parts, axis=-1)
    vn_ref[...] = vn
    a_out = u * (vn * sgu_w00_ref[...] + sgu_b0_ref[...])

    q = gate_c * hb
    q_ref[...] = q
    conv = (conv_b_ref[...] + prev0_ref[...] * conv_w_ref[0:1, :] + prev1_ref[...] * conv_w_ref[1:2, :]
            + q * conv_w_ref[2:3, :])
    b_out = gate_b * conv

    mix_in = jnp.concatenate([a_out, b_out], axis=-1).astype(BF16)
    mix = _dot(mix_in, w_out_ref[...])
    x1 = _layer_norm(alpha_res * x + g1 * mix, ln1_g_ref[...], ln1_b_ref[...])
    x1_ref[...] = x1
    h2 = x1 * (1.0 + sc2) + sh2
    h2_ref[...] = h2.astype(BF16)
    meta, cnt = _route(h2, router_w_ref[...], router_b_ref[...])
    meta_ref[...] = meta
    cnt_ref[...] = cnt.astype(jnp.int32)


def _mix_sample(x, mod, prev0, prev1, w_in_bf, sgu_g, sgu_bln, sgu_w00, sgu_b0, conv_w, conv_b,
                w_out_bf, ln1_g, ln1_b, router_wt2, router_b_col, alpha_res):
    n = x.shape[0]
    return pl.pallas_call(
        functools.partial(_mix_sample_kernel, alpha_res),
        out_shape=[
            jax.ShapeDtypeStruct((n, D_MODEL), F32),
            jax.ShapeDtypeStruct((n, D_MODEL), BF16),
            jax.ShapeDtypeStruct((SUBLANES, n), F32),
            jax.ShapeDtypeStruct((N_EXPERTS, LANES), jnp.int32),
            jax.ShapeDtypeStruct((n, B_WIDTH), F32),
            jax.ShapeDtypeStruct((n, A_WIDTH), F32),
        ],
        compiler_params=pltpu.CompilerParams(vmem_limit_bytes=VMEM_LIMIT),
        name="mix_sample",
    )(x, mod, prev0, prev1, w_in_bf, sgu_g, sgu_bln, sgu_w00, sgu_b0, conv_w, conv_b, w_out_bf,
      ln1_g, ln1_b, router_wt2, router_b_col)


def _strip_copy(src_ref, src_row, dst_ref, dst_row, n_rows, sem):
    @pl.when(n_rows > 0)
    def _():
        pltpu.make_async_copy(src_ref.at[pl.ds(src_row, n_rows)],
                              dst_ref.at[pl.ds(dst_row, n_rows)], sem).start()


def _wait_rows(hbm_ref, vmem_ref, n_rows, sem):
    pltpu.make_async_copy(hbm_ref.at[pl.ds(0, n_rows)], vmem_ref.at[pl.ds(0, n_rows)], sem).wait()


def _sort_kernel(n_tok, tile0, n_tiles, first, *refs):
    if first:
        (sorted_row_ref, off_ref, cnt_ref, pad_row_ref, pad_n_ref, h2_ref, meta_ref,
         xs_ref, stage_ref, zero_ref, sem, pad_sem) = refs
    else:
        (sorted_row_ref, off_ref, cnt_ref, pad_row_ref, pad_n_ref, h2_ref, meta_ref, _,
         xs_ref, stage_ref, sem) = refs
    j = pl.program_id(0)
    n_sorted = n_tok * TOP_K
    slot = j % 2

    if first:
        @pl.when(j == 0)
        def _():
            zero_ref[...] = jnp.zeros_like(zero_ref)

            def start(e, carry):
                _strip_copy(zero_ref, 0, xs_ref, pad_row_ref[e], pad_n_ref[e], pad_sem)
                return carry
            lax.fori_loop(0, N_EXPERTS, start, 0)

            def wait(e, carry):
                @pl.when(pad_n_ref[e] > 0)
                def _():
                    _wait_rows(xs_ref, zero_ref, pad_n_ref[e], pad_sem)
                return carry
            lax.fori_loop(0, N_EXPERTS, wait, 0)

    meta = meta_ref[...]
    sub_r = lax.broadcasted_iota(jnp.int32, (n_sorted, n_tok), 0)
    sel = sub_r == meta[0:1, :].astype(jnp.int32)
    for k in range(1, TOP_K):
        sel = jnp.logical_or(sel, sub_r == meta[k:k + 1, :].astype(jnp.int32))
    perm = jnp.where(sel, 1.0, 0.0).astype(BF16)
    rows = _dot(perm, h2_ref[...])
    _store_rows(stage_ref, slot, rows)

    base = (tile0 + j) * N_EXPERTS

    def start(e, carry):
        _strip_copy(stage_ref.at[slot], off_ref[base + e], xs_ref, sorted_row_ref[base + e],
                    cnt_ref[base + e], sem.at[slot])
        return carry
    lax.fori_loop(0, N_EXPERTS, start, 0)

    @pl.when(j > 0)
    def _():
        _wait_rows(xs_ref, stage_ref.at[0], n_sorted, sem.at[1 - slot])

    @pl.when(j == n_tiles - 1)
    def _():
        _wait_rows(xs_ref, stage_ref.at[0], n_sorted, sem.at[slot])


def _sort(tables, pad_row, pad_n, h2, meta, xs, n_sorted_rows, n_tok, tile0):
    sorted_row, off, cnt = tables
    first = xs is None
    n_tiles = h2.shape[0] // n_tok
    if meta.ndim == 2:
        meta_spec = pl.BlockSpec((SUBLANES, n_tok), lambda j, *_: (0, j))
    else:
        meta_spec = pl.BlockSpec((None, SUBLANES, n_tok), lambda j, *_: (j, 0, 0))
    in_specs = [pl.BlockSpec((n_tok, D_MODEL), lambda j, *_: (j, 0)), meta_spec]
    operands = [sorted_row, off, cnt, pad_row, pad_n, h2, meta]
    scratch = [pltpu.VMEM((2, n_tok * TOP_K, ROW_SLABS, LANES), BF16)]
    if first:
        scratch.append(pltpu.VMEM((ROW_TILE, ROW_SLABS, LANES), BF16))
        aliases = {}
    else:
        in_specs.append(pl.BlockSpec(memory_space=pl.ANY))
        operands.append(xs)
        aliases = {len(operands) - 1: 0}
    scratch.append(pltpu.SemaphoreType.DMA((2,)))
    if first:
        scratch.append(pltpu.SemaphoreType.DMA(()))
    grid_spec = pltpu.PrefetchScalarGridSpec(
        num_scalar_prefetch=5,
        grid=(n_tiles,),
        in_specs=in_specs,
        out_specs=pl.BlockSpec(memory_space=pl.ANY),
        scratch_shapes=scratch,
    )
    return pl.pallas_call(
        functools.partial(_sort_kernel, n_tok, tile0, n_tiles, first),
        grid_spec=grid_spec,
        out_shape=jax.ShapeDtypeStruct((n_sorted_rows, ROW_SLABS, LANES), BF16),
        input_output_aliases=aliases,
        compiler_params=pltpu.CompilerParams(
            dimension_semantics=("arbitrary",), vmem_limit_bytes=VMEM_LIMIT),
        name="sort_first" if first else "sort_more",
    )(*operands)


def _experts_kernel(tile_start_ref, n_tile_ref, n_used_ref, xs_ref, w_gu_ref, b_gu_ref, w_down_ref,
                    b_down_ref, ys_ref, w_gu_bf_ref, w_down_bf_ref, x_buf, y_buf, x_sem, y_sem):
    e = pl.program_id(0)
    n_used = n_used_ref[0]
    first_tile = tile_start_ref[e]
    n_tile = n_tile_ref[e]

    def x_copy(g):
        slot = g % TILE_SLOTS
        return pltpu.make_async_copy(xs_ref.at[pl.ds(g * ROW_TILE, ROW_TILE)], x_buf.at[slot],
                                     x_sem.at[slot])

    def y_copy(g):
        slot = g % TILE_SLOTS
        return pltpu.make_async_copy(y_buf.at[slot], ys_ref.at[pl.ds(g * ROW_TILE, ROW_TILE)],
                                     y_sem.at[slot])

    def request(g):
        @pl.when(g < n_used)
        def _():
            x_copy(g).start(priority=1)

    @pl.when(e == 0)
    def _():
        for g in range(TILE_AHEAD):
            request(g)

    @pl.when(n_tile > 0)
    def _():
        w_gu_bf_ref[...] = w_gu_ref[...].astype(BF16)
        w_down_bf_ref[...] = w_down_ref[...].astype(BF16)

    def begin(g):
        x_copy(g).wait()
        request(g + TILE_AHEAD)

        @pl.when(g >= TILE_SLOTS)
        def _():
            y_copy(g - TILE_SLOTS).wait()

    def compute(g):
        slot = g % TILE_SLOTS
        x = _load_rows(x_buf, slot)
        gu = _dot(x, w_gu_bf_ref[...]) + b_gu_ref[...]
        gate = jnp.minimum(gu[:, :D_EXPERT], SWIGLU_LIMIT)
        up = jnp.clip(gu[:, D_EXPERT:], -SWIGLU_LIMIT, SWIGLU_LIMIT)
        act = (up + 1.0) * gate * jax.nn.sigmoid(SWIGLU_ALPHA * gate)
        y = _dot(act.astype(BF16), w_down_bf_ref[...]) + b_down_ref[...]
        _store_rows(y_buf, slot, y)
        y_copy(g).start(priority=1)

    def pair_body(p, carry):
        g = first_tile + 2 * p
        begin(g)
        begin(g + 1)
        compute(g)
        compute(g + 1)
        return carry

    lax.fori_loop(0, n_tile // 2, pair_body, 0)

    @pl.when(n_tile % 2 == 1)
    def _():
        g = first_tile + n_tile - 1
        begin(g)
        compute(g)

    @pl.when(e == N_EXPERTS - 1)
    def _():
        for back in range(TILE_SLOTS, 0, -1):
            @pl.when(n_used >= back)
            def _(back=back):
                y_copy(n_used - back).wait()


def _experts(tile_start, n_tile_e, n_used, xs, w_gu, b_gu, w_down, b_down):
    w_blk = lambda e, *_: (e, 0, 0)
    grid_spec = pltpu.PrefetchScalarGridSpec(
        num_scalar_prefetch=3,
        grid=(N_EXPERTS,),
        in_specs=[
            pl.BlockSpec(memory_space=pl.ANY),
            pl.BlockSpec((None, D_MODEL, 2 * D_EXPERT), w_blk),
            pl.BlockSpec((None, 1, 2 * D_EXPERT), w_blk),
            pl.BlockSpec((None, D_EXPERT, D_MODEL), w_blk),
            pl.BlockSpec((None, 1, D_MODEL), w_blk),
        ],
        out_specs=pl.BlockSpec(memory_space=pl.ANY),
        scratch_shapes=[pltpu.VMEM((D_MODEL, 2 * D_EXPERT), BF16),
                        pltpu.VMEM((D_EXPERT, D_MODEL), BF16),
                        pltpu.VMEM((TILE_SLOTS, ROW_TILE, ROW_SLABS, LANES), BF16),
                        pltpu.VMEM((TILE_SLOTS, ROW_TILE, ROW_SLABS, LANES), BF16),
                        pltpu.SemaphoreType.DMA((TILE_SLOTS,)),
                        pltpu.SemaphoreType.DMA((TILE_SLOTS,))],
    )
    return pl.pallas_call(
        _experts_kernel,
        grid_spec=grid_spec,
        out_shape=jax.ShapeDtypeStruct(xs.shape, BF16),
        compiler_params=pltpu.CompilerParams(
            dimension_semantics=("arbitrary",), vmem_limit_bytes=VMEM_LIMIT),
        name="experts",
    )(tile_start, n_tile_e, n_used, xs, w_gu, b_gu.reshape(N_EXPERTS, 1, -1), w_down,
      b_down.reshape(N_EXPERTS, 1, -1))


def _combine_kernel(n_tok, tile0, n_tiles, mod_rows, alpha_res, sorted_row_ref, off_ref, cnt_ref,
                    ys_ref, x1_ref, meta_ref, mod_ref, ln2_g_ref, ln2_b_ref, out_ref,
                    stage_ref, sem):
    j = pl.program_id(0)
    n_sorted = n_tok * TOP_K
    slot = j % 2

    def start_tile(tile, to_slot):
        base = (tile0 + tile) * N_EXPERTS

        def start(e, carry):
            _strip_copy(ys_ref, sorted_row_ref[base + e], stage_ref.at[to_slot],
                        off_ref[base + e], cnt_ref[base + e], sem.at[to_slot])
            return carry
        lax.fori_loop(0, N_EXPERTS, start, 0)

    @pl.when(j == 0)
    def _():
        start_tile(j, slot)

    @pl.when(j + 1 < n_tiles)
    def _():
        start_tile(j + 1, 1 - slot)

    meta = meta_ref[...]
    sub_r = lax.broadcasted_iota(jnp.int32, (n_sorted, n_tok), 0)
    comb_t = jnp.zeros((n_sorted, n_tok), F32)
    for k in range(TOP_K):
        comb_t = jnp.where(sub_r == meta[k:k + 1, :].astype(jnp.int32),
                           meta[TOP_K + k:TOP_K + k + 1, :], comb_t)
    comb_t = comb_t.astype(BF16)

    _wait_rows(ys_ref, stage_ref.at[0], n_sorted, sem.at[slot])
    ys = _load_rows(stage_ref, slot)
    ffn = lax.dot_general(comb_t, ys, (((0,), (0,)), ((), ())), preferred_element_type=F32)
    if mod_rows:
        g2 = mod_ref[:, 5 * D_MODEL:6 * D_MODEL]
    else:
        g2 = mod_ref[5:6, :]
    out_ref[...] = _layer_norm(alpha_res * x1_ref[...] + g2 * ffn, ln2_g_ref[...], ln2_b_ref[...])


def _combine(tables, ys, x1, meta, mod, ln2_g, ln2_b, n_tok, tile0, alpha_res):
    sorted_row, off, cnt = tables
    n_tiles = x1.shape[0] // n_tok
    mod_rows = mod.ndim == 2
    if mod_rows:
        mod_spec = pl.BlockSpec((n_tok, N_MOD * D_MODEL), lambda j, *_: (j, 0))
    else:
        tiles_per_seq = n_tiles // mod.shape[0]
        mod_spec = pl.BlockSpec((None, SUBLANES, D_MODEL), lambda j, *_: (j // tiles_per_seq, 0, 0))
    if meta.ndim == 2:
        meta_spec = pl.BlockSpec((SUBLANES, n_tok), lambda j, *_: (0, j))
    else:
        meta_spec = pl.BlockSpec((None, SUBLANES, n_tok), lambda j, *_: (j, 0, 0))
    grid_spec = pltpu.PrefetchScalarGridSpec(
        num_scalar_prefetch=3,
        grid=(n_tiles,),
        in_specs=[
            pl.BlockSpec(memory_space=pl.ANY),
            pl.BlockSpec((n_tok, D_MODEL), lambda j, *_: (j, 0)),
            meta_spec,
            mod_spec,
            pl.BlockSpec((1, D_MODEL), lambda j, *_: (0, 0)),
            pl.BlockSpec((1, D_MODEL), lambda j, *_: (0, 0)),
        ],
        out_specs=pl.BlockSpec((n_tok, D_MODEL), lambda j, *_: (j, 0)),
        scratch_shapes=[pltpu.VMEM((2, n_tok * TOP_K, ROW_SLABS, LANES), BF16),
                        pltpu.SemaphoreType.DMA((2,))],
    )
    return pl.pallas_call(
        functools.partial(_combine_kernel, n_tok, tile0, n_tiles, mod_rows, alpha_res),
        grid_spec=grid_spec,
        out_shape=jax.ShapeDtypeStruct(x1.shape, F32),
        compiler_params=pltpu.CompilerParams(
            dimension_semantics=("arbitrary",), vmem_limit_bytes=VMEM_LIMIT),
        name="combine",
    )(sorted_row, off, cnt, ys, x1, meta, mod, ln2_g, ln2_b)


def _routing_tables(cnt_all):
    total = jnp.sum(cnt_all, axis=0)
    n_tile_e = (total + ROW_TILE - 1) // ROW_TILE
    tile_end = jnp.cumsum(n_tile_e)
    tile_start = tile_end - n_tile_e
    row_start = tile_start * ROW_TILE
    cum = jnp.cumsum(cnt_all, axis=0) - cnt_all
    off = jnp.cumsum(cnt_all, axis=1) - cnt_all
    sorted_row = row_start[None, :] + cum
    n_used = tile_end[-1]
    pad_row = row_start + total
    pad_n = n_tile_e * ROW_TILE - total
    i32 = lambda a: a.astype(jnp.int32)
    return ((i32(sorted_row).reshape(-1), i32(off).reshape(-1), i32(cnt_all).reshape(-1)),
            i32(pad_row), i32(pad_n), i32(tile_start), i32(n_tile_e), i32(n_used).reshape(1))


def kernel(x_prompt, x_sample, state_conv, c_prompt, c_sample, ada_w, ada_b, w_in, sgu_ln_g, sgu_ln_b,
           sgu_w, sgu_b, conv_w, conv_b, w_out, ln1_g, ln1_b, router_w, router_b, w_gu, b_gu,
           w_down, b_down, ln2_g, ln2_b):
    depth = ada_w.shape[0]
    assert depth == 1
    bsz, seq, _ = x_prompt.shape
    n_dec = x_sample.shape[0]
    assert x_sample.shape[1] == 1 and seq % TOK_TILE == 0
    alpha_res = (2.0 * depth) ** 0.25
    l = 0

    mod_s, mod_p = _ada(jnp.concatenate([c_sample, c_prompt], axis=0), n_dec, ada_w[l], ada_b[l])
    mod_p = jnp.pad(mod_p.reshape(bsz, N_MOD, D_MODEL), ((0, 0), (0, SUBLANES - N_MOD), (0, 0)))

    w_in_bf = w_in[l].astype(BF16)
    w_out_bf = w_out[l].astype(BF16)
    router_wt = jnp.transpose(router_w[l])
    router_wt_hi = router_wt.astype(BF16)
    router_wt_lo = (router_wt - router_wt_hi.astype(F32)).astype(BF16)
    router_wt2 = jnp.concatenate([router_wt_hi, router_wt_lo], axis=0)
    router_b_col = router_b[l].reshape(N_EXPERTS, 1)
    row = lambda a: a.reshape(1, -1)

    x1_p, h2_p, meta_p, cnt_p, convst_p = _mix_prompt(
        x_prompt, mod_p, w_in_bf, sgu_ln_g[l], sgu_ln_b[l], sgu_w[l], jnp.transpose(sgu_b[l]),
        conv_w[l], row(conv_b[l]), w_out_bf, row(ln1_g[l]), row(ln1_b[l]), router_wt2,
        router_b_col, alpha_res)
    x1_s, h2_s, meta_s, cnt_s, q_s, vn_s = _mix_sample(
        x_sample.reshape(n_dec, D_MODEL), mod_s, state_conv[l, :, 0, :], state_conv[l, :, 1, :],
        w_in_bf, sgu_ln_g[l], sgu_ln_b[l], row(jnp.repeat(sgu_w[l, :, 0, 0], A_HEAD_DIM)),
        row(jnp.repeat(sgu_b[l, :, 0], A_HEAD_DIM)), conv_w[l], row(conv_b[l]), w_out_bf,
        row(ln1_g[l]), row(ln1_b[l]), router_wt2, router_b_col, alpha_res)

    n_ptiles = bsz * seq // TOK_TILE
    cnt_all = jnp.concatenate([cnt_p[:, :, 0], cnt_s[None, :, 0]], axis=0)
    n_assign = (bsz * seq + n_dec) * TOP_K
    n_row_tiles = -(-n_assign // ROW_TILE) + N_EXPERTS
    tables, pad_row, pad_n, tile_start, n_tile_e, n_used = _routing_tables(cnt_all)

    n_sorted_rows = n_row_tiles * ROW_TILE
    xs = _sort(tables, pad_row, pad_n, h2_p, meta_p, None, n_sorted_rows, TOK_TILE, 0)
    xs = _sort(tables, pad_row, pad_n, h2_s, meta_s, xs, n_sorted_rows, n_dec, n_ptiles)
    ys = _experts(tile_start, n_tile_e, n_used, xs, w_gu[l], b_gu[l], w_down[l], b_down[l])
    y_p = _combine(tables, ys, x1_p, meta_p, mod_p, row(ln2_g[l]), row(ln2_b[l]), TOK_TILE, 0, alpha_res)
    y_s = _combine(tables, ys, x1_s, meta_s, mod_s, row(ln2_g[l]), row(ln2_b[l]), n_dec, n_ptiles,
                   alpha_res)

    conv_state_sample = jnp.stack([state_conv[l, :, 1, :], q_s], axis=1)[None]
    return (y_p.reshape(bsz, seq, D_MODEL),
            y_s.reshape(n_dec, 1, D_MODEL),
            convst_p[None],
            conv_state_sample,
            vn_s.reshape(1, n_dec, 1, A_HEADS, A_HEAD_DIM))
```

```python
import functools

import jax
import jax.numpy as jnp
from jax import lax
from jax.experimental import pallas as pl
from jax.experimental.pallas import tpu as pltpu

F32 = jnp.float32
BF16 = jnp.bfloat16

D_MODEL = 1024
A_WIDTH = 512
B_WIDTH = 512
A_HEADS = 4
A_HEAD_DIM = 128
CHUNK = 128
PROJ_COLS = 2 * A_WIDTH + 3 * B_WIDTH
N_EXPERTS = 32
TOP_K = 4
D_EXPERT = 1024
SWIGLU_LIMIT = 7.0
SWIGLU_ALPHA = 1.702
LN_EPS = 1e-5
N_MOD = 6

LANES = 128
SUBLANES = 8
ROW_SLABS = D_MODEL // LANES
TOK_TILE = 256
MIX_SUB = 2
ROW_TILE = 256
TILE_SLOTS = 4
TILE_AHEAD = 2
VMEM_LIMIT = 56 * 1024 * 1024


def _layer_norm(x, g, b):
    mu = jnp.mean(x, axis=-1, keepdims=True)
    xc = x - mu
    var = jnp.mean(xc * xc, axis=-1, keepdims=True)
    return xc * lax.rsqrt(var + LN_EPS) * g + b


def _dot(a, b):
    return jnp.dot(a, b, preferred_element_type=F32)


def _dot_nt(a, b):
    return lax.dot_general(a, b, (((1,), (1,)), ((), ())), preferred_element_type=F32)


def _store_rows(ref, slot, rows):
    n = rows.shape[0]
    ref[slot] = rows.astype(BF16).reshape(n, ROW_SLABS, LANES)


def _load_rows(ref, slot):
    return ref[slot].reshape(ref.shape[1], D_MODEL)


def _split_bf16(a):
    hi = a.astype(BF16)
    lo = (a - hi.astype(F32)).astype(BF16)
    return hi, lo


def _ada_kernel(n_first, c_ref, w_ref, b_ref, o_first_ref, o_rest_ref):
    c = c_ref[...]
    s_hi, s_lo = _split_bf16(c * jax.nn.sigmoid(c))
    w_hi, w_lo = _split_bf16(w_ref[...])
    m = _dot(s_hi, w_hi) + _dot(s_hi, w_lo) + _dot(s_lo, w_hi) + b_ref[...]
    o_first_ref[...] = m[:n_first]
    o_rest_ref[...] = m[n_first:]


def _ada(c_all, n_first, ada_w, ada_b):
    rows = c_all.shape[0]
    return pl.pallas_call(
        functools.partial(_ada_kernel, n_first),
        grid=(N_MOD,),
        in_specs=[
            pl.BlockSpec((rows, D_MODEL), lambda n: (0, 0)),
            pl.BlockSpec((D_MODEL, D_MODEL), lambda n: (0, n)),
            pl.BlockSpec((1, D_MODEL), lambda n: (0, n)),
        ],
        out_specs=[pl.BlockSpec((n_first, D_MODEL), lambda n: (0, n)),
                   pl.BlockSpec((rows - n_first, D_MODEL), lambda n: (0, n))],
        out_shape=[jax.ShapeDtypeStruct((n_first, N_MOD * D_MODEL), F32),
                   jax.ShapeDtypeStruct((rows - n_first, N_MOD * D_MODEL), F32)],
        name="ada",
    )(c_all, ada_w, ada_b.reshape(1, -1))


def _route(h2, router_wt2, router_b):
    n = h2.shape[0]
    h_hi = h2.astype(BF16)
    h_lo = (h2 - h_hi.astype(F32)).astype(BF16)
    l1 = _dot_nt(router_wt2, h_hi)
    l2 = _dot_nt(router_wt2[:N_EXPERTS], h_lo)
    logits = l1[:N_EXPERTS] + l1[N_EXPERTS:] + l2 + router_b

    sub = lax.broadcasted_iota(jnp.int32, (N_EXPERTS, n), 0)
    work = logits
    vals, hots = [], []
    for _ in range(TOP_K):
        m = jnp.max(work, axis=0, keepdims=True)
        idx = jnp.min(jnp.where(work == m, sub, N_EXPERTS), axis=0, keepdims=True)
        hot = sub == idx
        work = jnp.where(hot, -jnp.inf, work)
        vals.append(m)
        hots.append(hot)
    exps = [jnp.exp(v - vals[0]) for v in vals]
    denom = exps[0] + exps[1] + exps[2] + exps[3]
    gates = [e / denom for e in exps]

    onehot = jnp.zeros((N_EXPERTS, n), F32)
    for hot in hots:
        onehot = onehot + hot.astype(F32)
    onehot_bf = onehot.astype(BF16)
    t_r = lax.broadcasted_iota(jnp.int32, (n, n), 0)
    t_c = lax.broadcasted_iota(jnp.int32, (n, n), 1)
    rank = _dot(onehot_bf, (t_r < t_c).astype(BF16))
    cnt = jnp.sum(onehot, axis=1, keepdims=True)
    e_r = lax.broadcasted_iota(jnp.int32, (N_EXPERTS, N_EXPERTS), 0)
    e_c = lax.broadcasted_iota(jnp.int32, (N_EXPERTS, N_EXPERTS), 1)
    below = _dot((e_c < e_r).astype(BF16), onehot_bf)
    off = jnp.sum(below, axis=1, keepdims=True)
    base = rank + off

    row8 = lax.broadcasted_iota(jnp.int32, (SUBLANES, n), 0)
    meta = jnp.zeros((SUBLANES, n), F32)
    for k in range(TOP_K):
        pos_k = jnp.sum(jnp.where(hots[k], base, 0.0), axis=0, keepdims=True)
        meta = jnp.where(row8 == k, pos_k, meta)
        meta = jnp.where(row8 == TOP_K + k, gates[k], meta)
    return meta, jnp.broadcast_to(cnt, (N_EXPERTS, LANES))


def _mix_prompt_kernel(alpha_res, x_ref, mod_ref, w_in_ref, sgu_g_ref, sgu_bln_ref, sgu_w_ref,
                       sgu_bias_ref, conv_w_ref, conv_b_ref, w_out_ref, ln1_g_ref, ln1_b_ref,
                       router_w_ref, router_b_ref,
                       x1_ref, h2_ref, meta_ref, cnt_ref, convst_ref, w_in_bf_ref, w_out_bf_ref,
                       carry_ref):
    t = pl.program_id(1)

    @pl.when(jnp.logical_and(pl.program_id(0) == 0, t == 0))
    def _():
        w_in_bf_ref[...] = w_in_ref[...].astype(BF16)
        w_out_bf_ref[...] = w_out_ref[...].astype(BF16)

    @pl.when(t == 0)
    def _():
        carry_ref[...] = jnp.zeros_like(carry_ref)

    mod = mod_ref[...]
    sh1, sc1, g1, sh2, sc2 = mod[0:1], mod[1:2], mod[2:3], mod[3:4], mod[4:5]
    r_i = lax.broadcasted_iota(jnp.int32, (CHUNK, CHUNK), 0)
    c_i = lax.broadcasted_iota(jnp.int32, (CHUNK, CHUNK), 1)
    tril = c_i <= r_i
    tm = TOK_TILE
    prev2 = carry_ref[0:1, :]
    prev1 = carry_ref[1:2, :]

    subs = range(MIX_SUB)
    tile_rows = [slice(sub * tm, (sub + 1) * tm) for sub in subs]
    xs_in = [x_ref[rows, :] for rows in tile_rows]
    zs = [_dot((x * (1.0 + sc1) + sh1).astype(BF16), w_in_bf_ref[...]) for x in xs_in]

    mix_ins = []
    for sub in subs:
        z = zs[sub]
        u = z[:, 0:A_WIDTH]
        v = z[:, A_WIDTH:2 * A_WIDTH]
        gate_b = z[:, 2 * A_WIDTH:2 * A_WIDTH + B_WIDTH]
        gate_c = z[:, 2 * A_WIDTH + B_WIDTH:2 * A_WIDTH + 2 * B_WIDTH]
        hb = z[:, 2 * A_WIDTH + 2 * B_WIDTH:]

        a_parts = []
        for hd in range(A_HEADS):
            sl = slice(hd * A_HEAD_DIM, (hd + 1) * A_HEAD_DIM)
            vn = _layer_norm(v[:, sl], sgu_g_ref[hd:hd + 1, :], sgu_bln_ref[hd:hd + 1, :]).astype(BF16)
            wm = jnp.where(tril, sgu_w_ref[hd], 0.0).astype(BF16)
            bias = sgu_bias_ref[:, hd:hd + 1]
            s_parts = []
            for c in range(tm // CHUNK):
                s_parts.append(_dot(wm, vn[c * CHUNK:(c + 1) * CHUNK, :]) + bias)
            a_parts.append(u[:, sl] * jnp.concatenate(s_parts, axis=0))

        q = gate_c * hb
        row = lax.broadcasted_iota(jnp.int32, q.shape, 0)
        q_m1 = jnp.where(row == 0, prev1, pltpu.roll(q, 1, 0))
        q_m2 = jnp.where(row == 0, prev2, jnp.where(row == 1, prev1, pltpu.roll(q, 2, 0)))
        conv = (conv_b_ref[...] + q_m2 * conv_w_ref[0:1, :] + q_m1 * conv_w_ref[1:2, :]
                + q * conv_w_ref[2:3, :])
        b_out = gate_b * conv
        prev2 = q[tm - 2:tm - 1, :]
        prev1 = q[tm - 1:tm, :]

        mix_ins.append(jnp.concatenate(a_parts + [b_out], axis=-1).astype(BF16))

    mixes = [_dot(mix_in, w_out_bf_ref[...]) for mix_in in mix_ins]
    h2s = []
    for sub in subs:
        x1 = _layer_norm(alpha_res * xs_in[sub] + g1 * mixes[sub], ln1_g_ref[...], ln1_b_ref[...])
        x1_ref[tile_rows[sub], :] = x1
        h2 = x1 * (1.0 + sc2) + sh2
        h2_ref[tile_rows[sub], :] = h2.astype(BF16)
        h2s.append(h2)
    for sub in subs:
        meta, cnt = _route(h2s[sub], router_w_ref[...], router_b_ref[...])
        meta_ref[sub] = meta
        cnt_ref[sub] = cnt.astype(jnp.int32)

    last2 = jnp.concatenate([prev2, prev1], axis=0)
    carry_ref[0:2, :] = last2
    convst_ref[...] = last2


def _mix_prompt(x, mod, w_in, sgu_g, sgu_bln, sgu_w, sgu_bias_t, conv_w, conv_b, w_out,
                ln1_g, ln1_b, router_wt2, router_b_col, alpha_res):
    bsz, seq, _ = x.shape
    step_tok = MIX_SUB * TOK_TILE
    steps = seq // step_tok
    n_tok = bsz * seq
    tiles = seq // TOK_TILE
    const2 = lambda b, t: (0, 0)
    tok = lambda b, t: (b * steps + t, 0)
    tile3 = lambda b, t: (b * steps + t, 0, 0)
    return pl.pallas_call(
        functools.partial(_mix_prompt_kernel, alpha_res),
        grid=(bsz, steps),
        in_specs=[
            pl.BlockSpec((None, step_tok, D_MODEL), lambda b, t: (b, t, 0)),
            pl.BlockSpec((None, SUBLANES, D_MODEL), lambda b, t: (b, 0, 0)),
            pl.BlockSpec((D_MODEL, PROJ_COLS), const2, pipeline_mode=pl.Buffered(1)),
            pl.BlockSpec((A_HEADS, A_HEAD_DIM), const2),
            pl.BlockSpec((A_HEADS, A_HEAD_DIM), const2),
            pl.BlockSpec((A_HEADS, CHUNK, CHUNK), lambda b, t: (0, 0, 0)),
            pl.BlockSpec((CHUNK, A_HEADS), const2),
            pl.BlockSpec((3, B_WIDTH), const2),
            pl.BlockSpec((1, B_WIDTH), const2),
            pl.BlockSpec((D_MODEL, D_MODEL), const2, pipeline_mode=pl.Buffered(1)),
            pl.BlockSpec((1, D_MODEL), const2),
            pl.BlockSpec((1, D_MODEL), const2),
            pl.BlockSpec((2 * N_EXPERTS, D_MODEL), const2),
            pl.BlockSpec((N_EXPERTS, 1), const2),
        ],
        out_specs=[
            pl.BlockSpec((step_tok, D_MODEL), tok),
            pl.BlockSpec((step_tok, D_MODEL), tok),
            pl.BlockSpec((MIX_SUB, SUBLANES, TOK_TILE), tile3),
            pl.BlockSpec((MIX_SUB, N_EXPERTS, LANES), tile3),
            pl.BlockSpec((None, 2, B_WIDTH), lambda b, t: (b, 0, 0)),
            pl.BlockSpec((D_MODEL, PROJ_COLS), const2),
            pl.BlockSpec((D_MODEL, D_MODEL), const2),
        ],
        out_shape=[
            jax.ShapeDtypeStruct((n_tok, D_MODEL), F32),
            jax.ShapeDtypeStruct((n_tok, D_MODEL), BF16),
            jax.ShapeDtypeStruct((bsz * tiles, SUBLANES, TOK_TILE), F32),
            jax.ShapeDtypeStruct((bsz * tiles, N_EXPERTS, LANES), jnp.int32),
            jax.ShapeDtypeStruct((bsz, 2, B_WIDTH), F32),
            jax.ShapeDtypeStruct((D_MODEL, PROJ_COLS), BF16),
            jax.ShapeDtypeStruct((D_MODEL, D_MODEL), BF16),
        ],
        scratch_shapes=[pltpu.VMEM((SUBLANES, B_WIDTH), F32)],
        compiler_params=pltpu.CompilerParams(
            dimension_semantics=("arbitrary", "arbitrary"), vmem_limit_bytes=VMEM_LIMIT),
        name="mix_prompt",
    )(x, mod, w_in, sgu_g, sgu_bln, sgu_w, sgu_bias_t, conv_w, conv_b, w_out,
      ln1_g, ln1_b, router_wt2, router_b_col)


def _mix_sample_kernel(alpha_res, x_ref, mod_ref, prev0_ref, prev1_ref, w_in_ref, sgu_g_ref,
                       sgu_bln_ref, sgu_w00_ref, sgu_b0_ref, conv_w_ref, conv_b_ref, w_out_ref,
                       ln1_g_ref, ln1_b_ref, router_w_ref, router_b_ref,
                       x1_ref, h2_ref, meta_ref, cnt_ref, q_ref, vn_ref):
    x = x_ref[...]
    sh1 = mod_ref[:, 0:D_MODEL]
    sc1 = mod_ref[:, D_MODEL:2 * D_MODEL]
    g1 = mod_ref[:, 2 * D_MODEL:3 * D_MODEL]
    sh2 = mod_ref[:, 3 * D_MODEL:4 * D_MODEL]
    sc2 = mod_ref[:, 4 * D_MODEL:5 * D_MODEL]
    h = (x * (1.0 + sc1) + sh1).astype(BF16)
    z = _dot(h, w_in_ref[...])
    u = z[:, 0:A_WIDTH]
    v = z[:, A_WIDTH:2 * A_WIDTH]
    gate_b = z[:, 2 * A_WIDTH:2 * A_WIDTH + B_WIDTH]
    gate_c = z[:, 2 * A_WIDTH + B_WIDTH:2 * A_WIDTH + 2 * B_WIDTH]
    hb = z[:, 2 * A_WIDTH + 2 * B_WIDTH:]

    vn_parts = []
    for hd in range(A_HEADS):
        sl = slice(hd * A_HEAD_DIM, (hd + 1) * A_HEAD_DIM)
        vn_parts.append(_layer_norm(v[:, sl], sgu_g_ref[hd:hd + 1, :], sgu_bln_ref[hd:hd + 1, :]))
    vn = jnp.concatenate(vn_parts, axis=-1)
    vn_ref[...] = vn
    a_out = u * (vn * sgu_w00_ref[...] + sgu_b0_ref[...])

    q = gate_c * hb
    q_ref[...] = q
    conv = (conv_b_ref[...] + prev0_ref[...] * conv_w_ref[0:1, :] + prev1_ref[...] * conv_w_ref[1:2, :]
            + q * conv_w_ref[2:3, :])
    b_out = gate_b * conv

    mix_in = jnp.concatenate([a_out, b_out], axis=-1).astype(BF16)
    mix = _dot(mix_in, w_out_ref[...])
    x1 = _layer_norm(alpha_res * x + g1 * mix, ln1_g_ref[...], ln1_b_ref[...])
    x1_ref[...] = x1
    h2 = x1 * (1.0 + sc2) + sh2
    h2_ref[...] = h2.astype(BF16)
    meta, cnt = _route(h2, router_w_ref[...], router_b_ref[...])
    meta_ref[...] = meta
    cnt_ref[...] = cnt.astype(jnp.int32)


def _mix_sample(x, mod, prev0, prev1, w_in_bf, sgu_g, sgu_bln, sgu_w00, sgu_b0, conv_w, conv_b,
                w_out_bf, ln1_g, ln1_b, router_wt2, router_b_col, alpha_res):
    n = x.shape[0]
    return pl.pallas_call(
        functools.partial(_mix_sample_kernel, alpha_res),
        out_shape=[
            jax.ShapeDtypeStruct((n, D_MODEL), F32),
            jax.ShapeDtypeStruct((n, D_MODEL), BF16),
            jax.ShapeDtypeStruct((SUBLANES, n), F32),
            jax.ShapeDtypeStruct((N_EXPERTS, LANES), jnp.int32),
            jax.ShapeDtypeStruct((n, B_WIDTH), F32),
            jax.ShapeDtypeStruct((n, A_WIDTH), F32),
        ],
        compiler_params=pltpu.CompilerParams(vmem_limit_bytes=VMEM_LIMIT),
        name="mix_sample",
    )(x, mod, prev0, prev1, w_in_bf, sgu_g, sgu_bln, sgu_w00, sgu_b0, conv_w, conv_b, w_out_bf,
      ln1_g, ln1_b, router_wt2, router_b_col)


def _strip_copy(src_ref, src_row, dst_ref, dst_row, n_rows, sem):
    @pl.when(n_rows > 0)
    def _():
        pltpu.make_async_copy(src_ref.at[pl.ds(src_row, n_rows)],
                              dst_ref.at[pl.ds(dst_row, n_rows)], sem).start()


def _wait_rows(hbm_ref, vmem_ref, n_rows, sem):
    pltpu.make_async_copy(hbm_ref.at[pl.ds(0, n_rows)], vmem_ref.at[pl.ds(0, n_rows)], sem).wait()


def _sort_kernel(n_tok, tile0, n_tiles, first, *refs):
    if first:
        (sorted_row_ref, off_ref, cnt_ref, pad_row_ref, pad_n_ref, h2_ref, meta_ref,
         xs_ref, stage_ref, zero_ref, sem, pad_sem) = refs
    else:
        (sorted_row_ref, off_ref, cnt_ref, pad_row_ref, pad_n_ref, h2_ref, meta_ref, _,
         xs_ref, stage_ref, sem) = refs
    j = pl.program_id(0)
    n_sorted = n_tok * TOP_K
    slot = j % 2

    if first:
        @pl.when(j == 0)
        def _():
            zero_ref[...] = jnp.zeros_like(zero_ref)

            def start(e, carry):
                _strip_copy(zero_ref, 0, xs_ref, pad_row_ref[e], pad_n_ref[e], pad_sem)
                return carry
            lax.fori_loop(0, N_EXPERTS, start, 0)

            def wait(e, carry):
                @pl.when(pad_n_ref[e] > 0)
                def _():
                    _wait_rows(xs_ref, zero_ref, pad_n_ref[e], pad_sem)
                return carry
            lax.fori_loop(0, N_EXPERTS, wait, 0)

    meta = meta_ref[...]
    sub_r = lax.broadcasted_iota(jnp.int32, (n_sorted, n_tok), 0)
    sel = sub_r == meta[0:1, :].astype(jnp.int32)
    for k in range(1, TOP_K):
        sel = jnp.logical_or(sel, sub_r == meta[k:k + 1, :].astype(jnp.int32))
    perm = jnp.where(sel, 1.0, 0.0).astype(BF16)
    rows = _dot(perm, h2_ref[...])
    _store_rows(stage_ref, slot, rows)

    base = (tile0 + j) * N_EXPERTS

    def start(e, carry):
        _strip_copy(stage_ref.at[slot], off_ref[base + e], xs_ref, sorted_row_ref[base + e],
                    cnt_ref[base + e], sem.at[slot])
        return carry
    lax.fori_loop(0, N_EXPERTS, start, 0)

    @pl.when(j > 0)
    def _():
        _wait_rows(xs_ref, stage_ref.at[0], n_sorted, sem.at[1 - slot])

    @pl.when(j == n_tiles - 1)
    def _():
        _wait_rows(xs_ref, stage_ref.at[0], n_sorted, sem.at[slot])


def _sort(tables, pad_row, pad_n, h2, meta, xs, n_sorted_rows, n_tok, tile0):
    sorted_row, off, cnt = tables
    first = xs is None
    n_tiles = h2.shape[0] // n_tok
    if meta.ndim == 2:
        meta_spec = pl.BlockSpec((SUBLANES, n_tok), lambda j, *_: (0, j))
    else:
        meta_spec = pl.BlockSpec((None, SUBLANES, n_tok), lambda j, *_: (j, 0, 0))
    in_specs = [pl.BlockSpec((n_tok, D_MODEL), lambda j, *_: (j, 0)), meta_spec]
    operands = [sorted_row, off, cnt, pad_row, pad_n, h2, meta]
    scratch = [pltpu.VMEM((2, n_tok * TOP_K, ROW_SLABS, LANES), BF16)]
    if first:
        scratch.append(pltpu.VMEM((ROW_TILE, ROW_SLABS, LANES), BF16))
        aliases = {}
    else:
        in_specs.append(pl.BlockSpec(memory_space=pl.ANY))
        operands.append(xs)
        aliases = {len(operands) - 1: 0}
    scratch.append(pltpu.SemaphoreType.DMA((2,)))
    if first:
        scratch.append(pltpu.SemaphoreType.DMA(()))
    grid_spec = pltpu.PrefetchScalarGridSpec(
        num_scalar_prefetch=5,
        grid=(n_tiles,),
        in_specs=in_specs,
        out_specs=pl.BlockSpec(memory_space=pl.ANY),
        scratch_shapes=scratch,
    )
    return pl.pallas_call(
        functools.partial(_sort_kernel, n_tok, tile0, n_tiles, first),
        grid_spec=grid_spec,
        out_shape=jax.ShapeDtypeStruct((n_sorted_rows, ROW_SLABS, LANES), BF16),
        input_output_aliases=aliases,
        compiler_params=pltpu.CompilerParams(
            dimension_semantics=("arbitrary",), vmem_limit_bytes=VMEM_LIMIT),
        name="sort_first" if first else "sort_more",
    )(*operands)


def _experts_kernel(tile_start_ref, n_tile_ref, n_used_ref, xs_ref, w_gu_ref, b_gu_ref, w_down_ref,
                    b_down_ref, ys_ref, w_gu_bf_ref, w_down_bf_ref, x_buf, y_buf, x_sem, y_sem):
    e = pl.program_id(0)
    n_used = n_used_ref[0]
    first_tile = tile_start_ref[e]
    n_tile = n_tile_ref[e]

    def x_copy(g):
        slot = g % TILE_SLOTS
        return pltpu.make_async_copy(xs_ref.at[pl.ds(g * ROW_TILE, ROW_TILE)], x_buf.at[slot],
                                     x_sem.at[slot])

    def y_copy(g):
        slot = g % TILE_SLOTS
        return pltpu.make_async_copy(y_buf.at[slot], ys_ref.at[pl.ds(g * ROW_TILE, ROW_TILE)],
                                     y_sem.at[slot])

    def request(g):
        @pl.when(g < n_used)
        def _():
            x_copy(g).start(priority=1)

    @pl.when(e == 0)
    def _():
        for g in range(TILE_AHEAD):
            request(g)

    @pl.when(n_tile > 0)
    def _():
        w_gu_bf_ref[...] = w_gu_ref[...].astype(BF16)
        w_down_bf_ref[...] = w_down_ref[...].astype(BF16)

    def begin(g):
        x_copy(g).wait()
        request(g + TILE_AHEAD)

        @pl.when(g >= TILE_SLOTS)
        def _():
            y_copy(g - TILE_SLOTS).wait()

    def compute(g):
        slot = g % TILE_SLOTS
        x = _load_rows(x_buf, slot)
        gu = _dot(x, w_gu_bf_ref[...]) + b_gu_ref[...]
        gate = jnp.minimum(gu[:, :D_EXPERT], SWIGLU_LIMIT)
        up = jnp.clip(gu[:, D_EXPERT:], -SWIGLU_LIMIT, SWIGLU_LIMIT)
        act = (up + 1.0) * gate * jax.nn.sigmoid(SWIGLU_ALPHA * gate)
        y = _dot(act.astype(BF16), w_down_bf_ref[...]) + b_down_ref[...]
        _store_rows(y_buf, slot, y)
        y_copy(g).start(priority=1)

    def pair_body(p, carry):
        g = first_tile + 2 * p
        begin(g)
        begin(g + 1)
        compute(g)
        compute(g + 1)
        return carry

    lax.fori_loop(0, n_tile // 2, pair_body, 0)

    @pl.when(n_tile % 2 == 1)
    def _():
        g = first_tile + n_tile - 1
        begin(g)
        compute(g)

    @pl.when(e == N_EXPERTS - 1)
    def _():
        for back in range(TILE_SLOTS, 0, -1):
            @pl.when(n_used >= back)
            def _(back=back):
                y_copy(n_used - back).wait()


def _experts(tile_start, n_tile_e, n_used, xs, w_gu, b_gu, w_down, b_down):
    w_blk = lambda e, *_: (e, 0, 0)
    grid_spec = pltpu.PrefetchScalarGridSpec(
        num_scalar_prefetch=3,
        grid=(N_EXPERTS,),
        in_specs=[
            pl.BlockSpec(memory_space=pl.ANY),
            pl.BlockSpec((None, D_MODEL, 2 * D_EXPERT), w_blk),
            pl.BlockSpec((None, 1, 2 * D_EXPERT), w_blk),
            pl.BlockSpec((None, D_EXPERT, D_MODEL), w_blk),
            pl.BlockSpec((None, 1, D_MODEL), w_blk),
        ],
        out_specs=pl.BlockSpec(memory_space=pl.ANY),
        scratch_shapes=[pltpu.VMEM((D_MODEL, 2 * D_EXPERT), BF16),
                        pltpu.VMEM((D_EXPERT, D_MODEL), BF16),
                        pltpu.VMEM((TILE_SLOTS, ROW_TILE, ROW_SLABS, LANES), BF16),
                        pltpu.VMEM((TILE_SLOTS, ROW_TILE, ROW_SLABS, LANES), BF16),
                        pltpu.SemaphoreType.DMA((TILE_SLOTS,)),
                        pltpu.SemaphoreType.DMA((TILE_SLOTS,))],
    )
    return pl.pallas_call(
        _experts_kernel,
        grid_spec=grid_spec,
        out_shape=jax.ShapeDtypeStruct(xs.shape, BF16),
        compiler_params=pltpu.CompilerParams(
            dimension_semantics=("arbitrary",), vmem_limit_bytes=VMEM_LIMIT),
        name="experts",
    )(tile_start, n_tile_e, n_used, xs, w_gu, b_gu.reshape(N_EXPERTS, 1, -1), w_down,
      b_down.reshape(N_EXPERTS, 1, -1))


def _combine_kernel(n_tok, tile0, n_tiles, mod_rows, alpha_res, sorted_row_ref, off_ref, cnt_ref,
                    ys_ref, x1_ref, meta_ref, mod_ref, ln2_g_ref, ln2_b_ref, out_ref,
                    stage_ref, sem):
    j = pl.program_id(0)
    n_sorted = n_tok * TOP_K
    slot = j % 2

    def start_tile(tile, to_slot):
        base = (tile0 + tile) * N_EXPERTS

        def start(e, carry):
            _strip_copy(ys_ref, sorted_row_ref[base + e], stage_ref.at[to_slot],
                        off_ref[base + e], cnt_ref[base + e], sem.at[to_slot])
            return carry
        lax.fori_loop(0, N_EXPERTS, start, 0)

    @pl.when(j == 0)
    def _():
        start_tile(j, slot)

    @pl.when(j + 1 < n_tiles)
    def _():
        start_tile(j + 1, 1 - slot)

    meta = meta_ref[...]
    sub_r = lax.broadcasted_iota(jnp.int32, (n_sorted, n_tok), 0)
    comb_t = jnp.zeros((n_sorted, n_tok), F32)
    for k in range(TOP_K):
        comb_t = jnp.where(sub_r == meta[k:k + 1, :].astype(jnp.int32),
                           meta[TOP_K + k:TOP_K + k + 1, :], comb_t)
    comb_t = comb_t.astype(BF16)

    _wait_rows(ys_ref, stage_ref.at[0], n_sorted, sem.at[slot])
    ys = _load_rows(stage_ref, slot)
    ffn = lax.dot_general(comb_t, ys, (((0,), (0,)), ((), ())), preferred_element_type=F32)
    if mod_rows:
        g2 = mod_ref[:, 5 * D_MODEL:6 * D_MODEL]
    else:
        g2 = mod_ref[5:6, :]
    out_ref[...] = _layer_norm(alpha_res * x1_ref[...] + g2 * ffn, ln2_g_ref[...], ln2_b_ref[...])


def _combine(tables, ys, x1, meta, mod, ln2_g, ln2_b, n_tok, tile0, alpha_res):
    sorted_row, off, cnt = tables
    n_tiles = x1.shape[0] // n_tok
    mod_rows = mod.ndim == 2
    if mod_rows:
        mod_spec = pl.BlockSpec((n_tok, N_MOD * D_MODEL), lambda j, *_: (j, 0))
    else:
        tiles_per_seq = n_tiles // mod.shape[0]
        mod_spec = pl.BlockSpec((None, SUBLANES, D_MODEL), lambda j, *_: (j // tiles_per_seq, 0, 0))
    if meta.ndim == 2:
        meta_spec = pl.BlockSpec((SUBLANES, n_tok), lambda j, *_: (0, j))
    else:
        meta_spec = pl.BlockSpec((None, SUBLANES, n_tok), lambda j, *_: (j, 0, 0))
    grid_spec = pltpu.PrefetchScalarGridSpec(
        num_scalar_prefetch=3,
        grid=(n_tiles,),
        in_specs=[
            pl.BlockSpec(memory_space=pl.ANY),
            pl.BlockSpec((n_tok, D_MODEL), lambda j, *_: (j, 0)),
            meta_spec,
            mod_spec,
            pl.BlockSpec((1, D_MODEL), lambda j, *_: (0, 0)),
            pl.BlockSpec((1, D_MODEL), lambda j, *_: (0, 0)),
        ],
        out_specs=pl.BlockSpec((n_tok, D_MODEL), lambda j, *_: (j, 0)),
        scratch_shapes=[pltpu.VMEM((2, n_tok * TOP_K, ROW_SLABS, LANES), BF16),
                        pltpu.SemaphoreType.DMA((2,))],
    )
    return pl.pallas_call(
        functools.partial(_combine_kernel, n_tok, tile0, n_tiles, mod_rows, alpha_res),
        grid_spec=grid_spec,
        out_shape=jax.ShapeDtypeStruct(x1.shape, F32),
        compiler_params=pltpu.CompilerParams(
            dimension_semantics=("arbitrary",), vmem_limit_bytes=VMEM_LIMIT),
        name="combine",
    )(sorted_row, off, cnt, ys, x1, meta, mod, ln2_g, ln2_b)


def _routing_tables(cnt_all):
    total = jnp.sum(cnt_all, axis=0)
    n_tile_e = (total + ROW_TILE - 1) // ROW_TILE
    tile_end = jnp.cumsum(n_tile_e)
    tile_start = tile_end - n_tile_e
    row_start = tile_start * ROW_TILE
    cum = jnp.cumsum(cnt_all, axis=0) - cnt_all
    off = jnp.cumsum(cnt_all, axis=1) - cnt_all
    sorted_row = row_start[None, :] + cum
    n_used = tile_end[-1]
    pad_row = row_start + total
    pad_n = n_tile_e * ROW_TILE - total
    i32 = lambda a: a.astype(jnp.int32)
    return ((i32(sorted_row).reshape(-1), i32(off).reshape(-1), i32(cnt_all).reshape(-1)),
            i32(pad_row), i32(pad_n), i32(tile_start), i32(n_tile_e), i32(n_used).reshape(1))


def kernel(x_prompt, x_sample, state_conv, c_prompt, c_sample, ada_w, ada_b, w_in, sgu_ln_g, sgu_ln_b,
           sgu_w, sgu_b, conv_w, conv_b, w_out, ln1_g, ln1_b, router_w, router_b, w_gu, b_gu,
           w_down, b_down, ln2_g, ln2_b):
    depth = ada_w.shape[0]
    assert depth == 1
    bsz, seq, _ = x_prompt.shape
    n_dec = x_sample.shape[0]
    assert x_sample.shape[1] == 1 and seq % TOK_TILE == 0
    alpha_res = (2.0 * depth) ** 0.25
    l = 0

    mod_s, mod_p = _ada(jnp.concatenate([c_sample, c_prompt], axis=0), n_dec, ada_w[l], ada_b[l])
    mod_p = jnp.pad(mod_p.reshape(bsz, N_MOD, D_MODEL), ((0, 0), (0, SUBLANES - N_MOD), (0, 0)))

    router_wt =jnp.transpose(router_w[l])
    router_wt_hi = router_wt.astype(BF16)
    router_wt_lo = (router_wt - router_wt_hi.astype(F32)).astype(BF16)
    router_wt2 = jnp.concatenate([router_wt_hi, router_wt_lo], axis=0)
    router_b_col = router_b[l].reshape(N_EXPERTS, 1)
    row = lambda a: a.reshape(1, -1)

    x1_p, h2_p, meta_p, cnt_p, convst_p, w_in_bf, w_out_bf = _mix_prompt(
        x_prompt, mod_p, w_in[l], sgu_ln_g[l], sgu_ln_b[l], sgu_w[l], jnp.transpose(sgu_b[l]),
        conv_w[l], row(conv_b[l]), w_out[l], row(ln1_g[l]), row(ln1_b[l]), router_wt2,
        router_b_col, alpha_res)
    x1_s, h2_s, meta_s, cnt_s, q_s, vn_s = _mix_sample(
        x_sample.reshape(n_dec, D_MODEL), mod_s, state_conv[l, :, 0, :], state_conv[l, :, 1, :],
        w_in_bf, sgu_ln_g[l], sgu_ln_b[l], row(jnp.repeat(sgu_w[l, :, 0, 0], A_HEAD_DIM)),
        row(jnp.repeat(sgu_b[l, :, 0], A_HEAD_DIM)), conv_w[l], row(conv_b[l]), w_out_bf,
        row(ln1_g[l]), row(ln1_b[l]), router_wt2, router_b_col, alpha_res)

    n_ptiles = bsz * seq // TOK_TILE
    cnt_all = jnp.concatenate([cnt_p[:, :, 0], cnt_s[None, :, 0]], axis=0)
    n_assign = (bsz * seq + n_dec) * TOP_K
    n_row_tiles = -(-n_assign // ROW_TILE) + N_EXPERTS
    tables, pad_row, pad_n, tile_start, n_tile_e, n_used = _routing_tables(cnt_all)

    n_sorted_rows = n_row_tiles * ROW_TILE
    xs = _sort(tables, pad_row, pad_n, h2_p, meta_p, None, n_sorted_rows, TOK_TILE, 0)
    xs = _sort(tables, pad_row, pad_n, h2_s, meta_s, xs, n_sorted_rows, n_dec, n_ptiles)
    ys = _experts(tile_start, n_tile_e, n_used, xs, w_gu[l], b_gu[l], w_down[l], b_down[l])
    y_p = _combine(tables, ys, x1_p, meta_p, mod_p, row(ln2_g[l]), row(ln2_b[l]), TOK_TILE, 0, alpha_res)
    y_s = _combine(tables, ys, x1_s, meta_s, mod_s, row(ln2_g[l]), row(ln2_b[l]), n_dec, n_ptiles,
                   alpha_res)

    conv_state_sample = jnp.stack([state_conv[l, :, 1, :], q_s], axis=1)[None]
    return (y_p.reshape(bsz, seq, D_MODEL),
            y_s.reshape(n_dec, 1, D_MODEL),
            convst_p[None],
            conv_state_sample,
            vn_s.reshape(1, n_dec, 1, A_HEADS, A_HEAD_DIM))
```

```python
import functools

import jax
import jax.numpy as jnp
from jax import lax
from jax.experimental import pallas as pl
from jax.experimental.pallas import tpu as pltpu

F32 = jnp.float32
BF16 = jnp.bfloat16

D_MODEL = 1024
A_WIDTH = 512
B_WIDTH = 512
A_HEADS = 4
A_HEAD_DIM = 128
CHUNK = 128
PROJ_COLS = 2 * A_WIDTH + 3 * B_WIDTH
N_EXPERTS = 32
TOP_K = 4
D_EXPERT = 1024
SWIGLU_LIMIT = 7.0
SWIGLU_ALPHA = 1.702
LN_EPS = 1e-5
N_MOD = 6

LANES = 128
SUBLANES = 8
ROW_SLABS = D_MODEL // LANES
TOK_TILE = 256
MIX_SUB = 2
MOE_SUB = 2
ROW_TILE = 256
TILE_SLOTS = 4
TILE_AHEAD = 2
VMEM_LIMIT = 56 * 1024 * 1024


def _layer_norm(x, g, b):
    mu = jnp.mean(x, axis=-1, keepdims=True)
    xc = x - mu
    var = jnp.mean(xc * xc, axis=-1, keepdims=True)
    return xc * lax.rsqrt(var + LN_EPS) * g + b


def _dot(a, b):
    return jnp.dot(a, b, preferred_element_type=F32)


def _dot_nt(a, b):
    return lax.dot_general(a, b, (((1,), (1,)), ((), ())), preferred_element_type=F32)


def _store_rows(ref, slot, rows):
    n = rows.shape[0]
    ref[slot] = rows.astype(BF16).reshape(n, ROW_SLABS, LANES)


def _load_rows(ref, slot):
    return ref[slot].reshape(ref.shape[1], D_MODEL)


def _split_bf16(a):
    hi = a.astype(BF16)
    lo = (a - hi.astype(F32)).astype(BF16)
    return hi, lo


def _ada_kernel(n_first, c_ref, w_ref, b_ref, o_first_ref, o_rest_ref):
    c = c_ref[...]
    s_hi, s_lo = _split_bf16(c * jax.nn.sigmoid(c))
    w_hi, w_lo = _split_bf16(w_ref[...])
    m = _dot(s_hi, w_hi) + _dot(s_hi, w_lo) + _dot(s_lo, w_hi) + b_ref[...]
    o_first_ref[...] = m[:n_first]
    o_rest_ref[...] = m[n_first:]


def _ada(c_all, n_first, ada_w, ada_b):
    rows = c_all.shape[0]
    return pl.pallas_call(
        functools.partial(_ada_kernel, n_first),
        grid=(N_MOD,),
        in_specs=[
            pl.BlockSpec((rows, D_MODEL), lambda n: (0, 0)),
            pl.BlockSpec((D_MODEL, D_MODEL), lambda n: (0, n)),
            pl.BlockSpec((1, D_MODEL), lambda n: (0, n)),
        ],
        out_specs=[pl.BlockSpec((n_first, D_MODEL), lambda n: (0, n)),
                   pl.BlockSpec((rows - n_first, D_MODEL), lambda n: (0, n))],
        out_shape=[jax.ShapeDtypeStruct((n_first, N_MOD * D_MODEL), F32),
                   jax.ShapeDtypeStruct((rows - n_first, N_MOD * D_MODEL), F32)],
        name="ada",
    )(c_all, ada_w, ada_b.reshape(1, -1))


def _route(h2, router_wt2, router_b):
    n = h2.shape[0]
    h_hi = h2.astype(BF16)
    h_lo = (h2 - h_hi.astype(F32)).astype(BF16)
    l1 = _dot_nt(router_wt2, h_hi)
    l2 = _dot_nt(router_wt2[:N_EXPERTS], h_lo)
    logits = l1[:N_EXPERTS] + l1[N_EXPERTS:] + l2 + router_b

    sub = lax.broadcasted_iota(jnp.int32, (N_EXPERTS, n), 0)
    work = logits
    vals, hots = [], []
    for _ in range(TOP_K):
        m = jnp.max(work, axis=0, keepdims=True)
        idx = jnp.min(jnp.where(work == m, sub, N_EXPERTS), axis=0, keepdims=True)
        hot = sub == idx
        work = jnp.where(hot, -jnp.inf, work)
        vals.append(m)
        hots.append(hot)
    exps = [jnp.exp(v - vals[0]) for v in vals]
    denom = exps[0] + exps[1] + exps[2] + exps[3]
    gates = [e / denom for e in exps]

    onehot = jnp.zeros((N_EXPERTS, n), F32)
    for hot in hots:
        onehot = onehot + hot.astype(F32)
    onehot_bf = onehot.astype(BF16)
    t_r = lax.broadcasted_iota(jnp.int32, (n, n), 0)
    t_c = lax.broadcasted_iota(jnp.int32, (n, n), 1)
    rank = _dot(onehot_bf, (t_r < t_c).astype(BF16))
    cnt = jnp.sum(onehot, axis=1, keepdims=True)
    e_r = lax.broadcasted_iota(jnp.int32, (N_EXPERTS, N_EXPERTS), 0)
    e_c = lax.broadcasted_iota(jnp.int32, (N_EXPERTS, N_EXPERTS), 1)
    below = _dot((e_c < e_r).astype(BF16), onehot_bf)
    off = jnp.sum(below, axis=1, keepdims=True)
    base = rank + off

    row8 = lax.broadcasted_iota(jnp.int32, (SUBLANES, n), 0)
    meta = jnp.zeros((SUBLANES, n), F32)
    for k in range(TOP_K):
        pos_k = jnp.sum(jnp.where(hots[k], base, 0.0), axis=0, keepdims=True)
        meta = jnp.where(row8 == k, pos_k, meta)
        meta = jnp.where(row8 == TOP_K + k, gates[k], meta)
    return meta, jnp.broadcast_to(cnt, (N_EXPERTS, LANES))


def _mix_prompt_kernel(alpha_res, x_ref, mod_ref, w_in_ref, sgu_g_ref, sgu_bln_ref, sgu_w_ref,
                       sgu_bias_ref, conv_w_ref, conv_b_ref, w_out_ref, ln1_g_ref, ln1_b_ref,
                       router_w_ref, router_b_ref,
                       x1_ref, h2_ref, meta_ref, cnt_ref, convst_ref, w_in_bf_ref, w_out_bf_ref,
                       carry_ref):
    t = pl.program_id(1)

    @pl.when(jnp.logical_and(pl.program_id(0) == 0, t == 0))
    def _():
        w_in_bf_ref[...] = w_in_ref[...].astype(BF16)
        w_out_bf_ref[...] = w_out_ref[...].astype(BF16)

    @pl.when(t == 0)
    def _():
        carry_ref[...] = jnp.zeros_like(carry_ref)

    mod = mod_ref[...]
    sh1, sc1, g1, sh2, sc2 = mod[0:1], mod[1:2], mod[2:3], mod[3:4], mod[4:5]
    r_i = lax.broadcasted_iota(jnp.int32, (CHUNK, CHUNK), 0)
    c_i = lax.broadcasted_iota(jnp.int32, (CHUNK, CHUNK), 1)
    tril = c_i <= r_i
    tm = TOK_TILE
    prev2 = carry_ref[0:1, :]
    prev1 = carry_ref[1:2, :]

    subs = range(MIX_SUB)
    tile_rows = [slice(sub * tm, (sub + 1) * tm) for sub in subs]
    xs_in = [x_ref[rows, :] for rows in tile_rows]
    zs = [_dot((x * (1.0 + sc1) + sh1).astype(BF16), w_in_bf_ref[...]) for x in xs_in]

    mix_ins = []
    for sub in subs:
        z = zs[sub]
        u = z[:, 0:A_WIDTH]
        v = z[:, A_WIDTH:2 * A_WIDTH]
        gate_b = z[:, 2 * A_WIDTH:2 * A_WIDTH + B_WIDTH]
        gate_c = z[:, 2 * A_WIDTH + B_WIDTH:2 * A_WIDTH + 2 * B_WIDTH]
        hb = z[:, 2 * A_WIDTH + 2 * B_WIDTH:]

        a_parts = []
        for hd in range(A_HEADS):
            sl = slice(hd * A_HEAD_DIM, (hd + 1) * A_HEAD_DIM)
            vn = _layer_norm(v[:, sl], sgu_g_ref[hd:hd + 1, :], sgu_bln_ref[hd:hd + 1, :]).astype(BF16)
            wm = jnp.where(tril, sgu_w_ref[hd], 0.0).astype(BF16)
            bias = sgu_bias_ref[:, hd:hd + 1]
            s_parts = []
            for c in range(tm // CHUNK):
                s_parts.append(_dot(wm, vn[c * CHUNK:(c + 1) * CHUNK, :]) + bias)
            a_parts.append(u[:, sl] * jnp.concatenate(s_parts, axis=0))

        q = gate_c * hb
        row = lax.broadcasted_iota(jnp.int32, q.shape, 0)
        q_m1 = jnp.where(row == 0, prev1, pltpu.roll(q, 1, 0))
        q_m2 = jnp.where(row == 0, prev2, jnp.where(row == 1, prev1, pltpu.roll(q, 2, 0)))
        conv = (conv_b_ref[...] + q_m2 * conv_w_ref[0:1, :] + q_m1 * conv_w_ref[1:2, :]
                + q * conv_w_ref[2:3, :])
        b_out = gate_b * conv
        prev2 = q[tm - 2:tm - 1, :]
        prev1 = q[tm - 1:tm, :]

        mix_ins.append(jnp.concatenate(a_parts + [b_out], axis=-1).astype(BF16))

    mixes = [_dot(mix_in, w_out_bf_ref[...]) for mix_in in mix_ins]
    h2s = []
    for sub in subs:
        x1 = _layer_norm(alpha_res * xs_in[sub] + g1 * mixes[sub], ln1_g_ref[...], ln1_b_ref[...])
        x1_ref[tile_rows[sub], :] = x1
        h2 = x1 * (1.0 + sc2) + sh2
        h2_ref[tile_rows[sub], :] = h2.astype(BF16)
        h2s.append(h2)
    for sub in subs:
        meta, cnt = _route(h2s[sub], router_w_ref[...], router_b_ref[...])
        meta_ref[sub] = meta
        cnt_ref[sub] = cnt.astype(jnp.int32)

    last2 = jnp.concatenate([prev2, prev1], axis=0)
    carry_ref[0:2, :] = last2
    convst_ref[...] = last2


def _mix_prompt(x, mod, w_in, sgu_g, sgu_bln, sgu_w, sgu_bias_t, conv_w, conv_b, w_out,
                ln1_g, ln1_b, router_wt2, router_b_col, alpha_res):
    bsz, seq, _ = x.shape
    step_tok = MIX_SUB * TOK_TILE
    steps = seq // step_tok
    n_tok = bsz * seq
    tiles = seq // TOK_TILE
    const2 = lambda b, t: (0, 0)
    tok = lambda b, t: (b * steps + t, 0)
    tile3 = lambda b, t: (b * steps + t, 0, 0)
    return pl.pallas_call(
        functools.partial(_mix_prompt_kernel, alpha_res),
        grid=(bsz, steps),
        in_specs=[
            pl.BlockSpec((None, step_tok, D_MODEL), lambda b, t: (b, t, 0)),
            pl.BlockSpec((None, SUBLANES, D_MODEL), lambda b, t: (b, 0, 0)),
            pl.BlockSpec((D_MODEL, PROJ_COLS), const2, pipeline_mode=pl.Buffered(1)),
            pl.BlockSpec((A_HEADS, A_HEAD_DIM), const2),
            pl.BlockSpec((A_HEADS, A_HEAD_DIM), const2),
            pl.BlockSpec((A_HEADS, CHUNK, CHUNK), lambda b, t: (0, 0, 0)),
            pl.BlockSpec((CHUNK, A_HEADS), const2),
            pl.BlockSpec((3, B_WIDTH), const2),
            pl.BlockSpec((1, B_WIDTH), const2),
            pl.BlockSpec((D_MODEL, D_MODEL), const2, pipeline_mode=pl.Buffered(1)),
            pl.BlockSpec((1, D_MODEL), const2),
            pl.BlockSpec((1, D_MODEL), const2),
            pl.BlockSpec((2 * N_EXPERTS, D_MODEL), const2),
            pl.BlockSpec((N_EXPERTS, 1), const2),
        ],
        out_specs=[
            pl.BlockSpec((step_tok, D_MODEL), tok),
            pl.BlockSpec((step_tok, D_MODEL), tok),
            pl.BlockSpec((MIX_SUB, SUBLANES, TOK_TILE), tile3),
            pl.BlockSpec((MIX_SUB, N_EXPERTS, LANES), tile3),
            pl.BlockSpec((None, 2, B_WIDTH), lambda b, t: (b, 0, 0)),
            pl.BlockSpec((D_MODEL, PROJ_COLS), const2),
            pl.BlockSpec((D_MODEL, D_MODEL), const2),
        ],
        out_shape=[
            jax.ShapeDtypeStruct((n_tok, D_MODEL), F32),
            jax.ShapeDtypeStruct((n_tok, D_MODEL), BF16),
            jax.ShapeDtypeStruct((bsz * tiles, SUBLANES, TOK_TILE), F32),
            jax.ShapeDtypeStruct((bsz * tiles, N_EXPERTS, LANES), jnp.int32),
            jax.ShapeDtypeStruct((bsz, 2, B_WIDTH), F32),
            jax.ShapeDtypeStruct((D_MODEL, PROJ_COLS), BF16),
            jax.ShapeDtypeStruct((D_MODEL, D_MODEL), BF16),
        ],
        scratch_shapes=[pltpu.VMEM((SUBLANES, B_WIDTH), F32)],
        compiler_params=pltpu.CompilerParams(
            dimension_semantics=("arbitrary", "arbitrary"), vmem_limit_bytes=VMEM_LIMIT),
        name="mix_prompt",
    )(x, mod, w_in, sgu_g, sgu_bln, sgu_w, sgu_bias_t, conv_w, conv_b, w_out,
      ln1_g, ln1_b, router_wt2, router_b_col)


def _mix_sample_kernel(alpha_res, x_ref, mod_ref, prev0_ref, prev1_ref, w_in_ref, sgu_g_ref,
                       sgu_bln_ref, sgu_w00_ref, sgu_b0_ref, conv_w_ref, conv_b_ref, w_out_ref,
                       ln1_g_ref, ln1_b_ref, router_w_ref, router_b_ref,
                       x1_ref, h2_ref, meta_ref, cnt_ref, q_ref, vn_ref):
    x = x_ref[...]
    sh1 = mod_ref[:, 0:D_MODEL]
    sc1 = mod_ref[:, D_MODEL:2 * D_MODEL]
    g1 = mod_ref[:, 2 * D_MODEL:3 * D_MODEL]
    sh2 = mod_ref[:, 3 * D_MODEL:4 * D_MODEL]
    sc2 = mod_ref[:, 4 * D_MODEL:5 * D_MODEL]
    h = (x * (1.0 + sc1) + sh1).astype(BF16)
    z = _dot(h, w_in_ref[...])
    u = z[:, 0:A_WIDTH]
    v = z[:, A_WIDTH:2 * A_WIDTH]
    gate_b = z[:, 2 * A_WIDTH:2 * A_WIDTH + B_WIDTH]
    gate_c = z[:, 2 * A_WIDTH + B_WIDTH:2 * A_WIDTH + 2 * B_WIDTH]
    hb = z[:, 2 * A_WIDTH + 2 * B_WIDTH:]

    vn_parts = []
    for hd in range(A_HEADS):
        sl = slice(hd * A_HEAD_DIM, (hd + 1) * A_HEAD_DIM)
        vn_parts.append(_layer_norm(v[:, sl], sgu_g_ref[hd:hd + 1, :], sgu_bln_ref[hd:hd + 1, :]))
    vn = jnp.concatenate(vn_parts, axis=-1)
    vn_ref[...] = vn
    a_out = u * (vn * sgu_w00_ref[...] + sgu_b0_ref[...])

    q = gate_c * hb
    q_ref[...] = q
    conv = (conv_b_ref[...] + prev0_ref[...] * conv_w_ref[0:1, :] + prev1_ref[...] * conv_w_ref[1:2, :]
            + q * conv_w_ref[2:3, :])
    b_out = gate_b * conv

    mix_in = jnp.concatenate([a_out, b_out], axis=-1).astype(BF16)
    mix = _dot(mix_in, w_out_ref[...])
    x1 = _layer_norm(alpha_res * x + g1 * mix, ln1_g_ref[...], ln1_b_ref[...])
    x1_ref[...] = x1
    h2 = x1 * (1.0 + sc2) + sh2
    h2_ref[...] = h2.astype(BF16)
    meta, cnt = _route(h2, router_w_ref[...], router_b_ref[...])
    meta_ref[...] = meta
    cnt_ref[...] = cnt.astype(jnp.int32)


def _mix_sample(x, mod, prev0, prev1, w_in_bf, sgu_g, sgu_bln, sgu_w00, sgu_b0, conv_w, conv_b,
                w_out_bf, ln1_g, ln1_b, router_wt2, router_b_col, alpha_res):
    n = x.shape[0]
    return pl.pallas_call(
        functools.partial(_mix_sample_kernel, alpha_res),
        out_shape=[
            jax.ShapeDtypeStruct((n, D_MODEL), F32),
            jax.ShapeDtypeStruct((n, D_MODEL), BF16),
            jax.ShapeDtypeStruct((SUBLANES, n), F32),
            jax.ShapeDtypeStruct((N_EXPERTS, LANES), jnp.int32),
            jax.ShapeDtypeStruct((n, B_WIDTH), F32),
            jax.ShapeDtypeStruct((n, A_WIDTH), F32),
        ],
        compiler_params=pltpu.CompilerParams(vmem_limit_bytes=VMEM_LIMIT),
        name="mix_sample",
    )(x, mod, prev0, prev1, w_in_bf, sgu_g, sgu_bln, sgu_w00, sgu_b0, conv_w, conv_b, w_out_bf,
      ln1_g, ln1_b, router_wt2, router_b_col)


def _strip_copy(src_ref, src_row, dst_ref, dst_row, n_rows, sem):
    @pl.when(n_rows > 0)
    def _():
        pltpu.make_async_copy(src_ref.at[pl.ds(src_row, n_rows)],
                              dst_ref.at[pl.ds(dst_row, n_rows)], sem).start()


def _wait_rows(hbm_ref, vmem_ref, n_rows, sem):
    pltpu.make_async_copy(hbm_ref.at[pl.ds(0, n_rows)], vmem_ref.at[pl.ds(0, n_rows)], sem).wait()


def _sort_kernel(n_tok, tile0, n_steps, subs, first, *refs):
    if first:
        (sorted_row_ref, off_ref, cnt_ref, pad_row_ref, pad_n_ref, h2_ref, meta_ref,
         xs_ref, stage_ref, zero_ref, sem, pad_sem) = refs
    else:
        (sorted_row_ref, off_ref, cnt_ref, pad_row_ref, pad_n_ref, h2_ref, meta_ref, _,
         xs_ref, stage_ref, sem) = refs
    j = pl.program_id(0)
    n_sorted = n_tok * TOP_K
    parity = j % 2

    if first:
        @pl.when(j == 0)
        def _():
            zero_ref[...] = jnp.zeros_like(zero_ref)

            def start(e, carry):
                _strip_copy(zero_ref, 0, xs_ref, pad_row_ref[e], pad_n_ref[e], pad_sem)
                return carry
            lax.fori_loop(0, N_EXPERTS, start, 0)

            def wait(e, carry):
                @pl.when(pad_n_ref[e] > 0)
                def _():
                    _wait_rows(xs_ref, zero_ref, pad_n_ref[e], pad_sem)
                return carry
            lax.fori_loop(0, N_EXPERTS, wait, 0)

    sub_r = lax.broadcasted_iota(jnp.int32, (n_sorted, n_tok), 0)
    for sub in range(subs):
        slot = parity * subs + sub
        meta = meta_ref[sub]
        sel = sub_r == meta[0:1, :].astype(jnp.int32)
        for k in range(1, TOP_K):
            sel = jnp.logical_or(sel, sub_r == meta[k:k + 1, :].astype(jnp.int32))
        perm = jnp.where(sel, 1.0, 0.0).astype(BF16)
        rows = _dot(perm, h2_ref[sub * n_tok:(sub + 1) * n_tok, :])
        _store_rows(stage_ref, slot, rows)

        base = (tile0 + j * subs + sub) * N_EXPERTS

        def start(e, carry, slot=slot, base=base):
            _strip_copy(stage_ref.at[slot], off_ref[base + e], xs_ref, sorted_row_ref[base + e],
                        cnt_ref[base + e], sem.at[slot])
            return carry
        lax.fori_loop(0, N_EXPERTS, start, 0)

    @pl.when(j > 0)
    def _():
        for sub in range(subs):
            _wait_rows(xs_ref, stage_ref.at[0], n_sorted, sem.at[(1 - parity) * subs + sub])

    @pl.when(j == n_steps - 1)
    def _():
        for sub in range(subs):
            _wait_rows(xs_ref, stage_ref.at[0], n_sorted, sem.at[parity * subs + sub])


def _sort(tables, pad_row, pad_n, h2, meta, xs, n_sorted_rows, n_tok, tile0, subs):
    sorted_row, off, cnt = tables
    first = xs is None
    n_steps = h2.shape[0] // (n_tok * subs)
    in_specs = [pl.BlockSpec((subs * n_tok, D_MODEL), lambda j, *_: (j, 0)),
                pl.BlockSpec((subs, SUBLANES, n_tok), lambda j, *_: (j, 0, 0))]
    operands = [sorted_row, off, cnt, pad_row, pad_n, h2, meta]
    scratch = [pltpu.VMEM((2 * subs, n_tok * TOP_K, ROW_SLABS, LANES), BF16)]
    if first:
        scratch.append(pltpu.VMEM((ROW_TILE, ROW_SLABS, LANES), BF16))
        aliases = {}
    else:
        in_specs.append(pl.BlockSpec(memory_space=pl.ANY))
        operands.append(xs)
        aliases = {len(operands) - 1: 0}
    scratch.append(pltpu.SemaphoreType.DMA((2 * subs,)))
    if first:
        scratch.append(pltpu.SemaphoreType.DMA(()))
    grid_spec = pltpu.PrefetchScalarGridSpec(
        num_scalar_prefetch=5,
        grid=(n_steps,),
        in_specs=in_specs,
        out_specs=pl.BlockSpec(memory_space=pl.ANY),
        scratch_shapes=scratch,
    )
    return pl.pallas_call(
        functools.partial(_sort_kernel, n_tok, tile0, n_steps, subs, first),
        grid_spec=grid_spec,
        out_shape=jax.ShapeDtypeStruct((n_sorted_rows, ROW_SLABS, LANES), BF16),
        input_output_aliases=aliases,
        compiler_params=pltpu.CompilerParams(
            dimension_semantics=("arbitrary",), vmem_limit_bytes=VMEM_LIMIT),
        name="sort_first" if first else "sort_more",
    )(*operands)


def _experts_kernel(tile_start_ref, n_tile_ref, n_used_ref, xs_ref, w_gu_ref, b_gu_ref, w_down_ref,
                    b_down_ref, ys_ref, w_gu_bf_ref, w_down_bf_ref, x_buf, y_buf, x_sem, y_sem):
    e = pl.program_id(0)
    n_used = n_used_ref[0]
    first_tile = tile_start_ref[e]
    n_tile = n_tile_ref[e]

    def x_copy(g):
        slot = g % TILE_SLOTS
        return pltpu.make_async_copy(xs_ref.at[pl.ds(g * ROW_TILE, ROW_TILE)], x_buf.at[slot],
                                     x_sem.at[slot])

    def y_copy(g):
        slot = g % TILE_SLOTS
        return pltpu.make_async_copy(y_buf.at[slot], ys_ref.at[pl.ds(g * ROW_TILE, ROW_TILE)],
                                     y_sem.at[slot])

    def request(g):
        @pl.when(g < n_used)
        def _():
            x_copy(g).start(priority=1)

    @pl.when(e == 0)
    def _():
        for g in range(TILE_AHEAD):
            request(g)

    @pl.when(n_tile > 0)
    def _():
        w_gu_bf_ref[...] = w_gu_ref[...].astype(BF16)
        w_down_bf_ref[...] = w_down_ref[...].astype(BF16)

    def begin(g):
        x_copy(g).wait()
        request(g + TILE_AHEAD)

        @pl.when(g >= TILE_SLOTS)
        def _():
            y_copy(g - TILE_SLOTS).wait()

    def compute(g):
        slot = g % TILE_SLOTS
        x = _load_rows(x_buf, slot)
        gu = _dot(x, w_gu_bf_ref[...]) + b_gu_ref[...]
        gate = jnp.minimum(gu[:, :D_EXPERT], SWIGLU_LIMIT)
        up = jnp.clip(gu[:, D_EXPERT:], -SWIGLU_LIMIT, SWIGLU_LIMIT)
        act = (up + 1.0) * gate * jax.nn.sigmoid(SWIGLU_ALPHA * gate)
        y = _dot(act.astype(BF16), w_down_bf_ref[...]) + b_down_ref[...]
        _store_rows(y_buf, slot, y)
        y_copy(g).start(priority=1)

    def pair_body(p, carry):
        g = first_tile + 2 * p
        begin(g)
        begin(g + 1)
        compute(g)
        compute(g + 1)
        return carry

    lax.fori_loop(0, n_tile // 2, pair_body, 0)

    @pl.when(n_tile % 2 == 1)
    def _():
        g = first_tile + n_tile - 1
        begin(g)
        compute(g)

    @pl.when(e == N_EXPERTS - 1)
    def _():
        for back in range(TILE_SLOTS, 0, -1):
            @pl.when(n_used >= back)
            def _(back=back):
                y_copy(n_used - back).wait()


def _experts(tile_start, n_tile_e, n_used, xs, w_gu, b_gu, w_down, b_down):
    w_blk = lambda e, *_: (e, 0, 0)
    grid_spec = pltpu.PrefetchScalarGridSpec(
        num_scalar_prefetch=3,
        grid=(N_EXPERTS,),
        in_specs=[
            pl.BlockSpec(memory_space=pl.ANY),
            pl.BlockSpec((None, D_MODEL, 2 * D_EXPERT), w_blk),
            pl.BlockSpec((None, 1, 2 * D_EXPERT), w_blk),
            pl.BlockSpec((None, D_EXPERT, D_MODEL), w_blk),
            pl.BlockSpec((None, 1, D_MODEL), w_blk),
        ],
        out_specs=pl.BlockSpec(memory_space=pl.ANY),
        scratch_shapes=[pltpu.VMEM((D_MODEL, 2 * D_EXPERT), BF16),
                        pltpu.VMEM((D_EXPERT, D_MODEL), BF16),
                        pltpu.VMEM((TILE_SLOTS, ROW_TILE, ROW_SLABS, LANES), BF16),
                        pltpu.VMEM((TILE_SLOTS, ROW_TILE, ROW_SLABS, LANES), BF16),
                        pltpu.SemaphoreType.DMA((TILE_SLOTS,)),
                        pltpu.SemaphoreType.DMA((TILE_SLOTS,))],
    )
    return pl.pallas_call(
        _experts_kernel,
        grid_spec=grid_spec,
        out_shape=jax.ShapeDtypeStruct(xs.shape, BF16),
        compiler_params=pltpu.CompilerParams(
            dimension_semantics=("arbitrary",), vmem_limit_bytes=VMEM_LIMIT),
        name="experts",
    )(tile_start, n_tile_e, n_used, xs, w_gu, b_gu.reshape(N_EXPERTS, 1, -1), w_down,
      b_down.reshape(N_EXPERTS, 1, -1))


def _combine_kernel(n_tok, tile0, n_steps, subs, mod_rows, alpha_res, sorted_row_ref, off_ref, cnt_ref,
                    ys_ref, x1_ref, meta_ref, mod_ref, ln2_g_ref, ln2_b_ref, out_ref,
                    stage_ref, sem):
    j = pl.program_id(0)
    n_sorted = n_tok * TOP_K
    parity = j % 2

    def start_step(step, step_parity):
        for sub in range(subs):
            base = (tile0 + step * subs + sub) * N_EXPERTS
            to_slot = step_parity * subs + sub

            def start(e, carry, base=base, to_slot=to_slot):
                _strip_copy(ys_ref, sorted_row_ref[base + e], stage_ref.at[to_slot],
                            off_ref[base + e], cnt_ref[base + e], sem.at[to_slot])
                return carry
            lax.fori_loop(0, N_EXPERTS, start, 0)

    @pl.when(j == 0)
    def _():
        start_step(j, parity)

    @pl.when(j + 1 < n_steps)
    def _():
        start_step(j + 1, 1 - parity)

    sub_r = lax.broadcasted_iota(jnp.int32, (n_sorted, n_tok), 0)
    for sub in range(subs):
        slot = parity * subs + sub
        rows = slice(sub * n_tok, (sub + 1) * n_tok)
        meta = meta_ref[sub]
        comb_t = jnp.zeros((n_sorted, n_tok), F32)
        for k in range(TOP_K):
            comb_t = jnp.where(sub_r == meta[k:k + 1, :].astype(jnp.int32),
                               meta[TOP_K + k:TOP_K + k + 1, :], comb_t)
        comb_t = comb_t.astype(BF16)

        _wait_rows(ys_ref, stage_ref.at[0], n_sorted, sem.at[slot])
        ys = _load_rows(stage_ref, slot)
        ffn = lax.dot_general(comb_t, ys, (((0,), (0,)), ((), ())), preferred_element_type=F32)
        if mod_rows:
            g2 = mod_ref[rows, 5 * D_MODEL:6 * D_MODEL]
        else:
            g2 = mod_ref[5:6, :]
        out_ref[rows, :] = _layer_norm(alpha_res * x1_ref[rows, :] + g2 * ffn, ln2_g_ref[...],
                                       ln2_b_ref[...])


def _combine(tables, ys, x1, meta, mod, ln2_g, ln2_b, n_tok, tile0, subs, alpha_res):
    sorted_row, off, cnt = tables
    step_tok = subs * n_tok
    n_steps = x1.shape[0] // step_tok
    mod_rows = mod.ndim == 2
    if mod_rows:
        mod_spec = pl.BlockSpec((step_tok, N_MOD * D_MODEL), lambda j, *_: (j, 0))
    else:
        steps_per_seq = n_steps // mod.shape[0]
        mod_spec = pl.BlockSpec((None, SUBLANES, D_MODEL), lambda j, *_: (j // steps_per_seq, 0, 0))
    grid_spec = pltpu.PrefetchScalarGridSpec(
        num_scalar_prefetch=3,
        grid=(n_steps,),
        in_specs=[
            pl.BlockSpec(memory_space=pl.ANY),
            pl.BlockSpec((step_tok, D_MODEL), lambda j, *_: (j, 0)),
            pl.BlockSpec((subs, SUBLANES, n_tok), lambda j, *_: (j, 0, 0)),
            mod_spec,
            pl.BlockSpec((1, D_MODEL), lambda j, *_: (0, 0)),
            pl.BlockSpec((1, D_MODEL), lambda j, *_: (0, 0)),
        ],
        out_specs=pl.BlockSpec((step_tok, D_MODEL), lambda j, *_: (j, 0)),
        scratch_shapes=[pltpu.VMEM((2 * subs, n_tok * TOP_K, ROW_SLABS, LANES), BF16),
                        pltpu.SemaphoreType.DMA((2 * subs,))],
    )
    return pl.pallas_call(
        functools.partial(_combine_kernel, n_tok, tile0, n_steps, subs, mod_rows, alpha_res),
        grid_spec=grid_spec,
        out_shape=jax.ShapeDtypeStruct(x1.shape, F32),
        compiler_params=pltpu.CompilerParams(
            dimension_semantics=("arbitrary",), vmem_limit_bytes=VMEM_LIMIT),
        name="combine",
    )(sorted_row, off, cnt, ys, x1, meta, mod, ln2_g, ln2_b)


def _routing_tables(cnt_all):
    total = jnp.sum(cnt_all, axis=0)
    n_tile_e = (total + ROW_TILE - 1) // ROW_TILE
    tile_end = jnp.cumsum(n_tile_e)
    tile_start = tile_end - n_tile_e
    row_start = tile_start * ROW_TILE
    cum = jnp.cumsum(cnt_all, axis=0) - cnt_all
    off = jnp.cumsum(cnt_all, axis=1) - cnt_all
    sorted_row = row_start[None, :] + cum
    n_used = tile_end[-1]
    pad_row = row_start + total
    pad_n = n_tile_e * ROW_TILE - total
    i32 = lambda a: a.astype(jnp.int32)
    return ((i32(sorted_row).reshape(-1), i32(off).reshape(-1), i32(cnt_all).reshape(-1)),
            i32(pad_row), i32(pad_n), i32(tile_start), i32(n_tile_e), i32(n_used).reshape(1))


def kernel(x_prompt, x_sample, state_conv, c_prompt, c_sample, ada_w, ada_b, w_in, sgu_ln_g, sgu_ln_b,
           sgu_w, sgu_b, conv_w, conv_b, w_out, ln1_g, ln1_b, router_w, router_b, w_gu, b_gu,
           w_down, b_down, ln2_g, ln2_b):
    depth = ada_w.shape[0]
    assert depth == 1
    bsz, seq, _ = x_prompt.shape
    n_dec = x_sample.shape[0]
    assert x_sample.shape[1] == 1 and seq % TOK_TILE == 0
    alpha_res = (2.0 * depth) ** 0.25
    l = 0

    mod_s, mod_p = _ada(jnp.concatenate([c_sample, c_prompt], axis=0), n_dec, ada_w[l], ada_b[l])
    mod_p = jnp.pad(mod_p.reshape(bsz, N_MOD, D_MODEL), ((0, 0), (0, SUBLANES - N_MOD), (0, 0)))

    router_wt =jnp.transpose(router_w[l])
    router_wt_hi = router_wt.astype(BF16)
    router_wt_lo = (router_wt - router_wt_hi.astype(F32)).astype(BF16)
    router_wt2 = jnp.concatenate([router_wt_hi, router_wt_lo], axis=0)
    router_b_col = router_b[l].reshape(N_EXPERTS, 1)
    row = lambda a: a.reshape(1, -1)

    x1_p, h2_p, meta_p, cnt_p, convst_p, w_in_bf, w_out_bf = _mix_prompt(
        x_prompt, mod_p, w_in[l], sgu_ln_g[l], sgu_ln_b[l], sgu_w[l], jnp.transpose(sgu_b[l]),
        conv_w[l], row(conv_b[l]), w_out[l], row(ln1_g[l]), row(ln1_b[l]), router_wt2,
        router_b_col, alpha_res)
    x1_s, h2_s, meta_s, cnt_s, q_s, vn_s = _mix_sample(
        x_sample.reshape(n_dec, D_MODEL), mod_s, state_conv[l, :, 0, :], state_conv[l, :, 1, :],
        w_in_bf, sgu_ln_g[l], sgu_ln_b[l], row(jnp.repeat(sgu_w[l, :, 0, 0], A_HEAD_DIM)),
        row(jnp.repeat(sgu_b[l, :, 0], A_HEAD_DIM)), conv_w[l], row(conv_b[l]), w_out_bf,
        row(ln1_g[l]), row(ln1_b[l]), router_wt2, router_b_col, alpha_res)

    n_ptiles = bsz * seq // TOK_TILE
    cnt_all = jnp.concatenate([cnt_p[:, :, 0], cnt_s[None, :, 0]], axis=0)
    n_assign = (bsz * seq + n_dec) * TOP_K
    n_row_tiles = -(-n_assign // ROW_TILE) + N_EXPERTS
    tables, pad_row, pad_n, tile_start, n_tile_e, n_used = _routing_tables(cnt_all)

    n_sorted_rows = n_row_tiles * ROW_TILE
    meta_s = meta_s[None]
    xs = _sort(tables, pad_row, pad_n, h2_p, meta_p, None, n_sorted_rows, TOK_TILE, 0, MOE_SUB)
    xs = _sort(tables, pad_row, pad_n, h2_s, meta_s, xs, n_sorted_rows, n_dec, n_ptiles, 1)
    ys = _experts(tile_start, n_tile_e, n_used, xs, w_gu[l], b_gu[l], w_down[l], b_down[l])
    y_p = _combine(tables, ys, x1_p, meta_p, mod_p, row(ln2_g[l]), row(ln2_b[l]), TOK_TILE, 0,
                   MOE_SUB, alpha_res)
    y_s = _combine(tables, ys, x1_s, meta_s, mod_s, row(ln2_g[l]), row(ln2_b[l]), n_dec, n_ptiles,
                   1, alpha_res)

    conv_state_sample = jnp.stack([state_conv[l, :, 1, :], q_s], axis=1)[None]
    return (y_p.reshape(bsz, seq, D_MODEL),
            y_s.reshape(n_dec, 1, D_MODEL),
            convst_p[None],
            conv_state_sample,
            vn_s.reshape(1, n_dec, 1, A_HEADS, A_HEAD_DIM))
```

```python
import functools

import jax
import jax.numpy as jnp
from jax import lax
from jax.experimental import pallas as pl
from jax.experimental.pallas import tpu as pltpu

F32 = jnp.float32
BF16 = jnp.bfloat16

D_MODEL = 1024
A_WIDTH = 512
B_WIDTH = 512
A_HEADS = 4
A_HEAD_DIM = 128
CHUNK = 128
PROJ_COLS = 2 * A_WIDTH + 3 * B_WIDTH
N_EXPERTS = 32
TOP_K = 4
D_EXPERT = 1024
SWIGLU_LIMIT = 7.0
SWIGLU_ALPHA = 1.702
LN_EPS = 1e-5
N_MOD = 6

LANES = 128
SUBLANES = 8
ROW_SLABS = D_MODEL // LANES
TOK_TILE = 256
MIX_SUB = 2
MOE_SUB = 2
ROW_TILE = 256
TILE_GROUP = 4
TILE_AHEAD = TILE_GROUP
TILE_SLOTS = 2 * TILE_GROUP
VMEM_LIMIT = 56 * 1024 * 1024


def _layer_norm(x, g, b):
    mu = jnp.mean(x, axis=-1, keepdims=True)
    xc = x - mu
    var = jnp.mean(xc * xc, axis=-1, keepdims=True)
    return xc * lax.rsqrt(var + LN_EPS) * g + b


def _dot(a, b):
    return jnp.dot(a, b, preferred_element_type=F32)


def _dot_nt(a, b):
    return lax.dot_general(a, b, (((1,), (1,)), ((), ())), preferred_element_type=F32)


def _store_rows(ref, slot, rows):
    n = rows.shape[0]
    ref[slot] = rows.astype(BF16).reshape(n, ROW_SLABS, LANES)


def _load_rows(ref, slot):
    return ref[slot].reshape(ref.shape[1], D_MODEL)


def _split_bf16(a):
    hi = a.astype(BF16)
    lo = (a - hi.astype(F32)).astype(BF16)
    return hi, lo


def _ada_kernel(n_first, c_ref, w_ref, b_ref, o_first_ref, o_rest_ref):
    c = c_ref[...]
    s_hi, s_lo = _split_bf16(c * jax.nn.sigmoid(c))
    w_hi, w_lo = _split_bf16(w_ref[...])
    m = _dot(s_hi, w_hi) + _dot(s_hi, w_lo) + _dot(s_lo, w_hi) + b_ref[...]
    o_first_ref[...] = m[:n_first]
    o_rest_ref[...] = m[n_first:]


def _ada(c_all, n_first, ada_w, ada_b):
    rows = c_all.shape[0]
    cols = D_MODEL // 2
    return pl.pallas_call(
        functools.partial(_ada_kernel, n_first),
        grid=(N_MOD * D_MODEL // cols,),
        in_specs=[
            pl.BlockSpec((rows, D_MODEL), lambda n: (0, 0)),
            pl.BlockSpec((D_MODEL, cols), lambda n: (0, n)),
            pl.BlockSpec((1, cols), lambda n: (0, n)),
        ],
        out_specs=[pl.BlockSpec((n_first, cols), lambda n: (0, n)),
                   pl.BlockSpec((rows - n_first, cols), lambda n: (0, n))],
        out_shape=[jax.ShapeDtypeStruct((n_first, N_MOD * D_MODEL), F32),
                   jax.ShapeDtypeStruct((rows - n_first, N_MOD * D_MODEL), F32)],
        name="ada",
    )(c_all, ada_w, ada_b.reshape(1, -1))


def _route(h2, router_wt2, router_b):
    n = h2.shape[0]
    h_hi = h2.astype(BF16)
    h_lo = (h2 - h_hi.astype(F32)).astype(BF16)
    l1 = _dot_nt(router_wt2, h_hi)
    l2 = _dot_nt(router_wt2[:N_EXPERTS], h_lo)
    logits = l1[:N_EXPERTS] + l1[N_EXPERTS:] + l2 + router_b

    sub = lax.broadcasted_iota(jnp.int32, (N_EXPERTS, n), 0)
    work = logits
    vals, hots = [], []
    for _ in range(TOP_K):
        m = jnp.max(work, axis=0, keepdims=True)
        idx = jnp.min(jnp.where(work == m, sub, N_EXPERTS), axis=0, keepdims=True)
        hot = sub == idx
        work = jnp.where(hot, -jnp.inf, work)
        vals.append(m)
        hots.append(hot)
    exps = [jnp.exp(v - vals[0]) for v in vals]
    denom = exps[0] + exps[1] + exps[2] + exps[3]
    gates = [e / denom for e in exps]

    onehot = jnp.zeros((N_EXPERTS, n), F32)
    for hot in hots:
        onehot = onehot + hot.astype(F32)
    onehot_bf = onehot.astype(BF16)
    t_r = lax.broadcasted_iota(jnp.int32, (n, n), 0)
    t_c = lax.broadcasted_iota(jnp.int32, (n, n), 1)
    rank = _dot(onehot_bf, (t_r < t_c).astype(BF16))
    cnt = jnp.sum(onehot, axis=1, keepdims=True)
    e_r = lax.broadcasted_iota(jnp.int32, (N_EXPERTS, N_EXPERTS), 0)
    e_c = lax.broadcasted_iota(jnp.int32, (N_EXPERTS, N_EXPERTS), 1)
    below = _dot((e_c < e_r).astype(BF16), onehot_bf)
    off = jnp.sum(below, axis=1, keepdims=True)
    base = rank + off

    row8 = lax.broadcasted_iota(jnp.int32, (SUBLANES, n), 0)
    meta = jnp.zeros((SUBLANES, n), F32)
    for k in range(TOP_K):
        pos_k = jnp.sum(jnp.where(hots[k], base, 0.0), axis=0, keepdims=True)
        meta = jnp.where(row8 == k, pos_k, meta)
        meta = jnp.where(row8 == TOP_K + k, gates[k], meta)
    return meta, jnp.broadcast_to(cnt, (N_EXPERTS, LANES))


def _mix_prompt_kernel(alpha_res, x_ref, mod_ref, w_in_ref, sgu_g_ref, sgu_bln_ref, sgu_w_ref,
                       sgu_bias_ref, conv_w_ref, conv_b_ref, w_out_ref, ln1_g_ref, ln1_b_ref,
                       router_w_ref, router_b_ref,
                       x1_ref, h2_ref, meta_ref, cnt_ref, convst_ref, w_in_bf_ref, w_out_bf_ref,
                       carry_ref):
    t = pl.program_id(1)

    @pl.when(jnp.logical_and(pl.program_id(0) == 0, t == 0))
    def _():
        w_in_bf_ref[...] = w_in_ref[...].astype(BF16)
        w_out_bf_ref[...] = w_out_ref[...].astype(BF16)

    @pl.when(t == 0)
    def _():
        carry_ref[...] = jnp.zeros_like(carry_ref)

    mod = mod_ref[...]
    sh1, sc1, g1, sh2, sc2 = mod[0:1], mod[1:2], mod[2:3], mod[3:4], mod[4:5]
    r_i = lax.broadcasted_iota(jnp.int32, (CHUNK, CHUNK), 0)
    c_i = lax.broadcasted_iota(jnp.int32, (CHUNK, CHUNK), 1)
    tril = c_i <= r_i
    tm = TOK_TILE
    prev2 = carry_ref[0:1, :]
    prev1 = carry_ref[1:2, :]

    subs = range(MIX_SUB)
    tile_rows = [slice(sub * tm, (sub + 1) * tm) for sub in subs]
    xs_in = [x_ref[rows, :] for rows in tile_rows]
    zs = [_dot((x * (1.0 + sc1) + sh1).astype(BF16), w_in_bf_ref[...]) for x in xs_in]

    mix_ins = []
    for sub in subs:
        z = zs[sub]
        u = z[:, 0:A_WIDTH]
        v = z[:, A_WIDTH:2 * A_WIDTH]
        gate_b = z[:, 2 * A_WIDTH:2 * A_WIDTH + B_WIDTH]
        gate_c = z[:, 2 * A_WIDTH + B_WIDTH:2 * A_WIDTH + 2 * B_WIDTH]
        hb = z[:, 2 * A_WIDTH + 2 * B_WIDTH:]

        a_parts = []
        for hd in range(A_HEADS):
            sl = slice(hd * A_HEAD_DIM, (hd + 1) * A_HEAD_DIM)
            vn = _layer_norm(v[:, sl], sgu_g_ref[hd:hd + 1, :], sgu_bln_ref[hd:hd + 1, :]).astype(BF16)
            wm = jnp.where(tril, sgu_w_ref[hd], 0.0).astype(BF16)
            bias = sgu_bias_ref[:, hd:hd + 1]
            s_parts = []
            for c in range(tm // CHUNK):
                s_parts.append(_dot(wm, vn[c * CHUNK:(c + 1) * CHUNK, :]) + bias)
            a_parts.append(u[:, sl] * jnp.concatenate(s_parts, axis=0))

        q = gate_c * hb
        row = lax.broadcasted_iota(jnp.int32, q.shape, 0)
        q_m1 = jnp.where(row == 0, prev1, pltpu.roll(q, 1, 0))
        q_m2 = jnp.where(row == 0, prev2, jnp.where(row == 1, prev1, pltpu.roll(q, 2, 0)))
        conv = (conv_b_ref[...] + q_m2 * conv_w_ref[0:1, :] + q_m1 * conv_w_ref[1:2, :]
                + q * conv_w_ref[2:3, :])
        b_out = gate_b * conv
        prev2 = q[tm - 2:tm - 1, :]
        prev1 = q[tm - 1:tm, :]

        mix_ins.append(jnp.concatenate(a_parts + [b_out], axis=-1).astype(BF16))

    mixes = [_dot(mix_in, w_out_bf_ref[...]) for mix_in in mix_ins]
    h2s = []
    for sub in subs:
        x1 = _layer_norm(alpha_res * xs_in[sub] + g1 * mixes[sub], ln1_g_ref[...], ln1_b_ref[...])
        x1_ref[tile_rows[sub], :] = x1
        h2 = x1 * (1.0 + sc2) + sh2
        h2_ref[tile_rows[sub], :] = h2.astype(BF16)
        h2s.append(h2)
    for sub in subs:
        meta, cnt = _route(h2s[sub], router_w_ref[...], router_b_ref[...])
        meta_ref[sub] = meta
        cnt_ref[sub] = cnt.astype(jnp.int32)

    last2 = jnp.concatenate([prev2, prev1], axis=0)
    carry_ref[0:2, :] = last2
    convst_ref[...] = last2


def _mix_prompt(x, mod, w_in, sgu_g, sgu_bln, sgu_w, sgu_bias_t, conv_w, conv_b, w_out,
                ln1_g, ln1_b, router_wt2, router_b_col, alpha_res):
    bsz, seq, _ = x.shape
    step_tok = MIX_SUB * TOK_TILE
    steps = seq // step_tok
    n_tok = bsz * seq
    tiles = seq // TOK_TILE
    const2 = lambda b, t: (0, 0)
    tok = lambda b, t: (b * steps + t, 0)
    tile3 = lambda b, t: (b * steps + t, 0, 0)
    return pl.pallas_call(
        functools.partial(_mix_prompt_kernel, alpha_res),
        grid=(bsz, steps),
        in_specs=[
            pl.BlockSpec((None, step_tok, D_MODEL), lambda b, t: (b, t, 0)),
            pl.BlockSpec((None, SUBLANES, D_MODEL), lambda b, t: (b, 0, 0)),
            pl.BlockSpec((D_MODEL, PROJ_COLS), const2, pipeline_mode=pl.Buffered(1)),
            pl.BlockSpec((A_HEADS, A_HEAD_DIM), const2),
            pl.BlockSpec((A_HEADS, A_HEAD_DIM), const2),
            pl.BlockSpec((A_HEADS, CHUNK, CHUNK), lambda b, t: (0, 0, 0)),
            pl.BlockSpec((CHUNK, A_HEADS), const2),
            pl.BlockSpec((3, B_WIDTH), const2),
            pl.BlockSpec((1, B_WIDTH), const2),
            pl.BlockSpec((D_MODEL, D_MODEL), const2, pipeline_mode=pl.Buffered(1)),
            pl.BlockSpec((1, D_MODEL), const2),
            pl.BlockSpec((1, D_MODEL), const2),
            pl.BlockSpec((2 * N_EXPERTS, D_MODEL), const2),
            pl.BlockSpec((N_EXPERTS, 1), const2),
        ],
        out_specs=[
            pl.BlockSpec((step_tok, D_MODEL), tok),
            pl.BlockSpec((step_tok, D_MODEL), tok),
            pl.BlockSpec((MIX_SUB, SUBLANES, TOK_TILE), tile3),
            pl.BlockSpec((MIX_SUB, N_EXPERTS, LANES), tile3),
            pl.BlockSpec((None, 2, B_WIDTH), lambda b, t: (b, 0, 0)),
            pl.BlockSpec((D_MODEL, PROJ_COLS), const2),
            pl.BlockSpec((D_MODEL, D_MODEL), const2),
        ],
        out_shape=[
            jax.ShapeDtypeStruct((n_tok, D_MODEL), F32),
            jax.ShapeDtypeStruct((n_tok, D_MODEL), BF16),
            jax.ShapeDtypeStruct((bsz * tiles, SUBLANES, TOK_TILE), F32),
            jax.ShapeDtypeStruct((bsz * tiles, N_EXPERTS, LANES), jnp.int32),
            jax.ShapeDtypeStruct((bsz, 2, B_WIDTH), F32),
            jax.ShapeDtypeStruct((D_MODEL, PROJ_COLS), BF16),
            jax.ShapeDtypeStruct((D_MODEL, D_MODEL), BF16),
        ],
        scratch_shapes=[pltpu.VMEM((SUBLANES, B_WIDTH), F32)],
        compiler_params=pltpu.CompilerParams(
            dimension_semantics=("arbitrary", "arbitrary"), vmem_limit_bytes=VMEM_LIMIT),
        name="mix_prompt",
    )(x, mod, w_in, sgu_g, sgu_bln, sgu_w, sgu_bias_t, conv_w, conv_b, w_out,
      ln1_g, ln1_b, router_wt2, router_b_col)


def _mix_sample_kernel(alpha_res, x_ref, mod_ref, prev0_ref, prev1_ref, w_in_ref, sgu_g_ref,
                       sgu_bln_ref, sgu_w00_ref, sgu_b0_ref, conv_w_ref, conv_b_ref, w_out_ref,
                       ln1_g_ref, ln1_b_ref, router_w_ref, router_b_ref,
                       x1_ref, h2_ref, meta_ref, cnt_ref, q_ref, vn_ref):
    x = x_ref[...]
    sh1 = mod_ref[:, 0:D_MODEL]
    sc1 = mod_ref[:, D_MODEL:2 * D_MODEL]
    g1 = mod_ref[:, 2 * D_MODEL:3 * D_MODEL]
    sh2 = mod_ref[:, 3 * D_MODEL:4 * D_MODEL]
    sc2 = mod_ref[:, 4 * D_MODEL:5 * D_MODEL]
    h = (x * (1.0 + sc1) + sh1).astype(BF16)
    z = _dot(h, w_in_ref[...])
    u = z[:, 0:A_WIDTH]
    v = z[:, A_WIDTH:2 * A_WIDTH]
    gate_b = z[:, 2 * A_WIDTH:2 * A_WIDTH + B_WIDTH]
    gate_c = z[:, 2 * A_WIDTH + B_WIDTH:2 * A_WIDTH + 2 * B_WIDTH]
    hb = z[:, 2 * A_WIDTH + 2 * B_WIDTH:]

    vn_parts = []
    for hd in range(A_HEADS):
        sl = slice(hd * A_HEAD_DIM, (hd + 1) * A_HEAD_DIM)
        vn_parts.append(_layer_norm(v[:, sl], sgu_g_ref[hd:hd + 1, :], sgu_bln_ref[hd:hd + 1, :]))
    vn = jnp.concatenate(vn_parts, axis=-1)
    vn_ref[...] = vn
    a_out = u * (vn * sgu_w00_ref[...] + sgu_b0_ref[...])

    q = gate_c * hb
    q_ref[...] = q
    conv = (conv_b_ref[...] + prev0_ref[...] * conv_w_ref[0:1, :] + prev1_ref[...] * conv_w_ref[1:2, :]
            + q * conv_w_ref[2:3, :])
    b_out = gate_b * conv

    mix_in = jnp.concatenate([a_out, b_out], axis=-1).astype(BF16)
    mix = _dot(mix_in, w_out_ref[...])
    x1 = _layer_norm(alpha_res * x + g1 * mix, ln1_g_ref[...], ln1_b_ref[...])
    x1_ref[...] = x1
    h2 = x1 * (1.0 + sc2) + sh2
    h2_ref[...] = h2.astype(BF16)
    meta, cnt = _route(h2, router_w_ref[...], router_b_ref[...])
    meta_ref[...] = meta
    cnt_ref[...] = cnt.astype(jnp.int32)


def _mix_sample(x, mod, prev0, prev1, w_in_bf, sgu_g, sgu_bln, sgu_w00, sgu_b0, conv_w, conv_b,
                w_out_bf, ln1_g, ln1_b, router_wt2, router_b_col, alpha_res):
    n = x.shape[0]
    return pl.pallas_call(
        functools.partial(_mix_sample_kernel, alpha_res),
        out_shape=[
            jax.ShapeDtypeStruct((n, D_MODEL), F32),
            jax.ShapeDtypeStruct((n, D_MODEL), BF16),
            jax.ShapeDtypeStruct((SUBLANES, n), F32),
            jax.ShapeDtypeStruct((N_EXPERTS, LANES), jnp.int32),
            jax.ShapeDtypeStruct((n, B_WIDTH), F32),
            jax.ShapeDtypeStruct((n, A_WIDTH), F32),
        ],
        compiler_params=pltpu.CompilerParams(vmem_limit_bytes=VMEM_LIMIT),
        name="mix_sample",
    )(x, mod, prev0, prev1, w_in_bf, sgu_g, sgu_bln, sgu_w00, sgu_b0, conv_w, conv_b, w_out_bf,
      ln1_g, ln1_b, router_wt2, router_b_col)


def _strip_copy(src_ref, src_row, dst_ref, dst_row, n_rows, sem):
    @pl.when(n_rows > 0)
    def _():
        pltpu.make_async_copy(src_ref.at[pl.ds(src_row, n_rows)],
                              dst_ref.at[pl.ds(dst_row, n_rows)], sem).start()


def _wait_rows(hbm_ref, vmem_ref, n_rows, sem):
    pltpu.make_async_copy(hbm_ref.at[pl.ds(0, n_rows)], vmem_ref.at[pl.ds(0, n_rows)], sem).wait()


def _sort_kernel(n_tok, tile0, n_steps, subs, first, *refs):
    if first:
        (sorted_row_ref, off_ref, cnt_ref, pad_row_ref, pad_n_ref, h2_ref, meta_ref,
         xs_ref, stage_ref, zero_ref, sem, pad_sem) = refs
    else:
        (sorted_row_ref, off_ref, cnt_ref, pad_row_ref, pad_n_ref, h2_ref, meta_ref, _,
         xs_ref, stage_ref, sem) = refs
    j = pl.program_id(0)
    n_sorted = n_tok * TOP_K
    parity = j % 2

    if first:
        @pl.when(j == 0)
        def _():
            zero_ref[...] = jnp.zeros_like(zero_ref)

            def start(e, carry):
                _strip_copy(zero_ref, 0, xs_ref, pad_row_ref[e], pad_n_ref[e], pad_sem)
                return carry
            lax.fori_loop(0, N_EXPERTS, start, 0)

            def wait(e, carry):
                @pl.when(pad_n_ref[e] > 0)
                def _():
                    _wait_rows(xs_ref, zero_ref, pad_n_ref[e], pad_sem)
                return carry
            lax.fori_loop(0, N_EXPERTS, wait, 0)

    sub_r = lax.broadcasted_iota(jnp.int32, (n_sorted, n_tok), 0)
    for sub in range(subs):
        slot = parity * subs + sub
        meta = meta_ref[sub]
        sel = sub_r == meta[0:1, :].astype(jnp.int32)
        for k in range(1, TOP_K):
            sel = jnp.logical_or(sel, sub_r == meta[k:k + 1, :].astype(jnp.int32))
        perm = jnp.where(sel, 1.0, 0.0).astype(BF16)
        rows = _dot(perm, h2_ref[sub * n_tok:(sub + 1) * n_tok, :])
        _store_rows(stage_ref, slot, rows)

        base = (tile0 + j * subs + sub) * N_EXPERTS

        def start(e, carry, slot=slot, base=base):
            _strip_copy(stage_ref.at[slot], off_ref[base + e], xs_ref, sorted_row_ref[base + e],
                        cnt_ref[base + e], sem.at[slot])
            return carry
        lax.fori_loop(0, N_EXPERTS, start, 0)

    @pl.when(j > 0)
    def _():
        for sub in range(subs):
            _wait_rows(xs_ref, stage_ref.at[0], n_sorted, sem.at[(1 - parity) * subs + sub])

    @pl.when(j == n_steps - 1)
    def _():
        for sub in range(subs):
            _wait_rows(xs_ref, stage_ref.at[0], n_sorted, sem.at[parity * subs + sub])


def _sort(tables, pad_row, pad_n, h2, meta, xs, n_sorted_rows, n_tok, tile0, subs):
    sorted_row, off, cnt = tables
    first = xs is None
    n_steps = h2.shape[0] // (n_tok * subs)
    in_specs = [pl.BlockSpec((subs * n_tok, D_MODEL), lambda j, *_: (j, 0)),
                pl.BlockSpec((subs, SUBLANES, n_tok), lambda j, *_: (j, 0, 0))]
    operands = [sorted_row, off, cnt, pad_row, pad_n, h2, meta]
    scratch = [pltpu.VMEM((2 * subs, n_tok * TOP_K, ROW_SLABS, LANES), BF16)]
    if first:
        scratch.append(pltpu.VMEM((ROW_TILE, ROW_SLABS, LANES), BF16))
        aliases = {}
    else:
        in_specs.append(pl.BlockSpec(memory_space=pl.ANY))
        operands.append(xs)
        aliases = {len(operands) - 1: 0}
    scratch.append(pltpu.SemaphoreType.DMA((2 * subs,)))
    if first:
        scratch.append(pltpu.SemaphoreType.DMA(()))
    grid_spec = pltpu.PrefetchScalarGridSpec(
        num_scalar_prefetch=5,
        grid=(n_steps,),
        in_specs=in_specs,
        out_specs=pl.BlockSpec(memory_space=pl.ANY),
        scratch_shapes=scratch,
    )
    return pl.pallas_call(
        functools.partial(_sort_kernel, n_tok, tile0, n_steps, subs, first),
        grid_spec=grid_spec,
        out_shape=jax.ShapeDtypeStruct((n_sorted_rows, ROW_SLABS, LANES), BF16),
        input_output_aliases=aliases,
        compiler_params=pltpu.CompilerParams(
            dimension_semantics=("arbitrary",), vmem_limit_bytes=VMEM_LIMIT),
        name="sort_first" if first else "sort_more",
    )(*operands)


def _experts_kernel(tile_start_ref, n_tile_ref, n_used_ref, xs_ref, w_gu_ref, b_gu_ref, w_down_ref,
                    b_down_ref, ys_ref, w_gu_bf_ref, w_down_bf_ref, x_buf, y_buf, x_sem, y_sem):
    e = pl.program_id(0)
    n_used = n_used_ref[0]
    first_tile = tile_start_ref[e]
    n_tile = n_tile_ref[e]

    def x_copy(g):
        slot = g % TILE_SLOTS
        return pltpu.make_async_copy(xs_ref.at[pl.ds(g * ROW_TILE, ROW_TILE)], x_buf.at[slot],
                                     x_sem.at[slot])

    def y_copy(g):
        slot = g % TILE_SLOTS
        return pltpu.make_async_copy(y_buf.at[slot], ys_ref.at[pl.ds(g * ROW_TILE, ROW_TILE)],
                                     y_sem.at[slot])

    def request(g):
        @pl.when(g < n_used)
        def _():
            x_copy(g).start(priority=1)

    @pl.when(e == 0)
    def _():
        for g in range(TILE_AHEAD):
            request(g)

    @pl.when(n_tile > 0)
    def _():
        w_gu_bf_ref[...] = w_gu_ref[...].astype(BF16)
        w_down_bf_ref[...] = w_down_ref[...].astype(BF16)

    def begin(g):
        x_copy(g).wait()
        request(g + TILE_AHEAD)

        @pl.when(g >= TILE_SLOTS)
        def _():
            y_copy(g - TILE_SLOTS).wait()

    def compute(g):
        slot = g % TILE_SLOTS
        x = _load_rows(x_buf, slot)
        gu = _dot(x, w_gu_bf_ref[...]) + b_gu_ref[...]
        gate = jnp.minimum(gu[:, :D_EXPERT], SWIGLU_LIMIT)
        up = jnp.clip(gu[:, D_EXPERT:], -SWIGLU_LIMIT, SWIGLU_LIMIT)
        act = (up + 1.0) * gate * jax.nn.sigmoid(SWIGLU_ALPHA * gate)
        y = _dot(act.astype(BF16), w_down_bf_ref[...]) + b_down_ref[...]
        _store_rows(y_buf, slot, y)
        y_copy(g).start(priority=1)

    def group(g, size):
        for k in range(size):
            begin(g + k)
        for k in range(size):
            compute(g + k)

    def group_body(p, carry):
        group(first_tile + TILE_GROUP * p, TILE_GROUP)
        return carry

    n_group = n_tile // TILE_GROUP
    lax.fori_loop(0, n_group, group_body, 0)
    rest = n_tile - n_group * TILE_GROUP
    g_rest = first_tile + n_group * TILE_GROUP
    size = TILE_GROUP // 2
    while size >= 1:
        @pl.when((rest & size) != 0)
        def _(g_rest=g_rest, size=size):
            group(g_rest, size)
        g_rest = g_rest + (rest & size)
        size //= 2

    @pl.when(e == N_EXPERTS - 1)
    def _():
        for back in range(TILE_SLOTS, 0, -1):
            @pl.when(n_used >= back)
            def _(back=back):
                y_copy(n_used - back).wait()


def _experts(tile_start, n_tile_e, n_used, xs, w_gu, b_gu, w_down, b_down):
    w_blk = lambda e, *_: (e, 0, 0)
    grid_spec = pltpu.PrefetchScalarGridSpec(
        num_scalar_prefetch=3,
        grid=(N_EXPERTS,),
        in_specs=[
            pl.BlockSpec(memory_space=pl.ANY),
            pl.BlockSpec((None, D_MODEL, 2 * D_EXPERT), w_blk),
            pl.BlockSpec((None, 1, 2 * D_EXPERT), w_blk),
            pl.BlockSpec((None, D_EXPERT, D_MODEL), w_blk),
            pl.BlockSpec((None, 1, D_MODEL), w_blk),
        ],
        out_specs=pl.BlockSpec(memory_space=pl.ANY),
        scratch_shapes=[pltpu.VMEM((D_MODEL, 2 * D_EXPERT), BF16),
                        pltpu.VMEM((D_EXPERT, D_MODEL), BF16),
                        pltpu.VMEM((TILE_SLOTS, ROW_TILE, ROW_SLABS, LANES), BF16),
                        pltpu.VMEM((TILE_SLOTS, ROW_TILE, ROW_SLABS, LANES), BF16),
                        pltpu.SemaphoreType.DMA((TILE_SLOTS,)),
                        pltpu.SemaphoreType.DMA((TILE_SLOTS,))],
    )
    return pl.pallas_call(
        _experts_kernel,
        grid_spec=grid_spec,
        out_shape=jax.ShapeDtypeStruct(xs.shape, BF16),
        compiler_params=pltpu.CompilerParams(
            dimension_semantics=("arbitrary",), vmem_limit_bytes=VMEM_LIMIT),
        name="experts",
    )(tile_start, n_tile_e, n_used, xs, w_gu, b_gu.reshape(N_EXPERTS, 1, -1), w_down,
      b_down.reshape(N_EXPERTS, 1, -1))


def _combine_kernel(n_tok, tile0, n_steps, subs, mod_rows, alpha_res, sorted_row_ref, off_ref, cnt_ref,
                    ys_ref, x1_ref, meta_ref, mod_ref, ln2_g_ref, ln2_b_ref, out_ref,
                    stage_ref, sem):
    j = pl.program_id(0)
    n_sorted = n_tok * TOP_K
    parity = j % 2

    def start_step(step, step_parity):
        for sub in range(subs):
            base = (tile0 + step * subs + sub) * N_EXPERTS
            to_slot = step_parity * subs + sub

            def start(e, carry, base=base, to_slot=to_slot):
                _strip_copy(ys_ref, sorted_row_ref[base + e], stage_ref.at[to_slot],
                            off_ref[base + e], cnt_ref[base + e], sem.at[to_slot])
                return carry
            lax.fori_loop(0, N_EXPERTS, start, 0)

    @pl.when(j == 0)
    def _():
        start_step(j, parity)

    @pl.when(j + 1 < n_steps)
    def _():
        start_step(j + 1, 1 - parity)

    sub_r = lax.broadcasted_iota(jnp.int32, (n_sorted, n_tok), 0)
    for sub in range(subs):
        slot = parity * subs + sub
        rows = slice(sub * n_tok, (sub + 1) * n_tok)
        meta = meta_ref[sub]
        comb_t = jnp.zeros((n_sorted, n_tok), F32)
        for k in range(TOP_K):
            comb_t = jnp.where(sub_r == meta[k:k + 1, :].astype(jnp.int32),
                               meta[TOP_K + k:TOP_K + k + 1, :], comb_t)
        comb_t = comb_t.astype(BF16)

        _wait_rows(ys_ref, stage_ref.at[0], n_sorted, sem.at[slot])
        ys = _load_rows(stage_ref, slot)
        ffn = lax.dot_general(comb_t, ys, (((0,), (0,)), ((), ())), preferred_element_type=F32)
        if mod_rows:
            g2 = mod_ref[rows, 5 * D_MODEL:6 * D_MODEL]
        else:
            g2 = mod_ref[5:6, :]
        out_ref[rows, :] = _layer_norm(alpha_res * x1_ref[rows, :] + g2 * ffn, ln2_g_ref[...],
                                       ln2_b_ref[...])


def _combine(tables, ys, x1, meta, mod, ln2_g, ln2_b, n_tok, tile0, subs, alpha_res):
    sorted_row, off, cnt = tables
    step_tok = subs * n_tok
    n_steps = x1.shape[0] // step_tok
    mod_rows = mod.ndim == 2
    if mod_rows:
        mod_spec = pl.BlockSpec((step_tok, N_MOD * D_MODEL), lambda j, *_: (j, 0))
    else:
        steps_per_seq = n_steps // mod.shape[0]
        mod_spec = pl.BlockSpec((None, SUBLANES, D_MODEL), lambda j, *_: (j // steps_per_seq, 0, 0))
    grid_spec = pltpu.PrefetchScalarGridSpec(
        num_scalar_prefetch=3,
        grid=(n_steps,),
        in_specs=[
            pl.BlockSpec(memory_space=pl.ANY),
            pl.BlockSpec((step_tok, D_MODEL), lambda j, *_: (j, 0)),
            pl.BlockSpec((subs, SUBLANES, n_tok), lambda j, *_: (j, 0, 0)),
            mod_spec,
            pl.BlockSpec((1, D_MODEL), lambda j, *_: (0, 0)),
            pl.BlockSpec((1, D_MODEL), lambda j, *_: (0, 0)),
        ],
        out_specs=pl.BlockSpec((step_tok, D_MODEL), lambda j, *_: (j, 0)),
        scratch_shapes=[pltpu.VMEM((2 * subs, n_tok * TOP_K, ROW_SLABS, LANES), BF16),
                        pltpu.SemaphoreType.DMA((2 * subs,))],
    )
    return pl.pallas_call(
        functools.partial(_combine_kernel, n_tok, tile0, n_steps, subs, mod_rows, alpha_res),
        grid_spec=grid_spec,
        out_shape=jax.ShapeDtypeStruct(x1.shape, F32),
        compiler_params=pltpu.CompilerParams(
            dimension_semantics=("arbitrary",), vmem_limit_bytes=VMEM_LIMIT),
        name="combine",
    )(sorted_row, off, cnt, ys, x1, meta, mod, ln2_g, ln2_b)


def _routing_tables(cnt_all):
    total = jnp.sum(cnt_all, axis=0)
    n_tile_e = (total + ROW_TILE - 1) // ROW_TILE
    tile_end = jnp.cumsum(n_tile_e)
    tile_start = tile_end - n_tile_e
    row_start = tile_start * ROW_TILE
    cum = jnp.cumsum(cnt_all, axis=0) - cnt_all
    off = jnp.cumsum(cnt_all, axis=1) - cnt_all
    sorted_row = row_start[None, :] + cum
    n_used = tile_end[-1]
    pad_row = row_start + total
    pad_n = n_tile_e * ROW_TILE - total
    i32 = lambda a: a.astype(jnp.int32)
    return ((i32(sorted_row).reshape(-1), i32(off).reshape(-1), i32(cnt_all).reshape(-1)),
            i32(pad_row), i32(pad_n), i32(tile_start), i32(n_tile_e), i32(n_used).reshape(1))


def kernel(x_prompt, x_sample, state_conv, c_prompt, c_sample, ada_w, ada_b, w_in, sgu_ln_g, sgu_ln_b,
           sgu_w, sgu_b, conv_w, conv_b, w_out, ln1_g, ln1_b, router_w, router_b, w_gu, b_gu,
           w_down, b_down, ln2_g, ln2_b):
    depth = ada_w.shape[0]
    assert depth == 1
    bsz, seq, _ = x_prompt.shape
    n_dec = x_sample.shape[0]
    assert x_sample.shape[1] == 1 and seq % TOK_TILE == 0
    alpha_res = (2.0 * depth) ** 0.25
    l = 0

    mod_s, mod_p = _ada(jnp.concatenate([c_sample, c_prompt], axis=0), n_dec, ada_w[l], ada_b[l])
    mod_p = jnp.pad(mod_p.reshape(bsz, N_MOD, D_MODEL), ((0, 0), (0, SUBLANES - N_MOD), (0, 0)))

    router_wt =jnp.transpose(router_w[l])
    router_wt_hi = router_wt.astype(BF16)
    router_wt_lo = (router_wt - router_wt_hi.astype(F32)).astype(BF16)
    router_wt2 = jnp.concatenate([router_wt_hi, router_wt_lo], axis=0)
    router_b_col = router_b[l].reshape(N_EXPERTS, 1)
    row = lambda a: a.reshape(1, -1)

    x1_p, h2_p, meta_p, cnt_p, convst_p, w_in_bf, w_out_bf = _mix_prompt(
        x_prompt, mod_p, w_in[l], sgu_ln_g[l], sgu_ln_b[l], sgu_w[l], jnp.transpose(sgu_b[l]),
        conv_w[l], row(conv_b[l]), w_out[l], row(ln1_g[l]), row(ln1_b[l]), router_wt2,
        router_b_col, alpha_res)
    x1_s, h2_s, meta_s, cnt_s, q_s, vn_s = _mix_sample(
        x_sample.reshape(n_dec, D_MODEL), mod_s, state_conv[l, :, 0, :], state_conv[l, :, 1, :],
        w_in_bf, sgu_ln_g[l], sgu_ln_b[l], row(jnp.repeat(sgu_w[l, :, 0, 0], A_HEAD_DIM)),
        row(jnp.repeat(sgu_b[l, :, 0], A_HEAD_DIM)), conv_w[l], row(conv_b[l]), w_out_bf,
        row(ln1_g[l]), row(ln1_b[l]), router_wt2, router_b_col, alpha_res)

    n_ptiles = bsz * seq // TOK_TILE
    cnt_all = jnp.concatenate([cnt_p[:, :, 0], cnt_s[None, :, 0]], axis=0)
    n_assign = (bsz * seq + n_dec) * TOP_K
    n_row_tiles = -(-n_assign // ROW_TILE) + N_EXPERTS
    tables, pad_row, pad_n, tile_start, n_tile_e, n_used = _routing_tables(cnt_all)

    n_sorted_rows = n_row_tiles * ROW_TILE
    meta_s = meta_s[None]
    xs = _sort(tables, pad_row, pad_n, h2_p, meta_p, None, n_sorted_rows, TOK_TILE, 0, MOE_SUB)
    xs = _sort(tables, pad_row, pad_n, h2_s, meta_s, xs, n_sorted_rows, n_dec, n_ptiles, 1)
    ys = _experts(tile_start, n_tile_e, n_used, xs, w_gu[l], b_gu[l], w_down[l], b_down[l])
    y_p = _combine(tables, ys, x1_p, meta_p, mod_p, row(ln2_g[l]), row(ln2_b[l]), TOK_TILE, 0,
                   MOE_SUB, alpha_res)
    y_s = _combine(tables, ys, x1_s, meta_s, mod_s, row(ln2_g[l]), row(ln2_b[l]), n_dec, n_ptiles,
                   1, alpha_res)

    conv_state_sample = jnp.stack([state_conv[l, :, 1, :], q_s], axis=1)[None]
    return (y_p.reshape(bsz, seq, D_MODEL),
            y_s.reshape(n_dec, 1, D_MODEL),
            convst_p[None],
            conv_state_sample,
            vn_s.reshape(1, n_dec, 1, A_HEADS, A_HEAD_DIM))
```

```python
import functools

import jax
import jax.numpy as jnp
from jax import lax
from jax.experimental import pallas as pl
from jax.experimental.pallas import tpu as pltpu

F32 = jnp.float32
BF16 = jnp.bfloat16

D_MODEL = 1024
A_WIDTH = 512
B_WIDTH = 512
A_HEADS = 4
A_HEAD_DIM = 128
CHUNK = 128
PROJ_COLS = 2 * A_WIDTH + 3 * B_WIDTH
N_EXPERTS = 32
TOP_K = 4
D_EXPERT = 1024
SWIGLU_LIMIT = 7.0
SWIGLU_ALPHA = 1.702
LN_EPS = 1e-5
N_MOD = 6

LANES = 128
SUBLANES = 8
ROW_SLABS = D_MODEL // LANES
TOK_TILE = 256
MIX_SUB = 2
MOE_SUB = 2
ROW_TILE = 256
TILE_GROUP = 4
TILE_AHEAD = TILE_GROUP
TILE_SLOTS = 2 * TILE_GROUP
VMEM_LIMIT = 56 * 1024 * 1024


def _layer_norm(x, g, b):
    mu = jnp.mean(x, axis=-1, keepdims=True)
    xc = x - mu
    var = jnp.mean(xc * xc, axis=-1, keepdims=True)
    return xc * lax.rsqrt(var + LN_EPS) * g + b


def _dot(a, b):
    return jnp.dot(a, b, preferred_element_type=F32)


def _dot_nt(a, b):
    return lax.dot_general(a, b, (((1,), (1,)), ((), ())), preferred_element_type=F32)


def _store_rows(ref, slot, rows):
    n = rows.shape[0]
    ref[slot] = rows.astype(BF16).reshape(n, ROW_SLABS, LANES)


def _load_rows(ref, slot):
    return ref[slot].reshape(ref.shape[1], D_MODEL)


def _split_bf16(a):
    hi = a.astype(BF16)
    lo = (a - hi.astype(F32)).astype(BF16)
    return hi, lo


def _ada_kernel(c_a_ref, c_b_ref, w_ref, b_ref, o_a_ref, o_b_ref):
    n_a = c_a_ref.shape[0]
    c = jnp.concatenate([c_a_ref[...], c_b_ref[...]], axis=0)
    s_hi, s_lo = _split_bf16(c * jax.nn.sigmoid(c))
    w_hi, w_lo = _split_bf16(w_ref[...])
    m = _dot(s_hi, w_hi) + _dot(s_hi, w_lo) + _dot(s_lo, w_hi) + b_ref[...]
    o_a_ref[...] = m[:n_a]
    o_b_ref[...] = m[n_a:]


def _ada(c_a, c_b, ada_w, ada_b):
    cols = 3 * D_MODEL // 2
    rows_spec = lambda c: pl.BlockSpec((c.shape[0], D_MODEL), lambda n: (0, 0))
    out_spec = lambda c: pl.BlockSpec((c.shape[0], cols), lambda n: (0, n))
    return pl.pallas_call(
        _ada_kernel,
        grid=(N_MOD * D_MODEL // cols,),
        in_specs=[
            rows_spec(c_a),
            rows_spec(c_b),
            pl.BlockSpec((D_MODEL, cols), lambda n: (0, n)),
            pl.BlockSpec((1, cols), lambda n: (0, n)),
        ],
        out_specs=[out_spec(c_a), out_spec(c_b)],
        out_shape=[jax.ShapeDtypeStruct((c_a.shape[0], N_MOD * D_MODEL), F32),
                   jax.ShapeDtypeStruct((c_b.shape[0], N_MOD * D_MODEL), F32)],
        name="ada",
    )(c_a, c_b, ada_w, ada_b.reshape(1, -1))


def _route(h2_tiles, router_wt2, router_b):
    n_tiles = len(h2_tiles)
    n_tile_tok = h2_tiles[0].shape[0]
    h2 = h2_tiles[0] if n_tiles == 1 else jnp.concatenate(h2_tiles, axis=0)
    n = h2.shape[0]
    lanes = [slice(i * n_tile_tok, (i + 1) * n_tile_tok) for i in range(n_tiles)]
    h_hi = h2.astype(BF16)
    h_lo = (h2 - h_hi.astype(F32)).astype(BF16)
    l1 = _dot_nt(router_wt2, h_hi)
    l2 = _dot_nt(router_wt2[:N_EXPERTS], h_lo)
    logits = l1[:N_EXPERTS] + l1[N_EXPERTS:] + l2 + router_b

    sub = lax.broadcasted_iota(jnp.int32, (N_EXPERTS, n), 0)
    work = logits
    vals, hots = [], []
    for _ in range(TOP_K):
        m = jnp.max(work, axis=0, keepdims=True)
        idx = jnp.min(jnp.where(work == m, sub, N_EXPERTS), axis=0, keepdims=True)
        hot = sub == idx
        work = jnp.where(hot, -jnp.inf, work)
        vals.append(m)
        hots.append(hot)
    exps = [jnp.exp(v - vals[0]) for v in vals]
    denom = exps[0] + exps[1] + exps[2] + exps[3]
    gates = [e / denom for e in exps]

    onehot = jnp.zeros((N_EXPERTS, n), F32)
    for hot in hots:
        onehot = onehot + hot.astype(F32)
    onehot_bf = onehot.astype(BF16)
    t_r = lax.broadcasted_iota(jnp.int32, (n_tile_tok, n_tile_tok), 0)
    t_c = lax.broadcasted_iota(jnp.int32, (n_tile_tok, n_tile_tok), 1)
    stacked = onehot_bf if n_tiles == 1 else jnp.concatenate([onehot_bf[:, ln] for ln in lanes], axis=0)
    rank_st = _dot(stacked, (t_r < t_c).astype(BF16))
    e_r = lax.broadcasted_iota(jnp.int32, (N_EXPERTS, N_EXPERTS), 0)
    e_c = lax.broadcasted_iota(jnp.int32, (N_EXPERTS, N_EXPERTS), 1)
    below = _dot((e_c < e_r).astype(BF16), onehot_bf)
    cnts, bases = [], []
    for i, ln in enumerate(lanes):
        cnts.append(jnp.sum(onehot[:, ln], axis=1, keepdims=True))
        off = jnp.sum(below[:, ln], axis=1, keepdims=True)
        bases.append(rank_st[i * N_EXPERTS:(i + 1) * N_EXPERTS] + off)
    base = bases[0] if n_tiles == 1 else jnp.concatenate(bases, axis=1)

    row8 = lax.broadcasted_iota(jnp.int32, (SUBLANES, n), 0)
    meta = jnp.zeros((SUBLANES, n), F32)
    for k in range(TOP_K):
        pos_k = jnp.sum(jnp.where(hots[k], base, 0.0), axis=0, keepdims=True)
        meta = jnp.where(row8 == k, pos_k, meta)
        meta = jnp.where(row8 == TOP_K + k, gates[k], meta)
    return [(meta[:, ln], jnp.broadcast_to(c, (N_EXPERTS, LANES))) for ln, c in zip(lanes, cnts)]


def _mix_prompt_kernel(alpha_res, x_ref, mod_ref, w_in_ref, sgu_g_ref, sgu_bln_ref, sgu_w_ref,
                       sgu_bias_ref, conv_w_ref, conv_b_ref, w_out_ref, ln1_g_ref, ln1_b_ref,
                       router_w_ref, router_b_ref,
                       x1_ref, h2_ref, meta_ref, cnt_ref, convst_ref, w_in_bf_ref, w_out_bf_ref,
                       carry_ref):
    t = pl.program_id(1)

    @pl.when(jnp.logical_and(pl.program_id(0) == 0, t == 0))
    def _():
        w_in_bf_ref[...] = w_in_ref[...].astype(BF16)
        w_out_bf_ref[...] = w_out_ref[...].astype(BF16)

    @pl.when(t == 0)
    def _():
        carry_ref[...] = jnp.zeros_like(carry_ref)

    seq_row = pl.ds(pl.program_id(0), 1)
    sh1, sc1, g1, sh2, sc2 = [mod_ref[seq_row, k * D_MODEL:(k + 1) * D_MODEL] for k in range(5)]
    r_i = lax.broadcasted_iota(jnp.int32, (CHUNK, CHUNK), 0)
    c_i = lax.broadcasted_iota(jnp.int32, (CHUNK, CHUNK), 1)
    tril = c_i <= r_i
    tm = TOK_TILE
    prev2 = carry_ref[0:1, :]
    prev1 = carry_ref[1:2, :]

    subs = range(MIX_SUB)
    tile_rows = [slice(sub * tm, (sub + 1) * tm) for sub in subs]
    xs_in = [x_ref[rows, :] for rows in tile_rows]
    zs = [_dot((x * (1.0 + sc1) + sh1).astype(BF16), w_in_bf_ref[...]) for x in xs_in]

    mix_ins = []
    for sub in subs:
        z = zs[sub]
        u = z[:, 0:A_WIDTH]
        v = z[:, A_WIDTH:2 * A_WIDTH]
        gate_b = z[:, 2 * A_WIDTH:2 * A_WIDTH + B_WIDTH]
        gate_c = z[:, 2 * A_WIDTH + B_WIDTH:2 * A_WIDTH + 2 * B_WIDTH]
        hb = z[:, 2 * A_WIDTH + 2 * B_WIDTH:]

        a_parts = []
        for hd in range(A_HEADS):
            sl = slice(hd * A_HEAD_DIM, (hd + 1) * A_HEAD_DIM)
            vn = _layer_norm(v[:, sl], sgu_g_ref[hd:hd + 1, :], sgu_bln_ref[hd:hd + 1, :]).astype(BF16)
            wm = jnp.where(tril, sgu_w_ref[hd], 0.0).astype(BF16)
            bias = sgu_bias_ref[:, hd:hd + 1]
            s_parts = []
            for c in range(tm // CHUNK):
                s_parts.append(_dot(wm, vn[c * CHUNK:(c + 1) * CHUNK, :]) + bias)
            a_parts.append(u[:, sl] * jnp.concatenate(s_parts, axis=0))

        q = gate_c * hb
        row = lax.broadcasted_iota(jnp.int32, q.shape, 0)
        q_m1 = jnp.where(row == 0, prev1, pltpu.roll(q, 1, 0))
        q_m2 = jnp.where(row == 0, prev2, jnp.where(row == 1, prev1, pltpu.roll(q, 2, 0)))
        conv = (conv_b_ref[...] + q_m2 * conv_w_ref[0:1, :] + q_m1 * conv_w_ref[1:2, :]
                + q * conv_w_ref[2:3, :])
        b_out = gate_b * conv
        prev2 = q[tm - 2:tm - 1, :]
        prev1 = q[tm - 1:tm, :]

        mix_ins.append(jnp.concatenate(a_parts + [b_out], axis=-1).astype(BF16))

    mixes = [_dot(mix_in, w_out_bf_ref[...]) for mix_in in mix_ins]
    h2s = []
    for sub in subs:
        x1 = _layer_norm(alpha_res * xs_in[sub] + g1 * mixes[sub], ln1_g_ref[...], ln1_b_ref[...])
        x1_ref[tile_rows[sub], :] = x1
        h2 = x1 * (1.0 + sc2) + sh2
        h2_ref[tile_rows[sub], :] = h2.astype(BF16)
        h2s.append(h2)
    routed = _route(h2s, router_w_ref[...], router_b_ref[...])
    for sub in subs:
        meta, cnt = routed[sub]
        meta_ref[sub] = meta
        cnt_ref[sub] = cnt.astype(jnp.int32)

    last2 = jnp.concatenate([prev2, prev1], axis=0)
    carry_ref[0:2, :] = last2
    convst_ref[...] = last2


def _mix_prompt(x, mod, w_in, sgu_g, sgu_bln, sgu_w, sgu_bias_t, conv_w, conv_b, w_out,
                ln1_g, ln1_b, router_wt2, router_b_col, alpha_res):
    bsz, seq, _ = x.shape
    step_tok = MIX_SUB * TOK_TILE
    steps = seq // step_tok
    n_tok = bsz * seq
    tiles = seq // TOK_TILE
    const2 = lambda b, t: (0, 0)
    tok = lambda b, t: (b * steps + t, 0)
    tile3 = lambda b, t: (b * steps + t, 0, 0)
    return pl.pallas_call(
        functools.partial(_mix_prompt_kernel, alpha_res),
        grid=(bsz, steps),
        in_specs=[
            pl.BlockSpec((None, step_tok, D_MODEL), lambda b, t: (b, t, 0)),
            pl.BlockSpec((bsz, N_MOD * D_MODEL), const2),
            pl.BlockSpec((D_MODEL, PROJ_COLS), const2, pipeline_mode=pl.Buffered(1)),
            pl.BlockSpec((A_HEADS, A_HEAD_DIM), const2),
            pl.BlockSpec((A_HEADS, A_HEAD_DIM), const2),
            pl.BlockSpec((A_HEADS, CHUNK, CHUNK), lambda b, t: (0, 0, 0)),
            pl.BlockSpec((CHUNK, A_HEADS), const2),
            pl.BlockSpec((3, B_WIDTH), const2),
            pl.BlockSpec((1, B_WIDTH), const2),
            pl.BlockSpec((D_MODEL, D_MODEL), const2, pipeline_mode=pl.Buffered(1)),
            pl.BlockSpec((1, D_MODEL), const2),
            pl.BlockSpec((1, D_MODEL), const2),
            pl.BlockSpec((2 * N_EXPERTS, D_MODEL), const2),
            pl.BlockSpec((N_EXPERTS, 1), const2),
        ],
        out_specs=[
            pl.BlockSpec((step_tok, D_MODEL), tok),
            pl.BlockSpec((step_tok, D_MODEL), tok),
            pl.BlockSpec((MIX_SUB, SUBLANES, TOK_TILE), tile3),
            pl.BlockSpec((MIX_SUB, N_EXPERTS, LANES), tile3),
            pl.BlockSpec((None, 2, B_WIDTH), lambda b, t: (b, 0, 0)),
            pl.BlockSpec((D_MODEL, PROJ_COLS), const2),
            pl.BlockSpec((D_MODEL, D_MODEL), const2),
        ],
        out_shape=[
            jax.ShapeDtypeStruct((n_tok, D_MODEL), F32),
            jax.ShapeDtypeStruct((n_tok, D_MODEL), BF16),
            jax.ShapeDtypeStruct((bsz * tiles, SUBLANES, TOK_TILE), F32),
            jax.ShapeDtypeStruct((bsz * tiles, N_EXPERTS, LANES), jnp.int32),
            jax.ShapeDtypeStruct((bsz, 2, B_WIDTH), F32),
            jax.ShapeDtypeStruct((D_MODEL, PROJ_COLS), BF16),
            jax.ShapeDtypeStruct((D_MODEL, D_MODEL), BF16),
        ],
        scratch_shapes=[pltpu.VMEM((SUBLANES, B_WIDTH), F32)],
        compiler_params=pltpu.CompilerParams(
            dimension_semantics=("arbitrary", "arbitrary"), vmem_limit_bytes=VMEM_LIMIT),
        name="mix_prompt",
    )(x, mod, w_in, sgu_g, sgu_bln, sgu_w, sgu_bias_t, conv_w, conv_b, w_out,
      ln1_g, ln1_b, router_wt2, router_b_col)


def _mix_sample_kernel(alpha_res, x_ref, mod_ref, prev0_ref, prev1_ref, w_in_ref, sgu_g_ref,
                       sgu_bln_ref, sgu_w00_ref, sgu_b0_ref, conv_w_ref, conv_b_ref, w_out_ref,
                       ln1_g_ref, ln1_b_ref, router_w_ref, router_b_ref,
                       x1_ref, h2_ref, meta_ref, cnt_ref, q_ref, vn_ref):
    x = x_ref[...]
    sh1 = mod_ref[:, 0:D_MODEL]
    sc1 = mod_ref[:, D_MODEL:2 * D_MODEL]
    g1 = mod_ref[:, 2 * D_MODEL:3 * D_MODEL]
    sh2 = mod_ref[:, 3 * D_MODEL:4 * D_MODEL]
    sc2 = mod_ref[:, 4 * D_MODEL:5 * D_MODEL]
    h = (x * (1.0 + sc1) + sh1).astype(BF16)
    z = _dot(h, w_in_ref[...])
    u = z[:, 0:A_WIDTH]
    v = z[:, A_WIDTH:2 * A_WIDTH]
    gate_b = z[:, 2 * A_WIDTH:2 * A_WIDTH + B_WIDTH]
    gate_c = z[:, 2 * A_WIDTH + B_WIDTH:2 * A_WIDTH + 2 * B_WIDTH]
    hb = z[:, 2 * A_WIDTH + 2 * B_WIDTH:]

    vn_parts = []
    for hd in range(A_HEADS):
        sl = slice(hd * A_HEAD_DIM, (hd + 1) * A_HEAD_DIM)
        vn_parts.append(_layer_norm(v[:, sl], sgu_g_ref[hd:hd + 1, :], sgu_bln_ref[hd:hd + 1, :]))
    vn = jnp.concatenate(vn_parts, axis=-1)
    vn_ref[...] = vn
    a_out = u * (vn * sgu_w00_ref[...] + sgu_b0_ref[...])

    q = gate_c * hb
    q_ref[...] = q
    conv = (conv_b_ref[...] + prev0_ref[...] * conv_w_ref[0:1, :] + prev1_ref[...] * conv_w_ref[1:2, :]
            + q * conv_w_ref[2:3, :])
    b_out = gate_b * conv

    mix_in = jnp.concatenate([a_out, b_out], axis=-1).astype(BF16)
    mix = _dot(mix_in, w_out_ref[...])
    x1 = _layer_norm(alpha_res * x + g1 * mix, ln1_g_ref[...], ln1_b_ref[...])
    x1_ref[...] = x1
    h2 = x1 * (1.0 + sc2) + sh2
    h2_ref[...] = h2.astype(BF16)
    (meta, cnt), = _route([h2], router_w_ref[...], router_b_ref[...])
    meta_ref[...] = meta
    cnt_ref[...] = cnt.astype(jnp.int32)


def _mix_sample(x, mod, prev0, prev1, w_in_bf, sgu_g, sgu_bln, sgu_w00, sgu_b0, conv_w, conv_b,
                w_out_bf, ln1_g, ln1_b, router_wt2, router_b_col, alpha_res):
    n = x.shape[0]
    return pl.pallas_call(
        functools.partial(_mix_sample_kernel, alpha_res),
        out_shape=[
            jax.ShapeDtypeStruct((n, D_MODEL), F32),
            jax.ShapeDtypeStruct((n, D_MODEL), BF16),
            jax.ShapeDtypeStruct((SUBLANES, n), F32),
            jax.ShapeDtypeStruct((N_EXPERTS, LANES), jnp.int32),
            jax.ShapeDtypeStruct((n, B_WIDTH), F32),
            jax.ShapeDtypeStruct((n, A_WIDTH), F32),
        ],
        compiler_params=pltpu.CompilerParams(vmem_limit_bytes=VMEM_LIMIT),
        name="mix_sample",
    )(x, mod, prev0, prev1, w_in_bf, sgu_g, sgu_bln, sgu_w00, sgu_b0, conv_w, conv_b, w_out_bf,
      ln1_g, ln1_b, router_wt2, router_b_col)


def _strip_copy(src_ref, src_row, dst_ref, dst_row, n_rows, sem):
    @pl.when(n_rows > 0)
    def _():
        pltpu.make_async_copy(src_ref.at[pl.ds(src_row, n_rows)],
                              dst_ref.at[pl.ds(dst_row, n_rows)], sem).start()


def _wait_rows(hbm_ref, vmem_ref, n_rows, sem):
    pltpu.make_async_copy(hbm_ref.at[pl.ds(0, n_rows)], vmem_ref.at[pl.ds(0, n_rows)], sem).wait()


def _sort_kernel(n_tok, tile0, n_steps, subs, first, *refs):
    if first:
        (sorted_row_ref, off_ref, cnt_ref, pad_row_ref, pad_n_ref, h2_ref, meta_ref,
         xs_ref, stage_ref, zero_ref, sem, pad_sem) = refs
    else:
        (sorted_row_ref, off_ref, cnt_ref, pad_row_ref, pad_n_ref, h2_ref, meta_ref, _,
         xs_ref, stage_ref, sem) = refs
    j = pl.program_id(0)
    n_sorted = n_tok * TOP_K
    parity = j % 2

    if first:
        @pl.when(j == 0)
        def _():
            zero_ref[...] = jnp.zeros_like(zero_ref)

            def start(e, carry):
                _strip_copy(zero_ref, 0, xs_ref, pad_row_ref[e], pad_n_ref[e], pad_sem)
                return carry
            lax.fori_loop(0, N_EXPERTS, start, 0)

            def wait(e, carry):
                @pl.when(pad_n_ref[e] > 0)
                def _():
                    _wait_rows(xs_ref, zero_ref, pad_n_ref[e], pad_sem)
                return carry
            lax.fori_loop(0, N_EXPERTS, wait, 0)

    sub_r = lax.broadcasted_iota(jnp.int32, (n_sorted, n_tok), 0)
    for sub in range(subs):
        slot = parity * subs + sub
        meta = meta_ref[sub]
        sel = sub_r == meta[0:1, :].astype(jnp.int32)
        for k in range(1, TOP_K):
            sel = jnp.logical_or(sel, sub_r == meta[k:k + 1, :].astype(jnp.int32))
        perm = jnp.where(sel, 1.0, 0.0).astype(BF16)
        rows = _dot(perm, h2_ref[sub * n_tok:(sub + 1) * n_tok, :])
        _store_rows(stage_ref, slot, rows)

        base = (tile0 + j * subs + sub) * N_EXPERTS

        def start(e, carry, slot=slot, base=base):
            _strip_copy(stage_ref.at[slot], off_ref[base + e], xs_ref, sorted_row_ref[base + e],
                        cnt_ref[base + e], sem.at[slot])
            return carry
        lax.fori_loop(0, N_EXPERTS, start, 0)

    @pl.when(j > 0)
    def _():
        for sub in range(subs):
            _wait_rows(xs_ref, stage_ref.at[0], n_sorted, sem.at[(1 - parity) * subs + sub])

    @pl.when(j == n_steps - 1)
    def _():
        for sub in range(subs):
            _wait_rows(xs_ref, stage_ref.at[0], n_sorted, sem.at[parity * subs + sub])


def _sort(tables, pad_row, pad_n, h2, meta, xs, n_sorted_rows, n_tok, tile0, subs):
    sorted_row, off, cnt = tables
    first = xs is None
    n_steps = h2.shape[0] // (n_tok * subs)
    in_specs = [pl.BlockSpec((subs * n_tok, D_MODEL), lambda j, *_: (j, 0)),
                pl.BlockSpec((subs, SUBLANES, n_tok), lambda j, *_: (j, 0, 0))]
    operands = [sorted_row, off, cnt, pad_row, pad_n, h2, meta]
    scratch = [pltpu.VMEM((2 * subs, n_tok * TOP_K, ROW_SLABS, LANES), BF16)]
    if first:
        scratch.append(pltpu.VMEM((ROW_TILE, ROW_SLABS, LANES), BF16))
        aliases = {}
    else:
        in_specs.append(pl.BlockSpec(memory_space=pl.ANY))
        operands.append(xs)
        aliases = {len(operands) - 1: 0}
    scratch.append(pltpu.SemaphoreType.DMA((2 * subs,)))
    if first:
        scratch.append(pltpu.SemaphoreType.DMA(()))
    grid_spec = pltpu.PrefetchScalarGridSpec(
        num_scalar_prefetch=5,
        grid=(n_steps,),
        in_specs=in_specs,
        out_specs=pl.BlockSpec(memory_space=pl.ANY),
        scratch_shapes=scratch,
    )
    return pl.pallas_call(
        functools.partial(_sort_kernel, n_tok, tile0, n_steps, subs, first),
        grid_spec=grid_spec,
        out_shape=jax.ShapeDtypeStruct((n_sorted_rows, ROW_SLABS, LANES), BF16),
        input_output_aliases=aliases,
        compiler_params=pltpu.CompilerParams(
            dimension_semantics=("arbitrary",), vmem_limit_bytes=VMEM_LIMIT),
        name="sort_first" if first else "sort_more",
    )(*operands)


def _experts_kernel(tile_start_ref, n_tile_ref, n_used_ref, xs_ref, w_gu_ref, b_gu_ref, w_down_ref,
                    b_down_ref, ys_ref, w_gu_bf_ref, w_down_bf_ref, x_buf, y_buf, x_sem, y_sem):
    e = pl.program_id(0)
    n_used = n_used_ref[0]
    first_tile = tile_start_ref[e]
    n_tile = n_tile_ref[e]

    def x_copy(g):
        slot = g % TILE_SLOTS
        return pltpu.make_async_copy(xs_ref.at[pl.ds(g * ROW_TILE, ROW_TILE)], x_buf.at[slot],
                                     x_sem.at[slot])

    def y_copy(g):
        slot = g % TILE_SLOTS
        return pltpu.make_async_copy(y_buf.at[slot], ys_ref.at[pl.ds(g * ROW_TILE, ROW_TILE)],
                                     y_sem.at[slot])

    def request(g):
        @pl.when(g < n_used)
        def _():
            x_copy(g).start(priority=1)

    @pl.when(e == 0)
    def _():
        for g in range(TILE_AHEAD):
            request(g)

    @pl.when(n_tile > 0)
    def _():
        w_gu_bf_ref[...] = w_gu_ref[...].astype(BF16)
        w_down_bf_ref[...] = w_down_ref[...].astype(BF16)

    def begin(g):
        x_copy(g).wait()
        request(g + TILE_AHEAD)

        @pl.when(g >= TILE_SLOTS)
        def _():
            y_copy(g - TILE_SLOTS).wait()

    def compute(g):
        slot = g % TILE_SLOTS
        x = _load_rows(x_buf, slot)
        gu = _dot(x, w_gu_bf_ref[...]) + b_gu_ref[...]
        gate = jnp.minimum(gu[:, :D_EXPERT], SWIGLU_LIMIT)
        up = jnp.clip(gu[:, D_EXPERT:], -SWIGLU_LIMIT, SWIGLU_LIMIT)
        act = (up + 1.0) * gate * jax.nn.sigmoid(SWIGLU_ALPHA * gate)
        y = _dot(act.astype(BF16), w_down_bf_ref[...]) + b_down_ref[...]
        _store_rows(y_buf, slot, y)
        y_copy(g).start(priority=1)

    def group(g, size):
        for k in range(size):
            begin(g + k)
        for k in range(size):
            compute(g + k)

    def group_body(p, carry):
        group(first_tile + TILE_GROUP * p, TILE_GROUP)
        return carry

    n_group = n_tile // TILE_GROUP
    lax.fori_loop(0, n_group, group_body, 0)
    rest = n_tile - n_group * TILE_GROUP
    g_rest = first_tile + n_group * TILE_GROUP
    size = TILE_GROUP // 2
    while size >= 1:
        @pl.when((rest & size) != 0)
        def _(g_rest=g_rest, size=size):
            group(g_rest, size)
        g_rest = g_rest + (rest & size)
        size //= 2

    @pl.when(e == N_EXPERTS - 1)
    def _():
        for back in range(TILE_SLOTS, 0, -1):
            @pl.when(n_used >= back)
            def _(back=back):
                y_copy(n_used - back).wait()


def _experts(tile_start, n_tile_e, n_used, xs, w_gu, b_gu, w_down, b_down):
    w_blk = lambda e, *_: (e, 0, 0)
    grid_spec = pltpu.PrefetchScalarGridSpec(
        num_scalar_prefetch=3,
        grid=(N_EXPERTS,),
        in_specs=[
            pl.BlockSpec(memory_space=pl.ANY),
            pl.BlockSpec((None, D_MODEL, 2 * D_EXPERT), w_blk),
            pl.BlockSpec((None, 1, 2 * D_EXPERT), w_blk),
            pl.BlockSpec((None, D_EXPERT, D_MODEL), w_blk),
            pl.BlockSpec((None, 1, D_MODEL), w_blk),
        ],
        out_specs=pl.BlockSpec(memory_space=pl.ANY),
        scratch_shapes=[pltpu.VMEM((D_MODEL, 2 * D_EXPERT), BF16),
                        pltpu.VMEM((D_EXPERT, D_MODEL), BF16),
                        pltpu.VMEM((TILE_SLOTS, ROW_TILE, ROW_SLABS, LANES), BF16),
                        pltpu.VMEM((TILE_SLOTS, ROW_TILE, ROW_SLABS, LANES), BF16),
                        pltpu.SemaphoreType.DMA((TILE_SLOTS,)),
                        pltpu.SemaphoreType.DMA((TILE_SLOTS,))],
    )
    return pl.pallas_call(
        _experts_kernel,
        grid_spec=grid_spec,
        out_shape=jax.ShapeDtypeStruct(xs.shape, BF16),
        compiler_params=pltpu.CompilerParams(
            dimension_semantics=("arbitrary",), vmem_limit_bytes=VMEM_LIMIT),
        name="experts",
    )(tile_start, n_tile_e, n_used, xs, w_gu, b_gu.reshape(N_EXPERTS, 1, -1), w_down,
      b_down.reshape(N_EXPERTS, 1, -1))


def _combine_kernel(n_tok, tile0, n_steps, subs, seq_steps, alpha_res, sorted_row_ref, off_ref, cnt_ref,
                    ys_ref, x1_ref, meta_ref, mod_ref, ln2_g_ref, ln2_b_ref, out_ref,
                    stage_ref, sem):
    j = pl.program_id(0)
    n_sorted = n_tok * TOP_K
    parity = j % 2

    def start_step(step, step_parity):
        for sub in range(subs):
            base = (tile0 + step * subs + sub) * N_EXPERTS
            to_slot = step_parity * subs + sub

            def start(e, carry, base=base, to_slot=to_slot):
                _strip_copy(ys_ref, sorted_row_ref[base + e], stage_ref.at[to_slot],
                            off_ref[base + e], cnt_ref[base + e], sem.at[to_slot])
                return carry
            lax.fori_loop(0, N_EXPERTS, start, 0)

    @pl.when(j == 0)
    def _():
        start_step(j, parity)

    @pl.when(j + 1 < n_steps)
    def _():
        start_step(j + 1, 1 - parity)

    sub_r = lax.broadcasted_iota(jnp.int32, (n_sorted, n_tok), 0)
    for sub in range(subs):
        slot = parity * subs + sub
        rows = slice(sub * n_tok, (sub + 1) * n_tok)
        meta = meta_ref[sub]
        comb_t = jnp.zeros((n_sorted, n_tok), F32)
        for k in range(TOP_K):
            comb_t = jnp.where(sub_r == meta[k:k + 1, :].astype(jnp.int32),
                               meta[TOP_K + k:TOP_K + k + 1, :], comb_t)
        comb_t = comb_t.astype(BF16)

        _wait_rows(ys_ref, stage_ref.at[0], n_sorted, sem.at[slot])
        ys = _load_rows(stage_ref, slot)
        ffn = lax.dot_general(comb_t, ys, (((0,), (0,)), ((), ())), preferred_element_type=F32)
        if seq_steps is None:
            g2 = mod_ref[rows, 5 * D_MODEL:6 * D_MODEL]
        else:
            g2 = mod_ref[pl.ds(j // seq_steps, 1), 5 * D_MODEL:6 * D_MODEL]
        out_ref[rows, :] = _layer_norm(alpha_res * x1_ref[rows, :] + g2 * ffn, ln2_g_ref[...],
                                       ln2_b_ref[...])


def _combine(tables, ys, x1, meta, mod, ln2_g, ln2_b, n_tok, tile0, subs, alpha_res):
    sorted_row, off, cnt = tables
    step_tok = subs * n_tok
    n_steps = x1.shape[0] // step_tok
    if mod.shape[0] == x1.shape[0]:
        seq_steps = None
        mod_spec = pl.BlockSpec((step_tok, N_MOD * D_MODEL), lambda j, *_: (j, 0))
    else:
        seq_steps = n_steps // mod.shape[0]
        mod_spec = pl.BlockSpec(mod.shape, lambda j, *_: (0, 0))
    grid_spec = pltpu.PrefetchScalarGridSpec(
        num_scalar_prefetch=3,
        grid=(n_steps,),
        in_specs=[
            pl.BlockSpec(memory_space=pl.ANY),
            pl.BlockSpec((step_tok, D_MODEL), lambda j, *_: (j, 0)),
            pl.BlockSpec((subs, SUBLANES, n_tok), lambda j, *_: (j, 0, 0)),
            mod_spec,
            pl.BlockSpec((1, D_MODEL), lambda j, *_: (0, 0)),
            pl.BlockSpec((1, D_MODEL), lambda j, *_: (0, 0)),
        ],
        out_specs=pl.BlockSpec((step_tok, D_MODEL), lambda j, *_: (j, 0)),
        scratch_shapes=[pltpu.VMEM((2 * subs, n_tok * TOP_K, ROW_SLABS, LANES), BF16),
                        pltpu.SemaphoreType.DMA((2 * subs,))],
    )
    return pl.pallas_call(
        functools.partial(_combine_kernel, n_tok, tile0, n_steps, subs, seq_steps, alpha_res),
        grid_spec=grid_spec,
        out_shape=jax.ShapeDtypeStruct(x1.shape, F32),
        compiler_params=pltpu.CompilerParams(
            dimension_semantics=("arbitrary",), vmem_limit_bytes=VMEM_LIMIT),
        name="combine",
    )(sorted_row, off, cnt, ys, x1, meta, mod, ln2_g, ln2_b)


def _routing_tables(cnt_all):
    total = jnp.sum(cnt_all, axis=0)
    n_tile_e = (total + ROW_TILE - 1) // ROW_TILE
    tile_end = jnp.cumsum(n_tile_e)
    tile_start = tile_end - n_tile_e
    row_start = tile_start * ROW_TILE
    cum = jnp.cumsum(cnt_all, axis=0) - cnt_all
    off = jnp.cumsum(cnt_all, axis=1) - cnt_all
    sorted_row = row_start[None, :] + cum
    n_used = tile_end[-1]
    pad_row = row_start + total
    pad_n = n_tile_e * ROW_TILE - total
    i32 = lambda a: a.astype(jnp.int32)
    return ((i32(sorted_row).reshape(-1), i32(off).reshape(-1), i32(cnt_all).reshape(-1)),
            i32(pad_row), i32(pad_n), i32(tile_start), i32(n_tile_e), i32(n_used).reshape(1))


def kernel(x_prompt, x_sample, state_conv, c_prompt, c_sample, ada_w, ada_b, w_in, sgu_ln_g, sgu_ln_b,
           sgu_w, sgu_b, conv_w, conv_b, w_out, ln1_g, ln1_b, router_w, router_b, w_gu, b_gu,
           w_down, b_down, ln2_g, ln2_b):
    depth = ada_w.shape[0]
    assert depth == 1
    bsz, seq, _ = x_prompt.shape
    n_dec = x_sample.shape[0]
    assert x_sample.shape[1] == 1 and seq % TOK_TILE == 0
    alpha_res = (2.0 * depth) ** 0.25
    l = 0

    mod_s, mod_p = _ada(c_sample, c_prompt, ada_w[l], ada_b[l])

    router_wt =jnp.transpose(router_w[l])
    router_wt_hi = router_wt.astype(BF16)
    router_wt_lo = (router_wt - router_wt_hi.astype(F32)).astype(BF16)
    router_wt2 = jnp.concatenate([router_wt_hi, router_wt_lo], axis=0)
    router_b_col = router_b[l].reshape(N_EXPERTS, 1)
    row = lambda a: a.reshape(1, -1)

    x1_p, h2_p, meta_p, cnt_p, convst_p, w_in_bf, w_out_bf = _mix_prompt(
        x_prompt, mod_p, w_in[l], sgu_ln_g[l], sgu_ln_b[l], sgu_w[l], jnp.transpose(sgu_b[l]),
        conv_w[l], row(conv_b[l]), w_out[l], row(ln1_g[l]), row(ln1_b[l]), router_wt2,
        router_b_col, alpha_res)
    x1_s, h2_s, meta_s, cnt_s, q_s, vn_s = _mix_sample(
        x_sample.reshape(n_dec, D_MODEL), mod_s, state_conv[l, :, 0, :], state_conv[l, :, 1, :],
        w_in_bf, sgu_ln_g[l], sgu_ln_b[l], row(jnp.repeat(sgu_w[l, :, 0, 0], A_HEAD_DIM)),
        row(jnp.repeat(sgu_b[l, :, 0], A_HEAD_DIM)), conv_w[l], row(conv_b[l]), w_out_bf,
        row(ln1_g[l]), row(ln1_b[l]), router_wt2, router_b_col, alpha_res)

    n_ptiles = bsz * seq // TOK_TILE
    cnt_all = jnp.concatenate([cnt_p[:, :, 0], cnt_s[None, :, 0]], axis=0)
    n_assign = (bsz * seq + n_dec) * TOP_K
    n_row_tiles = -(-n_assign // ROW_TILE) + N_EXPERTS
    tables, pad_row, pad_n, tile_start, n_tile_e, n_used = _routing_tables(cnt_all)

    n_sorted_rows = n_row_tiles * ROW_TILE
    meta_s = meta_s[None]
    xs = _sort(tables, pad_row, pad_n, h2_p, meta_p, None, n_sorted_rows, TOK_TILE, 0, MOE_SUB)
    xs = _sort(tables, pad_row, pad_n, h2_s, meta_s, xs, n_sorted_rows, n_dec, n_ptiles, 1)
    ys = _experts(tile_start, n_tile_e, n_used, xs, w_gu[l], b_gu[l], w_down[l], b_down[l])
    y_p = _combine(tables, ys, x1_p, meta_p, mod_p, row(ln2_g[l]), row(ln2_b[l]), TOK_TILE, 0,
                   MOE_SUB, alpha_res)
    y_s = _combine(tables, ys, x1_s, meta_s, mod_s, row(ln2_g[l]), row(ln2_b[l]), n_dec, n_ptiles,
                   1, alpha_res)

    conv_state_sample = jnp.stack([state_conv[l, :, 1, :], q_s], axis=1)[None]
    return (y_p.reshape(bsz, seq, D_MODEL),
            y_s.reshape(n_dec, 1, D_MODEL),
            convst_p[None],
            conv_state_sample,
            vn_s.reshape(1, n_dec, 1, A_HEADS, A_HEAD_DIM))
```

```python
import functools

import jax
import jax.numpy as jnp
from jax import lax
from jax.experimental import pallas as pl
from jax.experimental.pallas import tpu as pltpu

F32 = jnp.float32
BF16 = jnp.bfloat16

D_MODEL = 1024
A_WIDTH = 512
B_WIDTH = 512
A_HEADS = 4
A_HEAD_DIM = 128
CHUNK = 128
PROJ_COLS = 2 * A_WIDTH + 3 * B_WIDTH
N_EXPERTS = 32
TOP_K = 4
D_EXPERT = 1024
SWIGLU_LIMIT = 7.0
SWIGLU_ALPHA = 1.702
LN_EPS = 1e-5
N_MOD = 6

LANES = 128
SUBLANES = 8
ROW_SLABS = D_MODEL // LANES
TOK_TILE = 256
MIX_SUB = 2
MOE_SUB = 4
ROW_TILE = 256
TILE_GROUP = 4
TILE_AHEAD = TILE_GROUP
TILE_SLOTS = 2 * TILE_GROUP
VMEM_LIMIT = 56 * 1024 * 1024


def _layer_norm(x, g, b):
    mu = jnp.mean(x, axis=-1, keepdims=True)
    xc = x - mu
    var = jnp.mean(xc * xc, axis=-1, keepdims=True)
    return xc * lax.rsqrt(var + LN_EPS) * g + b


def _dot(a, b):
    return jnp.dot(a, b, preferred_element_type=F32)


def _dot_nt(a, b):
    return lax.dot_general(a, b, (((1,), (1,)), ((), ())), preferred_element_type=F32)


def _store_rows(ref, slot, rows):
    n = rows.shape[0]
    ref[slot] = rows.astype(BF16).reshape(n, ROW_SLABS, LANES)


def _load_rows(ref, slot):
    return ref[slot].reshape(ref.shape[1], D_MODEL)


def _split_bf16(a):
    hi = a.astype(BF16)
    lo = (a - hi.astype(F32)).astype(BF16)
    return hi, lo


def _ada_kernel(c_a_ref, c_b_ref, w_ref, b_ref, o_a_ref, o_b_ref):
    n_a = c_a_ref.shape[0]
    c = jnp.concatenate([c_a_ref[...], c_b_ref[...]], axis=0)
    s_hi, s_lo = _split_bf16(c * jax.nn.sigmoid(c))
    w_hi, w_lo = _split_bf16(w_ref[...])
    m = _dot(s_hi, w_hi) + _dot(s_hi, w_lo) + _dot(s_lo, w_hi) + b_ref[...]
    o_a_ref[...] = m[:n_a]
    o_b_ref[...] = m[n_a:]


def _ada(c_a, c_b, ada_w, ada_b):
    cols = 3 * D_MODEL // 2
    rows_spec = lambda c: pl.BlockSpec((c.shape[0], D_MODEL), lambda n: (0, 0))
    out_spec = lambda c: pl.BlockSpec((c.shape[0], cols), lambda n: (0, n))
    return pl.pallas_call(
        _ada_kernel,
        grid=(N_MOD * D_MODEL // cols,),
        in_specs=[
            rows_spec(c_a),
            rows_spec(c_b),
            pl.BlockSpec((D_MODEL, cols), lambda n: (0, n)),
            pl.BlockSpec((1, cols), lambda n: (0, n)),
        ],
        out_specs=[out_spec(c_a), out_spec(c_b)],
        out_shape=[jax.ShapeDtypeStruct((c_a.shape[0], N_MOD * D_MODEL), F32),
                   jax.ShapeDtypeStruct((c_b.shape[0], N_MOD * D_MODEL), F32)],
        name="ada",
    )(c_a, c_b, ada_w, ada_b.reshape(1, -1))


def _route(h2_tiles, router_wt2, router_b):
    n_tiles = len(h2_tiles)
    n_tile_tok = h2_tiles[0].shape[0]
    h2 = h2_tiles[0] if n_tiles == 1 else jnp.concatenate(h2_tiles, axis=0)
    n = h2.shape[0]
    lanes = [slice(i * n_tile_tok, (i + 1) * n_tile_tok) for i in range(n_tiles)]
    h_hi = h2.astype(BF16)
    h_lo = (h2 - h_hi.astype(F32)).astype(BF16)
    l1 = _dot_nt(router_wt2, h_hi)
    l2 = _dot_nt(router_wt2[:N_EXPERTS], h_lo)
    logits = l1[:N_EXPERTS] + l1[N_EXPERTS:] + l2 + router_b

    sub = lax.broadcasted_iota(jnp.int32, (N_EXPERTS, n), 0)
    work = logits
    vals, hots = [], []
    for _ in range(TOP_K):
        m = jnp.max(work, axis=0, keepdims=True)
        idx = jnp.min(jnp.where(work == m, sub, N_EXPERTS), axis=0, keepdims=True)
        hot = sub == idx
        work = jnp.where(hot, -jnp.inf, work)
        vals.append(m)
        hots.append(hot)
    exps = [jnp.exp(v - vals[0]) for v in vals]
    denom = exps[0] + exps[1] + exps[2] + exps[3]
    gates = [e / denom for e in exps]

    onehot = jnp.zeros((N_EXPERTS, n), F32)
    for hot in hots:
        onehot = onehot + hot.astype(F32)
    onehot_bf = onehot.astype(BF16)
    t_r = lax.broadcasted_iota(jnp.int32, (n_tile_tok, n_tile_tok), 0)
    t_c = lax.broadcasted_iota(jnp.int32, (n_tile_tok, n_tile_tok), 1)
    stacked = onehot_bf if n_tiles == 1 else jnp.concatenate([onehot_bf[:, ln] for ln in lanes], axis=0)
    rank_st = _dot(stacked, (t_r < t_c).astype(BF16))
    e_r = lax.broadcasted_iota(jnp.int32, (N_EXPERTS, N_EXPERTS), 0)
    e_c = lax.broadcasted_iota(jnp.int32, (N_EXPERTS, N_EXPERTS), 1)
    below = _dot((e_c < e_r).astype(BF16), onehot_bf)
    cnts, bases = [], []
    for i, ln in enumerate(lanes):
        cnts.append(jnp.sum(onehot[:, ln], axis=1, keepdims=True))
        off = jnp.sum(below[:, ln], axis=1, keepdims=True)
        bases.append(rank_st[i * N_EXPERTS:(i + 1) * N_EXPERTS] + off)
    base = bases[0] if n_tiles == 1 else jnp.concatenate(bases, axis=1)

    row8 = lax.broadcasted_iota(jnp.int32, (SUBLANES, n), 0)
    meta = jnp.zeros((SUBLANES, n), F32)
    for k in range(TOP_K):
        pos_k = jnp.sum(jnp.where(hots[k], base, 0.0), axis=0, keepdims=True)
        meta = jnp.where(row8 == k, pos_k, meta)
        meta = jnp.where(row8 == TOP_K + k, gates[k], meta)
    return [(meta[:, ln], jnp.broadcast_to(c, (N_EXPERTS, LANES))) for ln, c in zip(lanes, cnts)]


def _mix_prompt_kernel(alpha_res, x_ref, mod_ref, w_in_ref, sgu_g_ref, sgu_bln_ref, sgu_w_ref,
                       sgu_bias_ref, conv_w_ref, conv_b_ref, w_out_ref, ln1_g_ref, ln1_b_ref,
                       router_w_ref, router_b_ref,
                       x1_ref, h2_ref, meta_ref, cnt_ref, convst_ref, w_in_bf_ref, w_out_bf_ref,
                       carry_ref):
    t = pl.program_id(1)

    @pl.when(jnp.logical_and(pl.program_id(0) == 0, t == 0))
    def _():
        w_in_bf_ref[...] = w_in_ref[...].astype(BF16)
        w_out_bf_ref[...] = w_out_ref[...].astype(BF16)

    @pl.when(t == 0)
    def _():
        carry_ref[...] = jnp.zeros_like(carry_ref)

    seq_row = pl.ds(pl.program_id(0), 1)
    sh1, sc1, g1, sh2, sc2 = [mod_ref[seq_row, k * D_MODEL:(k + 1) * D_MODEL] for k in range(5)]
    r_i = lax.broadcasted_iota(jnp.int32, (CHUNK, CHUNK), 0)
    c_i = lax.broadcasted_iota(jnp.int32, (CHUNK, CHUNK), 1)
    tril = c_i <= r_i
    tm = TOK_TILE
    prev2 = carry_ref[0:1, :]
    prev1 = carry_ref[1:2, :]

    subs = range(MIX_SUB)
    tile_rows = [slice(sub * tm, (sub + 1) * tm) for sub in subs]
    xs_in = [x_ref[rows, :] for rows in tile_rows]
    zs = [_dot((x * (1.0 + sc1) + sh1).astype(BF16), w_in_bf_ref[...]) for x in xs_in]

    mix_ins = []
    for sub in subs:
        z = zs[sub]
        u = z[:, 0:A_WIDTH]
        v = z[:, A_WIDTH:2 * A_WIDTH]
        gate_b = z[:, 2 * A_WIDTH:2 * A_WIDTH + B_WIDTH]
        gate_c = z[:, 2 * A_WIDTH + B_WIDTH:2 * A_WIDTH + 2 * B_WIDTH]
        hb = z[:, 2 * A_WIDTH + 2 * B_WIDTH:]

        a_parts = []
        for hd in range(A_HEADS):
            sl = slice(hd * A_HEAD_DIM, (hd + 1) * A_HEAD_DIM)
            vn = _layer_norm(v[:, sl], sgu_g_ref[hd:hd + 1, :], sgu_bln_ref[hd:hd + 1, :]).astype(BF16)
            wm = jnp.where(tril, sgu_w_ref[hd], 0.0).astype(BF16)
            bias = sgu_bias_ref[:, hd:hd + 1]
            s_parts = []
            for c in range(tm // CHUNK):
                s_parts.append(_dot(wm, vn[c * CHUNK:(c + 1) * CHUNK, :]) + bias)
            a_parts.append(u[:, sl] * jnp.concatenate(s_parts, axis=0))

        q = gate_c * hb
        row = lax.broadcasted_iota(jnp.int32, q.shape, 0)
        q_m1 = jnp.where(row == 0, prev1, pltpu.roll(q, 1, 0))
        q_m2 = jnp.where(row == 0, prev2, jnp.where(row == 1, prev1, pltpu.roll(q, 2, 0)))
        conv = (conv_b_ref[...] + q_m2 * conv_w_ref[0:1, :] + q_m1 * conv_w_ref[1:2, :]
                + q * conv_w_ref[2:3, :])
        b_out = gate_b * conv
        prev2 = q[tm - 2:tm - 1, :]
        prev1 = q[tm - 1:tm, :]

        mix_ins.append(jnp.concatenate(a_parts + [b_out], axis=-1).astype(BF16))

    mixes = [_dot(mix_in, w_out_bf_ref[...]) for mix_in in mix_ins]
    h2s = []
    for sub in subs:
        x1 = _layer_norm(alpha_res * xs_in[sub] + g1 * mixes[sub], ln1_g_ref[...], ln1_b_ref[...])
        x1_ref[tile_rows[sub], :] = x1
        h2 = x1 * (1.0 + sc2) + sh2
        h2_ref[tile_rows[sub], :] = h2.astype(BF16)
        h2s.append(h2)
    routed = _route(h2s, router_w_ref[...], router_b_ref[...])
    for sub in subs:
        meta, cnt = routed[sub]
        meta_ref[sub] = meta
        cnt_ref[sub] = cnt.astype(jnp.int32)

    last2 = jnp.concatenate([prev2, prev1], axis=0)
    carry_ref[0:2, :] = last2
    convst_ref[...] = last2


def _mix_prompt(x, mod, w_in, sgu_g, sgu_bln, sgu_w, sgu_bias_t, conv_w, conv_b, w_out,
                ln1_g, ln1_b, router_wt2, router_b_col, alpha_res):
    bsz, seq, _ = x.shape
    step_tok = MIX_SUB * TOK_TILE
    steps = seq // step_tok
    n_tok = bsz * seq
    tiles = seq // TOK_TILE
    const2 = lambda b, t: (0, 0)
    tok = lambda b, t: (b * steps + t, 0)
    tile3 = lambda b, t: (b * steps + t, 0, 0)
    return pl.pallas_call(
        functools.partial(_mix_prompt_kernel, alpha_res),
        grid=(bsz, steps),
        in_specs=[
            pl.BlockSpec((None, step_tok, D_MODEL), lambda b, t: (b, t, 0)),
            pl.BlockSpec((bsz, N_MOD * D_MODEL), const2),
            pl.BlockSpec((D_MODEL, PROJ_COLS), const2, pipeline_mode=pl.Buffered(1)),
            pl.BlockSpec((A_HEADS, A_HEAD_DIM), const2),
            pl.BlockSpec((A_HEADS, A_HEAD_DIM), const2),
            pl.BlockSpec((A_HEADS, CHUNK, CHUNK), lambda b, t: (0, 0, 0)),
            pl.BlockSpec((CHUNK, A_HEADS), const2),
            pl.BlockSpec((3, B_WIDTH), const2),
            pl.BlockSpec((1, B_WIDTH), const2),
            pl.BlockSpec((D_MODEL, D_MODEL), const2, pipeline_mode=pl.Buffered(1)),
            pl.BlockSpec((1, D_MODEL), const2),
            pl.BlockSpec((1, D_MODEL), const2),
            pl.BlockSpec((2 * N_EXPERTS, D_MODEL), const2),
            pl.BlockSpec((N_EXPERTS, 1), const2),
        ],
        out_specs=[
            pl.BlockSpec((step_tok, D_MODEL), tok),
            pl.BlockSpec((step_tok, D_MODEL), tok),
            pl.BlockSpec((MIX_SUB, SUBLANES, TOK_TILE), tile3),
            pl.BlockSpec((MIX_SUB, N_EXPERTS, LANES), tile3),
            pl.BlockSpec((None, 2, B_WIDTH), lambda b, t: (b, 0, 0)),
            pl.BlockSpec((D_MODEL, PROJ_COLS), const2),
            pl.BlockSpec((D_MODEL, D_MODEL), const2),
        ],
        out_shape=[
            jax.ShapeDtypeStruct((n_tok, D_MODEL), F32),
            jax.ShapeDtypeStruct((n_tok, D_MODEL), BF16),
            jax.ShapeDtypeStruct((bsz * tiles, SUBLANES, TOK_TILE), F32),
            jax.ShapeDtypeStruct((bsz * tiles, N_EXPERTS, LANES), jnp.int32),
            jax.ShapeDtypeStruct((bsz, 2, B_WIDTH), F32),
            jax.ShapeDtypeStruct((D_MODEL, PROJ_COLS), BF16),
            jax.ShapeDtypeStruct((D_MODEL, D_MODEL), BF16),
        ],
        scratch_shapes=[pltpu.VMEM((SUBLANES, B_WIDTH), F32)],
        compiler_params=pltpu.CompilerParams(
            dimension_semantics=("arbitrary", "arbitrary"), vmem_limit_bytes=VMEM_LIMIT),
        name="mix_prompt",
    )(x, mod, w_in, sgu_g, sgu_bln, sgu_w, sgu_bias_t, conv_w, conv_b, w_out,
      ln1_g, ln1_b, router_wt2, router_b_col)


def _mix_sample_kernel(alpha_res, x_ref, mod_ref, prev0_ref, prev1_ref, w_in_ref, sgu_g_ref,
                       sgu_bln_ref, sgu_w00_ref, sgu_b0_ref, conv_w_ref, conv_b_ref, w_out_ref,
                       ln1_g_ref, ln1_b_ref, router_w_ref, router_b_ref,
                       x1_ref, h2_ref, meta_ref, cnt_ref, q_ref, vn_ref):
    x = x_ref[...]
    sh1 = mod_ref[:, 0:D_MODEL]
    sc1 = mod_ref[:, D_MODEL:2 * D_MODEL]
    g1 = mod_ref[:, 2 * D_MODEL:3 * D_MODEL]
    sh2 = mod_ref[:, 3 * D_MODEL:4 * D_MODEL]
    sc2 = mod_ref[:, 4 * D_MODEL:5 * D_MODEL]
    h = (x * (1.0 + sc1) + sh1).astype(BF16)
    z = _dot(h, w_in_ref[...])
    u = z[:, 0:A_WIDTH]
    v = z[:, A_WIDTH:2 * A_WIDTH]
    gate_b = z[:, 2 * A_WIDTH:2 * A_WIDTH + B_WIDTH]
    gate_c = z[:, 2 * A_WIDTH + B_WIDTH:2 * A_WIDTH + 2 * B_WIDTH]
    hb = z[:, 2 * A_WIDTH + 2 * B_WIDTH:]

    vn_parts = []
    for hd in range(A_HEADS):
        sl = slice(hd * A_HEAD_DIM, (hd + 1) * A_HEAD_DIM)
        vn_parts.append(_layer_norm(v[:, sl], sgu_g_ref[hd:hd + 1, :], sgu_bln_ref[hd:hd + 1, :]))
    vn = jnp.concatenate(vn_parts, axis=-1)
    vn_ref[...] = vn
    a_out = u * (vn * sgu_w00_ref[...] + sgu_b0_ref[...])

    q = gate_c * hb
    q_ref[...] = q
    conv = (conv_b_ref[...] + prev0_ref[...] * conv_w_ref[0:1, :] + prev1_ref[...] * conv_w_ref[1:2, :]
            + q * conv_w_ref[2:3, :])
    b_out = gate_b * conv

    mix_in = jnp.concatenate([a_out, b_out], axis=-1).astype(BF16)
    mix = _dot(mix_in, w_out_ref[...])
    x1 = _layer_norm(alpha_res * x + g1 * mix, ln1_g_ref[...], ln1_b_ref[...])
    x1_ref[...] = x1
    h2 = x1 * (1.0 + sc2) + sh2
    h2_ref[...] = h2.astype(BF16)
    (meta, cnt), = _route([h2], router_w_ref[...], router_b_ref[...])
    meta_ref[...] = meta
    cnt_ref[...] = cnt.astype(jnp.int32)


def _mix_sample(x, mod, prev0, prev1, w_in_bf, sgu_g, sgu_bln, sgu_w00, sgu_b0, conv_w, conv_b,
                w_out_bf, ln1_g, ln1_b, router_wt2, router_b_col, alpha_res):
    n = x.shape[0]
    return pl.pallas_call(
        functools.partial(_mix_sample_kernel, alpha_res),
        out_shape=[
            jax.ShapeDtypeStruct((n, D_MODEL), F32),
            jax.ShapeDtypeStruct((n, D_MODEL), BF16),
            jax.ShapeDtypeStruct((SUBLANES, n), F32),
            jax.ShapeDtypeStruct((N_EXPERTS, LANES), jnp.int32),
            jax.ShapeDtypeStruct((n, B_WIDTH), F32),
            jax.ShapeDtypeStruct((n, A_WIDTH), F32),
        ],
        compiler_params=pltpu.CompilerParams(vmem_limit_bytes=VMEM_LIMIT),
        name="mix_sample",
    )(x, mod, prev0, prev1, w_in_bf, sgu_g, sgu_bln, sgu_w00, sgu_b0, conv_w, conv_b, w_out_bf,
      ln1_g, ln1_b, router_wt2, router_b_col)


def _strip_copy(src_ref, src_row, dst_ref, dst_row, n_rows, sem):
    @pl.when(n_rows > 0)
    def _():
        pltpu.make_async_copy(src_ref.at[pl.ds(src_row, n_rows)],
                              dst_ref.at[pl.ds(dst_row, n_rows)], sem).start()


def _wait_rows(hbm_ref, vmem_ref, n_rows, sem):
    pltpu.make_async_copy(hbm_ref.at[pl.ds(0, n_rows)], vmem_ref.at[pl.ds(0, n_rows)], sem).wait()


def _sort_kernel(n_tok, tile0, n_steps, subs, first, *refs):
    if first:
        (sorted_row_ref, off_ref, cnt_ref, pad_row_ref, pad_n_ref, h2_ref, meta_ref,
         xs_ref, stage_ref, zero_ref, sem, pad_sem) = refs
    else:
        (sorted_row_ref, off_ref, cnt_ref, pad_row_ref, pad_n_ref, h2_ref, meta_ref, _,
         xs_ref, stage_ref, sem) = refs
    j = pl.program_id(0)
    n_sorted = n_tok * TOP_K
    parity = j % 2

    if first:
        @pl.when(j == 0)
        def _():
            zero_ref[...] = jnp.zeros_like(zero_ref)

            def start(e, carry):
                _strip_copy(zero_ref, 0, xs_ref, pad_row_ref[e], pad_n_ref[e], pad_sem)
                return carry
            lax.fori_loop(0, N_EXPERTS, start, 0)

            def wait(e, carry):
                @pl.when(pad_n_ref[e] > 0)
                def _():
                    _wait_rows(xs_ref, zero_ref, pad_n_ref[e], pad_sem)
                return carry
            lax.fori_loop(0, N_EXPERTS, wait, 0)

    sub_r = lax.broadcasted_iota(jnp.int32, (n_sorted, n_tok), 0)
    for sub in range(subs):
        slot = parity * subs + sub
        meta = meta_ref[sub]
        sel = sub_r == meta[0:1, :].astype(jnp.int32)
        for k in range(1, TOP_K):
            sel = jnp.logical_or(sel, sub_r == meta[k:k + 1, :].astype(jnp.int32))
        perm = jnp.where(sel, 1.0, 0.0).astype(BF16)
        rows = _dot(perm, h2_ref[sub * n_tok:(sub + 1) * n_tok, :])
        _store_rows(stage_ref, slot, rows)

        base = (tile0 + j * subs + sub) * N_EXPERTS

        def start(e, carry, slot=slot, base=base):
            _strip_copy(stage_ref.at[slot], off_ref[base + e], xs_ref, sorted_row_ref[base + e],
                        cnt_ref[base + e], sem.at[slot])
            return carry
        lax.fori_loop(0, N_EXPERTS, start, 0)

    @pl.when(j > 0)
    def _():
        for sub in range(subs):
            _wait_rows(xs_ref, stage_ref.at[0], n_sorted, sem.at[(1 - parity) * subs + sub])

    @pl.when(j == n_steps - 1)
    def _():
        for sub in range(subs):
            _wait_rows(xs_ref, stage_ref.at[0], n_sorted, sem.at[parity * subs + sub])


def _sort(tables, pad_row, pad_n, h2, meta, xs, n_sorted_rows, n_tok, tile0, subs):
    sorted_row, off, cnt = tables
    first = xs is None
    n_steps = h2.shape[0] // (n_tok * subs)
    in_specs = [pl.BlockSpec((subs * n_tok, D_MODEL), lambda j, *_: (j, 0)),
                pl.BlockSpec((subs, SUBLANES, n_tok), lambda j, *_: (j, 0, 0))]
    operands = [sorted_row, off, cnt, pad_row, pad_n, h2, meta]
    scratch = [pltpu.VMEM((2 * subs, n_tok * TOP_K, ROW_SLABS, LANES), BF16)]
    if first:
        scratch.append(pltpu.VMEM((ROW_TILE, ROW_SLABS, LANES), BF16))
        aliases = {}
    else:
        in_specs.append(pl.BlockSpec(memory_space=pl.ANY))
        operands.append(xs)
        aliases = {len(operands) - 1: 0}
    scratch.append(pltpu.SemaphoreType.DMA((2 * subs,)))
    if first:
        scratch.append(pltpu.SemaphoreType.DMA(()))
    grid_spec = pltpu.PrefetchScalarGridSpec(
        num_scalar_prefetch=5,
        grid=(n_steps,),
        in_specs=in_specs,
        out_specs=pl.BlockSpec(memory_space=pl.ANY),
        scratch_shapes=scratch,
    )
    return pl.pallas_call(
        functools.partial(_sort_kernel, n_tok, tile0, n_steps, subs, first),
        grid_spec=grid_spec,
        out_shape=jax.ShapeDtypeStruct((n_sorted_rows, ROW_SLABS, LANES), BF16),
        input_output_aliases=aliases,
        compiler_params=pltpu.CompilerParams(
            dimension_semantics=("arbitrary",), vmem_limit_bytes=VMEM_LIMIT),
        name="sort_first" if first else "sort_more",
    )(*operands)


def _experts_kernel(tile_start_ref, n_tile_ref, n_used_ref, xs_ref, w_gu_ref, b_gu_ref, w_down_ref,
                    b_down_ref, ys_ref, w_gu_bf_ref, w_down_bf_ref, x_buf, y_buf, x_sem, y_sem):
    e = pl.program_id(0)
    n_used = n_used_ref[0]
    first_tile = tile_start_ref[e]
    n_tile = n_tile_ref[e]

    def x_copy(g):
        slot = g % TILE_SLOTS
        return pltpu.make_async_copy(xs_ref.at[pl.ds(g * ROW_TILE, ROW_TILE)], x_buf.at[slot],
                                     x_sem.at[slot])

    def y_copy(g):
        slot = g % TILE_SLOTS
        return pltpu.make_async_copy(y_buf.at[slot], ys_ref.at[pl.ds(g * ROW_TILE, ROW_TILE)],
                                     y_sem.at[slot])

    def request(g):
        @pl.when(g < n_used)
        def _():
            x_copy(g).start(priority=1)

    @pl.when(e == 0)
    def _():
        for g in range(TILE_AHEAD):
            request(g)

    @pl.when(n_tile > 0)
    def _():
        w_gu_bf_ref[...] = w_gu_ref[...].astype(BF16)
        w_down_bf_ref[...] = w_down_ref[...].astype(BF16)

    def begin(g):
        x_copy(g).wait()
        request(g + TILE_AHEAD)

        @pl.when(g >= TILE_SLOTS)
        def _():
            y_copy(g - TILE_SLOTS).wait()

    def compute(g):
        slot = g % TILE_SLOTS
        x = _load_rows(x_buf, slot)
        gu = _dot(x, w_gu_bf_ref[...]) + b_gu_ref[...]
        gate = jnp.minimum(gu[:, :D_EXPERT], SWIGLU_LIMIT)
        up = jnp.clip(gu[:, D_EXPERT:], -SWIGLU_LIMIT, SWIGLU_LIMIT)
        act = (up + 1.0) * gate * jax.nn.sigmoid(SWIGLU_ALPHA * gate)
        y = _dot(act.astype(BF16), w_down_bf_ref[...]) + b_down_ref[...]
        _store_rows(y_buf, slot, y)
        y_copy(g).start(priority=1)

    def group(g, size):
        for k in range(size):
            begin(g + k)
        for k in range(size):
            compute(g + k)

    def group_body(p, carry):
        group(first_tile + TILE_GROUP * p, TILE_GROUP)
        return carry

    n_group = n_tile // TILE_GROUP
    lax.fori_loop(0, n_group, group_body, 0)
    rest = n_tile - n_group * TILE_GROUP
    g_rest = first_tile + n_group * TILE_GROUP
    size = TILE_GROUP // 2
    while size >= 1:
        @pl.when((rest & size) != 0)
        def _(g_rest=g_rest, size=size):
            group(g_rest, size)
        g_rest = g_rest + (rest & size)
        size //= 2

    @pl.when(e == N_EXPERTS - 1)
    def _():
        for back in range(TILE_SLOTS, 0, -1):
            @pl.when(n_used >= back)
            def _(back=back):
                y_copy(n_used - back).wait()


def _experts(tile_start, n_tile_e, n_used, xs, w_gu, b_gu, w_down, b_down):
    w_blk = lambda e, *_: (e, 0, 0)
    grid_spec = pltpu.PrefetchScalarGridSpec(
        num_scalar_prefetch=3,
        grid=(N_EXPERTS,),
        in_specs=[
            pl.BlockSpec(memory_space=pl.ANY),
            pl.BlockSpec((None, D_MODEL, 2 * D_EXPERT), w_blk),
            pl.BlockSpec((None, 1, 2 * D_EXPERT), w_blk),
            pl.BlockSpec((None, D_EXPERT, D_MODEL), w_blk),
            pl.BlockSpec((None, 1, D_MODEL), w_blk),
        ],
        out_specs=pl.BlockSpec(memory_space=pl.ANY),
        scratch_shapes=[pltpu.VMEM((D_MODEL, 2 * D_EXPERT), BF16),
                        pltpu.VMEM((D_EXPERT, D_MODEL), BF16),
                        pltpu.VMEM((TILE_SLOTS, ROW_TILE, ROW_SLABS, LANES), BF16),
                        pltpu.VMEM((TILE_SLOTS, ROW_TILE, ROW_SLABS, LANES), BF16),
                        pltpu.SemaphoreType.DMA((TILE_SLOTS,)),
                        pltpu.SemaphoreType.DMA((TILE_SLOTS,))],
    )
    return pl.pallas_call(
        _experts_kernel,
        grid_spec=grid_spec,
        out_shape=jax.ShapeDtypeStruct(xs.shape, BF16),
        compiler_params=pltpu.CompilerParams(
            dimension_semantics=("arbitrary",), vmem_limit_bytes=VMEM_LIMIT),
        name="experts",
    )(tile_start, n_tile_e, n_used, xs, w_gu, b_gu.reshape(N_EXPERTS, 1, -1), w_down,
      b_down.reshape(N_EXPERTS, 1, -1))


def _combine_kernel(n_tok, tile0, n_steps, subs, seq_steps, alpha_res, sorted_row_ref, off_ref, cnt_ref,
                    ys_ref, x1_ref, meta_ref, mod_ref, ln2_g_ref, ln2_b_ref, out_ref,
                    stage_ref, sem):
    j = pl.program_id(0)
    n_sorted = n_tok * TOP_K
    parity = j % 2

    def start_step(step, step_parity):
        for sub in range(subs):
            base = (tile0 + step * subs + sub) * N_EXPERTS
            to_slot = step_parity * subs + sub

            def start(e, carry, base=base, to_slot=to_slot):
                _strip_copy(ys_ref, sorted_row_ref[base + e], stage_ref.at[to_slot],
                            off_ref[base + e], cnt_ref[base + e], sem.at[to_slot])
                return carry
            lax.fori_loop(0, N_EXPERTS, start, 0)

    @pl.when(j == 0)
    def _():
        start_step(j, parity)

    @pl.when(j + 1 < n_steps)
    def _():
        start_step(j + 1, 1 - parity)

    sub_r = lax.broadcasted_iota(jnp.int32, (n_sorted, n_tok), 0)
    for sub in range(subs):
        slot = parity * subs + sub
        rows = slice(sub * n_tok, (sub + 1) * n_tok)
        meta = meta_ref[sub]
        comb_t = jnp.zeros((n_sorted, n_tok), F32)
        for k in range(TOP_K):
            comb_t = jnp.where(sub_r == meta[k:k + 1, :].astype(jnp.int32),
                               meta[TOP_K + k:TOP_K + k + 1, :], comb_t)
        comb_t = comb_t.astype(BF16)

        _wait_rows(ys_ref, stage_ref.at[0], n_sorted, sem.at[slot])
        ys = _load_rows(stage_ref, slot)
        ffn = lax.dot_general(comb_t, ys, (((0,), (0,)), ((), ())), preferred_element_type=F32)
        if seq_steps is None:
            g2 = mod_ref[rows, 5 * D_MODEL:6 * D_MODEL]
        else:
            g2 = mod_ref[pl.ds(j // seq_steps, 1), 5 * D_MODEL:6 * D_MODEL]
        out_ref[rows, :] = _layer_norm(alpha_res * x1_ref[rows, :] + g2 * ffn, ln2_g_ref[...],
                                       ln2_b_ref[...])


def _combine(tables, ys, x1, meta, mod, ln2_g, ln2_b, n_tok, tile0, subs, alpha_res):
    sorted_row, off, cnt = tables
    step_tok = subs * n_tok
    n_steps = x1.shape[0] // step_tok
    if mod.shape[0] == x1.shape[0]:
        seq_steps = None
        mod_spec = pl.BlockSpec((step_tok, N_MOD * D_MODEL), lambda j, *_: (j, 0))
    else:
        seq_steps = n_steps // mod.shape[0]
        mod_spec = pl.BlockSpec(mod.shape, lambda j, *_: (0, 0))
    grid_spec = pltpu.PrefetchScalarGridSpec(
        num_scalar_prefetch=3,
        grid=(n_steps,),
        in_specs=[
            pl.BlockSpec(memory_space=pl.ANY),
            pl.BlockSpec((step_tok, D_MODEL), lambda j, *_: (j, 0)),
            pl.BlockSpec((subs, SUBLANES, n_tok), lambda j, *_: (j, 0, 0)),
            mod_spec,
            pl.BlockSpec((1, D_MODEL), lambda j, *_: (0, 0)),
            pl.BlockSpec((1, D_MODEL), lambda j, *_: (0, 0)),
        ],
        out_specs=pl.BlockSpec((step_tok, D_MODEL), lambda j, *_: (j, 0)),
        scratch_shapes=[pltpu.VMEM((2 * subs, n_tok * TOP_K, ROW_SLABS, LANES), BF16),
                        pltpu.SemaphoreType.DMA((2 * subs,))],
    )
    return pl.pallas_call(
        functools.partial(_combine_kernel, n_tok, tile0, n_steps, subs, seq_steps, alpha_res),
        grid_spec=grid_spec,
        out_shape=jax.ShapeDtypeStruct(x1.shape, F32),
        compiler_params=pltpu.CompilerParams(
            dimension_semantics=("arbitrary",), vmem_limit_bytes=VMEM_LIMIT),
        name="combine",
    )(sorted_row, off, cnt, ys, x1, meta, mod, ln2_g, ln2_b)


def _routing_tables(cnt_all):
    total = jnp.sum(cnt_all, axis=0)
    n_tile_e = (total + ROW_TILE - 1) // ROW_TILE
    tile_end = jnp.cumsum(n_tile_e)
    tile_start = tile_end - n_tile_e
    row_start = tile_start * ROW_TILE
    cum = jnp.cumsum(cnt_all, axis=0) - cnt_all
    off = jnp.cumsum(cnt_all, axis=1) - cnt_all
    sorted_row = row_start[None, :] + cum
    n_used = tile_end[-1]
    pad_row = row_start + total
    pad_n = n_tile_e * ROW_TILE - total
    i32 = lambda a: a.astype(jnp.int32)
    return ((i32(sorted_row).reshape(-1), i32(off).reshape(-1), i32(cnt_all).reshape(-1)),
            i32(pad_row), i32(pad_n), i32(tile_start), i32(n_tile_e), i32(n_used).reshape(1))


def kernel(x_prompt, x_sample, state_conv, c_prompt, c_sample, ada_w, ada_b, w_in, sgu_ln_g, sgu_ln_b,
           sgu_w, sgu_b, conv_w, conv_b, w_out, ln1_g, ln1_b, router_w, router_b, w_gu, b_gu,
           w_down, b_down, ln2_g, ln2_b):
    depth = ada_w.shape[0]
    assert depth == 1
    bsz, seq, _ = x_prompt.shape
    n_dec = x_sample.shape[0]
    assert x_sample.shape[1] == 1 and seq % TOK_TILE == 0
    alpha_res = (2.0 * depth) ** 0.25
    l = 0

    mod_s, mod_p = _ada(c_sample, c_prompt, ada_w[l], ada_b[l])

    router_wt =jnp.transpose(router_w[l])
    router_wt_hi = router_wt.astype(BF16)
    router_wt_lo = (router_wt - router_wt_hi.astype(F32)).astype(BF16)
    router_wt2 = jnp.concatenate([router_wt_hi, router_wt_lo], axis=0)
    router_b_col = router_b[l].reshape(N_EXPERTS, 1)
    row = lambda a: a.reshape(1, -1)

    x1_p, h2_p, meta_p, cnt_p, convst_p, w_in_bf, w_out_bf = _mix_prompt(
        x_prompt, mod_p, w_in[l], sgu_ln_g[l], sgu_ln_b[l], sgu_w[l], jnp.transpose(sgu_b[l]),
        conv_w[l], row(conv_b[l]), w_out[l], row(ln1_g[l]), row(ln1_b[l]), router_wt2,
        router_b_col, alpha_res)
    x1_s, h2_s, meta_s, cnt_s, q_s, vn_s = _mix_sample(
        x_sample.reshape(n_dec, D_MODEL), mod_s, state_conv[l, :, 0, :], state_conv[l, :, 1, :],
        w_in_bf, sgu_ln_g[l], sgu_ln_b[l], row(jnp.repeat(sgu_w[l, :, 0, 0], A_HEAD_DIM)),
        row(jnp.repeat(sgu_b[l, :, 0], A_HEAD_DIM)), conv_w[l], row(conv_b[l]), w_out_bf,
        row(ln1_g[l]), row(ln1_b[l]), router_wt2, router_b_col, alpha_res)

    n_ptiles = bsz * seq // TOK_TILE
    cnt_all = jnp.concatenate([cnt_p[:, :, 0], cnt_s[None, :, 0]], axis=0)
    n_assign = (bsz * seq + n_dec) * TOP_K
    n_row_tiles = -(-n_assign // ROW_TILE) + N_EXPERTS
    tables, pad_row, pad_n, tile_start, n_tile_e, n_used = _routing_tables(cnt_all)

    n_sorted_rows = n_row_tiles * ROW_TILE
    meta_s = meta_s[None]
    xs = _sort(tables, pad_row, pad_n, h2_p, meta_p, None, n_sorted_rows, TOK_TILE, 0, MOE_SUB)
    xs = _sort(tables, pad_row, pad_n, h2_s, meta_s, xs, n_sorted_rows, n_dec, n_ptiles, 1)
    ys = _experts(tile_start, n_tile_e, n_used, xs, w_gu[l], b_gu[l], w_down[l], b_down[l])
    y_p = _combine(tables, ys, x1_p, meta_p, mod_p, row(ln2_g[l]), row(ln2_b[l]), TOK_TILE, 0,
                   MOE_SUB, alpha_res)
    y_s = _combine(tables, ys, x1_s, meta_s, mod_s, row(ln2_g[l]), row(ln2_b[l]), n_dec, n_ptiles,
                   1, alpha_res)

    conv_state_sample = jnp.stack([state_conv[l, :, 1, :], q_s], axis=1)[None]
    return (y_p.reshape(bsz, seq, D_MODEL),
            y_s.reshape(n_dec, 1, D_MODEL),
            convst_p[None],
            conv_state_sample,
            vn_s.reshape(1, n_dec, 1, A_HEADS, A_HEAD_DIM))
```

```python
import functools

import jax
import jax.numpy as jnp
from jax import lax
from jax.experimental import pallas as pl
from jax.experimental.pallas import tpu as pltpu

F32 = jnp.float32
BF16 = jnp.bfloat16

D_MODEL = 1024
A_WIDTH = 512
B_WIDTH = 512
A_HEADS = 4
A_HEAD_DIM = 128
CHUNK = 128
PROJ_COLS = 2 * A_WIDTH + 3 * B_WIDTH
N_EXPERTS = 32
TOP_K = 4
D_EXPERT = 1024
SWIGLU_LIMIT = 7.0
SWIGLU_ALPHA = 1.702
LN_EPS = 1e-5
N_MOD = 6

LANES = 128
SUBLANES = 8
ROW_SLABS = D_MODEL // LANES
TOK_TILE = 256
MIX_SUB = 2
MOE_SUB = 2
ROW_TILE = 256
TILE_GROUP = 4
TILE_AHEAD = TILE_GROUP
TILE_SLOTS = 2 * TILE_GROUP
VMEM_LIMIT = 56 * 1024 * 1024


def _layer_norm(x, g, b):
    mu = jnp.mean(x, axis=-1, keepdims=True)
    xc = x - mu
    var = jnp.mean(xc * xc, axis=-1, keepdims=True)
    return xc * lax.rsqrt(var + LN_EPS) * g + b


def _dot(a, b):
    return jnp.dot(a, b, preferred_element_type=F32)


def _dot_nt(a, b):
    return lax.dot_general(a, b, (((1,), (1,)), ((), ())), preferred_element_type=F32)


def _store_rows(ref, slot, rows):
    n = rows.shape[0]
    ref[slot] = rows.astype(BF16).reshape(n, ROW_SLABS, LANES)


def _load_rows(ref, slot):
    return ref[slot].reshape(ref.shape[1], D_MODEL)


def _split_bf16(a):
    hi = a.astype(BF16)
    lo = (a - hi.astype(F32)).astype(BF16)
    return hi, lo


def _ada_kernel(c_a_ref, c_b_ref, w_ref, b_ref, o_a_ref, o_b_ref):
    n_a = c_a_ref.shape[0]
    c = jnp.concatenate([c_a_ref[...], c_b_ref[...]], axis=0)
    s_hi, s_lo = _split_bf16(c * jax.nn.sigmoid(c))
    w_hi, w_lo = _split_bf16(w_ref[...])
    m = _dot(s_hi, w_hi) + _dot(s_hi, w_lo) + _dot(s_lo, w_hi) + b_ref[...]
    o_a_ref[...] = m[:n_a]
    o_b_ref[...] = m[n_a:]


def _ada(c_a, c_b, ada_w, ada_b):
    cols = 3 * D_MODEL // 2
    rows_spec = lambda c: pl.BlockSpec((c.shape[0], D_MODEL), lambda n: (0, 0))
    out_spec = lambda c: pl.BlockSpec((c.shape[0], cols), lambda n: (0, n))
    return pl.pallas_call(
        _ada_kernel,
        grid=(N_MOD * D_MODEL // cols,),
        in_specs=[
            rows_spec(c_a),
            rows_spec(c_b),
            pl.BlockSpec((D_MODEL, cols), lambda n: (0, n)),
            pl.BlockSpec((1, cols), lambda n: (0, n)),
        ],
        out_specs=[out_spec(c_a), out_spec(c_b)],
        out_shape=[jax.ShapeDtypeStruct((c_a.shape[0], N_MOD * D_MODEL), F32),
                   jax.ShapeDtypeStruct((c_b.shape[0], N_MOD * D_MODEL), F32)],
        name="ada",
    )(c_a, c_b, ada_w, ada_b.reshape(1, -1))


def _route(h2_tiles, router_wt2, router_b):
    n_tiles = len(h2_tiles)
    n_tile_tok = h2_tiles[0].shape[0]
    h2 = h2_tiles[0] if n_tiles == 1 else jnp.concatenate(h2_tiles, axis=0)
    n = h2.shape[0]
    lanes = [slice(i * n_tile_tok, (i + 1) * n_tile_tok) for i in range(n_tiles)]
    h_hi = h2.astype(BF16)
    h_lo = (h2 - h_hi.astype(F32)).astype(BF16)
    l1 = _dot_nt(router_wt2, h_hi)
    l2 = _dot_nt(router_wt2[:N_EXPERTS], h_lo)
    logits = l1[:N_EXPERTS] + l1[N_EXPERTS:] + l2 + router_b

    sub = lax.broadcasted_iota(jnp.int32, (N_EXPERTS, n), 0)
    work = logits
    vals, hots = [], []
    for _ in range(TOP_K):
        m = jnp.max(work, axis=0, keepdims=True)
        idx = jnp.min(jnp.where(work == m, sub, N_EXPERTS), axis=0, keepdims=True)
        hot = sub == idx
        work = jnp.where(hot, -jnp.inf, work)
        vals.append(m)
        hots.append(hot)
    exps = [jnp.exp(v - vals[0]) for v in vals]
    denom = exps[0] + exps[1] + exps[2] + exps[3]
    gates = [e / denom for e in exps]

    onehot = jnp.zeros((N_EXPERTS, n), F32)
    for hot in hots:
        onehot = onehot + hot.astype(F32)
    onehot_bf = onehot.astype(BF16)
    t_r = lax.broadcasted_iota(jnp.int32, (n_tile_tok, n_tile_tok), 0)
    t_c = lax.broadcasted_iota(jnp.int32, (n_tile_tok, n_tile_tok), 1)
    stacked = onehot_bf if n_tiles == 1 else jnp.concatenate([onehot_bf[:, ln] for ln in lanes], axis=0)
    rank_st = _dot(stacked, (t_r < t_c).astype(BF16))
    e_r = lax.broadcasted_iota(jnp.int32, (N_EXPERTS, N_EXPERTS), 0)
    e_c = lax.broadcasted_iota(jnp.int32, (N_EXPERTS, N_EXPERTS), 1)
    below = _dot((e_c < e_r).astype(BF16), onehot_bf)
    cnts, bases = [], []
    for i, ln in enumerate(lanes):
        cnts.append(jnp.sum(onehot[:, ln], axis=1, keepdims=True))
        off = jnp.sum(below[:, ln], axis=1, keepdims=True)
        bases.append(rank_st[i * N_EXPERTS:(i + 1) * N_EXPERTS] + off)
    base = bases[0] if n_tiles == 1 else jnp.concatenate(bases, axis=1)

    row8 = lax.broadcasted_iota(jnp.int32, (SUBLANES, n), 0)
    meta = jnp.zeros((SUBLANES, n), F32)
    for k in range(TOP_K):
        pos_k = jnp.sum(jnp.where(hots[k], base, 0.0), axis=0, keepdims=True)
        meta = jnp.where(row8 == k, pos_k, meta)
        meta = jnp.where(row8 == TOP_K + k, gates[k], meta)
    return [(meta[:, ln], jnp.broadcast_to(c, (N_EXPERTS, LANES))) for ln, c in zip(lanes, cnts)]


def _mix_prompt_kernel(alpha_res, x_ref, mod_ref, w_in_ref, sgu_g_ref, sgu_bln_ref, sgu_w_ref,
                       sgu_bias_ref, conv_w_ref, conv_b_ref, w_out_ref, ln1_g_ref, ln1_b_ref,
                       router_w_ref, router_b_ref,
                       x1_ref, h2_ref, meta_ref, cnt_ref, convst_ref, w_in_bf_ref, w_out_bf_ref,
                       carry_ref):
    t = pl.program_id(1)

    @pl.when(jnp.logical_and(pl.program_id(0) == 0, t == 0))
    def _():
        w_in_bf_ref[...] = w_in_ref[...].astype(BF16)
        w_out_bf_ref[...] = w_out_ref[...].astype(BF16)

    @pl.when(t == 0)
    def _():
        carry_ref[...] = jnp.zeros_like(carry_ref)

    seq_row = pl.ds(pl.program_id(0), 1)
    sh1, sc1, g1, sh2, sc2 = [mod_ref[seq_row, k * D_MODEL:(k + 1) * D_MODEL] for k in range(5)]
    r_i = lax.broadcasted_iota(jnp.int32, (CHUNK, CHUNK), 0)
    c_i = lax.broadcasted_iota(jnp.int32, (CHUNK, CHUNK), 1)
    tril = c_i <= r_i
    tm = TOK_TILE
    prev2 = carry_ref[0:1, :]
    prev1 = carry_ref[1:2, :]

    subs = range(MIX_SUB)
    tile_rows = [slice(sub * tm, (sub + 1) * tm) for sub in subs]
    xs_in = [x_ref[rows, :] for rows in tile_rows]
    zs = [_dot((x * (1.0 + sc1) + sh1).astype(BF16), w_in_bf_ref[...]) for x in xs_in]

    mix_ins = []
    for sub in subs:
        z = zs[sub]
        u = z[:, 0:A_WIDTH]
        v = z[:, A_WIDTH:2 * A_WIDTH]
        gate_b = z[:, 2 * A_WIDTH:2 * A_WIDTH + B_WIDTH]
        gate_c = z[:, 2 * A_WIDTH + B_WIDTH:2 * A_WIDTH + 2 * B_WIDTH]
        hb = z[:, 2 * A_WIDTH + 2 * B_WIDTH:]

        a_parts = []
        for hd in range(A_HEADS):
            sl = slice(hd * A_HEAD_DIM, (hd + 1) * A_HEAD_DIM)
            vn = _layer_norm(v[:, sl], sgu_g_ref[hd:hd + 1, :], sgu_bln_ref[hd:hd + 1, :]).astype(BF16)
            wm = jnp.where(tril, sgu_w_ref[hd], 0.0).astype(BF16)
            bias = sgu_bias_ref[:, hd:hd + 1]
            s_parts = []
            for c in range(tm // CHUNK):
                s_parts.append(_dot(wm, vn[c * CHUNK:(c + 1) * CHUNK, :]) + bias)
            a_parts.append(u[:, sl] * jnp.concatenate(s_parts, axis=0))

        q = gate_c * hb
        row = lax.broadcasted_iota(jnp.int32, q.shape, 0)
        q_m1 = jnp.where(row == 0, prev1, pltpu.roll(q, 1, 0))
        q_m2 = jnp.where(row == 0, prev2, jnp.where(row == 1, prev1, pltpu.roll(q, 2, 0)))
        conv = (conv_b_ref[...] + q_m2 * conv_w_ref[0:1, :] + q_m1 * conv_w_ref[1:2, :]
                + q * conv_w_ref[2:3, :])
        b_out = gate_b * conv
        prev2 = q[tm - 2:tm - 1, :]
        prev1 = q[tm - 1:tm, :]

        mix_ins.append(jnp.concatenate(a_parts + [b_out], axis=-1).astype(BF16))

    mixes = [_dot(mix_in, w_out_bf_ref[...]) for mix_in in mix_ins]
    h2s = []
    for sub in subs:
        x1 = _layer_norm(alpha_res * xs_in[sub] + g1 * mixes[sub], ln1_g_ref[...], ln1_b_ref[...])
        x1_ref[tile_rows[sub], :] = x1
        h2 = x1 * (1.0 + sc2) + sh2
        h2_ref[tile_rows[sub], :] = h2.astype(BF16)
        h2s.append(h2)
    routed = _route(h2s, router_w_ref[...], router_b_ref[...])
    for sub in subs:
        meta, cnt = routed[sub]
        meta_ref[sub] = meta
        cnt_ref[sub] = cnt.astype(jnp.int32)

    last2 = jnp.concatenate([prev2, prev1], axis=0)
    carry_ref[0:2, :] = last2
    convst_ref[...] = last2


def _mix_prompt(x, mod, w_in, sgu_g, sgu_bln, sgu_w, sgu_bias_t, conv_w, conv_b, w_out,
                ln1_g, ln1_b, router_wt2, router_b_col, alpha_res):
    bsz, seq, _ = x.shape
    step_tok = MIX_SUB * TOK_TILE
    steps = seq // step_tok
    n_tok = bsz * seq
    tiles = seq // TOK_TILE
    const2 = lambda b, t: (0, 0)
    tok = lambda b, t: (b * steps + t, 0)
    tile3 = lambda b, t: (b * steps + t, 0, 0)
    return pl.pallas_call(
        functools.partial(_mix_prompt_kernel, alpha_res),
        grid=(bsz, steps),
        in_specs=[
            pl.BlockSpec((None, step_tok, D_MODEL), lambda b, t: (b, t, 0)),
            pl.BlockSpec((bsz, N_MOD * D_MODEL), const2),
            pl.BlockSpec((D_MODEL, PROJ_COLS), const2, pipeline_mode=pl.Buffered(1)),
            pl.BlockSpec((A_HEADS, A_HEAD_DIM), const2),
            pl.BlockSpec((A_HEADS, A_HEAD_DIM), const2),
            pl.BlockSpec((A_HEADS, CHUNK, CHUNK), lambda b, t: (0, 0, 0)),
            pl.BlockSpec((CHUNK, A_HEADS), const2),
            pl.BlockSpec((3, B_WIDTH), const2),
            pl.BlockSpec((1, B_WIDTH), const2),
            pl.BlockSpec((D_MODEL, D_MODEL), const2, pipeline_mode=pl.Buffered(1)),
            pl.BlockSpec((1, D_MODEL), const2),
            pl.BlockSpec((1, D_MODEL), const2),
            pl.BlockSpec((2 * N_EXPERTS, D_MODEL), const2),
            pl.BlockSpec((N_EXPERTS, 1), const2),
        ],
        out_specs=[
            pl.BlockSpec((step_tok, D_MODEL), tok),
            pl.BlockSpec((step_tok, D_MODEL), tok),
            pl.BlockSpec((MIX_SUB, SUBLANES, TOK_TILE), tile3),
            pl.BlockSpec((MIX_SUB, N_EXPERTS, LANES), tile3),
            pl.BlockSpec((None, 2, B_WIDTH), lambda b, t: (b, 0, 0)),
            pl.BlockSpec((D_MODEL, PROJ_COLS), const2),
            pl.BlockSpec((D_MODEL, D_MODEL), const2),
        ],
        out_shape=[
            jax.ShapeDtypeStruct((n_tok, D_MODEL), F32),
            jax.ShapeDtypeStruct((n_tok, D_MODEL), BF16),
            jax.ShapeDtypeStruct((bsz * tiles, SUBLANES, TOK_TILE), F32),
            jax.ShapeDtypeStruct((bsz * tiles, N_EXPERTS, LANES), jnp.int32),
            jax.ShapeDtypeStruct((bsz, 2, B_WIDTH), F32),
            jax.ShapeDtypeStruct((D_MODEL, PROJ_COLS), BF16),
            jax.ShapeDtypeStruct((D_MODEL, D_MODEL), BF16),
        ],
        scratch_shapes=[pltpu.VMEM((SUBLANES, B_WIDTH), F32)],
        compiler_params=pltpu.CompilerParams(
            dimension_semantics=("arbitrary", "arbitrary"), vmem_limit_bytes=VMEM_LIMIT),
        name="mix_prompt",
    )(x, mod, w_in, sgu_g, sgu_bln, sgu_w, sgu_bias_t, conv_w, conv_b, w_out,
      ln1_g, ln1_b, router_wt2, router_b_col)


def _mix_sample_kernel(alpha_res, x_ref, mod_ref, prev0_ref, prev1_ref, w_in_ref, sgu_g_ref,
                       sgu_bln_ref, sgu_w00_ref, sgu_b0_ref, conv_w_ref, conv_b_ref, w_out_ref,
                       ln1_g_ref, ln1_b_ref, router_w_ref, router_b_ref,
                       x1_ref, h2_ref, meta_ref, cnt_ref, q_ref, vn_ref):
    x = x_ref[...]
    sh1 = mod_ref[:, 0:D_MODEL]
    sc1 = mod_ref[:, D_MODEL:2 * D_MODEL]
    g1 = mod_ref[:, 2 * D_MODEL:3 * D_MODEL]
    sh2 = mod_ref[:, 3 * D_MODEL:4 * D_MODEL]
    sc2 = mod_ref[:, 4 * D_MODEL:5 * D_MODEL]
    h = (x * (1.0 + sc1) + sh1).astype(BF16)
    z = _dot(h, w_in_ref[...])
    u = z[:, 0:A_WIDTH]
    v = z[:, A_WIDTH:2 * A_WIDTH]
    gate_b = z[:, 2 * A_WIDTH:2 * A_WIDTH + B_WIDTH]
    gate_c = z[:, 2 * A_WIDTH + B_WIDTH:2 * A_WIDTH + 2 * B_WIDTH]
    hb = z[:, 2 * A_WIDTH + 2 * B_WIDTH:]

    vn_parts = []
    for hd in range(A_HEADS):
        sl = slice(hd * A_HEAD_DIM, (hd + 1) * A_HEAD_DIM)
        vn_parts.append(_layer_norm(v[:, sl], sgu_g_ref[hd:hd + 1, :], sgu_bln_ref[hd:hd + 1, :]))
    vn = jnp.concatenate(vn_parts, axis=-1)
    vn_ref[...] = vn
    a_out = u * (vn * sgu_w00_ref[...] + sgu_b0_ref[...])

    q = gate_c * hb
    q_ref[...] = q
    conv = (conv_b_ref[...] + prev0_ref[...] * conv_w_ref[0:1, :] + prev1_ref[...] * conv_w_ref[1:2, :]
            + q * conv_w_ref[2:3, :])
    b_out = gate_b * conv

    mix_in = jnp.concatenate([a_out, b_out], axis=-1).astype(BF16)
    mix = _dot(mix_in, w_out_ref[...])
    x1 = _layer_norm(alpha_res * x + g1 * mix, ln1_g_ref[...], ln1_b_ref[...])
    x1_ref[...] = x1
    h2 = x1 * (1.0 + sc2) + sh2
    h2_ref[...] = h2.astype(BF16)
    (meta, cnt), = _route([h2], router_w_ref[...], router_b_ref[...])
    meta_ref[...] = meta
    cnt_ref[...] = cnt.astype(jnp.int32)


def _mix_sample(x, mod, prev0, prev1, w_in_bf, sgu_g, sgu_bln, sgu_w00, sgu_b0, conv_w, conv_b,
                w_out_bf, ln1_g, ln1_b, router_wt2, router_b_col, alpha_res):
    n = x.shape[0]
    return pl.pallas_call(
        functools.partial(_mix_sample_kernel, alpha_res),
        out_shape=[
            jax.ShapeDtypeStruct((n, D_MODEL), F32),
            jax.ShapeDtypeStruct((n, D_MODEL), BF16),
            jax.ShapeDtypeStruct((SUBLANES, n), F32),
            jax.ShapeDtypeStruct((N_EXPERTS, LANES), jnp.int32),
            jax.ShapeDtypeStruct((n, B_WIDTH), F32),
            jax.ShapeDtypeStruct((n, A_WIDTH), F32),
        ],
        compiler_params=pltpu.CompilerParams(vmem_limit_bytes=VMEM_LIMIT),
        name="mix_sample",
    )(x, mod, prev0, prev1, w_in_bf, sgu_g, sgu_bln, sgu_w00, sgu_b0, conv_w, conv_b, w_out_bf,
      ln1_g, ln1_b, router_wt2, router_b_col)


def _strip_copy(src_ref, src_row, dst_ref, dst_row, n_rows, sem, priority=0):
    @pl.when(n_rows > 0)
    def _():
        pltpu.make_async_copy(src_ref.at[pl.ds(src_row, n_rows)],
                              dst_ref.at[pl.ds(dst_row, n_rows)], sem).start(priority=priority)


def _for_each_expert(strip):
    for e in range(N_EXPERTS):
        strip(e, 0)


def _wait_rows(hbm_ref, vmem_ref, n_rows, sem):
    pltpu.make_async_copy(hbm_ref.at[pl.ds(0, n_rows)], vmem_ref.at[pl.ds(0, n_rows)], sem).wait()


def _sort_kernel(n_tok, tile0, n_steps, subs, first, *refs):
    if first:
        (sorted_row_ref, off_ref, cnt_ref, pad_row_ref, pad_n_ref, h2_ref, meta_ref,
         xs_ref, stage_ref, zero_ref, sem, pad_sem) = refs
    else:
        (sorted_row_ref, off_ref, cnt_ref, pad_row_ref, pad_n_ref, h2_ref, meta_ref, _,
         xs_ref, stage_ref, sem) = refs
    j = pl.program_id(0)
    n_sorted = n_tok * TOP_K
    parity = j % 2

    if first:
        @pl.when(j == 0)
        def _():
            zero_ref[...] = jnp.zeros_like(zero_ref)

            def start(e, carry):
                _strip_copy(zero_ref, 0, xs_ref, pad_row_ref[e], pad_n_ref[e], pad_sem)
                return carry
            lax.fori_loop(0, N_EXPERTS, start, 0)

            def wait(e, carry):
                @pl.when(pad_n_ref[e] > 0)
                def _():
                    _wait_rows(xs_ref, zero_ref, pad_n_ref[e], pad_sem)
                return carry
            lax.fori_loop(0, N_EXPERTS, wait, 0)

    sub_r = lax.broadcasted_iota(jnp.int32, (n_sorted, n_tok), 0)
    for sub in range(subs):
        slot = parity * subs + sub
        meta = meta_ref[sub]
        sel = sub_r == meta[0:1, :].astype(jnp.int32)
        for k in range(1, TOP_K):
            sel = jnp.logical_or(sel, sub_r == meta[k:k + 1, :].astype(jnp.int32))
        perm = jnp.where(sel, 1.0, 0.0).astype(BF16)
        rows = _dot(perm, h2_ref[sub * n_tok:(sub + 1) * n_tok, :])
        _store_rows(stage_ref, slot, rows)

        base = (tile0 + j * subs + sub) * N_EXPERTS

        def strip(e, priority, slot=slot, base=base):
            _strip_copy(stage_ref.at[slot], off_ref[base + e], xs_ref, sorted_row_ref[base + e],
                        cnt_ref[base + e], sem.at[slot], priority)
        _for_each_expert(strip)

    @pl.when(j > 0)
    def _():
        for sub in range(subs):
            _wait_rows(xs_ref, stage_ref.at[0], n_sorted, sem.at[(1 - parity) * subs + sub])

    @pl.when(j == n_steps - 1)
    def _():
        for sub in range(subs):
            _wait_rows(xs_ref, stage_ref.at[0], n_sorted, sem.at[parity * subs + sub])


def _sort(tables, pad_row, pad_n, h2, meta, xs, n_sorted_rows, n_tok, tile0, subs):
    sorted_row, off, cnt = tables
    first = xs is None
    n_steps = h2.shape[0] // (n_tok * subs)
    in_specs = [pl.BlockSpec((subs * n_tok, D_MODEL), lambda j, *_: (j, 0)),
                pl.BlockSpec((subs, SUBLANES, n_tok), lambda j, *_: (j, 0, 0))]
    operands = [sorted_row, off, cnt, pad_row, pad_n, h2, meta]
    scratch = [pltpu.VMEM((2 * subs, n_tok * TOP_K, ROW_SLABS, LANES), BF16)]
    if first:
        scratch.append(pltpu.VMEM((ROW_TILE, ROW_SLABS, LANES), BF16))
        aliases = {}
    else:
        in_specs.append(pl.BlockSpec(memory_space=pl.ANY))
        operands.append(xs)
        aliases = {len(operands) - 1: 0}
    scratch.append(pltpu.SemaphoreType.DMA((2 * subs,)))
    if first:
        scratch.append(pltpu.SemaphoreType.DMA(()))
    grid_spec = pltpu.PrefetchScalarGridSpec(
        num_scalar_prefetch=5,
        grid=(n_steps,),
        in_specs=in_specs,
        out_specs=pl.BlockSpec(memory_space=pl.ANY),
        scratch_shapes=scratch,
    )
    return pl.pallas_call(
        functools.partial(_sort_kernel, n_tok, tile0, n_steps, subs, first),
        grid_spec=grid_spec,
        out_shape=jax.ShapeDtypeStruct((n_sorted_rows, ROW_SLABS, LANES), BF16),
        input_output_aliases=aliases,
        compiler_params=pltpu.CompilerParams(
            dimension_semantics=("arbitrary",), vmem_limit_bytes=VMEM_LIMIT),
        name="sort_first" if first else "sort_more",
    )(*operands)


def _experts_kernel(tile_start_ref, n_tile_ref, n_used_ref, xs_ref, w_gu_ref, b_gu_ref, w_down_ref,
                    b_down_ref, ys_ref, w_gu_bf_ref, w_down_bf_ref, x_buf, y_buf, x_sem, y_sem):
    e = pl.program_id(0)
    n_used = n_used_ref[0]
    first_tile = tile_start_ref[e]
    n_tile = n_tile_ref[e]

    def x_copy(g):
        slot = g % TILE_SLOTS
        return pltpu.make_async_copy(xs_ref.at[pl.ds(g * ROW_TILE, ROW_TILE)], x_buf.at[slot],
                                     x_sem.at[slot])

    def y_copy(g):
        slot = g % TILE_SLOTS
        return pltpu.make_async_copy(y_buf.at[slot], ys_ref.at[pl.ds(g * ROW_TILE, ROW_TILE)],
                                     y_sem.at[slot])

    def request(g):
        @pl.when(g < n_used)
        def _():
            x_copy(g).start(priority=1)

    @pl.when(e == 0)
    def _():
        for g in range(TILE_AHEAD):
            request(g)

    @pl.when(n_tile > 0)
    def _():
        w_gu_bf_ref[...] = w_gu_ref[...].astype(BF16)
        w_down_bf_ref[...] = w_down_ref[...].astype(BF16)

    def begin(g):
        x_copy(g).wait()
        request(g + TILE_AHEAD)

        @pl.when(g >= TILE_SLOTS)
        def _():
            y_copy(g - TILE_SLOTS).wait()

    def compute(g):
        slot = g % TILE_SLOTS
        x = _load_rows(x_buf, slot)
        gu = _dot(x, w_gu_bf_ref[...]) + b_gu_ref[...]
        gate = jnp.minimum(gu[:, :D_EXPERT], SWIGLU_LIMIT)
        up = jnp.clip(gu[:, D_EXPERT:], -SWIGLU_LIMIT, SWIGLU_LIMIT)
        act = (up + 1.0) * gate * jax.nn.sigmoid(SWIGLU_ALPHA * gate)
        y = _dot(act.astype(BF16), w_down_bf_ref[...]) + b_down_ref[...]
        _store_rows(y_buf, slot, y)
        y_copy(g).start(priority=1)

    def group(g, size):
        for k in range(size):
            begin(g + k)
        for k in range(size):
            compute(g + k)

    def group_body(p, carry):
        group(first_tile + TILE_GROUP * p, TILE_GROUP)
        return carry

    n_group = n_tile // TILE_GROUP
    lax.fori_loop(0, n_group, group_body, 0)
    rest = n_tile - n_group * TILE_GROUP
    g_rest = first_tile + n_group * TILE_GROUP
    size = TILE_GROUP // 2
    while size >= 1:
        @pl.when((rest & size) != 0)
        def _(g_rest=g_rest, size=size):
            group(g_rest, size)
        g_rest = g_rest + (rest & size)
        size //= 2

    @pl.when(e == N_EXPERTS - 1)
    def _():
        for back in range(TILE_SLOTS, 0, -1):
            @pl.when(n_used >= back)
            def _(back=back):
                y_copy(n_used - back).wait()


def _experts(tile_start, n_tile_e, n_used, xs, w_gu, b_gu, w_down, b_down):
    w_blk = lambda e, *_: (e, 0, 0)
    grid_spec = pltpu.PrefetchScalarGridSpec(
        num_scalar_prefetch=3,
        grid=(N_EXPERTS,),
        in_specs=[
            pl.BlockSpec(memory_space=pl.ANY),
            pl.BlockSpec((None, D_MODEL, 2 * D_EXPERT), w_blk),
            pl.BlockSpec((None, 1, 2 * D_EXPERT), w_blk),
            pl.BlockSpec((None, D_EXPERT, D_MODEL), w_blk),
            pl.BlockSpec((None, 1, D_MODEL), w_blk),
        ],
        out_specs=pl.BlockSpec(memory_space=pl.ANY),
        scratch_shapes=[pltpu.VMEM((D_MODEL, 2 * D_EXPERT), BF16),
                        pltpu.VMEM((D_EXPERT, D_MODEL), BF16),
                        pltpu.VMEM((TILE_SLOTS, ROW_TILE, ROW_SLABS, LANES), BF16),
                        pltpu.VMEM((TILE_SLOTS, ROW_TILE, ROW_SLABS, LANES), BF16),
                        pltpu.SemaphoreType.DMA((TILE_SLOTS,)),
                        pltpu.SemaphoreType.DMA((TILE_SLOTS,))],
    )
    return pl.pallas_call(
        _experts_kernel,
        grid_spec=grid_spec,
        out_shape=jax.ShapeDtypeStruct(xs.shape, BF16),
        compiler_params=pltpu.CompilerParams(
            dimension_semantics=("arbitrary",), vmem_limit_bytes=VMEM_LIMIT),
        name="experts",
    )(tile_start, n_tile_e, n_used, xs, w_gu, b_gu.reshape(N_EXPERTS, 1, -1), w_down,
      b_down.reshape(N_EXPERTS, 1, -1))


def _combine_kernel(n_tok, tile0, n_steps, subs, seq_steps, alpha_res, sorted_row_ref, off_ref, cnt_ref,
                    ys_ref, x1_ref, meta_ref, mod_ref, ln2_g_ref, ln2_b_ref, out_ref,
                    stage_ref, sem):
    j = pl.program_id(0)
    n_sorted = n_tok * TOP_K
    parity = j % 2

    def start_step(step, step_parity, live):
        for sub in range(subs):
            base = (tile0 + step * subs + sub) * N_EXPERTS
            to_slot = step_parity * subs + sub

            def strip(e, priority, base=base, to_slot=to_slot):
                _strip_copy(ys_ref, sorted_row_ref[base + e], stage_ref.at[to_slot],
                            off_ref[base + e], jnp.where(live, cnt_ref[base + e], 0),
                            sem.at[to_slot], priority)
            _for_each_expert(strip)

    @pl.when(j == 0)
    def _():
        start_step(j, parity, True)

    start_step(jnp.minimum(j + 1, n_steps - 1), 1 - parity, j + 1 < n_steps)

    sub_r = lax.broadcasted_iota(jnp.int32, (n_sorted, n_tok), 0)
    for sub in range(subs):
        slot = parity * subs + sub
        rows = slice(sub * n_tok, (sub + 1) * n_tok)
        meta = meta_ref[sub]
        comb_t = jnp.zeros((n_sorted, n_tok), F32)
        for k in range(TOP_K):
            comb_t = jnp.where(sub_r == meta[k:k + 1, :].astype(jnp.int32),
                               meta[TOP_K + k:TOP_K + k + 1, :], comb_t)
        comb_t = comb_t.astype(BF16)

        _wait_rows(ys_ref, stage_ref.at[0], n_sorted, sem.at[slot])
        ys = _load_rows(stage_ref, slot)
        ffn = lax.dot_general(comb_t, ys, (((0,), (0,)), ((), ())), preferred_element_type=F32)
        if seq_steps is None:
            g2 = mod_ref[rows, 5 * D_MODEL:6 * D_MODEL]
        else:
            g2 = mod_ref[pl.ds(j // seq_steps, 1), 5 * D_MODEL:6 * D_MODEL]
        out_ref[rows, :] = _layer_norm(alpha_res * x1_ref[rows, :] + g2 * ffn, ln2_g_ref[...],
                                       ln2_b_ref[...])


def _combine(tables, ys, x1, meta, mod, ln2_g, ln2_b, n_tok, tile0, subs, alpha_res):
    sorted_row, off, cnt = tables
    step_tok = subs * n_tok
    n_steps = x1.shape[0] // step_tok
    if mod.shape[0] == x1.shape[0]:
        seq_steps = None
        mod_spec = pl.BlockSpec((step_tok, N_MOD * D_MODEL), lambda j, *_: (j, 0))
    else:
        seq_steps = n_steps // mod.shape[0]
        mod_spec = pl.BlockSpec(mod.shape, lambda j, *_: (0, 0))
    grid_spec = pltpu.PrefetchScalarGridSpec(
        num_scalar_prefetch=3,
        grid=(n_steps,),
        in_specs=[
            pl.BlockSpec(memory_space=pl.ANY),
            pl.BlockSpec((step_tok, D_MODEL), lambda j, *_: (j, 0)),
            pl.BlockSpec((subs, SUBLANES, n_tok), lambda j, *_: (j, 0, 0)),
            mod_spec,
            pl.BlockSpec((1, D_MODEL), lambda j, *_: (0, 0)),
            pl.BlockSpec((1, D_MODEL), lambda j, *_: (0, 0)),
        ],
        out_specs=pl.BlockSpec((step_tok, D_MODEL), lambda j, *_: (j, 0)),
        scratch_shapes=[pltpu.VMEM((2 * subs, n_tok * TOP_K, ROW_SLABS, LANES), BF16),
                        pltpu.SemaphoreType.DMA((2 * subs,))],
    )
    return pl.pallas_call(
        functools.partial(_combine_kernel, n_tok, tile0, n_steps, subs, seq_steps, alpha_res),
        grid_spec=grid_spec,
        out_shape=jax.ShapeDtypeStruct(x1.shape, F32),
        compiler_params=pltpu.CompilerParams(
            dimension_semantics=("arbitrary",), vmem_limit_bytes=VMEM_LIMIT),
        name="combine",
    )(sorted_row, off, cnt, ys, x1, meta, mod, ln2_g, ln2_b)


def _routing_tables(cnt_all):
    total = jnp.sum(cnt_all, axis=0)
    n_tile_e = (total + ROW_TILE - 1) // ROW_TILE
    tile_end = jnp.cumsum(n_tile_e)
    tile_start = tile_end - n_tile_e
    row_start = tile_start * ROW_TILE
    cum = jnp.cumsum(cnt_all, axis=0) - cnt_all
    off = jnp.cumsum(cnt_all, axis=1) - cnt_all
    sorted_row = row_start[None, :] + cum
    n_used = tile_end[-1]
    pad_row = row_start + total
    pad_n = n_tile_e * ROW_TILE - total
    i32 = lambda a: a.astype(jnp.int32)
    return ((i32(sorted_row).reshape(-1), i32(off).reshape(-1), i32(cnt_all).reshape(-1)),
            i32(pad_row), i32(pad_n), i32(tile_start), i32(n_tile_e), i32(n_used).reshape(1))


def kernel(x_prompt, x_sample, state_conv, c_prompt, c_sample, ada_w, ada_b, w_in, sgu_ln_g, sgu_ln_b,
           sgu_w, sgu_b, conv_w, conv_b, w_out, ln1_g, ln1_b, router_w, router_b, w_gu, b_gu,
           w_down, b_down, ln2_g, ln2_b):
    depth = ada_w.shape[0]
    assert depth == 1
    bsz, seq, _ = x_prompt.shape
    n_dec = x_sample.shape[0]
    assert x_sample.shape[1] == 1 and seq % TOK_TILE == 0
    alpha_res = (2.0 * depth) ** 0.25
    l = 0

    mod_s, mod_p = _ada(c_sample, c_prompt, ada_w[l], ada_b[l])

    router_wt =jnp.transpose(router_w[l])
    router_wt_hi = router_wt.astype(BF16)
    router_wt_lo = (router_wt - router_wt_hi.astype(F32)).astype(BF16)
    router_wt2 = jnp.concatenate([router_wt_hi, router_wt_lo], axis=0)
    router_b_col = router_b[l].reshape(N_EXPERTS, 1)
    row = lambda a: a.reshape(1, -1)

    x1_p, h2_p, meta_p, cnt_p, convst_p, w_in_bf, w_out_bf = _mix_prompt(
        x_prompt, mod_p, w_in[l], sgu_ln_g[l], sgu_ln_b[l], sgu_w[l], jnp.transpose(sgu_b[l]),
        conv_w[l], row(conv_b[l]), w_out[l], row(ln1_g[l]), row(ln1_b[l]), router_wt2,
        router_b_col, alpha_res)
    x1_s, h2_s, meta_s, cnt_s, q_s, vn_s = _mix_sample(
        x_sample.reshape(n_dec, D_MODEL), mod_s, state_conv[l, :, 0, :], state_conv[l, :, 1, :],
        w_in_bf, sgu_ln_g[l], sgu_ln_b[l], row(jnp.repeat(sgu_w[l, :, 0, 0], A_HEAD_DIM)),
        row(jnp.repeat(sgu_b[l, :, 0], A_HEAD_DIM)), conv_w[l], row(conv_b[l]), w_out_bf,
        row(ln1_g[l]), row(ln1_b[l]), router_wt2, router_b_col, alpha_res)

    n_ptiles = bsz * seq // TOK_TILE
    cnt_all = jnp.concatenate([cnt_p[:, :, 0], cnt_s[None, :, 0]], axis=0)
    n_assign = (bsz * seq + n_dec) * TOP_K
    n_row_tiles = -(-n_assign // ROW_TILE) + N_EXPERTS
    tables, pad_row, pad_n, tile_start, n_tile_e, n_used = _routing_tables(cnt_all)

    n_sorted_rows = n_row_tiles * ROW_TILE
    meta_s = meta_s[None]
    xs = _sort(tables, pad_row, pad_n, h2_p, meta_p, None, n_sorted_rows, TOK_TILE, 0, MOE_SUB)
    xs = _sort(tables, pad_row, pad_n, h2_s, meta_s, xs, n_sorted_rows, n_dec, n_ptiles, 1)
    ys = _experts(tile_start, n_tile_e, n_used, xs, w_gu[l], b_gu[l], w_down[l], b_down[l])
    y_p = _combine(tables, ys, x1_p, meta_p, mod_p, row(ln2_g[l]), row(ln2_b[l]), TOK_TILE, 0,
                   MOE_SUB, alpha_res)
    y_s = _combine(tables, ys, x1_s, meta_s, mod_s, row(ln2_g[l]), row(ln2_b[l]), n_dec, n_ptiles,
                   1, alpha_res)

    conv_state_sample = jnp.stack([state_conv[l, :, 1, :], q_s], axis=1)[None]
    return (y_p.reshape(bsz, seq, D_MODEL),
            y_s.reshape(n_dec, 1, D_MODEL),
            convst_p[None],
            conv_state_sample,
            vn_s.reshape(1, n_dec, 1, A_HEADS, A_HEAD_DIM))
```

```python
import functools

import jax
import jax.numpy as jnp
from jax import lax
from jax.experimental import pallas as pl
from jax.experimental.pallas import tpu as pltpu

F32 = jnp.float32
BF16 = jnp.bfloat16

D_MODEL = 1024
A_WIDTH = 512
B_WIDTH = 512
A_HEADS = 4
A_HEAD_DIM = 128
CHUNK = 128
PROJ_COLS = 2 * A_WIDTH + 3 * B_WIDTH
N_EXPERTS = 32
TOP_K = 4
D_EXPERT = 1024
SWIGLU_LIMIT = 7.0
SWIGLU_ALPHA = 1.702
LN_EPS = 1e-5
N_MOD = 6

LANES = 128
SUBLANES = 8
ROW_SLABS = D_MODEL // LANES
TOK_TILE = 256
MIX_SUB = 2
SORT_SUB = 4
COMBINE_SUB = 2
ROW_TILE = 256
TILE_GROUP = 4
TILE_AHEAD = TILE_GROUP
TILE_SLOTS = 2 * TILE_GROUP
VMEM_LIMIT = 56 * 1024 * 1024


def _layer_norm(x, g, b):
    mu = jnp.mean(x, axis=-1, keepdims=True)
    xc = x - mu
    var = jnp.mean(xc * xc, axis=-1, keepdims=True)
    return xc * lax.rsqrt(var + LN_EPS) * g + b


def _dot(a, b):
    return jnp.dot(a, b, preferred_element_type=F32)


def _dot_nt(a, b):
    return lax.dot_general(a, b, (((1,), (1,)), ((), ())), preferred_element_type=F32)


def _store_rows(ref, slot, rows):
    n = rows.shape[0]
    ref[slot] = rows.astype(BF16).reshape(n, ROW_SLABS, LANES)


def _load_rows(ref, slot):
    return ref[slot].reshape(ref.shape[1], D_MODEL)


def _split_bf16(a):
    hi = a.astype(BF16)
    lo = (a - hi.astype(F32)).astype(BF16)
    return hi, lo


def _ada_kernel(c_a_ref, c_b_ref, w_ref, b_ref, o_a_ref, o_b_ref):
    n_a = c_a_ref.shape[0]
    c = jnp.concatenate([c_a_ref[...], c_b_ref[...]], axis=0)
    s_hi, s_lo = _split_bf16(c * jax.nn.sigmoid(c))
    w_hi, w_lo = _split_bf16(w_ref[...])
    m = _dot(s_hi, w_hi) + _dot(s_hi, w_lo) + _dot(s_lo, w_hi) + b_ref[...]
    o_a_ref[...] = m[:n_a]
    o_b_ref[...] = m[n_a:]


def _ada(c_a, c_b, ada_w, ada_b):
    cols = 3 * D_MODEL // 2
    rows_spec = lambda c: pl.BlockSpec((c.shape[0], D_MODEL), lambda n: (0, 0))
    out_spec = lambda c: pl.BlockSpec((c.shape[0], cols), lambda n: (0, n))
    return pl.pallas_call(
        _ada_kernel,
        grid=(N_MOD * D_MODEL // cols,),
        in_specs=[
            rows_spec(c_a),
            rows_spec(c_b),
            pl.BlockSpec((D_MODEL, cols), lambda n: (0, n)),
            pl.BlockSpec((1, cols), lambda n: (0, n)),
        ],
        out_specs=[out_spec(c_a), out_spec(c_b)],
        out_shape=[jax.ShapeDtypeStruct((c_a.shape[0], N_MOD * D_MODEL), F32),
                   jax.ShapeDtypeStruct((c_b.shape[0], N_MOD * D_MODEL), F32)],
        name="ada",
    )(c_a, c_b, ada_w, ada_b.reshape(1, -1))


def _route(h2_tiles, router_wt2, router_b):
    n_tiles = len(h2_tiles)
    n_tile_tok = h2_tiles[0].shape[0]
    h2 = h2_tiles[0] if n_tiles == 1 else jnp.concatenate(h2_tiles, axis=0)
    n = h2.shape[0]
    lanes = [slice(i * n_tile_tok, (i + 1) * n_tile_tok) for i in range(n_tiles)]
    h_hi = h2.astype(BF16)
    h_lo = (h2 - h_hi.astype(F32)).astype(BF16)
    l1 = _dot_nt(router_wt2, h_hi)
    l2 = _dot_nt(router_wt2[:N_EXPERTS], h_lo)
    logits = l1[:N_EXPERTS] + l1[N_EXPERTS:] + l2 + router_b

    sub = lax.broadcasted_iota(jnp.int32, (N_EXPERTS, n), 0)
    work = logits
    vals, hots = [], []
    for _ in range(TOP_K):
        m = jnp.max(work, axis=0, keepdims=True)
        idx = jnp.min(jnp.where(work == m, sub, N_EXPERTS), axis=0, keepdims=True)
        hot = sub == idx
        work = jnp.where(hot, -jnp.inf, work)
        vals.append(m)
        hots.append(hot)
    exps = [jnp.exp(v - vals[0]) for v in vals]
    denom = exps[0] + exps[1] + exps[2] + exps[3]
    gates = [e / denom for e in exps]

    onehot = jnp.zeros((N_EXPERTS, n), F32)
    for hot in hots:
        onehot = onehot + hot.astype(F32)
    onehot_bf = onehot.astype(BF16)
    t_r = lax.broadcasted_iota(jnp.int32, (n_tile_tok, n_tile_tok), 0)
    t_c = lax.broadcasted_iota(jnp.int32, (n_tile_tok, n_tile_tok), 1)
    stacked = onehot_bf if n_tiles == 1 else jnp.concatenate([onehot_bf[:, ln] for ln in lanes], axis=0)
    rank_st = _dot(stacked, (t_r < t_c).astype(BF16))
    e_r = lax.broadcasted_iota(jnp.int32, (N_EXPERTS, N_EXPERTS), 0)
    e_c = lax.broadcasted_iota(jnp.int32, (N_EXPERTS, N_EXPERTS), 1)
    below = _dot((e_c < e_r).astype(BF16), onehot_bf)
    cnts, bases = [], []
    for i, ln in enumerate(lanes):
        cnts.append(jnp.sum(onehot[:, ln], axis=1, keepdims=True))
        off = jnp.sum(below[:, ln], axis=1, keepdims=True)
        bases.append(rank_st[i * N_EXPERTS:(i + 1) * N_EXPERTS] + off)
    base = bases[0] if n_tiles == 1 else jnp.concatenate(bases, axis=1)

    row8 = lax.broadcasted_iota(jnp.int32, (SUBLANES, n), 0)
    meta = jnp.zeros((SUBLANES, n), F32)
    for k in range(TOP_K):
        pos_k = jnp.sum(jnp.where(hots[k], base, 0.0), axis=0, keepdims=True)
        meta = jnp.where(row8 == k, pos_k, meta)
        meta = jnp.where(row8 == TOP_K + k, gates[k], meta)
    return [(meta[:, ln], jnp.broadcast_to(c, (N_EXPERTS, LANES))) for ln, c in zip(lanes, cnts)]


def _mix_prompt_kernel(alpha_res, x_ref, mod_ref, w_in_ref, sgu_g_ref, sgu_bln_ref, sgu_w_ref,
                       sgu_bias_ref, conv_w_ref, conv_b_ref, w_out_ref, ln1_g_ref, ln1_b_ref,
                       router_w_ref, router_b_ref,
                       x1_ref, h2_ref, meta_ref, cnt_ref, convst_ref, w_in_bf_ref, w_out_bf_ref,
                       carry_ref):
    t = pl.program_id(1)

    @pl.when(jnp.logical_and(pl.program_id(0) == 0, t == 0))
    def _():
        w_in_bf_ref[...] = w_in_ref[...].astype(BF16)
        w_out_bf_ref[...] = w_out_ref[...].astype(BF16)

    @pl.when(t == 0)
    def _():
        carry_ref[...] = jnp.zeros_like(carry_ref)

    seq_row = pl.ds(pl.program_id(0), 1)
    sh1, sc1, g1, sh2, sc2 = [mod_ref[seq_row, k * D_MODEL:(k + 1) * D_MODEL] for k in range(5)]
    r_i = lax.broadcasted_iota(jnp.int32, (CHUNK, CHUNK), 0)
    c_i = lax.broadcasted_iota(jnp.int32, (CHUNK, CHUNK), 1)
    tril = c_i <= r_i
    tm = TOK_TILE
    prev2 = carry_ref[0:1, :]
    prev1 = carry_ref[1:2, :]

    subs = range(MIX_SUB)
    tile_rows = [slice(sub * tm, (sub + 1) * tm) for sub in subs]
    xs_in = [x_ref[rows, :] for rows in tile_rows]
    zs = [_dot((x * (1.0 + sc1) + sh1).astype(BF16), w_in_bf_ref[...]) for x in xs_in]

    mix_ins = []
    for sub in subs:
        z = zs[sub]
        u = z[:, 0:A_WIDTH]
        v = z[:, A_WIDTH:2 * A_WIDTH]
        gate_b = z[:, 2 * A_WIDTH:2 * A_WIDTH + B_WIDTH]
        gate_c = z[:, 2 * A_WIDTH + B_WIDTH:2 * A_WIDTH + 2 * B_WIDTH]
        hb = z[:, 2 * A_WIDTH + 2 * B_WIDTH:]

        a_parts = []
        for hd in range(A_HEADS):
            sl = slice(hd * A_HEAD_DIM, (hd + 1) * A_HEAD_DIM)
            vn = _layer_norm(v[:, sl], sgu_g_ref[hd:hd + 1, :], sgu_bln_ref[hd:hd + 1, :]).astype(BF16)
            wm = jnp.where(tril, sgu_w_ref[hd], 0.0).astype(BF16)
            bias = sgu_bias_ref[:, hd:hd + 1]
            s_parts = []
            for c in range(tm // CHUNK):
                s_parts.append(_dot(wm, vn[c * CHUNK:(c + 1) * CHUNK, :]) + bias)
            a_parts.append(u[:, sl] * jnp.concatenate(s_parts, axis=0))

        q = gate_c * hb
        row = lax.broadcasted_iota(jnp.int32, q.shape, 0)
        q_m1 = jnp.where(row == 0, prev1, pltpu.roll(q, 1, 0))
        q_m2 = jnp.where(row == 0, prev2, jnp.where(row == 1, prev1, pltpu.roll(q, 2, 0)))
        conv = (conv_b_ref[...] + q_m2 * conv_w_ref[0:1, :] + q_m1 * conv_w_ref[1:2, :]
                + q * conv_w_ref[2:3, :])
        b_out = gate_b * conv
        prev2 = q[tm - 2:tm - 1, :]
        prev1 = q[tm - 1:tm, :]

        mix_ins.append(jnp.concatenate(a_parts + [b_out], axis=-1).astype(BF16))

    mixes = [_dot(mix_in, w_out_bf_ref[...]) for mix_in in mix_ins]
    h2s = []
    for sub in subs:
        x1 = _layer_norm(alpha_res * xs_in[sub] + g1 * mixes[sub], ln1_g_ref[...], ln1_b_ref[...])
        x1_ref[tile_rows[sub], :] = x1
        h2 = x1 * (1.0 + sc2) + sh2
        h2_ref[tile_rows[sub], :] = h2.astype(BF16)
        h2s.append(h2)
    routed = _route(h2s, router_w_ref[...], router_b_ref[...])
    for sub in subs:
        meta, cnt = routed[sub]
        meta_ref[sub] = meta
        cnt_ref[sub] = cnt.astype(jnp.int32)

    last2 = jnp.concatenate([prev2, prev1], axis=0)
    carry_ref[0:2, :] = last2
    convst_ref[...] = last2


def _mix_prompt(x, mod, w_in, sgu_g, sgu_bln, sgu_w, sgu_bias_t, conv_w, conv_b, w_out,
                ln1_g, ln1_b, router_wt2, router_b_col, alpha_res):
    bsz, seq, _ = x.shape
    step_tok = MIX_SUB * TOK_TILE
    steps = seq // step_tok
    n_tok = bsz * seq
    tiles = seq // TOK_TILE
    const2 = lambda b, t: (0, 0)
    tok = lambda b, t: (b * steps + t, 0)
    tile3 = lambda b, t: (b * steps + t, 0, 0)
    return pl.pallas_call(
        functools.partial(_mix_prompt_kernel, alpha_res),
        grid=(bsz, steps),
        in_specs=[
            pl.BlockSpec((None, step_tok, D_MODEL), lambda b, t: (b, t, 0)),
            pl.BlockSpec((bsz, N_MOD * D_MODEL), const2),
            pl.BlockSpec((D_MODEL, PROJ_COLS), const2, pipeline_mode=pl.Buffered(1)),
            pl.BlockSpec((A_HEADS, A_HEAD_DIM), const2),
            pl.BlockSpec((A_HEADS, A_HEAD_DIM), const2),
            pl.BlockSpec((A_HEADS, CHUNK, CHUNK), lambda b, t: (0, 0, 0)),
            pl.BlockSpec((CHUNK, A_HEADS), const2),
            pl.BlockSpec((3, B_WIDTH), const2),
            pl.BlockSpec((1, B_WIDTH), const2),
            pl.BlockSpec((D_MODEL, D_MODEL), const2, pipeline_mode=pl.Buffered(1)),
            pl.BlockSpec((1, D_MODEL), const2),
            pl.BlockSpec((1, D_MODEL), const2),
            pl.BlockSpec((2 * N_EXPERTS, D_MODEL), const2),
            pl.BlockSpec((N_EXPERTS, 1), const2),
        ],
        out_specs=[
            pl.BlockSpec((step_tok, D_MODEL), tok),
            pl.BlockSpec((step_tok, D_MODEL), tok),
            pl.BlockSpec((MIX_SUB, SUBLANES, TOK_TILE), tile3),
            pl.BlockSpec((MIX_SUB, N_EXPERTS, LANES), tile3),
            pl.BlockSpec((None, 2, B_WIDTH), lambda b, t: (b, 0, 0)),
            pl.BlockSpec((D_MODEL, PROJ_COLS), const2),
            pl.BlockSpec((D_MODEL, D_MODEL), const2),
        ],
        out_shape=[
            jax.ShapeDtypeStruct((n_tok, D_MODEL), F32),
            jax.ShapeDtypeStruct((n_tok, D_MODEL), BF16),
            jax.ShapeDtypeStruct((bsz * tiles, SUBLANES, TOK_TILE), F32),
            jax.ShapeDtypeStruct((bsz * tiles, N_EXPERTS, LANES), jnp.int32),
            jax.ShapeDtypeStruct((bsz, 2, B_WIDTH), F32),
            jax.ShapeDtypeStruct((D_MODEL, PROJ_COLS), BF16),
            jax.ShapeDtypeStruct((D_MODEL, D_MODEL), BF16),
        ],
        scratch_shapes=[pltpu.VMEM((SUBLANES, B_WIDTH), F32)],
        compiler_params=pltpu.CompilerParams(
            dimension_semantics=("arbitrary", "arbitrary"), vmem_limit_bytes=VMEM_LIMIT),
        name="mix_prompt",
    )(x, mod, w_in, sgu_g, sgu_bln, sgu_w, sgu_bias_t, conv_w, conv_b, w_out,
      ln1_g, ln1_b, router_wt2, router_b_col)


def _mix_sample_kernel(alpha_res, x_ref, mod_ref, prev0_ref, prev1_ref, w_in_ref, sgu_g_ref,
                       sgu_bln_ref, sgu_w00_ref, sgu_b0_ref, conv_w_ref, conv_b_ref, w_out_ref,
                       ln1_g_ref, ln1_b_ref, router_w_ref, router_b_ref,
                       x1_ref, h2_ref, meta_ref, cnt_ref, q_ref, vn_ref):
    x = x_ref[...]
    sh1 = mod_ref[:, 0:D_MODEL]
    sc1 = mod_ref[:, D_MODEL:2 * D_MODEL]
    g1 = mod_ref[:, 2 * D_MODEL:3 * D_MODEL]
    sh2 = mod_ref[:, 3 * D_MODEL:4 * D_MODEL]
    sc2 = mod_ref[:, 4 * D_MODEL:5 * D_MODEL]
    h = (x * (1.0 + sc1) + sh1).astype(BF16)
    z = _dot(h, w_in_ref[...])
    u = z[:, 0:A_WIDTH]
    v = z[:, A_WIDTH:2 * A_WIDTH]
    gate_b = z[:, 2 * A_WIDTH:2 * A_WIDTH + B_WIDTH]
    gate_c = z[:, 2 * A_WIDTH + B_WIDTH:2 * A_WIDTH + 2 * B_WIDTH]
    hb = z[:, 2 * A_WIDTH + 2 * B_WIDTH:]

    vn_parts = []
    for hd in range(A_HEADS):
        sl = slice(hd * A_HEAD_DIM, (hd + 1) * A_HEAD_DIM)
        vn_parts.append(_layer_norm(v[:, sl], sgu_g_ref[hd:hd + 1, :], sgu_bln_ref[hd:hd + 1, :]))
    vn = jnp.concatenate(vn_parts, axis=-1)
    vn_ref[...] = vn
    a_out = u * (vn * sgu_w00_ref[...] + sgu_b0_ref[...])

    q = gate_c * hb
    q_ref[...] = q
    conv = (conv_b_ref[...] + prev0_ref[...] * conv_w_ref[0:1, :] + prev1_ref[...] * conv_w_ref[1:2, :]
            + q * conv_w_ref[2:3, :])
    b_out = gate_b * conv

    mix_in = jnp.concatenate([a_out, b_out], axis=-1).astype(BF16)
    mix = _dot(mix_in, w_out_ref[...])
    x1 = _layer_norm(alpha_res * x + g1 * mix, ln1_g_ref[...], ln1_b_ref[...])
    x1_ref[...] = x1
    h2 = x1 * (1.0 + sc2) + sh2
    h2_ref[...] = h2.astype(BF16)
    (meta, cnt), = _route([h2], router_w_ref[...], router_b_ref[...])
    meta_ref[...] = meta
    cnt_ref[...] = cnt.astype(jnp.int32)


def _mix_sample(x, mod, prev0, prev1, w_in_bf, sgu_g, sgu_bln, sgu_w00, sgu_b0, conv_w, conv_b,
                w_out_bf, ln1_g, ln1_b, router_wt2, router_b_col, alpha_res):
    n = x.shape[0]
    return pl.pallas_call(
        functools.partial(_mix_sample_kernel, alpha_res),
        out_shape=[
            jax.ShapeDtypeStruct((n, D_MODEL), F32),
            jax.ShapeDtypeStruct((n, D_MODEL), BF16),
            jax.ShapeDtypeStruct((SUBLANES, n), F32),
            jax.ShapeDtypeStruct((N_EXPERTS, LANES), jnp.int32),
            jax.ShapeDtypeStruct((n, B_WIDTH), F32),
            jax.ShapeDtypeStruct((n, A_WIDTH), F32),
        ],
        compiler_params=pltpu.CompilerParams(vmem_limit_bytes=VMEM_LIMIT),
        name="mix_sample",
    )(x, mod, prev0, prev1, w_in_bf, sgu_g, sgu_bln, sgu_w00, sgu_b0, conv_w, conv_b, w_out_bf,
      ln1_g, ln1_b, router_wt2, router_b_col)


def _strip_copy(src_ref, src_row, dst_ref, dst_row, n_rows, sem, priority=0):
    @pl.when(n_rows > 0)
    def _():
        pltpu.make_async_copy(src_ref.at[pl.ds(src_row, n_rows)],
                              dst_ref.at[pl.ds(dst_row, n_rows)], sem).start(priority=priority)


def _for_each_expert(strip):
    for e in range(N_EXPERTS):
        strip(e, 0)


def _wait_rows(hbm_ref, vmem_ref, n_rows, sem):
    pltpu.make_async_copy(hbm_ref.at[pl.ds(0, n_rows)], vmem_ref.at[pl.ds(0, n_rows)], sem).wait()


def _sort_kernel(n_tok, tile0, n_steps, subs, first, *refs):
    if first:
        (sorted_row_ref, off_ref, cnt_ref, pad_row_ref, pad_n_ref, h2_ref, meta_ref,
         xs_ref, stage_ref, zero_ref, sem, pad_sem) = refs
    else:
        (sorted_row_ref, off_ref, cnt_ref, pad_row_ref, pad_n_ref, h2_ref, meta_ref, _,
         xs_ref, stage_ref, sem) = refs
    j = pl.program_id(0)
    n_sorted = n_tok * TOP_K
    parity = j % 2

    if first:
        @pl.when(j == 0)
        def _():
            zero_ref[...] = jnp.zeros_like(zero_ref)

            def start(e, carry):
                _strip_copy(zero_ref, 0, xs_ref, pad_row_ref[e], pad_n_ref[e], pad_sem)
                return carry
            lax.fori_loop(0, N_EXPERTS, start, 0)

            def wait(e, carry):
                @pl.when(pad_n_ref[e] > 0)
                def _():
                    _wait_rows(xs_ref, zero_ref, pad_n_ref[e], pad_sem)
                return carry
            lax.fori_loop(0, N_EXPERTS, wait, 0)

    sub_r = lax.broadcasted_iota(jnp.int32, (n_sorted, n_tok), 0)
    for sub in range(subs):
        slot = parity * subs + sub
        meta = meta_ref[sub]
        sel = sub_r == meta[0:1, :].astype(jnp.int32)
        for k in range(1, TOP_K):
            sel = jnp.logical_or(sel, sub_r == meta[k:k + 1, :].astype(jnp.int32))
        perm = jnp.where(sel, 1.0, 0.0).astype(BF16)
        rows = _dot(perm, h2_ref[sub * n_tok:(sub + 1) * n_tok, :])
        _store_rows(stage_ref, slot, rows)

        base = (tile0 + j * subs + sub) * N_EXPERTS

        def strip(e, priority, slot=slot, base=base):
            _strip_copy(stage_ref.at[slot], off_ref[base + e], xs_ref, sorted_row_ref[base + e],
                        cnt_ref[base + e], sem.at[slot], priority)
        _for_each_expert(strip)

    @pl.when(j > 0)
    def _():
        for sub in range(subs):
            _wait_rows(xs_ref, stage_ref.at[0], n_sorted, sem.at[(1 - parity) * subs + sub])

    @pl.when(j == n_steps - 1)
    def _():
        for sub in range(subs):
            _wait_rows(xs_ref, stage_ref.at[0], n_sorted, sem.at[parity * subs + sub])


def _sort(tables, pad_row, pad_n, h2, meta, xs, n_sorted_rows, n_tok, tile0, subs):
    sorted_row, off, cnt = tables
    first = xs is None
    n_steps = h2.shape[0] // (n_tok * subs)
    in_specs = [pl.BlockSpec((subs * n_tok, D_MODEL), lambda j, *_: (j, 0)),
                pl.BlockSpec((subs, SUBLANES, n_tok), lambda j, *_: (j, 0, 0))]
    operands = [sorted_row, off, cnt, pad_row, pad_n, h2, meta]
    scratch = [pltpu.VMEM((2 * subs, n_tok * TOP_K, ROW_SLABS, LANES), BF16)]
    if first:
        scratch.append(pltpu.VMEM((ROW_TILE, ROW_SLABS, LANES), BF16))
        aliases = {}
    else:
        in_specs.append(pl.BlockSpec(memory_space=pl.ANY))
        operands.append(xs)
        aliases = {len(operands) - 1: 0}
    scratch.append(pltpu.SemaphoreType.DMA((2 * subs,)))
    if first:
        scratch.append(pltpu.SemaphoreType.DMA(()))
    grid_spec = pltpu.PrefetchScalarGridSpec(
        num_scalar_prefetch=5,
        grid=(n_steps,),
        in_specs=in_specs,
        out_specs=pl.BlockSpec(memory_space=pl.ANY),
        scratch_shapes=scratch,
    )
    return pl.pallas_call(
        functools.partial(_sort_kernel, n_tok, tile0, n_steps, subs, first),
        grid_spec=grid_spec,
        out_shape=jax.ShapeDtypeStruct((n_sorted_rows, ROW_SLABS, LANES), BF16),
        input_output_aliases=aliases,
        compiler_params=pltpu.CompilerParams(
            dimension_semantics=("arbitrary",), vmem_limit_bytes=VMEM_LIMIT),
        name="sort_first" if first else "sort_more",
    )(*operands)


def _experts_kernel(tile_start_ref, n_tile_ref, n_used_ref, xs_ref, w_gu_ref, b_gu_ref, w_down_ref,
                    b_down_ref, ys_ref, w_gu_bf_ref, w_down_bf_ref, x_buf, y_buf, x_sem, y_sem):
    e = pl.program_id(0)
    n_used = n_used_ref[0]
    first_tile = tile_start_ref[e]
    n_tile = n_tile_ref[e]

    def x_copy(g):
        slot = g % TILE_SLOTS
        return pltpu.make_async_copy(xs_ref.at[pl.ds(g * ROW_TILE, ROW_TILE)], x_buf.at[slot],
                                     x_sem.at[slot])

    def y_copy(g):
        slot = g % TILE_SLOTS
        return pltpu.make_async_copy(y_buf.at[slot], ys_ref.at[pl.ds(g * ROW_TILE, ROW_TILE)],
                                     y_sem.at[slot])

    def request(g):
        @pl.when(g < n_used)
        def _():
            x_copy(g).start(priority=1)

    @pl.when(e == 0)
    def _():
        for g in range(TILE_AHEAD):
            request(g)

    @pl.when(n_tile > 0)
    def _():
        w_gu_bf_ref[...] = w_gu_ref[...].astype(BF16)
        w_down_bf_ref[...] = w_down_ref[...].astype(BF16)

    def begin(g):
        x_copy(g).wait()
        request(g + TILE_AHEAD)

        @pl.when(g >= TILE_SLOTS)
        def _():
            y_copy(g - TILE_SLOTS).wait()

    def compute(g):
        slot = g % TILE_SLOTS
        x = _load_rows(x_buf, slot)
        gu = _dot(x, w_gu_bf_ref[...]) + b_gu_ref[...]
        gate = jnp.minimum(gu[:, :D_EXPERT], SWIGLU_LIMIT)
        up = jnp.clip(gu[:, D_EXPERT:], -SWIGLU_LIMIT, SWIGLU_LIMIT)
        act = (up + 1.0) * gate * jax.nn.sigmoid(SWIGLU_ALPHA * gate)
        y = _dot(act.astype(BF16), w_down_bf_ref[...]) + b_down_ref[...]
        _store_rows(y_buf, slot, y)
        y_copy(g).start(priority=1)

    def group(g, size):
        for k in range(size):
            begin(g + k)
        for k in range(size):
            compute(g + k)

    def group_body(p, carry):
        group(first_tile + TILE_GROUP * p, TILE_GROUP)
        return carry

    n_group = n_tile // TILE_GROUP
    lax.fori_loop(0, n_group, group_body, 0)
    rest = n_tile - n_group * TILE_GROUP
    g_rest = first_tile + n_group * TILE_GROUP
    size = TILE_GROUP // 2
    while size >= 1:
        @pl.when((rest & size) != 0)
        def _(g_rest=g_rest, size=size):
            group(g_rest, size)
        g_rest = g_rest + (rest & size)
        size //= 2

    @pl.when(e == N_EXPERTS - 1)
    def _():
        for back in range(TILE_SLOTS, 0, -1):
            @pl.when(n_used >= back)
            def _(back=back):
                y_copy(n_used - back).wait()


def _experts(tile_start, n_tile_e, n_used, xs, w_gu, b_gu, w_down, b_down):
    w_blk = lambda e, *_: (e, 0, 0)
    grid_spec = pltpu.PrefetchScalarGridSpec(
        num_scalar_prefetch=3,
        grid=(N_EXPERTS,),
        in_specs=[
            pl.BlockSpec(memory_space=pl.ANY),
            pl.BlockSpec((None, D_MODEL, 2 * D_EXPERT), w_blk),
            pl.BlockSpec((None, 1, 2 * D_EXPERT), w_blk),
            pl.BlockSpec((None, D_EXPERT, D_MODEL), w_blk),
            pl.BlockSpec((None, 1, D_MODEL), w_blk),
        ],
        out_specs=pl.BlockSpec(memory_space=pl.ANY),
        scratch_shapes=[pltpu.VMEM((D_MODEL, 2 * D_EXPERT), BF16),
                        pltpu.VMEM((D_EXPERT, D_MODEL), BF16),
                        pltpu.VMEM((TILE_SLOTS, ROW_TILE, ROW_SLABS, LANES), BF16),
                        pltpu.VMEM((TILE_SLOTS, ROW_TILE, ROW_SLABS, LANES), BF16),
                        pltpu.SemaphoreType.DMA((TILE_SLOTS,)),
                        pltpu.SemaphoreType.DMA((TILE_SLOTS,))],
    )
    return pl.pallas_call(
        _experts_kernel,
        grid_spec=grid_spec,
        out_shape=jax.ShapeDtypeStruct(xs.shape, BF16),
        compiler_params=pltpu.CompilerParams(
            dimension_semantics=("arbitrary",), vmem_limit_bytes=VMEM_LIMIT),
        name="experts",
    )(tile_start, n_tile_e, n_used, xs, w_gu, b_gu.reshape(N_EXPERTS, 1, -1), w_down,
      b_down.reshape(N_EXPERTS, 1, -1))


def _combine_kernel(n_tok, tile0, n_steps, subs, seq_steps, alpha_res, sorted_row_ref, off_ref, cnt_ref,
                    ys_ref, x1_ref, meta_ref, mod_ref, ln2_g_ref, ln2_b_ref, out_ref,
                    stage_ref, sem):
    j = pl.program_id(0)
    n_sorted = n_tok * TOP_K
    parity = j % 2

    def start_step(step, step_parity, live):
        for sub in range(subs):
            base = (tile0 + step * subs + sub) * N_EXPERTS
            to_slot = step_parity * subs + sub

            def strip(e, priority, base=base, to_slot=to_slot):
                _strip_copy(ys_ref, sorted_row_ref[base + e], stage_ref.at[to_slot],
                            off_ref[base + e], jnp.where(live, cnt_ref[base + e], 0),
                            sem.at[to_slot], priority)
            _for_each_expert(strip)

    @pl.when(j == 0)
    def _():
        start_step(j, parity, True)

    start_step(jnp.minimum(j + 1, n_steps - 1), 1 - parity, j + 1 < n_steps)

    sub_r = lax.broadcasted_iota(jnp.int32, (n_sorted, n_tok), 0)
    for sub in range(subs):
        slot = parity * subs + sub
        rows = slice(sub * n_tok, (sub + 1) * n_tok)
        meta = meta_ref[sub]
        comb_t = jnp.zeros((n_sorted, n_tok), F32)
        for k in range(TOP_K):
            comb_t = jnp.where(sub_r == meta[k:k + 1, :].astype(jnp.int32),
                               meta[TOP_K + k:TOP_K + k + 1, :], comb_t)
        comb_t = comb_t.astype(BF16)

        _wait_rows(ys_ref, stage_ref.at[0], n_sorted, sem.at[slot])
        ys = _load_rows(stage_ref, slot)
        ffn = lax.dot_general(comb_t, ys, (((0,), (0,)), ((), ())), preferred_element_type=F32)
        if seq_steps is None:
            g2 = mod_ref[rows, 5 * D_MODEL:6 * D_MODEL]
        else:
            g2 = mod_ref[pl.ds(j // seq_steps, 1), 5 * D_MODEL:6 * D_MODEL]
        y = _layer_norm(alpha_res * x1_ref[rows, :] + g2 * ffn, ln2_g_ref[...], ln2_b_ref[...])
        if seq_steps is None:
            out_ref[rows, 0, :] = y
        else:
            out_ref[rows, :] = y


def _combine(tables, ys, x1, meta, mod, ln2_g, ln2_b, n_tok, tile0, subs, alpha_res):
    sorted_row, off, cnt = tables
    step_tok = subs * n_tok
    n_steps = x1.shape[0] // step_tok
    if mod.shape[0] == x1.shape[0]:
        seq_steps = None
        mod_spec = pl.BlockSpec((step_tok, N_MOD * D_MODEL), lambda j, *_: (j, 0))
        out_spec = pl.BlockSpec((step_tok, 1, D_MODEL), lambda j, *_: (j, 0, 0))
        out_shape = jax.ShapeDtypeStruct((x1.shape[0], 1, D_MODEL), F32)
    else:
        seq_steps = n_steps // mod.shape[0]
        mod_spec = pl.BlockSpec(mod.shape, lambda j, *_: (0, 0))
        out_spec = pl.BlockSpec((step_tok, D_MODEL), lambda j, *_: (j, 0))
        out_shape = jax.ShapeDtypeStruct(x1.shape, F32)
    grid_spec = pltpu.PrefetchScalarGridSpec(
        num_scalar_prefetch=3,
        grid=(n_steps,),
        in_specs=[
            pl.BlockSpec(memory_space=pl.ANY),
            pl.BlockSpec((step_tok, D_MODEL), lambda j, *_: (j, 0)),
            pl.BlockSpec((subs, SUBLANES, n_tok), lambda j, *_: (j, 0, 0)),
            mod_spec,
            pl.BlockSpec((1, D_MODEL), lambda j, *_: (0, 0)),
            pl.BlockSpec((1, D_MODEL), lambda j, *_: (0, 0)),
        ],
        out_specs=out_spec,
        scratch_shapes=[pltpu.VMEM((2 * subs, n_tok * TOP_K, ROW_SLABS, LANES), BF16),
                        pltpu.SemaphoreType.DMA((2 * subs,))],
    )
    return pl.pallas_call(
        functools.partial(_combine_kernel, n_tok, tile0, n_steps, subs, seq_steps, alpha_res),
        grid_spec=grid_spec,
        out_shape=out_shape,
        compiler_params=pltpu.CompilerParams(
            dimension_semantics=("arbitrary",), vmem_limit_bytes=VMEM_LIMIT),
        name="combine",
    )(sorted_row, off, cnt, ys, x1, meta, mod, ln2_g, ln2_b)


def _routing_tables(cnt_all):
    total = jnp.sum(cnt_all, axis=0)
    n_tile_e = (total + ROW_TILE - 1) // ROW_TILE
    tile_end = jnp.cumsum(n_tile_e)
    tile_start = tile_end - n_tile_e
    row_start = tile_start * ROW_TILE
    cum = jnp.cumsum(cnt_all, axis=0) - cnt_all
    off = jnp.cumsum(cnt_all, axis=1) - cnt_all
    sorted_row = row_start[None, :] + cum
    n_used = tile_end[-1]
    pad_row = row_start + total
    pad_n = n_tile_e * ROW_TILE - total
    i32 = lambda a: a.astype(jnp.int32)
    return ((i32(sorted_row).reshape(-1), i32(off).reshape(-1), i32(cnt_all).reshape(-1)),
            i32(pad_row), i32(pad_n), i32(tile_start), i32(n_tile_e), i32(n_used).reshape(1))


def kernel(x_prompt, x_sample, state_conv, c_prompt, c_sample, ada_w, ada_b, w_in, sgu_ln_g, sgu_ln_b,
           sgu_w, sgu_b, conv_w, conv_b, w_out, ln1_g, ln1_b, router_w, router_b, w_gu, b_gu,
           w_down, b_down, ln2_g, ln2_b):
    depth = ada_w.shape[0]
    assert depth == 1
    bsz, seq, _ = x_prompt.shape
    n_dec = x_sample.shape[0]
    assert x_sample.shape[1] == 1 and seq % TOK_TILE == 0
    alpha_res = (2.0 * depth) ** 0.25
    l = 0

    mod_s, mod_p = _ada(c_sample, c_prompt, ada_w[l], ada_b[l])

    router_wt =jnp.transpose(router_w[l])
    router_wt_hi = router_wt.astype(BF16)
    router_wt_lo = (router_wt - router_wt_hi.astype(F32)).astype(BF16)
    router_wt2 = jnp.concatenate([router_wt_hi, router_wt_lo], axis=0)
    router_b_col = router_b[l].reshape(N_EXPERTS, 1)
    row = lambda a: a.reshape(1, -1)

    x1_p, h2_p, meta_p, cnt_p, convst_p, w_in_bf, w_out_bf = _mix_prompt(
        x_prompt, mod_p, w_in[l], sgu_ln_g[l], sgu_ln_b[l], sgu_w[l], jnp.transpose(sgu_b[l]),
        conv_w[l], row(conv_b[l]), w_out[l], row(ln1_g[l]), row(ln1_b[l]), router_wt2,
        router_b_col, alpha_res)
    x1_s, h2_s, meta_s, cnt_s, q_s, vn_s = _mix_sample(
        x_sample.reshape(n_dec, D_MODEL), mod_s, state_conv[l, :, 0, :], state_conv[l, :, 1, :],
        w_in_bf, sgu_ln_g[l], sgu_ln_b[l], row(jnp.repeat(sgu_w[l, :, 0, 0], A_HEAD_DIM)),
        row(jnp.repeat(sgu_b[l, :, 0], A_HEAD_DIM)), conv_w[l], row(conv_b[l]), w_out_bf,
        row(ln1_g[l]), row(ln1_b[l]), router_wt2, router_b_col, alpha_res)

    n_ptiles = bsz * seq // TOK_TILE
    cnt_all = jnp.concatenate([cnt_p[:, :, 0], cnt_s[None, :, 0]], axis=0)
    n_assign = (bsz * seq + n_dec) * TOP_K
    n_row_tiles = -(-n_assign // ROW_TILE) + N_EXPERTS
    tables, pad_row, pad_n, tile_start, n_tile_e, n_used = _routing_tables(cnt_all)

    n_sorted_rows = n_row_tiles * ROW_TILE
    meta_s = meta_s[None]
    xs = _sort(tables, pad_row, pad_n, h2_p, meta_p, None, n_sorted_rows, TOK_TILE, 0, SORT_SUB)
    xs = _sort(tables, pad_row, pad_n, h2_s, meta_s, xs, n_sorted_rows, n_dec, n_ptiles, 1)
    ys = _experts(tile_start, n_tile_e, n_used, xs, w_gu[l], b_gu[l], w_down[l], b_down[l])
    y_p = _combine(tables, ys, x1_p, meta_p, mod_p, row(ln2_g[l]), row(ln2_b[l]), TOK_TILE, 0,
                   COMBINE_SUB, alpha_res)
    y_s = _combine(tables, ys, x1_s, meta_s, mod_s, row(ln2_g[l]), row(ln2_b[l]), n_dec, n_ptiles,
                   1, alpha_res)

    conv_state_sample = jnp.stack([state_conv[l, :, 1, :], q_s], axis=1)[None]
    return (y_p.reshape(bsz, seq, D_MODEL),
            y_s,
            convst_p[None],
            conv_state_sample,
            vn_s.reshape(1, n_dec, 1, A_HEADS, A_HEAD_DIM))
```

```python
import functools

import jax
import jax.numpy as jnp
from jax import lax
from jax.experimental import pallas as pl
from jax.experimental.pallas import tpu as pltpu

F32 = jnp.float32
BF16 = jnp.bfloat16

D_MODEL = 1024
A_WIDTH = 512
B_WIDTH = 512
A_HEADS = 4
A_HEAD_DIM = 128
CHUNK = 128
PROJ_COLS = 2 * A_WIDTH + 3 * B_WIDTH
N_EXPERTS = 32
TOP_K = 4
D_EXPERT = 1024
SWIGLU_LIMIT = 7.0
SWIGLU_ALPHA = 1.702
LN_EPS = 1e-5
N_MOD = 6

LANES = 128
SUBLANES = 8
ROW_SLABS = D_MODEL // LANES
TOK_TILE = 256
MIX_SUB = 2
SORT_SUB = 4
COMBINE_SUB = 2
ROW_TILE = 256
TILE_GROUP = 4
TILE_AHEAD = TILE_GROUP
TILE_SLOTS = 2 * TILE_GROUP
VMEM_LIMIT = 56 * 1024 * 1024


def _layer_norm(x, g, b):
    mu = jnp.mean(x, axis=-1, keepdims=True)
    xc = x - mu
    var = jnp.mean(xc * xc, axis=-1, keepdims=True)
    return xc * lax.rsqrt(var + LN_EPS) * g + b


def _dot(a, b):
    return jnp.dot(a, b, preferred_element_type=F32)


def _dot_nt(a, b):
    return lax.dot_general(a, b, (((1,), (1,)), ((), ())), preferred_element_type=F32)


def _store_rows(ref, slot, rows):
    n = rows.shape[0]
    ref[slot] = rows.astype(BF16).reshape(n, ROW_SLABS, LANES)


def _load_rows(ref, slot):
    return ref[slot].reshape(ref.shape[1], D_MODEL)


def _split_bf16(a):
    hi = a.astype(BF16)
    lo = (a - hi.astype(F32)).astype(BF16)
    return hi, lo


def _ada_kernel(c_a_ref, c_b_ref, w_ref, b_ref, o_a_ref, o_b_ref):
    n_a = c_a_ref.shape[0]
    c = jnp.concatenate([c_a_ref[...], c_b_ref[...]], axis=0)
    s_hi, s_lo = _split_bf16(c * jax.nn.sigmoid(c))
    w_hi, w_lo = _split_bf16(w_ref[...])
    m = _dot(s_hi, w_hi) + _dot(s_hi, w_lo) + _dot(s_lo, w_hi) + b_ref[...]
    o_a_ref[...] = m[:n_a]
    o_b_ref[...] = m[n_a:]


def _ada(c_a, c_b, ada_w, ada_b):
    cols = 3 * D_MODEL // 2
    rows_spec = lambda c: pl.BlockSpec((c.shape[0], D_MODEL), lambda n: (0, 0))
    out_spec = lambda c: pl.BlockSpec((c.shape[0], cols), lambda n: (0, n))
    return pl.pallas_call(
        _ada_kernel,
        grid=(N_MOD * D_MODEL // cols,),
        in_specs=[
            rows_spec(c_a),
            rows_spec(c_b),
            pl.BlockSpec((D_MODEL, cols), lambda n: (0, n)),
            pl.BlockSpec((1, cols), lambda n: (0, n)),
        ],
        out_specs=[out_spec(c_a), out_spec(c_b)],
        out_shape=[jax.ShapeDtypeStruct((c_a.shape[0], N_MOD * D_MODEL), F32),
                   jax.ShapeDtypeStruct((c_b.shape[0], N_MOD * D_MODEL), F32)],
        name="ada",
    )(c_a, c_b, ada_w, ada_b.reshape(1, -1))


def _route(h2_tiles, router_wt2, router_b):
    n_tiles = len(h2_tiles)
    n_tile_tok = h2_tiles[0].shape[0]
    h2 = h2_tiles[0] if n_tiles == 1 else jnp.concatenate(h2_tiles, axis=0)
    n = h2.shape[0]
    lanes = [slice(i * n_tile_tok, (i + 1) * n_tile_tok) for i in range(n_tiles)]
    h_hi = h2.astype(BF16)
    h_lo = (h2 - h_hi.astype(F32)).astype(BF16)
    l1 = _dot_nt(router_wt2, h_hi)
    l2 = _dot_nt(router_wt2[:N_EXPERTS], h_lo)
    logits = l1[:N_EXPERTS] + l1[N_EXPERTS:] + l2 + router_b

    sub = lax.broadcasted_iota(jnp.int32, (N_EXPERTS, n), 0)
    work = logits
    vals, hots = [], []
    for _ in range(TOP_K):
        m = jnp.max(work, axis=0, keepdims=True)
        idx = jnp.min(jnp.where(work == m, sub, N_EXPERTS), axis=0, keepdims=True)
        hot = sub == idx
        work = jnp.where(hot, -jnp.inf, work)
        vals.append(m)
        hots.append(hot)
    exps = [jnp.exp(v - vals[0]) for v in vals]
    denom = exps[0] + exps[1] + exps[2] + exps[3]
    gates = [e / denom for e in exps]

    onehot = jnp.zeros((N_EXPERTS, n), F32)
    for hot in hots:
        onehot = onehot + hot.astype(F32)
    onehot_bf = onehot.astype(BF16)
    t_r = lax.broadcasted_iota(jnp.int32, (n_tile_tok, n_tile_tok), 0)
    t_c = lax.broadcasted_iota(jnp.int32, (n_tile_tok, n_tile_tok), 1)
    stacked = onehot_bf if n_tiles == 1 else jnp.concatenate([onehot_bf[:, ln] for ln in lanes], axis=0)
    rank_st = _dot(stacked, (t_r < t_c).astype(BF16))
    e_r = lax.broadcasted_iota(jnp.int32, (N_EXPERTS, N_EXPERTS), 0)
    e_c = lax.broadcasted_iota(jnp.int32, (N_EXPERTS, N_EXPERTS), 1)
    below = _dot((e_c < e_r).astype(BF16), onehot_bf)
    cnts, bases = [], []
    for i, ln in enumerate(lanes):
        cnts.append(jnp.sum(onehot[:, ln], axis=1, keepdims=True))
        off = jnp.sum(below[:, ln], axis=1, keepdims=True)
        bases.append(rank_st[i * N_EXPERTS:(i + 1) * N_EXPERTS] + off)
    base = bases[0] if n_tiles == 1 else jnp.concatenate(bases, axis=1)

    row8 = lax.broadcasted_iota(jnp.int32, (SUBLANES, n), 0)
    meta = jnp.zeros((SUBLANES, n), F32)
    for k in range(TOP_K):
        pos_k = jnp.sum(jnp.where(hots[k], base, 0.0), axis=0, keepdims=True)
        meta = jnp.where(row8 == k, pos_k, meta)
        meta = jnp.where(row8 == TOP_K + k, gates[k], meta)
    return [(meta[:, ln], jnp.broadcast_to(c, (N_EXPERTS, LANES))) for ln, c in zip(lanes, cnts)]


def _mix_prompt_kernel(alpha_res, x_ref, mod_ref, w_in_ref, sgu_g_ref, sgu_bln_ref, sgu_w_ref,
                       sgu_bias_ref, conv_w_ref, conv_b_ref, w_out_ref, ln1_g_ref, ln1_b_ref,
                       router_w_ref, router_b_ref,
                       x1_ref, h2_ref, meta_ref, cnt_ref, convst_ref, w_in_bf_ref, w_out_bf_ref,
                       carry_ref):
    t = pl.program_id(1)

    @pl.when(jnp.logical_and(pl.program_id(0) == 0, t == 0))
    def _():
        w_in_bf_ref[...] = w_in_ref[...].astype(BF16)
        w_out_bf_ref[...] = w_out_ref[...].astype(BF16)

    @pl.when(t == 0)
    def _():
        carry_ref[...] = jnp.zeros_like(carry_ref)

    seq_row = pl.ds(pl.program_id(0), 1)
    sh1, sc1, g1, sh2, sc2 = [mod_ref[seq_row, k * D_MODEL:(k + 1) * D_MODEL] for k in range(5)]
    r_i = lax.broadcasted_iota(jnp.int32, (CHUNK, CHUNK), 0)
    c_i = lax.broadcasted_iota(jnp.int32, (CHUNK, CHUNK), 1)
    tril = c_i <= r_i
    tm = TOK_TILE
    prev2 = carry_ref[0:1, :]
    prev1 = carry_ref[1:2, :]

    subs = range(MIX_SUB)
    tile_rows = [slice(sub * tm, (sub + 1) * tm) for sub in subs]
    xs_in = [x_ref[rows, :] for rows in tile_rows]
    zs = [_dot((x * (1.0 + sc1) + sh1).astype(BF16), w_in_bf_ref[...]) for x in xs_in]

    mix_ins = []
    for sub in subs:
        z = zs[sub]
        u = z[:, 0:A_WIDTH]
        v = z[:, A_WIDTH:2 * A_WIDTH]
        gate_b = z[:, 2 * A_WIDTH:2 * A_WIDTH + B_WIDTH]
        gate_c = z[:, 2 * A_WIDTH + B_WIDTH:2 * A_WIDTH + 2 * B_WIDTH]
        hb = z[:, 2 * A_WIDTH + 2 * B_WIDTH:]

        a_parts = []
        for hd in range(A_HEADS):
            sl = slice(hd * A_HEAD_DIM, (hd + 1) * A_HEAD_DIM)
            vn = _layer_norm(v[:, sl], sgu_g_ref[hd:hd + 1, :], sgu_bln_ref[hd:hd + 1, :]).astype(BF16)
            wm = jnp.where(tril, sgu_w_ref[hd], 0.0).astype(BF16)
            bias = sgu_bias_ref[:, hd:hd + 1]
            s_parts = []
            for c in range(tm // CHUNK):
                s_parts.append(_dot(wm, vn[c * CHUNK:(c + 1) * CHUNK, :]) + bias)
            a_parts.append(u[:, sl] * jnp.concatenate(s_parts, axis=0))

        q = gate_c * hb
        row = lax.broadcasted_iota(jnp.int32, q.shape, 0)
        q_m1 = jnp.where(row == 0, prev1, pltpu.roll(q, 1, 0))
        q_m2 = jnp.where(row == 0, prev2, jnp.where(row == 1, prev1, pltpu.roll(q, 2, 0)))
        conv = (conv_b_ref[...] + q_m2 * conv_w_ref[0:1, :] + q_m1 * conv_w_ref[1:2, :]
                + q * conv_w_ref[2:3, :])
        b_out = gate_b * conv
        prev2 = q[tm - 2:tm - 1, :]
        prev1 = q[tm - 1:tm, :]

        mix_ins.append(jnp.concatenate(a_parts + [b_out], axis=-1).astype(BF16))

    mixes = [_dot(mix_in, w_out_bf_ref[...]) for mix_in in mix_ins]
    h2s = []
    for sub in subs:
        x1 = _layer_norm(alpha_res * xs_in[sub] + g1 * mixes[sub], ln1_g_ref[...], ln1_b_ref[...])
        x1_ref[tile_rows[sub], :] = x1
        h2 = x1 * (1.0 + sc2) + sh2
        h2_ref[tile_rows[sub], :] = h2.astype(BF16)
        h2s.append(h2)
    routed = _route(h2s, router_w_ref[...], router_b_ref[...])
    for sub in subs:
        meta, cnt = routed[sub]
        meta_ref[sub] = meta
        cnt_ref[sub] = cnt.astype(jnp.int32)

    last2 = jnp.concatenate([prev2, prev1], axis=0)
    carry_ref[0:2, :] = last2
    convst_ref[...] = last2


def _mix_prompt(x, mod, w_in, sgu_g, sgu_bln, sgu_w, sgu_bias_t, conv_w, conv_b, w_out,
                ln1_g, ln1_b, router_wt2, router_b_col, alpha_res):
    bsz, seq, _ = x.shape
    step_tok = MIX_SUB * TOK_TILE
    steps = seq // step_tok
    n_tok = bsz * seq
    tiles = seq // TOK_TILE
    const2 = lambda b, t: (0, 0)
    tok = lambda b, t: (b * steps + t, 0)
    tile3 = lambda b, t: (b * steps + t, 0, 0)
    return pl.pallas_call(
        functools.partial(_mix_prompt_kernel, alpha_res),
        grid=(bsz, steps),
        in_specs=[
            pl.BlockSpec((None, step_tok, D_MODEL), lambda b, t: (b, t, 0)),
            pl.BlockSpec((bsz, N_MOD * D_MODEL), const2),
            pl.BlockSpec((D_MODEL, PROJ_COLS), const2, pipeline_mode=pl.Buffered(1)),
            pl.BlockSpec((A_HEADS, A_HEAD_DIM), const2),
            pl.BlockSpec((A_HEADS, A_HEAD_DIM), const2),
            pl.BlockSpec((A_HEADS, CHUNK, CHUNK), lambda b, t: (0, 0, 0)),
            pl.BlockSpec((CHUNK, A_HEADS), const2),
            pl.BlockSpec((3, B_WIDTH), const2),
            pl.BlockSpec((1, B_WIDTH), const2),
            pl.BlockSpec((D_MODEL, D_MODEL), const2, pipeline_mode=pl.Buffered(1)),
            pl.BlockSpec((1, D_MODEL), const2),
            pl.BlockSpec((1, D_MODEL), const2),
            pl.BlockSpec((2 * N_EXPERTS, D_MODEL), const2),
            pl.BlockSpec((N_EXPERTS, 1), const2),
        ],
        out_specs=[
            pl.BlockSpec((step_tok, D_MODEL), tok),
            pl.BlockSpec((step_tok, D_MODEL), tok),
            pl.BlockSpec((MIX_SUB, SUBLANES, TOK_TILE), tile3),
            pl.BlockSpec((MIX_SUB, N_EXPERTS, LANES), tile3),
            pl.BlockSpec((None, 2, B_WIDTH), lambda b, t: (b, 0, 0)),
            pl.BlockSpec((D_MODEL, PROJ_COLS), const2),
            pl.BlockSpec((D_MODEL, D_MODEL), const2),
        ],
        out_shape=[
            jax.ShapeDtypeStruct((n_tok, D_MODEL), F32),
            jax.ShapeDtypeStruct((n_tok, D_MODEL), BF16),
            jax.ShapeDtypeStruct((bsz * tiles, SUBLANES, TOK_TILE), F32),
            jax.ShapeDtypeStruct((bsz * tiles, N_EXPERTS, LANES), jnp.int32),
            jax.ShapeDtypeStruct((bsz, 2, B_WIDTH), F32),
            jax.ShapeDtypeStruct((D_MODEL, PROJ_COLS), BF16),
            jax.ShapeDtypeStruct((D_MODEL, D_MODEL), BF16),
        ],
        scratch_shapes=[pltpu.VMEM((SUBLANES, B_WIDTH), F32)],
        compiler_params=pltpu.CompilerParams(
            dimension_semantics=("arbitrary", "arbitrary"), vmem_limit_bytes=VMEM_LIMIT),
        name="mix_prompt",
    )(x, mod, w_in, sgu_g, sgu_bln, sgu_w, sgu_bias_t, conv_w, conv_b, w_out,
      ln1_g, ln1_b, router_wt2, router_b_col)


def _mix_sample_kernel(alpha_res, x_ref, mod_ref, prev0_ref, prev1_ref, w_in_ref, sgu_g_ref,
                       sgu_bln_ref, sgu_w00_ref, sgu_b0_ref, conv_w_ref, conv_b_ref, w_out_ref,
                       ln1_g_ref, ln1_b_ref, router_w_ref, router_b_ref,
                       x1_ref, h2_ref, meta_ref, cnt_ref, q_ref, vn_ref):
    x = x_ref[...]
    sh1 = mod_ref[:, 0:D_MODEL]
    sc1 = mod_ref[:, D_MODEL:2 * D_MODEL]
    g1 = mod_ref[:, 2 * D_MODEL:3 * D_MODEL]
    sh2 = mod_ref[:, 3 * D_MODEL:4 * D_MODEL]
    sc2 = mod_ref[:, 4 * D_MODEL:5 * D_MODEL]
    h = (x * (1.0 + sc1) + sh1).astype(BF16)
    z = _dot(h, w_in_ref[...])
    u = z[:, 0:A_WIDTH]
    v = z[:, A_WIDTH:2 * A_WIDTH]
    gate_b = z[:, 2 * A_WIDTH:2 * A_WIDTH + B_WIDTH]
    gate_c = z[:, 2 * A_WIDTH + B_WIDTH:2 * A_WIDTH + 2 * B_WIDTH]
    hb = z[:, 2 * A_WIDTH + 2 * B_WIDTH:]

    vn_parts = []
    for hd in range(A_HEADS):
        sl = slice(hd * A_HEAD_DIM, (hd + 1) * A_HEAD_DIM)
        vn_parts.append(_layer_norm(v[:, sl], sgu_g_ref[hd:hd + 1, :], sgu_bln_ref[hd:hd + 1, :]))
    vn = jnp.concatenate(vn_parts, axis=-1)
    vn_ref[...] = vn
    a_out = u * (vn * sgu_w00_ref[...] + sgu_b0_ref[...])

    q = gate_c * hb
    q_ref[...] = q
    conv = (conv_b_ref[...] + prev0_ref[...] * conv_w_ref[0:1, :] + prev1_ref[...] * conv_w_ref[1:2, :]
            + q * conv_w_ref[2:3, :])
    b_out = gate_b * conv

    mix_in = jnp.concatenate([a_out, b_out], axis=-1).astype(BF16)
    mix = _dot(mix_in, w_out_ref[...])
    x1 = _layer_norm(alpha_res * x + g1 * mix, ln1_g_ref[...], ln1_b_ref[...])
    x1_ref[...] = x1
    h2 = x1 * (1.0 + sc2) + sh2
    h2_ref[...] = h2.astype(BF16)
    (meta, cnt), = _route([h2], router_w_ref[...], router_b_ref[...])
    meta_ref[...] = meta
    cnt_ref[...] = cnt.astype(jnp.int32)


def _mix_sample(x, mod, prev0, prev1, w_in_bf, sgu_g, sgu_bln, sgu_w00, sgu_b0, conv_w, conv_b,
                w_out_bf, ln1_g, ln1_b, router_wt2, router_b_col, alpha_res):
    n = x.shape[0]
    return pl.pallas_call(
        functools.partial(_mix_sample_kernel, alpha_res),
        out_shape=[
            jax.ShapeDtypeStruct((n, D_MODEL), F32),
            jax.ShapeDtypeStruct((n, D_MODEL), BF16),
            jax.ShapeDtypeStruct((SUBLANES, n), F32),
            jax.ShapeDtypeStruct((N_EXPERTS, LANES), jnp.int32),
            jax.ShapeDtypeStruct((n, B_WIDTH), F32),
            jax.ShapeDtypeStruct((n, A_WIDTH), F32),
        ],
        compiler_params=pltpu.CompilerParams(vmem_limit_bytes=VMEM_LIMIT),
        name="mix_sample",
    )(x, mod, prev0, prev1, w_in_bf, sgu_g, sgu_bln, sgu_w00, sgu_b0, conv_w, conv_b, w_out_bf,
      ln1_g, ln1_b, router_wt2, router_b_col)


def _strip_copy(src_ref, src_row, dst_ref, dst_row, n_rows, sem, priority=0):
    @pl.when(n_rows > 0)
    def _():
        pltpu.make_async_copy(src_ref.at[pl.ds(src_row, n_rows)],
                              dst_ref.at[pl.ds(dst_row, n_rows)], sem).start(priority=priority)


def _for_each_expert(strip):
    for e in range(N_EXPERTS):
        strip(e, 0)


def _wait_rows(hbm_ref, vmem_ref, n_rows, sem):
    pltpu.make_async_copy(hbm_ref.at[pl.ds(0, n_rows)], vmem_ref.at[pl.ds(0, n_rows)], sem).wait()


def _sort_kernel(n_tok, tile0, n_steps, subs, first, *refs):
    if first:
        (sorted_row_ref, off_ref, cnt_ref, pad_row_ref, pad_n_ref, h2_ref, meta_ref,
         xs_ref, stage_ref, zero_ref, sem, pad_sem) = refs
    else:
        (sorted_row_ref, off_ref, cnt_ref, pad_row_ref, pad_n_ref, h2_ref, meta_ref, _,
         xs_ref, stage_ref, sem) = refs
    j = pl.program_id(0)
    n_sorted = n_tok * TOP_K
    parity = j % 2

    if first:
        @pl.when(j == 0)
        def _():
            zero_ref[...] = jnp.zeros_like(zero_ref)

            def start(e, carry):
                _strip_copy(zero_ref, 0, xs_ref, pad_row_ref[e], pad_n_ref[e], pad_sem)
                return carry
            lax.fori_loop(0, N_EXPERTS, start, 0)

            def wait(e, carry):
                @pl.when(pad_n_ref[e] > 0)
                def _():
                    _wait_rows(xs_ref, zero_ref, pad_n_ref[e], pad_sem)
                return carry
            lax.fori_loop(0, N_EXPERTS, wait, 0)

    sub_r = lax.broadcasted_iota(jnp.int16, (n_sorted, n_tok), 0)
    for sub in range(subs):
        slot = parity * subs + sub
        pos = meta_ref[sub][0:TOP_K, :].astype(jnp.int16)
        perm = jnp.zeros((n_sorted, n_tok), BF16)
        for k in range(TOP_K):
            perm = jnp.where(sub_r == pos[k:k + 1, :], jnp.ones((), BF16), perm)
        rows = _dot(perm, h2_ref[sub * n_tok:(sub + 1) * n_tok, :])
        _store_rows(stage_ref, slot, rows)

        base = (tile0 + j * subs + sub) * N_EXPERTS

        def strip(e, priority, slot=slot, base=base):
            _strip_copy(stage_ref.at[slot], off_ref[base + e], xs_ref, sorted_row_ref[base + e],
                        cnt_ref[base + e], sem.at[slot], priority)
        _for_each_expert(strip)

    @pl.when(j > 0)
    def _():
        for sub in range(subs):
            _wait_rows(xs_ref, stage_ref.at[0], n_sorted, sem.at[(1 - parity) * subs + sub])

    @pl.when(j == n_steps - 1)
    def _():
        for sub in range(subs):
            _wait_rows(xs_ref, stage_ref.at[0], n_sorted, sem.at[parity * subs + sub])


def _sort(tables, pad_row, pad_n, h2, meta, xs, n_sorted_rows, n_tok, tile0, subs):
    sorted_row, off, cnt = tables
    first = xs is None
    n_steps = h2.shape[0] // (n_tok * subs)
    in_specs = [pl.BlockSpec((subs * n_tok, D_MODEL), lambda j, *_: (j, 0)),
                pl.BlockSpec((subs, SUBLANES, n_tok), lambda j, *_: (j, 0, 0))]
    operands = [sorted_row, off, cnt, pad_row, pad_n, h2, meta]
    scratch = [pltpu.VMEM((2 * subs, n_tok * TOP_K, ROW_SLABS, LANES), BF16)]
    if first:
        scratch.append(pltpu.VMEM((ROW_TILE, ROW_SLABS, LANES), BF16))
        aliases = {}
    else:
        in_specs.append(pl.BlockSpec(memory_space=pl.ANY))
        operands.append(xs)
        aliases = {len(operands) - 1: 0}
    scratch.append(pltpu.SemaphoreType.DMA((2 * subs,)))
    if first:
        scratch.append(pltpu.SemaphoreType.DMA(()))
    grid_spec = pltpu.PrefetchScalarGridSpec(
        num_scalar_prefetch=5,
        grid=(n_steps,),
        in_specs=in_specs,
        out_specs=pl.BlockSpec(memory_space=pl.ANY),
        scratch_shapes=scratch,
    )
    return pl.pallas_call(
        functools.partial(_sort_kernel, n_tok, tile0, n_steps, subs, first),
        grid_spec=grid_spec,
        out_shape=jax.ShapeDtypeStruct((n_sorted_rows, ROW_SLABS, LANES), BF16),
        input_output_aliases=aliases,
        compiler_params=pltpu.CompilerParams(
            dimension_semantics=("arbitrary",), vmem_limit_bytes=VMEM_LIMIT),
        name="sort_first" if first else "sort_more",
    )(*operands)


def _experts_kernel(tile_start_ref, n_tile_ref, n_used_ref, xs_ref, w_gu_ref, b_gu_ref, w_down_ref,
                    b_down_ref, ys_ref, w_gu_bf_ref, w_down_bf_ref, x_buf, y_buf, x_sem, y_sem):
    e = pl.program_id(0)
    n_used = n_used_ref[0]
    first_tile = tile_start_ref[e]
    n_tile = n_tile_ref[e]

    def x_copy(g):
        slot = g % TILE_SLOTS
        return pltpu.make_async_copy(xs_ref.at[pl.ds(g * ROW_TILE, ROW_TILE)], x_buf.at[slot],
                                     x_sem.at[slot])

    def y_copy(g):
        slot = g % TILE_SLOTS
        return pltpu.make_async_copy(y_buf.at[slot], ys_ref.at[pl.ds(g * ROW_TILE, ROW_TILE)],
                                     y_sem.at[slot])

    def request(g):
        @pl.when(g < n_used)
        def _():
            x_copy(g).start(priority=1)

    @pl.when(e == 0)
    def _():
        for g in range(TILE_AHEAD):
            request(g)

    @pl.when(n_tile > 0)
    def _():
        w_gu_bf_ref[...] = w_gu_ref[...].astype(BF16)
        w_down_bf_ref[...] = w_down_ref[...].astype(BF16)

    def begin(g):
        x_copy(g).wait()
        request(g + TILE_AHEAD)

        @pl.when(g >= TILE_SLOTS)
        def _():
            y_copy(g - TILE_SLOTS).wait()

    def compute(g):
        slot = g % TILE_SLOTS
        x = _load_rows(x_buf, slot)
        gu = _dot(x, w_gu_bf_ref[...]) + b_gu_ref[...]
        gate = jnp.minimum(gu[:, :D_EXPERT], SWIGLU_LIMIT)
        up = jnp.clip(gu[:, D_EXPERT:], -SWIGLU_LIMIT, SWIGLU_LIMIT)
        act = (up + 1.0) * gate * jax.nn.sigmoid(SWIGLU_ALPHA * gate)
        y = _dot(act.astype(BF16), w_down_bf_ref[...]) + b_down_ref[...]
        _store_rows(y_buf, slot, y)
        y_copy(g).start(priority=1)

    def group(g, size):
        for k in range(size):
            begin(g + k)
        for k in range(size):
            compute(g + k)

    def group_body(p, carry):
        group(first_tile + TILE_GROUP * p, TILE_GROUP)
        return carry

    n_group = n_tile // TILE_GROUP
    lax.fori_loop(0, n_group, group_body, 0)
    rest = n_tile - n_group * TILE_GROUP
    g_rest = first_tile + n_group * TILE_GROUP
    size = TILE_GROUP // 2
    while size >= 1:
        @pl.when((rest & size) != 0)
        def _(g_rest=g_rest, size=size):
            group(g_rest, size)
        g_rest = g_rest + (rest & size)
        size //= 2

    @pl.when(e == N_EXPERTS - 1)
    def _():
        for back in range(TILE_SLOTS, 0, -1):
            @pl.when(n_used >= back)
            def _(back=back):
                y_copy(n_used - back).wait()


def _experts(tile_start, n_tile_e, n_used, xs, w_gu, b_gu, w_down, b_down):
    w_blk = lambda e, *_: (e, 0, 0)
    grid_spec = pltpu.PrefetchScalarGridSpec(
        num_scalar_prefetch=3,
        grid=(N_EXPERTS,),
        in_specs=[
            pl.BlockSpec(memory_space=pl.ANY),
            pl.BlockSpec((None, D_MODEL, 2 * D_EXPERT), w_blk),
            pl.BlockSpec((None, 1, 2 * D_EXPERT), w_blk),
            pl.BlockSpec((None, D_EXPERT, D_MODEL), w_blk),
            pl.BlockSpec((None, 1, D_MODEL), w_blk),
        ],
        out_specs=pl.BlockSpec(memory_space=pl.ANY),
        scratch_shapes=[pltpu.VMEM((D_MODEL, 2 * D_EXPERT), BF16),
                        pltpu.VMEM((D_EXPERT, D_MODEL), BF16),
                        pltpu.VMEM((TILE_SLOTS, ROW_TILE, ROW_SLABS, LANES), BF16),
                        pltpu.VMEM((TILE_SLOTS, ROW_TILE, ROW_SLABS, LANES), BF16),
                        pltpu.SemaphoreType.DMA((TILE_SLOTS,)),
                        pltpu.SemaphoreType.DMA((TILE_SLOTS,))],
    )
    return pl.pallas_call(
        _experts_kernel,
        grid_spec=grid_spec,
        out_shape=jax.ShapeDtypeStruct(xs.shape, BF16),
        compiler_params=pltpu.CompilerParams(
            dimension_semantics=("arbitrary",), vmem_limit_bytes=VMEM_LIMIT),
        name="experts",
    )(tile_start, n_tile_e, n_used, xs, w_gu, b_gu.reshape(N_EXPERTS, 1, -1), w_down,
      b_down.reshape(N_EXPERTS, 1, -1))


def _combine_kernel(n_tok, tile0, n_steps, subs, seq_steps, alpha_res, sorted_row_ref, off_ref, cnt_ref,
                    ys_ref, x1_ref, meta_ref, mod_ref, ln2_g_ref, ln2_b_ref, out_ref,
                    stage_ref, sem):
    j = pl.program_id(0)
    n_sorted = n_tok * TOP_K
    parity = j % 2

    def start_step(step, step_parity, live):
        for sub in range(subs):
            base = (tile0 + step * subs + sub) * N_EXPERTS
            to_slot = step_parity * subs + sub

            def strip(e, priority, base=base, to_slot=to_slot):
                _strip_copy(ys_ref, sorted_row_ref[base + e], stage_ref.at[to_slot],
                            off_ref[base + e], jnp.where(live, cnt_ref[base + e], 0),
                            sem.at[to_slot], priority)
            _for_each_expert(strip)

    @pl.when(j == 0)
    def _():
        start_step(j, parity, True)

    start_step(jnp.minimum(j + 1, n_steps - 1), 1 - parity, j + 1 < n_steps)

    sub_r = lax.broadcasted_iota(jnp.int16, (n_sorted, n_tok), 0)
    for sub in range(subs):
        slot = parity * subs + sub
        rows = slice(sub * n_tok, (sub + 1) * n_tok)
        meta = meta_ref[sub]
        pos = meta[0:TOP_K, :].astype(jnp.int16)
        gate = meta[TOP_K:2 * TOP_K, :].astype(BF16)
        comb_t = jnp.zeros((n_sorted, n_tok), BF16)
        for k in range(TOP_K):
            comb_t = jnp.where(sub_r == pos[k:k + 1, :], gate[k:k + 1, :], comb_t)

        _wait_rows(ys_ref, stage_ref.at[0], n_sorted, sem.at[slot])
        ys = _load_rows(stage_ref, slot)
        ffn = lax.dot_general(comb_t, ys, (((0,), (0,)), ((), ())), preferred_element_type=F32)
        if seq_steps is None:
            g2 = mod_ref[rows, 5 * D_MODEL:6 * D_MODEL]
        else:
            g2 = mod_ref[pl.ds(j // seq_steps, 1), 5 * D_MODEL:6 * D_MODEL]
        y = _layer_norm(alpha_res * x1_ref[rows, :] + g2 * ffn, ln2_g_ref[...], ln2_b_ref[...])
        if seq_steps is None:
            out_ref[rows, 0, :] = y
        else:
            out_ref[rows, :] = y


def _combine(tables, ys, x1, meta, mod, ln2_g, ln2_b, n_tok, tile0, subs, alpha_res):
    sorted_row, off, cnt = tables
    step_tok = subs * n_tok
    n_steps = x1.shape[0] // step_tok
    if mod.shape[0] == x1.shape[0]:
        seq_steps = None
        mod_spec = pl.BlockSpec((step_tok, N_MOD * D_MODEL), lambda j, *_: (j, 0))
        out_spec = pl.BlockSpec((step_tok, 1, D_MODEL), lambda j, *_: (j, 0, 0))
        out_shape = jax.ShapeDtypeStruct((x1.shape[0], 1, D_MODEL), F32)
    else:
        seq_steps = n_steps // mod.shape[0]
        mod_spec = pl.BlockSpec(mod.shape, lambda j, *_: (0, 0))
        out_spec = pl.BlockSpec((step_tok, D_MODEL), lambda j, *_: (j, 0))
        out_shape = jax.ShapeDtypeStruct(x1.shape, F32)
    grid_spec = pltpu.PrefetchScalarGridSpec(
        num_scalar_prefetch=3,
        grid=(n_steps,),
        in_specs=[
            pl.BlockSpec(memory_space=pl.ANY),
            pl.BlockSpec((step_tok, D_MODEL), lambda j, *_: (j, 0)),
            pl.BlockSpec((subs, SUBLANES, n_tok), lambda j, *_: (j, 0, 0)),
            mod_spec,
            pl.BlockSpec((1, D_MODEL), lambda j, *_: (0, 0)),
            pl.BlockSpec((1, D_MODEL), lambda j, *_: (0, 0)),
        ],
        out_specs=out_spec,
        scratch_shapes=[pltpu.VMEM((2 * subs, n_tok * TOP_K, ROW_SLABS, LANES), BF16),
                        pltpu.SemaphoreType.DMA((2 * subs,))],
    )
    return pl.pallas_call(
        functools.partial(_combine_kernel, n_tok, tile0, n_steps, subs, seq_steps, alpha_res),
        grid_spec=grid_spec,
        out_shape=out_shape,
        compiler_params=pltpu.CompilerParams(
            dimension_semantics=("arbitrary",), vmem_limit_bytes=VMEM_LIMIT),
        name="combine",
    )(sorted_row, off, cnt, ys, x1, meta, mod, ln2_g, ln2_b)


def _routing_tables(cnt_all):
    total = jnp.sum(cnt_all, axis=0)
    n_tile_e = (total + ROW_TILE - 1) // ROW_TILE
    tile_end = jnp.cumsum(n_tile_e)
    tile_start = tile_end - n_tile_e
    row_start = tile_start * ROW_TILE
    cum = jnp.cumsum(cnt_all, axis=0) - cnt_all
    off = jnp.cumsum(cnt_all, axis=1) - cnt_all
    sorted_row = row_start[None, :] + cum
    n_used = tile_end[-1]
    pad_row = row_start + total
    pad_n = n_tile_e * ROW_TILE - total
    i32 = lambda a: a.astype(jnp.int32)
    return ((i32(sorted_row).reshape(-1), i32(off).reshape(-1), i32(cnt_all).reshape(-1)),
            i32(pad_row), i32(pad_n), i32(tile_start), i32(n_tile_e), i32(n_used).reshape(1))


def kernel(x_prompt, x_sample, state_conv, c_prompt, c_sample, ada_w, ada_b, w_in, sgu_ln_g, sgu_ln_b,
           sgu_w, sgu_b, conv_w, conv_b, w_out, ln1_g, ln1_b, router_w, router_b, w_gu, b_gu,
           w_down, b_down, ln2_g, ln2_b):
    depth = ada_w.shape[0]
    assert depth == 1
    bsz, seq, _ = x_prompt.shape
    n_dec = x_sample.shape[0]
    assert x_sample.shape[1] == 1 and seq % TOK_TILE == 0
    alpha_res = (2.0 * depth) ** 0.25
    l = 0

    mod_s, mod_p = _ada(c_sample, c_prompt, ada_w[l], ada_b[l])

    router_wt =jnp.transpose(router_w[l])
    router_wt_hi = router_wt.astype(BF16)
    router_wt_lo = (router_wt - router_wt_hi.astype(F32)).astype(BF16)
    router_wt2 = jnp.concatenate([router_wt_hi, router_wt_lo], axis=0)
    router_b_col = router_b[l].reshape(N_EXPERTS, 1)
    row = lambda a: a.reshape(1, -1)

    x1_p, h2_p, meta_p, cnt_p, convst_p, w_in_bf, w_out_bf = _mix_prompt(
        x_prompt, mod_p, w_in[l], sgu_ln_g[l], sgu_ln_b[l], sgu_w[l], jnp.transpose(sgu_b[l]),
        conv_w[l], row(conv_b[l]), w_out[l], row(ln1_g[l]), row(ln1_b[l]), router_wt2,
        router_b_col, alpha_res)
    x1_s, h2_s, meta_s, cnt_s, q_s, vn_s = _mix_sample(
        x_sample.reshape(n_dec, D_MODEL), mod_s, state_conv[l, :, 0, :], state_conv[l, :, 1, :],
        w_in_bf, sgu_ln_g[l], sgu_ln_b[l], row(jnp.repeat(sgu_w[l, :, 0, 0], A_HEAD_DIM)),
        row(jnp.repeat(sgu_b[l, :, 0], A_HEAD_DIM)), conv_w[l], row(conv_b[l]), w_out_bf,
        row(ln1_g[l]), row(ln1_b[l]), router_wt2, router_b_col, alpha_res)

    n_ptiles = bsz * seq // TOK_TILE
    cnt_all = jnp.concatenate([cnt_p[:, :, 0], cnt_s[None, :, 0]], axis=0)
    n_assign = (bsz * seq + n_dec) * TOP_K
    n_row_tiles = -(-n_assign // ROW_TILE) + N_EXPERTS
    tables, pad_row, pad_n, tile_start, n_tile_e, n_used = _routing_tables(cnt_all)

    n_sorted_rows = n_row_tiles * ROW_TILE
    meta_s = meta_s[None]
    xs = _sort(tables, pad_row, pad_n, h2_p, meta_p, None, n_sorted_rows, TOK_TILE, 0, SORT_SUB)
    xs = _sort(tables, pad_row, pad_n, h2_s, meta_s, xs, n_sorted_rows, n_dec, n_ptiles, 1)
    ys = _experts(tile_start, n_tile_e, n_used, xs, w_gu[l], b_gu[l], w_down[l], b_down[l])
    y_p = _combine(tables, ys, x1_p, meta_p, mod_p, row(ln2_g[l]), row(ln2_b[l]), TOK_TILE, 0,
                   COMBINE_SUB, alpha_res)
    y_s = _combine(tables, ys, x1_s, meta_s, mod_s, row(ln2_g[l]), row(ln2_b[l]), n_dec, n_ptiles,
                   1, alpha_res)

    conv_state_sample = jnp.stack([state_conv[l, :, 1, :], q_s], axis=1)[None]
    return (y_p.reshape(bsz, seq, D_MODEL),
            y_s,
            convst_p[None],
            conv_state_sample,
            vn_s.reshape(1, n_dec, 1, A_HEADS, A_HEAD_DIM))
```

```python
import functools

import jax
import jax.numpy as jnp
from jax import lax
from jax.experimental import pallas as pl
from jax.experimental.pallas import tpu as pltpu

F32 = jnp.float32
BF16 = jnp.bfloat16

D_MODEL = 1024
A_WIDTH = 512
B_WIDTH = 512
A_HEADS = 4
A_HEAD_DIM = 128
CHUNK = 128
PROJ_COLS = 2 * A_WIDTH + 3 * B_WIDTH
N_EXPERTS = 32
TOP_K = 4
D_EXPERT = 1024
SWIGLU_LIMIT = 7.0
SWIGLU_ALPHA = 1.702
LN_EPS = 1e-5
N_MOD = 6

LANES = 128
SUBLANES = 8
ROW_SLABS = D_MODEL // LANES
TOK_TILE = 256
MIX_SUB = 2
SORT_SUB = 4
COMBINE_SUB = 2
ROW_TILE = 256
TILE_GROUP = 4
TILE_AHEAD = TILE_GROUP
TILE_SLOTS = 2 * TILE_GROUP
VMEM_LIMIT = 56 * 1024 * 1024


def _layer_norm(x, g, b):
    mu = jnp.mean(x, axis=-1, keepdims=True)
    xc = x - mu
    var = jnp.mean(xc * xc, axis=-1, keepdims=True)
    return xc * lax.rsqrt(var + LN_EPS) * g + b


def _dot(a, b):
    return jnp.dot(a, b, preferred_element_type=F32)


def _dot_nt(a, b):
    return lax.dot_general(a, b, (((1,), (1,)), ((), ())), preferred_element_type=F32)


def _store_rows(ref, slot, rows):
    n = rows.shape[0]
    ref[slot] = rows.astype(BF16).reshape(n, ROW_SLABS, LANES)


def _load_rows(ref, slot):
    return ref[slot].reshape(ref.shape[1], D_MODEL)


def _split_bf16(a):
    hi = a.astype(BF16)
    lo = (a - hi.astype(F32)).astype(BF16)
    return hi, lo


def _ada_kernel(c_a_ref, c_b_ref, w_ref, b_ref, o_a_ref, o_b_ref):
    n_a = c_a_ref.shape[0]
    c = jnp.concatenate([c_a_ref[...], c_b_ref[...]], axis=0)
    s_hi, s_lo = _split_bf16(c * jax.nn.sigmoid(c))
    w_hi, w_lo = _split_bf16(w_ref[...])
    m = _dot(s_hi, w_hi) + _dot(s_hi, w_lo) + _dot(s_lo, w_hi) + b_ref[...]
    o_a_ref[...] = m[:n_a]
    o_b_ref[...] = m[n_a:]


def _ada(c_a, c_b, ada_w, ada_b):
    cols = 3 * D_MODEL // 2
    rows_spec = lambda c: pl.BlockSpec((c.shape[0], D_MODEL), lambda n: (0, 0))
    out_spec = lambda c: pl.BlockSpec((c.shape[0], cols), lambda n: (0, n))
    return pl.pallas_call(
        _ada_kernel,
        grid=(N_MOD * D_MODEL // cols,),
        in_specs=[
            rows_spec(c_a),
            rows_spec(c_b),
            pl.BlockSpec((D_MODEL, cols), lambda n: (0, n)),
            pl.BlockSpec((1, cols), lambda n: (0, n)),
        ],
        out_specs=[out_spec(c_a), out_spec(c_b)],
        out_shape=[jax.ShapeDtypeStruct((c_a.shape[0], N_MOD * D_MODEL), F32),
                   jax.ShapeDtypeStruct((c_b.shape[0], N_MOD * D_MODEL), F32)],
        name="ada",
    )(c_a, c_b, ada_w, ada_b.reshape(1, -1))


def _route(h2_tiles, router_wt2, router_b):
    n_tiles = len(h2_tiles)
    n_tile_tok = h2_tiles[0].shape[0]
    h2 = h2_tiles[0] if n_tiles == 1 else jnp.concatenate(h2_tiles, axis=0)
    n = h2.shape[0]
    lanes = [slice(i * n_tile_tok, (i + 1) * n_tile_tok) for i in range(n_tiles)]
    h_hi = h2.astype(BF16)
    h_lo = (h2 - h_hi.astype(F32)).astype(BF16)
    l1 = _dot_nt(router_wt2, h_hi)
    l2 = _dot_nt(router_wt2[:N_EXPERTS], h_lo)
    logits = l1[:N_EXPERTS] + l1[N_EXPERTS:] + l2 + router_b

    sub = lax.broadcasted_iota(jnp.int32, (N_EXPERTS, n), 0)
    work = logits
    vals, hots = [], []
    for _ in range(TOP_K):
        m = jnp.max(work, axis=0, keepdims=True)
        idx = jnp.min(jnp.where(work == m, sub, N_EXPERTS), axis=0, keepdims=True)
        hot = sub == idx
        work = jnp.where(hot, -jnp.inf, work)
        vals.append(m)
        hots.append(hot)
    exps = [jnp.exp(v - vals[0]) for v in vals]
    denom = exps[0] + exps[1] + exps[2] + exps[3]
    gates = [e / denom for e in exps]

    onehot = jnp.zeros((N_EXPERTS, n), F32)
    for hot in hots:
        onehot = onehot + hot.astype(F32)
    onehot_bf = onehot.astype(BF16)
    t_r = lax.broadcasted_iota(jnp.int32, (n_tile_tok, n_tile_tok), 0)
    t_c = lax.broadcasted_iota(jnp.int32, (n_tile_tok, n_tile_tok), 1)
    stacked = onehot_bf if n_tiles == 1 else jnp.concatenate([onehot_bf[:, ln] for ln in lanes], axis=0)
    rank_st = _dot(stacked, (t_r < t_c).astype(BF16))
    e_r = lax.broadcasted_iota(jnp.int32, (N_EXPERTS, N_EXPERTS), 0)
    e_c = lax.broadcasted_iota(jnp.int32, (N_EXPERTS, N_EXPERTS), 1)
    below = _dot((e_c < e_r).astype(BF16), onehot_bf)
    cnts, bases = [], []
    for i, ln in enumerate(lanes):
        cnts.append(jnp.sum(onehot[:, ln], axis=1, keepdims=True))
        off = jnp.sum(below[:, ln], axis=1, keepdims=True)
        bases.append(rank_st[i * N_EXPERTS:(i + 1) * N_EXPERTS] + off)
    base = bases[0] if n_tiles == 1 else jnp.concatenate(bases, axis=1)

    row8 = lax.broadcasted_iota(jnp.int32, (SUBLANES, n), 0)
    meta = jnp.zeros((SUBLANES, n), F32)
    for k in range(TOP_K):
        pos_k = jnp.sum(jnp.where(hots[k], base, 0.0), axis=0, keepdims=True)
        meta = jnp.where(row8 == k, pos_k, meta)
        meta = jnp.where(row8 == TOP_K + k, gates[k], meta)
    return [(meta[:, ln], jnp.broadcast_to(c, (N_EXPERTS, LANES))) for ln, c in zip(lanes, cnts)]


def _mix_prompt_kernel(alpha_res, x_ref, mod_ref, w_in_ref, sgu_g_ref, sgu_bln_ref, sgu_w_ref,
                       sgu_bias_ref, conv_w_ref, conv_b_ref, w_out_ref, ln1_g_ref, ln1_b_ref,
                       router_w_ref, router_b_ref,
                       x1_ref, h2_ref, meta_ref, cnt_ref, convst_ref, w_in_bf_ref, w_out_bf_ref,
                       carry_ref):
    t = pl.program_id(1)

    @pl.when(jnp.logical_and(pl.program_id(0) == 0, t == 0))
    def _():
        w_in_bf_ref[...] = w_in_ref[...].astype(BF16)
        w_out_bf_ref[...] = w_out_ref[...].astype(BF16)

    @pl.when(t == 0)
    def _():
        carry_ref[...] = jnp.zeros_like(carry_ref)

    seq_row = pl.ds(pl.program_id(0), 1)
    sh1, sc1, g1, sh2, sc2 = [mod_ref[seq_row, k * D_MODEL:(k + 1) * D_MODEL] for k in range(5)]
    r_i = lax.broadcasted_iota(jnp.int32, (CHUNK, CHUNK), 0)
    c_i = lax.broadcasted_iota(jnp.int32, (CHUNK, CHUNK), 1)
    tril = c_i <= r_i
    tm = TOK_TILE
    prev2 = carry_ref[0:1, :]
    prev1 = carry_ref[1:2, :]

    subs = range(MIX_SUB)
    tile_rows = [slice(sub * tm, (sub + 1) * tm) for sub in subs]
    xs_in = [x_ref[rows, :] for rows in tile_rows]
    zs = [_dot((x * (1.0 + sc1) + sh1).astype(BF16), w_in_bf_ref[...]) for x in xs_in]

    mix_ins = []
    for sub in subs:
        z = zs[sub]
        u = z[:, 0:A_WIDTH]
        v = z[:, A_WIDTH:2 * A_WIDTH]
        gate_b = z[:, 2 * A_WIDTH:2 * A_WIDTH + B_WIDTH]
        gate_c = z[:, 2 * A_WIDTH + B_WIDTH:2 * A_WIDTH + 2 * B_WIDTH]
        hb = z[:, 2 * A_WIDTH + 2 * B_WIDTH:]

        a_parts = []
        for hd in range(A_HEADS):
            sl = slice(hd * A_HEAD_DIM, (hd + 1) * A_HEAD_DIM)
            vn = _layer_norm(v[:, sl], sgu_g_ref[hd:hd + 1, :], sgu_bln_ref[hd:hd + 1, :]).astype(BF16)
            wm = jnp.where(tril, sgu_w_ref[hd], 0.0).astype(BF16)
            bias = sgu_bias_ref[:, hd:hd + 1]
            s_parts = []
            for c in range(tm // CHUNK):
                s_parts.append(_dot(wm, vn[c * CHUNK:(c + 1) * CHUNK, :]) + bias)
            a_parts.append(u[:, sl] * jnp.concatenate(s_parts, axis=0))

        q = gate_c * hb
        row = lax.broadcasted_iota(jnp.int32, q.shape, 0)
        q_m1 = jnp.where(row == 0, prev1, pltpu.roll(q, 1, 0))
        q_m2 = jnp.where(row == 0, prev2, jnp.where(row == 1, prev1, pltpu.roll(q, 2, 0)))
        conv = (conv_b_ref[...] + q_m2 * conv_w_ref[0:1, :] + q_m1 * conv_w_ref[1:2, :]
                + q * conv_w_ref[2:3, :])
        b_out = gate_b * conv
        prev2 = q[tm - 2:tm - 1, :]
        prev1 = q[tm - 1:tm, :]

        mix_ins.append(jnp.concatenate(a_parts + [b_out], axis=-1).astype(BF16))

    mixes = [_dot(mix_in, w_out_bf_ref[...]) for mix_in in mix_ins]
    h2s = []
    for sub in subs:
        x1 = _layer_norm(alpha_res * xs_in[sub] + g1 * mixes[sub], ln1_g_ref[...], ln1_b_ref[...])
        x1_ref[tile_rows[sub], :] = x1
        h2 = x1 * (1.0 + sc2) + sh2
        h2_ref[tile_rows[sub], :] = h2.astype(BF16)
        h2s.append(h2)
    routed = _route(h2s, router_w_ref[...], router_b_ref[...])
    for sub in subs:
        meta, cnt = routed[sub]
        meta_ref[sub] = meta
        cnt_ref[sub] = cnt.astype(jnp.int32)

    last2 = jnp.concatenate([prev2, prev1], axis=0)
    carry_ref[0:2, :] = last2
    convst_ref[...] = last2


def _mix_prompt(x, mod, w_in, sgu_g, sgu_bln, sgu_w, sgu_bias_t, conv_w, conv_b, w_out,
                ln1_g, ln1_b, router_wt2, router_b_col, alpha_res):
    bsz, seq, _ = x.shape
    step_tok = MIX_SUB * TOK_TILE
    steps = seq // step_tok
    n_tok = bsz * seq
    tiles = seq // TOK_TILE
    const2 = lambda b, t: (0, 0)
    tok = lambda b, t: (b * steps + t, 0)
    tile3 = lambda b, t: (b * steps + t, 0, 0)
    return pl.pallas_call(
        functools.partial(_mix_prompt_kernel, alpha_res),
        grid=(bsz, steps),
        in_specs=[
            pl.BlockSpec((None, step_tok, D_MODEL), lambda b, t: (b, t, 0)),
            pl.BlockSpec((bsz, N_MOD * D_MODEL), const2),
            pl.BlockSpec((D_MODEL, PROJ_COLS), const2, pipeline_mode=pl.Buffered(1)),
            pl.BlockSpec((A_HEADS, A_HEAD_DIM), const2),
            pl.BlockSpec((A_HEADS, A_HEAD_DIM), const2),
            pl.BlockSpec((A_HEADS, CHUNK, CHUNK), lambda b, t: (0, 0, 0)),
            pl.BlockSpec((CHUNK, A_HEADS), const2),
            pl.BlockSpec((3, B_WIDTH), const2),
            pl.BlockSpec((1, B_WIDTH), const2),
            pl.BlockSpec((D_MODEL, D_MODEL), const2, pipeline_mode=pl.Buffered(1)),
            pl.BlockSpec((1, D_MODEL), const2),
            pl.BlockSpec((1, D_MODEL), const2),
            pl.BlockSpec((2 * N_EXPERTS, D_MODEL), const2),
            pl.BlockSpec((N_EXPERTS, 1), const2),
        ],
        out_specs=[
            pl.BlockSpec((step_tok, D_MODEL), tok),
            pl.BlockSpec((step_tok, D_MODEL), tok),
            pl.BlockSpec((MIX_SUB, SUBLANES, TOK_TILE), tile3),
            pl.BlockSpec((MIX_SUB, N_EXPERTS, LANES), tile3),
            pl.BlockSpec((None, 2, B_WIDTH), lambda b, t: (b, 0, 0)),
            pl.BlockSpec((D_MODEL, PROJ_COLS), const2),
            pl.BlockSpec((D_MODEL, D_MODEL), const2),
        ],
        out_shape=[
            jax.ShapeDtypeStruct((n_tok, D_MODEL), F32),
            jax.ShapeDtypeStruct((n_tok, D_MODEL), BF16),
            jax.ShapeDtypeStruct((bsz * tiles, SUBLANES, TOK_TILE), F32),
            jax.ShapeDtypeStruct((bsz * tiles, N_EXPERTS, LANES), jnp.int32),
            jax.ShapeDtypeStruct((bsz, 2, B_WIDTH), F32),
            jax.ShapeDtypeStruct((D_MODEL, PROJ_COLS), BF16),
            jax.ShapeDtypeStruct((D_MODEL, D_MODEL), BF16),
        ],
        scratch_shapes=[pltpu.VMEM((SUBLANES, B_WIDTH), F32)],
        compiler_params=pltpu.CompilerParams(
            dimension_semantics=("arbitrary", "arbitrary"), vmem_limit_bytes=VMEM_LIMIT),
        name="mix_prompt",
    )(x, mod, w_in, sgu_g, sgu_bln, sgu_w, sgu_bias_t, conv_w, conv_b, w_out,
      ln1_g, ln1_b, router_wt2, router_b_col)


def _mix_sample_kernel(alpha_res, x_ref, mod_ref, prev0_ref, prev1_ref, w_in_ref, sgu_g_ref,
                       sgu_bln_ref, sgu_w00_ref, sgu_b0_ref, conv_w_ref, conv_b_ref, w_out_ref,
                       ln1_g_ref, ln1_b_ref, router_w_ref, router_b_ref,
                       x1_ref, h2_ref, meta_ref, cnt_ref, q_ref, vn_ref):
    x = x_ref[...]
    sh1 = mod_ref[:, 0:D_MODEL]
    sc1 = mod_ref[:, D_MODEL:2 * D_MODEL]
    g1 = mod_ref[:, 2 * D_MODEL:3 * D_MODEL]
    sh2 = mod_ref[:, 3 * D_MODEL:4 * D_MODEL]
    sc2 = mod_ref[:, 4 * D_MODEL:5 * D_MODEL]
    h = (x * (1.0 + sc1) + sh1).astype(BF16)
    z = _dot(h, w_in_ref[...])
    u = z[:, 0:A_WIDTH]
    v = z[:, A_WIDTH:2 * A_WIDTH]
    gate_b = z[:, 2 * A_WIDTH:2 * A_WIDTH + B_WIDTH]
    gate_c = z[:, 2 * A_WIDTH + B_WIDTH:2 * A_WIDTH + 2 * B_WIDTH]
    hb = z[:, 2 * A_WIDTH + 2 * B_WIDTH:]

    vn_parts = []
    for hd in range(A_HEADS):
        sl = slice(hd * A_HEAD_DIM, (hd + 1) * A_HEAD_DIM)
        vn_parts.append(_layer_norm(v[:, sl], sgu_g_ref[hd:hd + 1, :], sgu_bln_ref[hd:hd + 1, :]))
    vn = jnp.concatenate(vn_parts, axis=-1)
    vn_ref[...] = vn
    a_out = u * (vn * sgu_w00_ref[...] + sgu_b0_ref[...])

    q = gate_c * hb
    q_ref[...] = q
    conv = (conv_b_ref[...] + prev0_ref[...] * conv_w_ref[0:1, :] + prev1_ref[...] * conv_w_ref[1:2, :]
            + q * conv_w_ref[2:3, :])
    b_out = gate_b * conv

    mix_in = jnp.concatenate([a_out, b_out], axis=-1).astype(BF16)
    mix = _dot(mix_in, w_out_ref[...])
    x1 = _layer_norm(alpha_res * x + g1 * mix, ln1_g_ref[...], ln1_b_ref[...])
    x1_ref[...] = x1
    h2 = x1 * (1.0 + sc2) + sh2
    h2_ref[...] = h2.astype(BF16)
    (meta, cnt), = _route([h2], router_w_ref[...], router_b_ref[...])
    meta_ref[...] = meta
    cnt_ref[...] = cnt.astype(jnp.int32)


def _mix_sample(x, mod, prev0, prev1, w_in_bf, sgu_g, sgu_bln, sgu_w00, sgu_b0, conv_w, conv_b,
                w_out_bf, ln1_g, ln1_b, router_wt2, router_b_col, alpha_res):
    n = x.shape[0]
    return pl.pallas_call(
        functools.partial(_mix_sample_kernel, alpha_res),
        out_shape=[
            jax.ShapeDtypeStruct((n, D_MODEL), F32),
            jax.ShapeDtypeStruct((n, D_MODEL), BF16),
            jax.ShapeDtypeStruct((SUBLANES, n), F32),
            jax.ShapeDtypeStruct((N_EXPERTS, LANES), jnp.int32),
            jax.ShapeDtypeStruct((n, B_WIDTH), F32),
            jax.ShapeDtypeStruct((n, A_WIDTH), F32),
        ],
        compiler_params=pltpu.CompilerParams(vmem_limit_bytes=VMEM_LIMIT),
        name="mix_sample",
    )(x, mod, prev0, prev1, w_in_bf, sgu_g, sgu_bln, sgu_w00, sgu_b0, conv_w, conv_b, w_out_bf,
      ln1_g, ln1_b, router_wt2, router_b_col)


def _strip_copy(src_ref, src_row, dst_ref, dst_row, n_rows, sem, priority=0):
    @pl.when(n_rows > 0)
    def _():
        pltpu.make_async_copy(src_ref.at[pl.ds(src_row, n_rows)],
                              dst_ref.at[pl.ds(dst_row, n_rows)], sem).start(priority=priority)


def _for_each_expert(strip):
    for e in range(N_EXPERTS):
        strip(e, 0)


def _wait_rows(hbm_ref, vmem_ref, n_rows, sem):
    pltpu.make_async_copy(hbm_ref.at[pl.ds(0, n_rows)], vmem_ref.at[pl.ds(0, n_rows)], sem).wait()


def _sort_kernel(n_tok, tile0, n_steps, subs, first, *refs):
    if first:
        (sorted_row_ref, off_ref, cnt_ref, pad_row_ref, pad_n_ref, h2_ref, meta_ref,
         xs_ref, stage_ref, zero_ref, sem, pad_sem) = refs
    else:
        (sorted_row_ref, off_ref, cnt_ref, pad_row_ref, pad_n_ref, h2_ref, meta_ref, _,
         xs_ref, stage_ref, sem) = refs
    j = pl.program_id(0)
    n_sorted = n_tok * TOP_K
    parity = j % 2

    if first:
        @pl.when(j == 0)
        def _():
            zero_ref[...] = jnp.zeros_like(zero_ref)

            def start(e, carry):
                _strip_copy(zero_ref, 0, xs_ref, pad_row_ref[e], pad_n_ref[e], pad_sem)
                return carry
            lax.fori_loop(0, N_EXPERTS, start, 0)

            def wait(e, carry):
                @pl.when(pad_n_ref[e] > 0)
                def _():
                    _wait_rows(xs_ref, zero_ref, pad_n_ref[e], pad_sem)
                return carry
            lax.fori_loop(0, N_EXPERTS, wait, 0)

    sub_r = lax.broadcasted_iota(jnp.int16, (n_sorted, n_tok), 0)
    for sub in range(subs):
        slot = parity * subs + sub
        pos = meta_ref[sub][0:TOP_K, :].astype(jnp.int16)
        perm = jnp.zeros((n_sorted, n_tok), BF16)
        for k in range(TOP_K):
            perm = jnp.where(sub_r == pos[k:k + 1, :], jnp.ones((), BF16), perm)
        rows = _dot(perm, h2_ref[sub * n_tok:(sub + 1) * n_tok, :])
        _store_rows(stage_ref, slot, rows)

        base = (tile0 + j * subs + sub) * N_EXPERTS

        def strip(e, priority, slot=slot, base=base):
            _strip_copy(stage_ref.at[slot], off_ref[base + e], xs_ref, sorted_row_ref[base + e],
                        cnt_ref[base + e], sem.at[slot], priority)
        _for_each_expert(strip)

    @pl.when(j > 0)
    def _():
        for sub in range(subs):
            _wait_rows(xs_ref, stage_ref.at[0], n_sorted, sem.at[(1 - parity) * subs + sub])

    @pl.when(j == n_steps - 1)
    def _():
        for sub in range(subs):
            _wait_rows(xs_ref, stage_ref.at[0], n_sorted, sem.at[parity * subs + sub])


def _sort(tables, pad_row, pad_n, h2, meta, xs, n_sorted_rows, n_tok, tile0, subs):
    sorted_row, off, cnt = tables
    first = xs is None
    n_steps = h2.shape[0] // (n_tok * subs)
    in_specs = [pl.BlockSpec((subs * n_tok, D_MODEL), lambda j, *_: (j, 0)),
                pl.BlockSpec((subs, SUBLANES, n_tok), lambda j, *_: (j, 0, 0))]
    operands = [sorted_row, off, cnt, pad_row, pad_n, h2, meta]
    scratch = [pltpu.VMEM((2 * subs, n_tok * TOP_K, ROW_SLABS, LANES), BF16)]
    if first:
        scratch.append(pltpu.VMEM((ROW_TILE, ROW_SLABS, LANES), BF16))
        aliases = {}
    else:
        in_specs.append(pl.BlockSpec(memory_space=pl.ANY))
        operands.append(xs)
        aliases = {len(operands) - 1: 0}
    scratch.append(pltpu.SemaphoreType.DMA((2 * subs,)))
    if first:
        scratch.append(pltpu.SemaphoreType.DMA(()))
    grid_spec = pltpu.PrefetchScalarGridSpec(
        num_scalar_prefetch=5,
        grid=(n_steps,),
        in_specs=in_specs,
        out_specs=pl.BlockSpec(memory_space=pl.ANY),
        scratch_shapes=scratch,
    )
    return pl.pallas_call(
        functools.partial(_sort_kernel, n_tok, tile0, n_steps, subs, first),
        grid_spec=grid_spec,
        out_shape=jax.ShapeDtypeStruct((n_sorted_rows, ROW_SLABS, LANES), BF16),
        input_output_aliases=aliases,
        compiler_params=pltpu.CompilerParams(
            dimension_semantics=("arbitrary",), vmem_limit_bytes=VMEM_LIMIT),
        name="sort_first" if first else "sort_more",
    )(*operands)


def _experts_kernel(tile_start_ref, n_tile_ref, short_last_ref, n_used_ref, xs_ref, w_gu_ref, b_gu_ref,
                    w_down_ref, b_down_ref, ys_ref, w_gu_bf_ref, w_down_bf_ref, x_buf, y_buf, x_sem,
                    y_sem):
    e = pl.program_id(0)
    n_used = n_used_ref[0]
    first_tile = tile_start_ref[e]
    n_tile = n_tile_ref[e]

    def x_copy(g):
        slot = g % TILE_SLOTS
        return pltpu.make_async_copy(xs_ref.at[pl.ds(g * ROW_TILE, ROW_TILE)], x_buf.at[slot],
                                     x_sem.at[slot])

    def y_copy(g):
        slot = g % TILE_SLOTS
        return pltpu.make_async_copy(y_buf.at[slot], ys_ref.at[pl.ds(g * ROW_TILE, ROW_TILE)],
                                     y_sem.at[slot])

    def request(g):
        @pl.when(g < n_used)
        def _():
            x_copy(g).start(priority=1)

    @pl.when(e == 0)
    def _():
        for g in range(TILE_AHEAD):
            request(g)

    @pl.when(n_tile > 0)
    def _():
        w_gu_bf_ref[...] = w_gu_ref[...].astype(BF16)
        w_down_bf_ref[...] = w_down_ref[...].astype(BF16)

    def begin(g):
        x_copy(g).wait()
        request(g + TILE_AHEAD)

        @pl.when(g >= TILE_SLOTS)
        def _():
            y_copy(g - TILE_SLOTS).wait()

    def compute(g, rows=ROW_TILE):
        slot = g % TILE_SLOTS
        x = x_buf[slot, 0:rows].reshape(rows, D_MODEL)
        gu = _dot(x, w_gu_bf_ref[...]) + b_gu_ref[...]
        gate = jnp.minimum(gu[:, :D_EXPERT], SWIGLU_LIMIT)
        up = jnp.clip(gu[:, D_EXPERT:], -SWIGLU_LIMIT, SWIGLU_LIMIT)
        act = (up + 1.0) * gate * jax.nn.sigmoid(SWIGLU_ALPHA * gate)
        y = _dot(act.astype(BF16), w_down_bf_ref[...]) + b_down_ref[...]
        y_buf[slot, 0:rows] = y.astype(BF16).reshape(rows, ROW_SLABS, LANES)
        if rows < ROW_TILE:
            y_buf[slot, rows:ROW_TILE] = jnp.zeros((ROW_TILE - rows, ROW_SLABS, LANES), BF16)
        y_copy(g).start(priority=1)

    def group(g, size):
        for k in range(size):
            begin(g + k)
        for k in range(size):
            compute(g + k)

    def group_body(p, carry):
        group(first_tile + TILE_GROUP * p, TILE_GROUP)
        return carry

    short_last = short_last_ref[e]
    n_full = n_tile - short_last
    n_group = n_full // TILE_GROUP
    lax.fori_loop(0, n_group, group_body, 0)
    rest = n_full - n_group * TILE_GROUP
    g_rest = first_tile + n_group * TILE_GROUP
    size = TILE_GROUP // 2
    while size >= 1:
        @pl.when((rest & size) != 0)
        def _(g_rest=g_rest, size=size):
            group(g_rest, size)
        g_rest = g_rest + (rest & size)
        size //= 2

    @pl.when(short_last == 1)
    def _():
        g = first_tile + n_full
        begin(g)
        compute(g, ROW_TILE // 2)

    @pl.when(e == N_EXPERTS - 1)
    def _():
        for back in range(TILE_SLOTS, 0, -1):
            @pl.when(n_used >= back)
            def _(back=back):
                y_copy(n_used - back).wait()


def _experts(tile_start, n_tile_e, pad_n, n_used, xs, w_gu, b_gu, w_down, b_down):
    short_last = (pad_n >= ROW_TILE // 2).astype(jnp.int32)
    w_blk = lambda e, *_: (e, 0, 0)
    grid_spec = pltpu.PrefetchScalarGridSpec(
        num_scalar_prefetch=4,
        grid=(N_EXPERTS,),
        in_specs=[
            pl.BlockSpec(memory_space=pl.ANY),
            pl.BlockSpec((None, D_MODEL, 2 * D_EXPERT), w_blk),
            pl.BlockSpec((None, 1, 2 * D_EXPERT), w_blk),
            pl.BlockSpec((None, D_EXPERT, D_MODEL), w_blk),
            pl.BlockSpec((None, 1, D_MODEL), w_blk),
        ],
        out_specs=pl.BlockSpec(memory_space=pl.ANY),
        scratch_shapes=[pltpu.VMEM((D_MODEL, 2 * D_EXPERT), BF16),
                        pltpu.VMEM((D_EXPERT, D_MODEL), BF16),
                        pltpu.VMEM((TILE_SLOTS, ROW_TILE, ROW_SLABS, LANES), BF16),
                        pltpu.VMEM((TILE_SLOTS, ROW_TILE, ROW_SLABS, LANES), BF16),
                        pltpu.SemaphoreType.DMA((TILE_SLOTS,)),
                        pltpu.SemaphoreType.DMA((TILE_SLOTS,))],
    )
    return pl.pallas_call(
        _experts_kernel,
        grid_spec=grid_spec,
        out_shape=jax.ShapeDtypeStruct(xs.shape, BF16),
        compiler_params=pltpu.CompilerParams(
            dimension_semantics=("arbitrary",), vmem_limit_bytes=VMEM_LIMIT),
        name="experts",
    )(tile_start, n_tile_e, short_last, n_used, xs, w_gu, b_gu.reshape(N_EXPERTS, 1, -1), w_down,
      b_down.reshape(N_EXPERTS, 1, -1))


def _combine_kernel(n_tok, tile0, n_steps, subs, seq_steps, alpha_res, sorted_row_ref, off_ref, cnt_ref,
                    ys_ref, x1_ref, meta_ref, mod_ref, ln2_g_ref, ln2_b_ref, out_ref,
                    stage_ref, sem):
    j = pl.program_id(0)
    n_sorted = n_tok * TOP_K
    parity = j % 2

    def start_step(step, step_parity, live):
        for sub in range(subs):
            base = (tile0 + step * subs + sub) * N_EXPERTS
            to_slot = step_parity * subs + sub

            def strip(e, priority, base=base, to_slot=to_slot):
                _strip_copy(ys_ref, sorted_row_ref[base + e], stage_ref.at[to_slot],
                            off_ref[base + e], jnp.where(live, cnt_ref[base + e], 0),
                            sem.at[to_slot], priority)
            _for_each_expert(strip)

    @pl.when(j == 0)
    def _():
        start_step(j, parity, True)

    start_step(jnp.minimum(j + 1, n_steps - 1), 1 - parity, j + 1 < n_steps)

    sub_r = lax.broadcasted_iota(jnp.int16, (n_sorted, n_tok), 0)
    for sub in range(subs):
        slot = parity * subs + sub
        rows = slice(sub * n_tok, (sub + 1) * n_tok)
        meta = meta_ref[sub]
        pos = meta[0:TOP_K, :].astype(jnp.int16)
        gate = meta[TOP_K:2 * TOP_K, :].astype(BF16)
        comb_t = jnp.zeros((n_sorted, n_tok), BF16)
        for k in range(TOP_K):
            comb_t = jnp.where(sub_r == pos[k:k + 1, :], gate[k:k + 1, :], comb_t)

        _wait_rows(ys_ref, stage_ref.at[0], n_sorted, sem.at[slot])
        ys = _load_rows(stage_ref, slot)
        ffn = lax.dot_general(comb_t, ys, (((0,), (0,)), ((), ())), preferred_element_type=F32)
        if seq_steps is None:
            g2 = mod_ref[rows, 5 * D_MODEL:6 * D_MODEL]
        else:
            g2 = mod_ref[pl.ds(j // seq_steps, 1), 5 * D_MODEL:6 * D_MODEL]
        y = _layer_norm(alpha_res * x1_ref[rows, :] + g2 * ffn, ln2_g_ref[...], ln2_b_ref[...])
        if seq_steps is None:
            out_ref[rows, 0, :] = y
        else:
            out_ref[rows, :] = y


def _combine(tables, ys, x1, meta, mod, ln2_g, ln2_b, n_tok, tile0, subs, alpha_res):
    sorted_row, off, cnt = tables
    step_tok = subs * n_tok
    n_steps = x1.shape[0] // step_tok
    if mod.shape[0] == x1.shape[0]:
        seq_steps = None
        mod_spec = pl.BlockSpec((step_tok, N_MOD * D_MODEL), lambda j, *_: (j, 0))
        out_spec = pl.BlockSpec((step_tok, 1, D_MODEL), lambda j, *_: (j, 0, 0))
        out_shape = jax.ShapeDtypeStruct((x1.shape[0], 1, D_MODEL), F32)
    else:
        seq_steps = n_steps // mod.shape[0]
        mod_spec = pl.BlockSpec(mod.shape, lambda j, *_: (0, 0))
        out_spec = pl.BlockSpec((step_tok, D_MODEL), lambda j, *_: (j, 0))
        out_shape = jax.ShapeDtypeStruct(x1.shape, F32)
    grid_spec = pltpu.PrefetchScalarGridSpec(
        num_scalar_prefetch=3,
        grid=(n_steps,),
        in_specs=[
            pl.BlockSpec(memory_space=pl.ANY),
            pl.BlockSpec((step_tok, D_MODEL), lambda j, *_: (j, 0)),
            pl.BlockSpec((subs, SUBLANES, n_tok), lambda j, *_: (j, 0, 0)),
            mod_spec,
            pl.BlockSpec((1, D_MODEL), lambda j, *_: (0, 0)),
            pl.BlockSpec((1, D_MODEL), lambda j, *_: (0, 0)),
        ],
        out_specs=out_spec,
        scratch_shapes=[pltpu.VMEM((2 * subs, n_tok * TOP_K, ROW_SLABS, LANES), BF16),
                        pltpu.SemaphoreType.DMA((2 * subs,))],
    )
    return pl.pallas_call(
        functools.partial(_combine_kernel, n_tok, tile0, n_steps, subs, seq_steps, alpha_res),
        grid_spec=grid_spec,
        out_shape=out_shape,
        compiler_params=pltpu.CompilerParams(
            dimension_semantics=("arbitrary",), vmem_limit_bytes=VMEM_LIMIT),
        name="combine",
    )(sorted_row, off, cnt, ys, x1, meta, mod, ln2_g, ln2_b)


def _routing_tables(cnt_all):
    total = jnp.sum(cnt_all, axis=0)
    n_tile_e = (total + ROW_TILE - 1) // ROW_TILE
    tile_end = jnp.cumsum(n_tile_e)
    tile_start = tile_end - n_tile_e
    row_start = tile_start * ROW_TILE
    cum = jnp.cumsum(cnt_all, axis=0) - cnt_all
    off = jnp.cumsum(cnt_all, axis=1) - cnt_all
    sorted_row = row_start[None, :] + cum
    n_used = tile_end[-1]
    pad_row = row_start + total
    pad_n = n_tile_e * ROW_TILE - total
    i32 = lambda a: a.astype(jnp.int32)
    return ((i32(sorted_row).reshape(-1), i32(off).reshape(-1), i32(cnt_all).reshape(-1)),
            i32(pad_row), i32(pad_n), i32(tile_start), i32(n_tile_e), i32(n_used).reshape(1))


def kernel(x_prompt, x_sample, state_conv, c_prompt, c_sample, ada_w, ada_b, w_in, sgu_ln_g, sgu_ln_b,
           sgu_w, sgu_b, conv_w, conv_b, w_out, ln1_g, ln1_b, router_w, router_b, w_gu, b_gu,
           w_down, b_down, ln2_g, ln2_b):
    depth = ada_w.shape[0]
    assert depth == 1
    bsz, seq, _ = x_prompt.shape
    n_dec = x_sample.shape[0]
    assert x_sample.shape[1] == 1 and seq % TOK_TILE == 0
    alpha_res = (2.0 * depth) ** 0.25
    l = 0

    mod_s, mod_p = _ada(c_sample, c_prompt, ada_w[l], ada_b[l])

    router_wt =jnp.transpose(router_w[l])
    router_wt_hi = router_wt.astype(BF16)
    router_wt_lo = (router_wt - router_wt_hi.astype(F32)).astype(BF16)
    router_wt2 = jnp.concatenate([router_wt_hi, router_wt_lo], axis=0)
    router_b_col = router_b[l].reshape(N_EXPERTS, 1)
    row = lambda a: a.reshape(1, -1)

    x1_p, h2_p, meta_p, cnt_p, convst_p, w_in_bf, w_out_bf = _mix_prompt(
        x_prompt, mod_p, w_in[l], sgu_ln_g[l], sgu_ln_b[l], sgu_w[l], jnp.transpose(sgu_b[l]),
        conv_w[l], row(conv_b[l]), w_out[l], row(ln1_g[l]), row(ln1_b[l]), router_wt2,
        router_b_col, alpha_res)
    x1_s, h2_s, meta_s, cnt_s, q_s, vn_s = _mix_sample(
        x_sample.reshape(n_dec, D_MODEL), mod_s, state_conv[l, :, 0, :], state_conv[l, :, 1, :],
        w_in_bf, sgu_ln_g[l], sgu_ln_b[l], row(jnp.repeat(sgu_w[l, :, 0, 0], A_HEAD_DIM)),
        row(jnp.repeat(sgu_b[l, :, 0], A_HEAD_DIM)), conv_w[l], row(conv_b[l]), w_out_bf,
        row(ln1_g[l]), row(ln1_b[l]), router_wt2, router_b_col, alpha_res)

    n_ptiles = bsz * seq // TOK_TILE
    cnt_all = jnp.concatenate([cnt_p[:, :, 0], cnt_s[None, :, 0]], axis=0)
    n_assign = (bsz * seq + n_dec) * TOP_K
    n_row_tiles = -(-n_assign // ROW_TILE) + N_EXPERTS
    tables, pad_row, pad_n, tile_start, n_tile_e, n_used = _routing_tables(cnt_all)

    n_sorted_rows = n_row_tiles * ROW_TILE
    meta_s = meta_s[None]
    xs = _sort(tables, pad_row, pad_n, h2_p, meta_p, None, n_sorted_rows, TOK_TILE, 0, SORT_SUB)
    xs = _sort(tables, pad_row, pad_n, h2_s, meta_s, xs, n_sorted_rows, n_dec, n_ptiles, 1)
    ys = _experts(tile_start, n_tile_e, pad_n, n_used, xs, w_gu[l], b_gu[l], w_down[l], b_down[l])
    y_p = _combine(tables, ys, x1_p, meta_p, mod_p, row(ln2_g[l]), row(ln2_b[l]), TOK_TILE, 0,
                   COMBINE_SUB, alpha_res)
    y_s = _combine(tables, ys, x1_s, meta_s, mod_s, row(ln2_g[l]), row(ln2_b[l]), n_dec, n_ptiles,
                   1, alpha_res)

    conv_state_sample = jnp.stack([state_conv[l, :, 1, :], q_s], axis=1)[None]
    return (y_p.reshape(bsz, seq, D_MODEL),
            y_s,
            convst_p[None],
            conv_state_sample,
            vn_s.reshape(1, n_dec, 1, A_HEADS, A_HEAD_DIM))
```

```python
import functools

import jax
import jax.numpy as jnp
from jax import lax
from jax.experimental import pallas as pl
from jax.experimental.pallas import tpu as pltpu

F32 = jnp.float32
BF16 = jnp.bfloat16

D_MODEL = 1024
A_WIDTH = 512
B_WIDTH = 512
A_HEADS = 4
A_HEAD_DIM = 128
CHUNK = 128
PROJ_COLS = 2 * A_WIDTH + 3 * B_WIDTH
N_EXPERTS = 32
TOP_K = 4
D_EXPERT = 1024
SWIGLU_LIMIT = 7.0
SWIGLU_ALPHA = 1.702
LN_EPS = 1e-5
N_MOD = 6

LANES = 128
SUBLANES = 8
ROW_SLABS = D_MODEL // LANES
TOK_TILE = 256
MIX_SUB = 2
SORT_SUB = 4
COMBINE_SUB = 2
ROW_TILE = 256
TILE_GROUP = 4
TILE_AHEAD = TILE_GROUP
TILE_SLOTS = 2 * TILE_GROUP
W_CHUNKS = 4
VMEM_LIMIT = 56 * 1024 * 1024


def _layer_norm(x, g, b):
    mu = jnp.mean(x, axis=-1, keepdims=True)
    xc = x - mu
    var = jnp.mean(xc * xc, axis=-1, keepdims=True)
    return xc * lax.rsqrt(var + LN_EPS) * g + b


def _dot(a, b):
    return jnp.dot(a, b, preferred_element_type=F32)


def _dot_nt(a, b):
    return lax.dot_general(a, b, (((1,), (1,)), ((), ())), preferred_element_type=F32)


def _store_rows(ref, slot, rows):
    n = rows.shape[0]
    ref[slot] = rows.astype(BF16).reshape(n, ROW_SLABS, LANES)


def _load_rows(ref, slot):
    return ref[slot].reshape(ref.shape[1], D_MODEL)


def _split_bf16(a):
    hi = a.astype(BF16)
    lo = (a - hi.astype(F32)).astype(BF16)
    return hi, lo


def _ada_kernel(c_a_ref, c_b_ref, w_ref, b_ref, o_a_ref, o_b_ref):
    n_a = c_a_ref.shape[0]
    c = jnp.concatenate([c_a_ref[...], c_b_ref[...]], axis=0)
    s_hi, s_lo = _split_bf16(c * jax.nn.sigmoid(c))
    w_hi, w_lo = _split_bf16(w_ref[...])
    m = _dot(s_hi, w_hi) + _dot(s_hi, w_lo) + _dot(s_lo, w_hi) + b_ref[...]
    o_a_ref[...] = m[:n_a]
    o_b_ref[...] = m[n_a:]


def _ada(c_a, c_b, ada_w, ada_b):
    cols = 3 * D_MODEL // 2
    rows_spec = lambda c: pl.BlockSpec((c.shape[0], D_MODEL), lambda n: (0, 0))
    out_spec = lambda c: pl.BlockSpec((c.shape[0], cols), lambda n: (0, n))
    return pl.pallas_call(
        _ada_kernel,
        grid=(N_MOD * D_MODEL // cols,),
        in_specs=[
            rows_spec(c_a),
            rows_spec(c_b),
            pl.BlockSpec((D_MODEL, cols), lambda n: (0, n)),
            pl.BlockSpec((1, cols), lambda n: (0, n)),
        ],
        out_specs=[out_spec(c_a), out_spec(c_b)],
        out_shape=[jax.ShapeDtypeStruct((c_a.shape[0], N_MOD * D_MODEL), F32),
                   jax.ShapeDtypeStruct((c_b.shape[0], N_MOD * D_MODEL), F32)],
        name="ada",
    )(c_a, c_b, ada_w, ada_b.reshape(1, -1))


def _route(h2_tiles, router_wt2, router_b):
    n_tiles = len(h2_tiles)
    n_tile_tok = h2_tiles[0].shape[0]
    h2 = h2_tiles[0] if n_tiles == 1 else jnp.concatenate(h2_tiles, axis=0)
    n = h2.shape[0]
    lanes = [slice(i * n_tile_tok, (i + 1) * n_tile_tok) for i in range(n_tiles)]
    h_hi = h2.astype(BF16)
    h_lo = (h2 - h_hi.astype(F32)).astype(BF16)
    l1 = _dot_nt(router_wt2, h_hi)
    l2 = _dot_nt(router_wt2[:N_EXPERTS], h_lo)
    logits = l1[:N_EXPERTS] + l1[N_EXPERTS:] + l2 + router_b

    sub = lax.broadcasted_iota(jnp.int32, (N_EXPERTS, n), 0)
    work = logits
    vals, hots = [], []
    for _ in range(TOP_K):
        m = jnp.max(work, axis=0, keepdims=True)
        idx = jnp.min(jnp.where(work == m, sub, N_EXPERTS), axis=0, keepdims=True)
        hot = sub == idx
        work = jnp.where(hot, -jnp.inf, work)
        vals.append(m)
        hots.append(hot)
    exps = [jnp.exp(v - vals[0]) for v in vals]
    denom = exps[0] + exps[1] + exps[2] + exps[3]
    gates = [e / denom for e in exps]

    onehot = jnp.zeros((N_EXPERTS, n), F32)
    for hot in hots:
        onehot = onehot + hot.astype(F32)
    onehot_bf = onehot.astype(BF16)
    t_r = lax.broadcasted_iota(jnp.int32, (n_tile_tok, n_tile_tok), 0)
    t_c = lax.broadcasted_iota(jnp.int32, (n_tile_tok, n_tile_tok), 1)
    stacked = onehot_bf if n_tiles == 1 else jnp.concatenate([onehot_bf[:, ln] for ln in lanes], axis=0)
    rank_st = _dot(stacked, (t_r < t_c).astype(BF16))
    e_r = lax.broadcasted_iota(jnp.int32, (N_EXPERTS, N_EXPERTS), 0)
    e_c = lax.broadcasted_iota(jnp.int32, (N_EXPERTS, N_EXPERTS), 1)
    below = _dot((e_c < e_r).astype(BF16), onehot_bf)
    cnts, bases = [], []
    for i, ln in enumerate(lanes):
        cnts.append(jnp.sum(onehot[:, ln], axis=1, keepdims=True))
        off = jnp.sum(below[:, ln], axis=1, keepdims=True)
        bases.append(rank_st[i * N_EXPERTS:(i + 1) * N_EXPERTS] + off)
    base = bases[0] if n_tiles == 1 else jnp.concatenate(bases, axis=1)

    row8 = lax.broadcasted_iota(jnp.int32, (SUBLANES, n), 0)
    meta = jnp.zeros((SUBLANES, n), F32)
    for k in range(TOP_K):
        pos_k = jnp.sum(jnp.where(hots[k], base, 0.0), axis=0, keepdims=True)
        meta = jnp.where(row8 == k, pos_k, meta)
        meta = jnp.where(row8 == TOP_K + k, gates[k], meta)
    return [(meta[:, ln], jnp.broadcast_to(c, (N_EXPERTS, LANES))) for ln, c in zip(lanes, cnts)]


def _mix_prompt_kernel(alpha_res, x_ref, mod_ref, w_in_ref, sgu_g_ref, sgu_bln_ref, sgu_w_ref,
                       sgu_bias_ref, conv_w_ref, conv_b_ref, w_out_ref, ln1_g_ref, ln1_b_ref,
                       router_w_ref, router_b_ref,
                       x1_ref, h2_ref, meta_ref, cnt_ref, convst_ref, w_in_bf_ref, w_out_bf_ref,
                       carry_ref):
    t = pl.program_id(1)

    @pl.when(jnp.logical_and(pl.program_id(0) == 0, t == 0))
    def _():
        w_in_bf_ref[...] = w_in_ref[...].astype(BF16)
        w_out_bf_ref[...] = w_out_ref[...].astype(BF16)

    @pl.when(t == 0)
    def _():
        carry_ref[...] = jnp.zeros_like(carry_ref)

    seq_row = pl.ds(pl.program_id(0), 1)
    sh1, sc1, g1, sh2, sc2 = [mod_ref[seq_row, k * D_MODEL:(k + 1) * D_MODEL] for k in range(5)]
    r_i = lax.broadcasted_iota(jnp.int32, (CHUNK, CHUNK), 0)
    c_i = lax.broadcasted_iota(jnp.int32, (CHUNK, CHUNK), 1)
    tril = c_i <= r_i
    tm = TOK_TILE
    prev2 = carry_ref[0:1, :]
    prev1 = carry_ref[1:2, :]

    subs = range(MIX_SUB)
    tile_rows = [slice(sub * tm, (sub + 1) * tm) for sub in subs]
    xs_in = [x_ref[rows, :] for rows in tile_rows]
    zs = [_dot((x * (1.0 + sc1) + sh1).astype(BF16), w_in_bf_ref[...]) for x in xs_in]

    mix_ins = []
    for sub in subs:
        z = zs[sub]
        u = z[:, 0:A_WIDTH]
        v = z[:, A_WIDTH:2 * A_WIDTH]
        gate_b = z[:, 2 * A_WIDTH:2 * A_WIDTH + B_WIDTH]
        gate_c = z[:, 2 * A_WIDTH + B_WIDTH:2 * A_WIDTH + 2 * B_WIDTH]
        hb = z[:, 2 * A_WIDTH + 2 * B_WIDTH:]

        a_parts = []
        for hd in range(A_HEADS):
            sl = slice(hd * A_HEAD_DIM, (hd + 1) * A_HEAD_DIM)
            vn = _layer_norm(v[:, sl], sgu_g_ref[hd:hd + 1, :], sgu_bln_ref[hd:hd + 1, :]).astype(BF16)
            wm = jnp.where(tril, sgu_w_ref[hd], 0.0).astype(BF16)
            bias = sgu_bias_ref[:, hd:hd + 1]
            s_parts = []
            for c in range(tm // CHUNK):
                s_parts.append(_dot(wm, vn[c * CHUNK:(c + 1) * CHUNK, :]) + bias)
            a_parts.append(u[:, sl] * jnp.concatenate(s_parts, axis=0))

        q = gate_c * hb
        row = lax.broadcasted_iota(jnp.int32, q.shape, 0)
        q_m1 = jnp.where(row == 0, prev1, pltpu.roll(q, 1, 0))
        q_m2 = jnp.where(row == 0, prev2, jnp.where(row == 1, prev1, pltpu.roll(q, 2, 0)))
        conv = (conv_b_ref[...] + q_m2 * conv_w_ref[0:1, :] + q_m1 * conv_w_ref[1:2, :]
                + q * conv_w_ref[2:3, :])
        b_out = gate_b * conv
        prev2 = q[tm - 2:tm - 1, :]
        prev1 = q[tm - 1:tm, :]

        mix_ins.append(jnp.concatenate(a_parts + [b_out], axis=-1).astype(BF16))

    mixes = [_dot(mix_in, w_out_bf_ref[...]) for mix_in in mix_ins]
    h2s = []
    for sub in subs:
        x1 = _layer_norm(alpha_res * xs_in[sub] + g1 * mixes[sub], ln1_g_ref[...], ln1_b_ref[...])
        x1_ref[tile_rows[sub], :] = x1
        h2 = x1 * (1.0 + sc2) + sh2
        h2_ref[tile_rows[sub], :] = h2.astype(BF16)
        h2s.append(h2)
    routed = _route(h2s, router_w_ref[...], router_b_ref[...])
    for sub in subs:
        meta, cnt = routed[sub]
        meta_ref[sub] = meta
        cnt_ref[sub] = cnt.astype(jnp.int32)

    last2 = jnp.concatenate([prev2, prev1], axis=0)
    carry_ref[0:2, :] = last2
    convst_ref[...] = last2


def _mix_prompt(x, mod, w_in, sgu_g, sgu_bln, sgu_w, sgu_bias_t, conv_w, conv_b, w_out,
                ln1_g, ln1_b, router_wt2, router_b_col, alpha_res):
    bsz, seq, _ = x.shape
    step_tok = MIX_SUB * TOK_TILE
    steps = seq // step_tok
    n_tok = bsz * seq
    tiles = seq // TOK_TILE
    const2 = lambda b, t: (0, 0)
    tok = lambda b, t: (b * steps + t, 0)
    tile3 = lambda b, t: (b * steps + t, 0, 0)
    return pl.pallas_call(
        functools.partial(_mix_prompt_kernel, alpha_res),
        grid=(bsz, steps),
        in_specs=[
            pl.BlockSpec((None, step_tok, D_MODEL), lambda b, t: (b, t, 0)),
            pl.BlockSpec((bsz, N_MOD * D_MODEL), const2),
            pl.BlockSpec((D_MODEL, PROJ_COLS), const2, pipeline_mode=pl.Buffered(1)),
            pl.BlockSpec((A_HEADS, A_HEAD_DIM), const2),
            pl.BlockSpec((A_HEADS, A_HEAD_DIM), const2),
            pl.BlockSpec((A_HEADS, CHUNK, CHUNK), lambda b, t: (0, 0, 0)),
            pl.BlockSpec((CHUNK, A_HEADS), const2),
            pl.BlockSpec((3, B_WIDTH), const2),
            pl.BlockSpec((1, B_WIDTH), const2),
            pl.BlockSpec((D_MODEL, D_MODEL), const2, pipeline_mode=pl.Buffered(1)),
            pl.BlockSpec((1, D_MODEL), const2),
            pl.BlockSpec((1, D_MODEL), const2),
            pl.BlockSpec((2 * N_EXPERTS, D_MODEL), const2),
            pl.BlockSpec((N_EXPERTS, 1), const2),
        ],
        out_specs=[
            pl.BlockSpec((step_tok, D_MODEL), tok),
            pl.BlockSpec((step_tok, D_MODEL), tok),
            pl.BlockSpec((MIX_SUB, SUBLANES, TOK_TILE), tile3),
            pl.BlockSpec((MIX_SUB, N_EXPERTS, LANES), tile3),
            pl.BlockSpec((None, 2, B_WIDTH), lambda b, t: (b, 0, 0)),
            pl.BlockSpec((D_MODEL, PROJ_COLS), const2),
            pl.BlockSpec((D_MODEL, D_MODEL), const2),
        ],
        out_shape=[
            jax.ShapeDtypeStruct((n_tok, D_MODEL), F32),
            jax.ShapeDtypeStruct((n_tok, D_MODEL), BF16),
            jax.ShapeDtypeStruct((bsz * tiles, SUBLANES, TOK_TILE), F32),
            jax.ShapeDtypeStruct((bsz * tiles, N_EXPERTS, LANES), jnp.int32),
            jax.ShapeDtypeStruct((bsz, 2, B_WIDTH), F32),
            jax.ShapeDtypeStruct((D_MODEL, PROJ_COLS), BF16),
            jax.ShapeDtypeStruct((D_MODEL, D_MODEL), BF16),
        ],
        scratch_shapes=[pltpu.VMEM((SUBLANES, B_WIDTH), F32)],
        compiler_params=pltpu.CompilerParams(
            dimension_semantics=("arbitrary", "arbitrary"), vmem_limit_bytes=VMEM_LIMIT),
        name="mix_prompt",
    )(x, mod, w_in, sgu_g, sgu_bln, sgu_w, sgu_bias_t, conv_w, conv_b, w_out,
      ln1_g, ln1_b, router_wt2, router_b_col)


def _mix_sample_kernel(alpha_res, x_ref, mod_ref, prev0_ref, prev1_ref, w_in_ref, sgu_g_ref,
                       sgu_bln_ref, sgu_w00_ref, sgu_b0_ref, conv_w_ref, conv_b_ref, w_out_ref,
                       ln1_g_ref, ln1_b_ref, router_w_ref, router_b_ref,
                       x1_ref, h2_ref, meta_ref, cnt_ref, q_ref, vn_ref):
    x = x_ref[...]
    sh1 = mod_ref[:, 0:D_MODEL]
    sc1 = mod_ref[:, D_MODEL:2 * D_MODEL]
    g1 = mod_ref[:, 2 * D_MODEL:3 * D_MODEL]
    sh2 = mod_ref[:, 3 * D_MODEL:4 * D_MODEL]
    sc2 = mod_ref[:, 4 * D_MODEL:5 * D_MODEL]
    h = (x * (1.0 + sc1) + sh1).astype(BF16)
    z = _dot(h, w_in_ref[...])
    u = z[:, 0:A_WIDTH]
    v = z[:, A_WIDTH:2 * A_WIDTH]
    gate_b = z[:, 2 * A_WIDTH:2 * A_WIDTH + B_WIDTH]
    gate_c = z[:, 2 * A_WIDTH + B_WIDTH:2 * A_WIDTH + 2 * B_WIDTH]
    hb = z[:, 2 * A_WIDTH + 2 * B_WIDTH:]

    vn_parts = []
    for hd in range(A_HEADS):
        sl = slice(hd * A_HEAD_DIM, (hd + 1) * A_HEAD_DIM)
        vn_parts.append(_layer_norm(v[:, sl], sgu_g_ref[hd:hd + 1, :], sgu_bln_ref[hd:hd + 1, :]))
    vn = jnp.concatenate(vn_parts, axis=-1)
    vn_ref[...] = vn
    a_out = u * (vn * sgu_w00_ref[...] + sgu_b0_ref[...])

    q = gate_c * hb
    q_ref[...] = q
    conv = (conv_b_ref[...] + prev0_ref[...] * conv_w_ref[0:1, :] + prev1_ref[...] * conv_w_ref[1:2, :]
            + q * conv_w_ref[2:3, :])
    b_out = gate_b * conv

    mix_in = jnp.concatenate([a_out, b_out], axis=-1).astype(BF16)
    mix = _dot(mix_in, w_out_ref[...])
    x1 = _layer_norm(alpha_res * x + g1 * mix, ln1_g_ref[...], ln1_b_ref[...])
    x1_ref[...] = x1
    h2 = x1 * (1.0 + sc2) + sh2
    h2_ref[...] = h2.astype(BF16)
    (meta, cnt), = _route([h2], router_w_ref[...], router_b_ref[...])
    meta_ref[...] = meta
    cnt_ref[...] = cnt.astype(jnp.int32)


def _mix_sample(x, mod, prev0, prev1, w_in_bf, sgu_g, sgu_bln, sgu_w00, sgu_b0, conv_w, conv_b,
                w_out_bf, ln1_g, ln1_b, router_wt2, router_b_col, alpha_res):
    n = x.shape[0]
    return pl.pallas_call(
        functools.partial(_mix_sample_kernel, alpha_res),
        out_shape=[
            jax.ShapeDtypeStruct((n, D_MODEL), F32),
            jax.ShapeDtypeStruct((n, D_MODEL), BF16),
            jax.ShapeDtypeStruct((SUBLANES, n), F32),
            jax.ShapeDtypeStruct((N_EXPERTS, LANES), jnp.int32),
            jax.ShapeDtypeStruct((n, B_WIDTH), F32),
            jax.ShapeDtypeStruct((n, A_WIDTH), F32),
        ],
        compiler_params=pltpu.CompilerParams(vmem_limit_bytes=VMEM_LIMIT),
        name="mix_sample",
    )(x, mod, prev0, prev1, w_in_bf, sgu_g, sgu_bln, sgu_w00, sgu_b0, conv_w, conv_b, w_out_bf,
      ln1_g, ln1_b, router_wt2, router_b_col)


def _strip_copy(src_ref, src_row, dst_ref, dst_row, n_rows, sem, priority=0):
    @pl.when(n_rows > 0)
    def _():
        pltpu.make_async_copy(src_ref.at[pl.ds(src_row, n_rows)],
                              dst_ref.at[pl.ds(dst_row, n_rows)], sem).start(priority=priority)


def _for_each_expert(strip):
    for e in range(N_EXPERTS):
        strip(e, 0)


def _wait_rows(hbm_ref, vmem_ref, n_rows, sem):
    pltpu.make_async_copy(hbm_ref.at[pl.ds(0, n_rows)], vmem_ref.at[pl.ds(0, n_rows)], sem).wait()


def _sort_kernel(n_tok, tile0, n_steps, subs, first, *refs):
    if first:
        (sorted_row_ref, off_ref, cnt_ref, pad_row_ref, pad_n_ref, h2_ref, meta_ref,
         xs_ref, stage_ref, zero_ref, sem, pad_sem) = refs
    else:
        (sorted_row_ref, off_ref, cnt_ref, pad_row_ref, pad_n_ref, h2_ref, meta_ref, _,
         xs_ref, stage_ref, sem) = refs
    j = pl.program_id(0)
    n_sorted = n_tok * TOP_K
    parity = j % 2

    if first:
        @pl.when(j == 0)
        def _():
            zero_ref[...] = jnp.zeros_like(zero_ref)

            def start(e, carry):
                _strip_copy(zero_ref, 0, xs_ref, pad_row_ref[e], pad_n_ref[e], pad_sem)
                return carry
            lax.fori_loop(0, N_EXPERTS, start, 0)

            def wait(e, carry):
                @pl.when(pad_n_ref[e] > 0)
                def _():
                    _wait_rows(xs_ref, zero_ref, pad_n_ref[e], pad_sem)
                return carry
            lax.fori_loop(0, N_EXPERTS, wait, 0)

    sub_r = lax.broadcasted_iota(jnp.int16, (n_sorted, n_tok), 0)
    for sub in range(subs):
        slot = parity * subs + sub
        pos = meta_ref[sub][0:TOP_K, :].astype(jnp.int16)
        perm = jnp.zeros((n_sorted, n_tok), BF16)
        for k in range(TOP_K):
            perm = jnp.where(sub_r == pos[k:k + 1, :], jnp.ones((), BF16), perm)
        rows = _dot(perm, h2_ref[sub * n_tok:(sub + 1) * n_tok, :])
        _store_rows(stage_ref, slot, rows)

        base = (tile0 + j * subs + sub) * N_EXPERTS

        def strip(e, priority, slot=slot, base=base):
            _strip_copy(stage_ref.at[slot], off_ref[base + e], xs_ref, sorted_row_ref[base + e],
                        cnt_ref[base + e], sem.at[slot], priority)
        _for_each_expert(strip)

    @pl.when(j > 0)
    def _():
        for sub in range(subs):
            _wait_rows(xs_ref, stage_ref.at[0], n_sorted, sem.at[(1 - parity) * subs + sub])

    @pl.when(j == n_steps - 1)
    def _():
        for sub in range(subs):
            _wait_rows(xs_ref, stage_ref.at[0], n_sorted, sem.at[parity * subs + sub])


def _sort(tables, pad_row, pad_n, h2, meta, xs, n_sorted_rows, n_tok, tile0, subs):
    sorted_row, off, cnt = tables
    first = xs is None
    n_steps = h2.shape[0] // (n_tok * subs)
    in_specs = [pl.BlockSpec((subs * n_tok, D_MODEL), lambda j, *_: (j, 0)),
                pl.BlockSpec((subs, SUBLANES, n_tok), lambda j, *_: (j, 0, 0))]
    operands = [sorted_row, off, cnt, pad_row, pad_n, h2, meta]
    scratch = [pltpu.VMEM((2 * subs, n_tok * TOP_K, ROW_SLABS, LANES), BF16)]
    if first:
        scratch.append(pltpu.VMEM((ROW_TILE, ROW_SLABS, LANES), BF16))
        aliases = {}
    else:
        in_specs.append(pl.BlockSpec(memory_space=pl.ANY))
        operands.append(xs)
        aliases = {len(operands) - 1: 0}
    scratch.append(pltpu.SemaphoreType.DMA((2 * subs,)))
    if first:
        scratch.append(pltpu.SemaphoreType.DMA(()))
    grid_spec = pltpu.PrefetchScalarGridSpec(
        num_scalar_prefetch=5,
        grid=(n_steps,),
        in_specs=in_specs,
        out_specs=pl.BlockSpec(memory_space=pl.ANY),
        scratch_shapes=scratch,
    )
    return pl.pallas_call(
        functools.partial(_sort_kernel, n_tok, tile0, n_steps, subs, first),
        grid_spec=grid_spec,
        out_shape=jax.ShapeDtypeStruct((n_sorted_rows, ROW_SLABS, LANES), BF16),
        input_output_aliases=aliases,
        compiler_params=pltpu.CompilerParams(
            dimension_semantics=("arbitrary",), vmem_limit_bytes=VMEM_LIMIT),
        name="sort_first" if first else "sort_more",
    )(*operands)


def _experts_kernel(tile_start_ref, n_tile_ref, short_last_ref, n_used_ref, xs_ref, w_gu_hbm, b_gu_ref,
                    w_down_hbm, b_down_ref, ys_ref, w_gu_ref, w_down_ref, w_gu_bf_ref, w_down_bf_ref,
                    x_buf, y_buf, w_sem, x_sem, y_sem):
    e = pl.program_id(0)
    n_used = n_used_ref[0]
    first_tile = tile_start_ref[e]
    n_tile = n_tile_ref[e]

    def x_copy(g):
        slot = g % TILE_SLOTS
        return pltpu.make_async_copy(xs_ref.at[pl.ds(g * ROW_TILE, ROW_TILE)], x_buf.at[slot],
                                     x_sem.at[slot])

    def y_copy(g):
        slot = g % TILE_SLOTS
        return pltpu.make_async_copy(y_buf.at[slot], ys_ref.at[pl.ds(g * ROW_TILE, ROW_TILE)],
                                     y_sem.at[slot])

    def request(g):
        @pl.when(g < n_used)
        def _():
            x_copy(g).start(priority=1)

    @pl.when(e == 0)
    def _():
        for g in range(TILE_AHEAD):
            request(g)

    par = e % 2
    chunk_rows = D_MODEL // W_CHUNKS

    def w_chunk_start(expert, c, to_par):
        rows = pl.ds(c * chunk_rows, chunk_rows)
        pltpu.make_async_copy(w_gu_hbm.at[expert, rows], w_gu_ref.at[to_par, rows],
                              w_sem.at[to_par]).start()
        pltpu.make_async_copy(w_down_hbm.at[expert, rows], w_down_ref.at[to_par, rows],
                              w_sem.at[to_par]).start()

    @pl.when(e == 0)
    def _():
        for c in range(W_CHUNKS):
            w_chunk_start(0, c, 0)

    pltpu.make_async_copy(w_gu_hbm.at[0], w_gu_ref.at[par], w_sem.at[par]).wait()
    pltpu.make_async_copy(w_down_hbm.at[0], w_down_ref.at[par], w_sem.at[par]).wait()

    @pl.when(n_tile > 0)
    def _():
        w_gu_bf_ref[...] = w_gu_ref[par].astype(BF16)
        w_down_bf_ref[...] = w_down_ref[par].astype(BF16)

    has_next = e + 1 < N_EXPERTS

    def begin(g):
        x_copy(g).wait()
        request(g + TILE_AHEAD)

        @pl.when(g >= TILE_SLOTS)
        def _():
            y_copy(g - TILE_SLOTS).wait()

    def compute(g, rows=ROW_TILE):
        slot = g % TILE_SLOTS
        x = x_buf[slot, 0:rows].reshape(rows, D_MODEL)
        gu = _dot(x, w_gu_bf_ref[...]) + b_gu_ref[...]
        gate = jnp.minimum(gu[:, :D_EXPERT], SWIGLU_LIMIT)
        up = jnp.clip(gu[:, D_EXPERT:], -SWIGLU_LIMIT, SWIGLU_LIMIT)
        act = (up + 1.0) * gate * jax.nn.sigmoid(SWIGLU_ALPHA * gate)
        y = _dot(act.astype(BF16), w_down_bf_ref[...]) + b_down_ref[...]
        y_buf[slot, 0:rows] = y.astype(BF16).reshape(rows, ROW_SLABS, LANES)
        if rows < ROW_TILE:
            y_buf[slot, rows:ROW_TILE] = jnp.zeros((ROW_TILE - rows, ROW_SLABS, LANES), BF16)
        y_copy(g).start(priority=1)

    def group(g, size):
        for k in range(size):
            begin(g + k)
        for k in range(size):
            compute(g + k)

    def group_body(p, carry):
        @pl.when(jnp.logical_and(has_next, p < W_CHUNKS))
        def _():
            w_chunk_start(e + 1, p, 1 - par)
        group(first_tile + TILE_GROUP * p, TILE_GROUP)
        return carry

    short_last = short_last_ref[e]
    n_full = n_tile - short_last
    n_group = n_full // TILE_GROUP
    lax.fori_loop(0, n_group, group_body, 0)
    for c in range(W_CHUNKS):
        @pl.when(jnp.logical_and(has_next, c >= n_group))
        def _(c=c):
            w_chunk_start(e + 1, c, 1 - par)
    rest = n_full - n_group * TILE_GROUP
    g_rest = first_tile + n_group * TILE_GROUP
    size = TILE_GROUP // 2
    while size >= 1:
        @pl.when((rest & size) != 0)
        def _(g_rest=g_rest, size=size):
            group(g_rest, size)
        g_rest = g_rest + (rest & size)
        size //= 2

    @pl.when(short_last == 1)
    def _():
        g = first_tile + n_full
        begin(g)
        compute(g, ROW_TILE // 2)

    @pl.when(e == N_EXPERTS - 1)
    def _():
        for back in range(TILE_SLOTS, 0, -1):
            @pl.when(n_used >= back)
            def _(back=back):
                y_copy(n_used - back).wait()


def _experts(tile_start, n_tile_e, pad_n, n_used, xs, w_gu, b_gu, w_down, b_down):
    short_last = (pad_n >= ROW_TILE // 2).astype(jnp.int32)
    w_blk = lambda e, *_: (e, 0, 0)
    grid_spec = pltpu.PrefetchScalarGridSpec(
        num_scalar_prefetch=4,
        grid=(N_EXPERTS,),
        in_specs=[
            pl.BlockSpec(memory_space=pl.ANY),
            pl.BlockSpec(memory_space=pl.ANY),
            pl.BlockSpec((None, 1, 2 * D_EXPERT), w_blk),
            pl.BlockSpec(memory_space=pl.ANY),
            pl.BlockSpec((None, 1, D_MODEL), w_blk),
        ],
        out_specs=pl.BlockSpec(memory_space=pl.ANY),
        scratch_shapes=[pltpu.VMEM((2, D_MODEL, 2 * D_EXPERT), F32),
                        pltpu.VMEM((2, D_EXPERT, D_MODEL), F32),
                        pltpu.VMEM((D_MODEL, 2 * D_EXPERT), BF16),
                        pltpu.VMEM((D_EXPERT, D_MODEL), BF16),
                        pltpu.VMEM((TILE_SLOTS, ROW_TILE, ROW_SLABS, LANES), BF16),
                        pltpu.VMEM((TILE_SLOTS, ROW_TILE, ROW_SLABS, LANES), BF16),
                        pltpu.SemaphoreType.DMA((2,)),
                        pltpu.SemaphoreType.DMA((TILE_SLOTS,)),
                        pltpu.SemaphoreType.DMA((TILE_SLOTS,))],
    )
    return pl.pallas_call(
        _experts_kernel,
        grid_spec=grid_spec,
        out_shape=jax.ShapeDtypeStruct(xs.shape, BF16),
        compiler_params=pltpu.CompilerParams(
            dimension_semantics=("arbitrary",), vmem_limit_bytes=VMEM_LIMIT),
        name="experts",
    )(tile_start, n_tile_e, short_last, n_used, xs, w_gu, b_gu.reshape(N_EXPERTS, 1, -1), w_down,
      b_down.reshape(N_EXPERTS, 1, -1))


def _combine_kernel(n_tok, tile0, n_steps, subs, seq_steps, alpha_res, sorted_row_ref, off_ref, cnt_ref,
                    ys_ref, x1_ref, meta_ref, mod_ref, ln2_g_ref, ln2_b_ref, out_ref,
                    stage_ref, sem):
    j = pl.program_id(0)
    n_sorted = n_tok * TOP_K
    parity = j % 2

    def start_step(step, step_parity, live):
        for sub in range(subs):
            base = (tile0 + step * subs + sub) * N_EXPERTS
            to_slot = step_parity * subs + sub

            def strip(e, priority, base=base, to_slot=to_slot):
                _strip_copy(ys_ref, sorted_row_ref[base + e], stage_ref.at[to_slot],
                            off_ref[base + e], jnp.where(live, cnt_ref[base + e], 0),
                            sem.at[to_slot], priority)
            _for_each_expert(strip)

    @pl.when(j == 0)
    def _():
        start_step(j, parity, True)

    start_step(jnp.minimum(j + 1, n_steps - 1), 1 - parity, j + 1 < n_steps)

    sub_r = lax.broadcasted_iota(jnp.int16, (n_sorted, n_tok), 0)
    for sub in range(subs):
        slot = parity * subs + sub
        rows = slice(sub * n_tok, (sub + 1) * n_tok)
        meta = meta_ref[sub]
        pos = meta[0:TOP_K, :].astype(jnp.int16)
        gate = meta[TOP_K:2 * TOP_K, :].astype(BF16)
        comb_t = jnp.zeros((n_sorted, n_tok), BF16)
        for k in range(TOP_K):
            comb_t = jnp.where(sub_r == pos[k:k + 1, :], gate[k:k + 1, :], comb_t)

        _wait_rows(ys_ref, stage_ref.at[0], n_sorted, sem.at[slot])
        ys = _load_rows(stage_ref, slot)
        ffn = lax.dot_general(comb_t, ys, (((0,), (0,)), ((), ())), preferred_element_type=F32)
        if seq_steps is None:
            g2 = mod_ref[rows, 5 * D_MODEL:6 * D_MODEL]
        else:
            g2 = mod_ref[pl.ds(j // seq_steps, 1), 5 * D_MODEL:6 * D_MODEL]
        y = _layer_norm(alpha_res * x1_ref[rows, :] + g2 * ffn, ln2_g_ref[...], ln2_b_ref[...])
        if seq_steps is None:
            out_ref[rows, 0, :] = y
        else:
            out_ref[rows, :] = y


def _combine(tables, ys, x1, meta, mod, ln2_g, ln2_b, n_tok, tile0, subs, alpha_res):
    sorted_row, off, cnt = tables
    step_tok = subs * n_tok
    n_steps = x1.shape[0] // step_tok
    if mod.shape[0] == x1.shape[0]:
        seq_steps = None
        mod_spec = pl.BlockSpec((step_tok, N_MOD * D_MODEL), lambda j, *_: (j, 0))
        out_spec = pl.BlockSpec((step_tok, 1, D_MODEL), lambda j, *_: (j, 0, 0))
        out_shape = jax.ShapeDtypeStruct((x1.shape[0], 1, D_MODEL), F32)
    else:
        seq_steps = n_steps // mod.shape[0]
        mod_spec = pl.BlockSpec(mod.shape, lambda j, *_: (0, 0))
        out_spec = pl.BlockSpec((step_tok, D_MODEL), lambda j, *_: (j, 0))
        out_shape = jax.ShapeDtypeStruct(x1.shape, F32)
    grid_spec = pltpu.PrefetchScalarGridSpec(
        num_scalar_prefetch=3,
        grid=(n_steps,),
        in_specs=[
            pl.BlockSpec(memory_space=pl.ANY),
            pl.BlockSpec((step_tok, D_MODEL), lambda j, *_: (j, 0)),
            pl.BlockSpec((subs, SUBLANES, n_tok), lambda j, *_: (j, 0, 0)),
            mod_spec,
            pl.BlockSpec((1, D_MODEL), lambda j, *_: (0, 0)),
            pl.BlockSpec((1, D_MODEL), lambda j, *_: (0, 0)),
        ],
        out_specs=out_spec,
        scratch_shapes=[pltpu.VMEM((2 * subs, n_tok * TOP_K, ROW_SLABS, LANES), BF16),
                        pltpu.SemaphoreType.DMA((2 * subs,))],
    )
    return pl.pallas_call(
        functools.partial(_combine_kernel, n_tok, tile0, n_steps, subs, seq_steps, alpha_res),
        grid_spec=grid_spec,
        out_shape=out_shape,
        compiler_params=pltpu.CompilerParams(
            dimension_semantics=("arbitrary",), vmem_limit_bytes=VMEM_LIMIT),
        name="combine",
    )(sorted_row, off, cnt, ys, x1, meta, mod, ln2_g, ln2_b)


def _routing_tables(cnt_all):
    total = jnp.sum(cnt_all, axis=0)
    n_tile_e = (total + ROW_TILE - 1) // ROW_TILE
    tile_end = jnp.cumsum(n_tile_e)
    tile_start = tile_end - n_tile_e
    row_start = tile_start * ROW_TILE
    cum = jnp.cumsum(cnt_all, axis=0) - cnt_all
    off = jnp.cumsum(cnt_all, axis=1) - cnt_all
    sorted_row = row_start[None, :] + cum
    n_used = tile_end[-1]
    pad_row = row_start + total
    pad_n = n_tile_e * ROW_TILE - total
    i32 = lambda a: a.astype(jnp.int32)
    return ((i32(sorted_row).reshape(-1), i32(off).reshape(-1), i32(cnt_all).reshape(-1)),
            i32(pad_row), i32(pad_n), i32(tile_start), i32(n_tile_e), i32(n_used).reshape(1))


def kernel(x_prompt, x_sample, state_conv, c_prompt, c_sample, ada_w, ada_b, w_in, sgu_ln_g, sgu_ln_b,
           sgu_w, sgu_b, conv_w, conv_b, w_out, ln1_g, ln1_b, router_w, router_b, w_gu, b_gu,
           w_down, b_down, ln2_g, ln2_b):
    depth = ada_w.shape[0]
    assert depth == 1
    bsz, seq, _ = x_prompt.shape
    n_dec = x_sample.shape[0]
    assert x_sample.shape[1] == 1 and seq % TOK_TILE == 0
    alpha_res = (2.0 * depth) ** 0.25
    l = 0

    mod_s, mod_p = _ada(c_sample, c_prompt, ada_w[l], ada_b[l])

    router_wt =jnp.transpose(router_w[l])
    router_wt_hi = router_wt.astype(BF16)
    router_wt_lo = (router_wt - router_wt_hi.astype(F32)).astype(BF16)
    router_wt2 = jnp.concatenate([router_wt_hi, router_wt_lo], axis=0)
    router_b_col = router_b[l].reshape(N_EXPERTS, 1)
    row = lambda a: a.reshape(1, -1)

    x1_p, h2_p, meta_p, cnt_p, convst_p, w_in_bf, w_out_bf = _mix_prompt(
        x_prompt, mod_p, w_in[l], sgu_ln_g[l], sgu_ln_b[l], sgu_w[l], jnp.transpose(sgu_b[l]),
        conv_w[l], row(conv_b[l]), w_out[l], row(ln1_g[l]), row(ln1_b[l]), router_wt2,
        router_b_col, alpha_res)
    x1_s, h2_s, meta_s, cnt_s, q_s, vn_s = _mix_sample(
        x_sample.reshape(n_dec, D_MODEL), mod_s, state_conv[l, :, 0, :], state_conv[l, :, 1, :],
        w_in_bf, sgu_ln_g[l], sgu_ln_b[l], row(jnp.repeat(sgu_w[l, :, 0, 0], A_HEAD_DIM)),
        row(jnp.repeat(sgu_b[l, :, 0], A_HEAD_DIM)), conv_w[l], row(conv_b[l]), w_out_bf,
        row(ln1_g[l]), row(ln1_b[l]), router_wt2, router_b_col, alpha_res)

    n_ptiles = bsz * seq // TOK_TILE
    cnt_all = jnp.concatenate([cnt_p[:, :, 0], cnt_s[None, :, 0]], axis=0)
    n_assign = (bsz * seq + n_dec) * TOP_K
    n_row_tiles = -(-n_assign // ROW_TILE) + N_EXPERTS
    tables, pad_row, pad_n, tile_start, n_tile_e, n_used = _routing_tables(cnt_all)

    n_sorted_rows = n_row_tiles * ROW_TILE
    meta_s = meta_s[None]
    xs = _sort(tables, pad_row, pad_n, h2_p, meta_p, None, n_sorted_rows, TOK_TILE, 0, SORT_SUB)
    xs = _sort(tables, pad_row, pad_n, h2_s, meta_s, xs, n_sorted_rows, n_dec, n_ptiles, 1)
    ys = _experts(tile_start, n_tile_e, pad_n, n_used, xs, w_gu[l], b_gu[l], w_down[l], b_down[l])
    y_p = _combine(tables, ys, x1_p, meta_p, mod_p, row(ln2_g[l]), row(ln2_b[l]), TOK_TILE, 0,
                   COMBINE_SUB, alpha_res)
    y_s = _combine(tables, ys, x1_s, meta_s, mod_s, row(ln2_g[l]), row(ln2_b[l]), n_dec, n_ptiles,
                   1, alpha_res)

    conv_state_sample = jnp.stack([state_conv[l, :, 1, :], q_s], axis=1)[None]
    return (y_p.reshape(bsz, seq, D_MODEL),
            y_s,
            convst_p[None],
            conv_state_sample,
            vn_s.reshape(1, n_dec, 1, A_HEADS, A_HEAD_DIM))
```

```python
import functools

import jax
import jax.numpy as jnp
from jax import lax
from jax.experimental import pallas as pl
from jax.experimental.pallas import tpu as pltpu

F32 = jnp.float32
BF16 = jnp.bfloat16

D_MODEL = 1024
A_WIDTH = 512
B_WIDTH = 512
A_HEADS = 4
A_HEAD_DIM = 128
CHUNK = 128
PROJ_COLS = 2 * A_WIDTH + 3 * B_WIDTH
N_EXPERTS = 32
TOP_K = 4
D_EXPERT = 1024
SWIGLU_LIMIT = 7.0
SWIGLU_ALPHA = 1.702
LN_EPS = 1e-5
N_MOD = 6

LANES = 128
SUBLANES = 8
ROW_SLABS = D_MODEL // LANES
TOK_TILE = 256
MIX_SUB = 2
SORT_SUB = 4
COMBINE_SUB = 2
ROW_TILE = 256
TILE_GROUP = 4
TILE_AHEAD = TILE_GROUP
TILE_SLOTS = 2 * TILE_GROUP
VMEM_LIMIT = 56 * 1024 * 1024


def _layer_norm(x, g, b):
    mu = jnp.mean(x, axis=-1, keepdims=True)
    xc = x - mu
    var = jnp.mean(xc * xc, axis=-1, keepdims=True)
    return xc * lax.rsqrt(var + LN_EPS) * g + b


def _dot(a, b):
    return jnp.dot(a, b, preferred_element_type=F32)


def _dot_nt(a, b):
    return lax.dot_general(a, b, (((1,), (1,)), ((), ())), preferred_element_type=F32)


def _store_rows(ref, slot, rows):
    n = rows.shape[0]
    ref[slot] = rows.astype(BF16).reshape(n, ROW_SLABS, LANES)


def _load_rows(ref, slot):
    return ref[slot].reshape(ref.shape[1], D_MODEL)


def _split_bf16(a):
    hi = a.astype(BF16)
    lo = (a - hi.astype(F32)).astype(BF16)
    return hi, lo


ADA_COLS = D_MODEL // 2
ADA_BUFS = 4


def _ada_kernel(c_a_ref, c_b_ref, w_hbm, b_ref, o_a_ref, o_b_ref, w_buf, sem):
    n_a = c_a_ref.shape[0]
    n_chunks = w_hbm.shape[1] // ADA_COLS

    def copy(k):
        return pltpu.make_async_copy(w_hbm.at[:, pl.ds(k * ADA_COLS, ADA_COLS)],
                                     w_buf.at[k % ADA_BUFS], sem.at[k % ADA_BUFS])

    for k in range(ADA_BUFS):
        copy(k).start()
    c = jnp.concatenate([c_a_ref[...], c_b_ref[...]], axis=0)
    s_hi, s_lo = _split_bf16(c * jax.nn.sigmoid(c))
    for k in range(n_chunks):
        cols = slice(k * ADA_COLS, (k + 1) * ADA_COLS)
        copy(k).wait()
        w_hi, w_lo = _split_bf16(w_buf[k % ADA_BUFS])
        m = _dot(s_hi, w_hi) + _dot(s_hi, w_lo) + _dot(s_lo, w_hi) + b_ref[:, cols]
        o_a_ref[:, cols] = m[:n_a]
        o_b_ref[:, cols] = m[n_a:]
        if k + ADA_BUFS < n_chunks:
            copy(k + ADA_BUFS).start()


def _ada(c_a, c_b, ada_w, ada_b):
    vmem = pl.BlockSpec(memory_space=pltpu.VMEM)
    return pl.pallas_call(
        _ada_kernel,
        in_specs=[vmem, vmem, pl.BlockSpec(memory_space=pl.ANY), vmem],
        out_specs=[vmem, vmem],
        out_shape=[jax.ShapeDtypeStruct((c_a.shape[0], N_MOD * D_MODEL), F32),
                   jax.ShapeDtypeStruct((c_b.shape[0], N_MOD * D_MODEL), F32)],
        scratch_shapes=[pltpu.VMEM((ADA_BUFS, D_MODEL, ADA_COLS), F32),
                        pltpu.SemaphoreType.DMA((ADA_BUFS,))],
        name="ada",
    )(c_a, c_b, ada_w, ada_b.reshape(1, -1))


def _route(h2_tiles, router_wt2, router_b):
    n_tiles = len(h2_tiles)
    n_tile_tok = h2_tiles[0].shape[0]
    h2 = h2_tiles[0] if n_tiles == 1 else jnp.concatenate(h2_tiles, axis=0)
    n = h2.shape[0]
    lanes = [slice(i * n_tile_tok, (i + 1) * n_tile_tok) for i in range(n_tiles)]
    h_hi = h2.astype(BF16)
    h_lo = (h2 - h_hi.astype(F32)).astype(BF16)
    l1 = _dot_nt(router_wt2, h_hi)
    l2 = _dot_nt(router_wt2[:N_EXPERTS], h_lo)
    logits = l1[:N_EXPERTS] + l1[N_EXPERTS:] + l2 + router_b

    sub = lax.broadcasted_iota(jnp.int32, (N_EXPERTS, n), 0)
    work = logits
    vals, hots = [], []
    for _ in range(TOP_K):
        m = jnp.max(work, axis=0, keepdims=True)
        idx = jnp.min(jnp.where(work == m, sub, N_EXPERTS), axis=0, keepdims=True)
        hot = sub == idx
        work = jnp.where(hot, -jnp.inf, work)
        vals.append(m)
        hots.append(hot)
    exps = [jnp.exp(v - vals[0]) for v in vals]
    denom = exps[0] + exps[1] + exps[2] + exps[3]
    gates = [e / denom for e in exps]

    onehot = jnp.zeros((N_EXPERTS, n), F32)
    for hot in hots:
        onehot = onehot + hot.astype(F32)
    onehot_bf = onehot.astype(BF16)
    t_r = lax.broadcasted_iota(jnp.int32, (n_tile_tok, n_tile_tok), 0)
    t_c = lax.broadcasted_iota(jnp.int32, (n_tile_tok, n_tile_tok), 1)
    stacked = onehot_bf if n_tiles == 1 else jnp.concatenate([onehot_bf[:, ln] for ln in lanes], axis=0)
    rank_st = _dot(stacked, (t_r < t_c).astype(BF16))
    e_r = lax.broadcasted_iota(jnp.int32, (N_EXPERTS, N_EXPERTS), 0)
    e_c = lax.broadcasted_iota(jnp.int32, (N_EXPERTS, N_EXPERTS), 1)
    below = _dot((e_c < e_r).astype(BF16), onehot_bf)
    cnts, bases = [], []
    for i, ln in enumerate(lanes):
        cnts.append(jnp.sum(onehot[:, ln], axis=1, keepdims=True))
        off = jnp.sum(below[:, ln], axis=1, keepdims=True)
        bases.append(rank_st[i * N_EXPERTS:(i + 1) * N_EXPERTS] + off)
    base = bases[0] if n_tiles == 1 else jnp.concatenate(bases, axis=1)

    row8 = lax.broadcasted_iota(jnp.int32, (SUBLANES, n), 0)
    meta = jnp.zeros((SUBLANES, n), F32)
    for k in range(TOP_K):
        pos_k = jnp.sum(jnp.where(hots[k], base, 0.0), axis=0, keepdims=True)
        meta = jnp.where(row8 == k, pos_k, meta)
        meta = jnp.where(row8 == TOP_K + k, gates[k], meta)
    return [(meta[:, ln], jnp.broadcast_to(c, (N_EXPERTS, LANES))) for ln, c in zip(lanes, cnts)]


def _mix_prompt_kernel(alpha_res, x_ref, mod_ref, w_in_ref, sgu_g_ref, sgu_bln_ref, sgu_w_ref,
                       sgu_bias_ref, conv_w_ref, conv_b_ref, w_out_ref, ln1_g_ref, ln1_b_ref,
                       router_w_ref, router_b_ref,
                       x1_ref, h2_ref, meta_ref, cnt_ref, convst_ref, w_in_bf_ref, w_out_bf_ref,
                       carry_ref):
    t = pl.program_id(1)

    @pl.when(jnp.logical_and(pl.program_id(0) == 0, t == 0))
    def _():
        w_in_bf_ref[...] = w_in_ref[...].astype(BF16)
        w_out_bf_ref[...] = w_out_ref[...].astype(BF16)

    @pl.when(t == 0)
    def _():
        carry_ref[...] = jnp.zeros_like(carry_ref)

    seq_row = pl.ds(pl.program_id(0), 1)
    sh1, sc1, g1, sh2, sc2 = [mod_ref[seq_row, k * D_MODEL:(k + 1) * D_MODEL] for k in range(5)]
    r_i = lax.broadcasted_iota(jnp.int32, (CHUNK, CHUNK), 0)
    c_i = lax.broadcasted_iota(jnp.int32, (CHUNK, CHUNK), 1)
    tril = c_i <= r_i
    tm = TOK_TILE
    prev2 = carry_ref[0:1, :]
    prev1 = carry_ref[1:2, :]

    subs = range(MIX_SUB)
    tile_rows = [slice(sub * tm, (sub + 1) * tm) for sub in subs]
    xs_in = [x_ref[rows, :] for rows in tile_rows]
    zs = [_dot((x * (1.0 + sc1) + sh1).astype(BF16), w_in_bf_ref[...]) for x in xs_in]

    mix_ins = []
    for sub in subs:
        z = zs[sub]
        u = z[:, 0:A_WIDTH]
        v = z[:, A_WIDTH:2 * A_WIDTH]
        gate_b = z[:, 2 * A_WIDTH:2 * A_WIDTH + B_WIDTH]
        gate_c = z[:, 2 * A_WIDTH + B_WIDTH:2 * A_WIDTH + 2 * B_WIDTH]
        hb = z[:, 2 * A_WIDTH + 2 * B_WIDTH:]

        a_parts = []
        for hd in range(A_HEADS):
            sl = slice(hd * A_HEAD_DIM, (hd + 1) * A_HEAD_DIM)
            vn = _layer_norm(v[:, sl], sgu_g_ref[hd:hd + 1, :], sgu_bln_ref[hd:hd + 1, :]).astype(BF16)
            wm = jnp.where(tril, sgu_w_ref[hd], 0.0).astype(BF16)
            bias = sgu_bias_ref[:, hd:hd + 1]
            s_parts = []
            for c in range(tm // CHUNK):
                s_parts.append(_dot(wm, vn[c * CHUNK:(c + 1) * CHUNK, :]) + bias)
            a_parts.append(u[:, sl] * jnp.concatenate(s_parts, axis=0))

        q = gate_c * hb
        row = lax.broadcasted_iota(jnp.int32, q.shape, 0)
        q_m1 = jnp.where(row == 0, prev1, pltpu.roll(q, 1, 0))
        q_m2 = jnp.where(row == 0, prev2, jnp.where(row == 1, prev1, pltpu.roll(q, 2, 0)))
        conv = (conv_b_ref[...] + q_m2 * conv_w_ref[0:1, :] + q_m1 * conv_w_ref[1:2, :]
                + q * conv_w_ref[2:3, :])
        b_out = gate_b * conv
        prev2 = q[tm - 2:tm - 1, :]
        prev1 = q[tm - 1:tm, :]

        mix_ins.append(jnp.concatenate(a_parts + [b_out], axis=-1).astype(BF16))

    mixes = [_dot(mix_in, w_out_bf_ref[...]) for mix_in in mix_ins]
    h2s = []
    for sub in subs:
        x1 = _layer_norm(alpha_res * xs_in[sub] + g1 * mixes[sub], ln1_g_ref[...], ln1_b_ref[...])
        x1_ref[tile_rows[sub], :] = x1
        h2 = x1 * (1.0 + sc2) + sh2
        h2_ref[tile_rows[sub], :] = h2.astype(BF16)
        h2s.append(h2)
    routed = _route(h2s, router_w_ref[...], router_b_ref[...])
    for sub in subs:
        meta, cnt = routed[sub]
        meta_ref[sub] = meta
        cnt_ref[sub] = cnt.astype(jnp.int32)

    last2 = jnp.concatenate([prev2, prev1], axis=0)
    carry_ref[0:2, :] = last2
    convst_ref[...] = last2


def _mix_prompt(x, mod, w_in, sgu_g, sgu_bln, sgu_w, sgu_bias_t, conv_w, conv_b, w_out,
                ln1_g, ln1_b, router_wt2, router_b_col, alpha_res):
    bsz, seq, _ = x.shape
    step_tok = MIX_SUB * TOK_TILE
    steps = seq // step_tok
    n_tok = bsz * seq
    tiles = seq // TOK_TILE
    const2 = lambda b, t: (0, 0)
    tok = lambda b, t: (b * steps + t, 0)
    tile3 = lambda b, t: (b * steps + t, 0, 0)
    return pl.pallas_call(
        functools.partial(_mix_prompt_kernel, alpha_res),
        grid=(bsz, steps),
        in_specs=[
            pl.BlockSpec((None, step_tok, D_MODEL), lambda b, t: (b, t, 0)),
            pl.BlockSpec((bsz, N_MOD * D_MODEL), const2),
            pl.BlockSpec((D_MODEL, PROJ_COLS), const2, pipeline_mode=pl.Buffered(1)),
            pl.BlockSpec((A_HEADS, A_HEAD_DIM), const2),
            pl.BlockSpec((A_HEADS, A_HEAD_DIM), const2),
            pl.BlockSpec((A_HEADS, CHUNK, CHUNK), lambda b, t: (0, 0, 0)),
            pl.BlockSpec((CHUNK, A_HEADS), const2),
            pl.BlockSpec((3, B_WIDTH), const2),
            pl.BlockSpec((1, B_WIDTH), const2),
            pl.BlockSpec((D_MODEL, D_MODEL), const2, pipeline_mode=pl.Buffered(1)),
            pl.BlockSpec((1, D_MODEL), const2),
            pl.BlockSpec((1, D_MODEL), const2),
            pl.BlockSpec((2 * N_EXPERTS, D_MODEL), const2),
            pl.BlockSpec((N_EXPERTS, 1), const2),
        ],
        out_specs=[
            pl.BlockSpec((step_tok, D_MODEL), tok),
            pl.BlockSpec((step_tok, D_MODEL), tok),
            pl.BlockSpec((MIX_SUB, SUBLANES, TOK_TILE), tile3),
            pl.BlockSpec((MIX_SUB, N_EXPERTS, LANES), tile3),
            pl.BlockSpec((None, 2, B_WIDTH), lambda b, t: (b, 0, 0)),
            pl.BlockSpec((D_MODEL, PROJ_COLS), const2),
            pl.BlockSpec((D_MODEL, D_MODEL), const2),
        ],
        out_shape=[
            jax.ShapeDtypeStruct((n_tok, D_MODEL), F32),
            jax.ShapeDtypeStruct((n_tok, D_MODEL), BF16),
            jax.ShapeDtypeStruct((bsz * tiles, SUBLANES, TOK_TILE), F32),
            jax.ShapeDtypeStruct((bsz * tiles, N_EXPERTS, LANES), jnp.int32),
            jax.ShapeDtypeStruct((bsz, 2, B_WIDTH), F32),
            jax.ShapeDtypeStruct((D_MODEL, PROJ_COLS), BF16),
            jax.ShapeDtypeStruct((D_MODEL, D_MODEL), BF16),
        ],
        scratch_shapes=[pltpu.VMEM((SUBLANES, B_WIDTH), F32)],
        compiler_params=pltpu.CompilerParams(
            dimension_semantics=("arbitrary", "arbitrary"), vmem_limit_bytes=VMEM_LIMIT),
        name="mix_prompt",
    )(x, mod, w_in, sgu_g, sgu_bln, sgu_w, sgu_bias_t, conv_w, conv_b, w_out,
      ln1_g, ln1_b, router_wt2, router_b_col)


def _mix_sample_kernel(alpha_res, x_ref, mod_ref, prev0_ref, prev1_ref, w_in_ref, sgu_g_ref,
                       sgu_bln_ref, sgu_w00_ref, sgu_b0_ref, conv_w_ref, conv_b_ref, w_out_ref,
                       ln1_g_ref, ln1_b_ref, router_w_ref, router_b_ref,
                       x1_ref, h2_ref, meta_ref, cnt_ref, q_ref, vn_ref):
    x = x_ref[...]
    sh1 = mod_ref[:, 0:D_MODEL]
    sc1 = mod_ref[:, D_MODEL:2 * D_MODEL]
    g1 = mod_ref[:, 2 * D_MODEL:3 * D_MODEL]
    sh2 = mod_ref[:, 3 * D_MODEL:4 * D_MODEL]
    sc2 = mod_ref[:, 4 * D_MODEL:5 * D_MODEL]
    h = (x * (1.0 + sc1) + sh1).astype(BF16)
    z = _dot(h, w_in_ref[...])
    u = z[:, 0:A_WIDTH]
    v = z[:, A_WIDTH:2 * A_WIDTH]
    gate_b = z[:, 2 * A_WIDTH:2 * A_WIDTH + B_WIDTH]
    gate_c = z[:, 2 * A_WIDTH + B_WIDTH:2 * A_WIDTH + 2 * B_WIDTH]
    hb = z[:, 2 * A_WIDTH + 2 * B_WIDTH:]

    vn_parts = []
    for hd in range(A_HEADS):
        sl = slice(hd * A_HEAD_DIM, (hd + 1) * A_HEAD_DIM)
        vn_parts.append(_layer_norm(v[:, sl], sgu_g_ref[hd:hd + 1, :], sgu_bln_ref[hd:hd + 1, :]))
    vn = jnp.concatenate(vn_parts, axis=-1)
    vn_ref[...] = vn
    a_out = u * (vn * sgu_w00_ref[...] + sgu_b0_ref[...])

    q = gate_c * hb
    q_ref[...] = q
    conv = (conv_b_ref[...] + prev0_ref[...] * conv_w_ref[0:1, :] + prev1_ref[...] * conv_w_ref[1:2, :]
            + q * conv_w_ref[2:3, :])
    b_out = gate_b * conv

    mix_in = jnp.concatenate([a_out, b_out], axis=-1).astype(BF16)
    mix = _dot(mix_in, w_out_ref[...])
    x1 = _layer_norm(alpha_res * x + g1 * mix, ln1_g_ref[...], ln1_b_ref[...])
    x1_ref[...] = x1
    h2 = x1 * (1.0 + sc2) + sh2
    h2_ref[...] = h2.astype(BF16)
    (meta, cnt), = _route([h2], router_w_ref[...], router_b_ref[...])
    meta_ref[...] = meta
    cnt_ref[...] = cnt.astype(jnp.int32)


def _mix_sample(x, mod, prev0, prev1, w_in_bf, sgu_g, sgu_bln, sgu_w00, sgu_b0, conv_w, conv_b,
                w_out_bf, ln1_g, ln1_b, router_wt2, router_b_col, alpha_res):
    n = x.shape[0]
    return pl.pallas_call(
        functools.partial(_mix_sample_kernel, alpha_res),
        out_shape=[
            jax.ShapeDtypeStruct((n, D_MODEL), F32),
            jax.ShapeDtypeStruct((n, D_MODEL), BF16),
            jax.ShapeDtypeStruct((SUBLANES, n), F32),
            jax.ShapeDtypeStruct((N_EXPERTS, LANES), jnp.int32),
            jax.ShapeDtypeStruct((n, B_WIDTH), F32),
            jax.ShapeDtypeStruct((n, A_WIDTH), F32),
        ],
        compiler_params=pltpu.CompilerParams(vmem_limit_bytes=VMEM_LIMIT),
        name="mix_sample",
    )(x, mod, prev0, prev1, w_in_bf, sgu_g, sgu_bln, sgu_w00, sgu_b0, conv_w, conv_b, w_out_bf,
      ln1_g, ln1_b, router_wt2, router_b_col)


def _strip_copy(src_ref, src_row, dst_ref, dst_row, n_rows, sem, priority=0):
    @pl.when(n_rows > 0)
    def _():
        pltpu.make_async_copy(src_ref.at[pl.ds(src_row, n_rows)],
                              dst_ref.at[pl.ds(dst_row, n_rows)], sem).start(priority=priority)


def _for_each_expert(strip):
    for e in range(N_EXPERTS):
        strip(e, 0)


def _wait_rows(hbm_ref, vmem_ref, n_rows, sem):
    pltpu.make_async_copy(hbm_ref.at[pl.ds(0, n_rows)], vmem_ref.at[pl.ds(0, n_rows)], sem).wait()


def _sort_kernel(n_tok, tile0, n_steps, subs, first, *refs):
    if first:
        (sorted_row_ref, off_ref, cnt_ref, pad_row_ref, pad_n_ref, h2_ref, meta_ref,
         xs_ref, stage_ref, zero_ref, sem, pad_sem) = refs
    else:
        (sorted_row_ref, off_ref, cnt_ref, pad_row_ref, pad_n_ref, h2_ref, meta_ref, _,
         xs_ref, stage_ref, sem) = refs
    j = pl.program_id(0)
    n_sorted = n_tok * TOP_K
    parity = j % 2

    if first:
        @pl.when(j == 0)
        def _():
            zero_ref[...] = jnp.zeros_like(zero_ref)

            def start(e, carry):
                _strip_copy(zero_ref, 0, xs_ref, pad_row_ref[e], pad_n_ref[e], pad_sem)
                return carry
            lax.fori_loop(0, N_EXPERTS, start, 0)

            def wait(e, carry):
                @pl.when(pad_n_ref[e] > 0)
                def _():
                    _wait_rows(xs_ref, zero_ref, pad_n_ref[e], pad_sem)
                return carry
            lax.fori_loop(0, N_EXPERTS, wait, 0)

    sub_r = lax.broadcasted_iota(jnp.int16, (n_sorted, n_tok), 0)
    for sub in range(subs):
        slot = parity * subs + sub
        pos = meta_ref[sub][0:TOP_K, :].astype(jnp.int16)
        perm = jnp.zeros((n_sorted, n_tok), BF16)
        for k in range(TOP_K):
            perm = jnp.where(sub_r == pos[k:k + 1, :], jnp.ones((), BF16), perm)
        rows = _dot(perm, h2_ref[sub * n_tok:(sub + 1) * n_tok, :])
        _store_rows(stage_ref, slot, rows)

        base = (tile0 + j * subs + sub) * N_EXPERTS

        def strip(e, priority, slot=slot, base=base):
            _strip_copy(stage_ref.at[slot], off_ref[base + e], xs_ref, sorted_row_ref[base + e],
                        cnt_ref[base + e], sem.at[slot], priority)
        _for_each_expert(strip)

    @pl.when(j > 0)
    def _():
        for sub in range(subs):
            _wait_rows(xs_ref, stage_ref.at[0], n_sorted, sem.at[(1 - parity) * subs + sub])

    @pl.when(j == n_steps - 1)
    def _():
        for sub in range(subs):
            _wait_rows(xs_ref, stage_ref.at[0], n_sorted, sem.at[parity * subs + sub])


def _sort(tables, pad_row, pad_n, h2, meta, xs, n_sorted_rows, n_tok, tile0, subs):
    sorted_row, off, cnt = tables
    first = xs is None
    n_steps = h2.shape[0] // (n_tok * subs)
    in_specs = [pl.BlockSpec((subs * n_tok, D_MODEL), lambda j, *_: (j, 0)),
                pl.BlockSpec((subs, SUBLANES, n_tok), lambda j, *_: (j, 0, 0))]
    operands = [sorted_row, off, cnt, pad_row, pad_n, h2, meta]
    scratch = [pltpu.VMEM((2 * subs, n_tok * TOP_K, ROW_SLABS, LANES), BF16)]
    if first:
        scratch.append(pltpu.VMEM((ROW_TILE, ROW_SLABS, LANES), BF16))
        aliases = {}
    else:
        in_specs.append(pl.BlockSpec(memory_space=pl.ANY))
        operands.append(xs)
        aliases = {len(operands) - 1: 0}
    scratch.append(pltpu.SemaphoreType.DMA((2 * subs,)))
    if first:
        scratch.append(pltpu.SemaphoreType.DMA(()))
    grid_spec = pltpu.PrefetchScalarGridSpec(
        num_scalar_prefetch=5,
        grid=(n_steps,),
        in_specs=in_specs,
        out_specs=pl.BlockSpec(memory_space=pl.ANY),
        scratch_shapes=scratch,
    )
    return pl.pallas_call(
        functools.partial(_sort_kernel, n_tok, tile0, n_steps, subs, first),
        grid_spec=grid_spec,
        out_shape=jax.ShapeDtypeStruct((n_sorted_rows, ROW_SLABS, LANES), BF16),
        input_output_aliases=aliases,
        compiler_params=pltpu.CompilerParams(
            dimension_semantics=("arbitrary",), vmem_limit_bytes=VMEM_LIMIT),
        name="sort_first" if first else "sort_more",
    )(*operands)


def _experts_kernel(tile_start_ref, n_tile_ref, short_last_ref, n_used_ref, xs_ref, w_gu_ref, b_gu_ref,
                    w_down_ref, b_down_ref, ys_ref, w_gu_bf_ref, w_down_bf_ref, x_buf, y_buf, x_sem,
                    y_sem):
    e = pl.program_id(0)
    n_used = n_used_ref[0]
    first_tile = tile_start_ref[e]
    n_tile = n_tile_ref[e]

    def x_copy(g):
        slot = g % TILE_SLOTS
        return pltpu.make_async_copy(xs_ref.at[pl.ds(g * ROW_TILE, ROW_TILE)], x_buf.at[slot],
                                     x_sem.at[slot])

    def y_copy(g):
        slot = g % TILE_SLOTS
        return pltpu.make_async_copy(y_buf.at[slot], ys_ref.at[pl.ds(g * ROW_TILE, ROW_TILE)],
                                     y_sem.at[slot])

    def request(g):
        @pl.when(g < n_used)
        def _():
            x_copy(g).start(priority=1)

    @pl.when(e == 0)
    def _():
        for g in range(TILE_AHEAD):
            request(g)

    @pl.when(n_tile > 0)
    def _():
        w_gu_bf_ref[...] = w_gu_ref[...].astype(BF16)
        w_down_bf_ref[...] = w_down_ref[...].astype(BF16)

    def begin(g):
        x_copy(g).wait()
        request(g + TILE_AHEAD)

        @pl.when(g >= TILE_SLOTS)
        def _():
            y_copy(g - TILE_SLOTS).wait()

    def compute(g, rows=ROW_TILE):
        slot = g % TILE_SLOTS
        x = x_buf[slot, 0:rows].reshape(rows, D_MODEL)
        gu = _dot(x, w_gu_bf_ref[...]) + b_gu_ref[...]
        gate = jnp.minimum(gu[:, :D_EXPERT], SWIGLU_LIMIT)
        up = jnp.clip(gu[:, D_EXPERT:], -SWIGLU_LIMIT, SWIGLU_LIMIT)
        act = (up + 1.0) * gate * jax.nn.sigmoid(SWIGLU_ALPHA * gate)
        y = _dot(act.astype(BF16), w_down_bf_ref[...]) + b_down_ref[...]
        y_buf[slot, 0:rows] = y.astype(BF16).reshape(rows, ROW_SLABS, LANES)
        if rows < ROW_TILE:
            y_buf[slot, rows:ROW_TILE] = jnp.zeros((ROW_TILE - rows, ROW_SLABS, LANES), BF16)
        y_copy(g).start(priority=1)

    def group(g, size):
        for k in range(size):
            begin(g + k)
        for k in range(size):
            compute(g + k)

    def group_body(p, carry):
        group(first_tile + TILE_GROUP * p, TILE_GROUP)
        return carry

    short_last = short_last_ref[e]
    n_full = n_tile - short_last
    n_group = n_full // TILE_GROUP
    lax.fori_loop(0, n_group, group_body, 0)
    rest = n_full - n_group * TILE_GROUP
    g_rest = first_tile + n_group * TILE_GROUP
    size = TILE_GROUP // 2
    while size >= 1:
        @pl.when((rest & size) != 0)
        def _(g_rest=g_rest, size=size):
            group(g_rest, size)
        g_rest = g_rest + (rest & size)
        size //= 2

    @pl.when(short_last == 1)
    def _():
        g = first_tile + n_full
        begin(g)
        compute(g, ROW_TILE // 2)

    @pl.when(e == N_EXPERTS - 1)
    def _():
        for back in range(TILE_SLOTS, 0, -1):
            @pl.when(n_used >= back)
            def _(back=back):
                y_copy(n_used - back).wait()


def _experts(tile_start, n_tile_e, pad_n, n_used, xs, w_gu, b_gu, w_down, b_down):
    short_last = (pad_n >= ROW_TILE // 2).astype(jnp.int32)
    w_blk = lambda e, *_: (e, 0, 0)
    grid_spec = pltpu.PrefetchScalarGridSpec(
        num_scalar_prefetch=4,
        grid=(N_EXPERTS,),
        in_specs=[
            pl.BlockSpec(memory_space=pl.ANY),
            pl.BlockSpec((None, D_MODEL, 2 * D_EXPERT), w_blk),
            pl.BlockSpec((None, 1, 2 * D_EXPERT), w_blk),
            pl.BlockSpec((None, D_EXPERT, D_MODEL), w_blk),
            pl.BlockSpec((None, 1, D_MODEL), w_blk),
        ],
        out_specs=pl.BlockSpec(memory_space=pl.ANY),
        scratch_shapes=[pltpu.VMEM((D_MODEL, 2 * D_EXPERT), BF16),
                        pltpu.VMEM((D_EXPERT, D_MODEL), BF16),
                        pltpu.VMEM((TILE_SLOTS, ROW_TILE, ROW_SLABS, LANES), BF16),
                        pltpu.VMEM((TILE_SLOTS, ROW_TILE, ROW_SLABS, LANES), BF16),
                        pltpu.SemaphoreType.DMA((TILE_SLOTS,)),
                        pltpu.SemaphoreType.DMA((TILE_SLOTS,))],
    )
    return pl.pallas_call(
        _experts_kernel,
        grid_spec=grid_spec,
        out_shape=jax.ShapeDtypeStruct(xs.shape, BF16),
        compiler_params=pltpu.CompilerParams(
            dimension_semantics=("arbitrary",), vmem_limit_bytes=VMEM_LIMIT),
        name="experts",
    )(tile_start, n_tile_e, short_last, n_used, xs, w_gu, b_gu.reshape(N_EXPERTS, 1, -1), w_down,
      b_down.reshape(N_EXPERTS, 1, -1))


def _combine_kernel(n_tok, tile0, n_steps, subs, seq_steps, alpha_res, sorted_row_ref, off_ref, cnt_ref,
                    ys_ref, x1_ref, meta_ref, mod_ref, ln2_g_ref, ln2_b_ref, out_ref,
                    stage_ref, sem):
    j = pl.program_id(0)
    n_sorted = n_tok * TOP_K
    parity = j % 2

    def start_step(step, step_parity, live):
        for sub in range(subs):
            base = (tile0 + step * subs + sub) * N_EXPERTS
            to_slot = step_parity * subs + sub

            def strip(e, priority, base=base, to_slot=to_slot):
                _strip_copy(ys_ref, sorted_row_ref[base + e], stage_ref.at[to_slot],
                            off_ref[base + e], jnp.where(live, cnt_ref[base + e], 0),
                            sem.at[to_slot], priority)
            _for_each_expert(strip)

    @pl.when(j == 0)
    def _():
        start_step(j, parity, True)

    start_step(jnp.minimum(j + 1, n_steps - 1), 1 - parity, j + 1 < n_steps)

    sub_r = lax.broadcasted_iota(jnp.int16, (n_sorted, n_tok), 0)
    for sub in range(subs):
        slot = parity * subs + sub
        rows = slice(sub * n_tok, (sub + 1) * n_tok)
        meta = meta_ref[sub]
        pos = meta[0:TOP_K, :].astype(jnp.int16)
        gate = meta[TOP_K:2 * TOP_K, :].astype(BF16)
        comb_t = jnp.zeros((n_sorted, n_tok), BF16)
        for k in range(TOP_K):
            comb_t = jnp.where(sub_r == pos[k:k + 1, :], gate[k:k + 1, :], comb_t)

        _wait_rows(ys_ref, stage_ref.at[0], n_sorted, sem.at[slot])
        ys = _load_rows(stage_ref, slot)
        ffn = lax.dot_general(comb_t, ys, (((0,), (0,)), ((), ())), preferred_element_type=F32)
        if seq_steps is None:
            g2 = mod_ref[rows, 5 * D_MODEL:6 * D_MODEL]
        else:
            g2 = mod_ref[pl.ds(j // seq_steps, 1), 5 * D_MODEL:6 * D_MODEL]
        y = _layer_norm(alpha_res * x1_ref[rows, :] + g2 * ffn, ln2_g_ref[...], ln2_b_ref[...])
        if seq_steps is None:
            out_ref[rows, 0, :] = y
        else:
            out_ref[rows, :] = y


def _combine(tables, ys, x1, meta, mod, ln2_g, ln2_b, n_tok, tile0, subs, alpha_res):
    sorted_row, off, cnt = tables
    step_tok = subs * n_tok
    n_steps = x1.shape[0] // step_tok
    if mod.shape[0] == x1.shape[0]:
        seq_steps = None
        mod_spec = pl.BlockSpec((step_tok, N_MOD * D_MODEL), lambda j, *_: (j, 0))
        out_spec = pl.BlockSpec((step_tok, 1, D_MODEL), lambda j, *_: (j, 0, 0))
        out_shape = jax.ShapeDtypeStruct((x1.shape[0], 1, D_MODEL), F32)
    else:
        seq_steps = n_steps // mod.shape[0]
        mod_spec = pl.BlockSpec(mod.shape, lambda j, *_: (0, 0))
        out_spec = pl.BlockSpec((step_tok, D_MODEL), lambda j, *_: (j, 0))
        out_shape = jax.ShapeDtypeStruct(x1.shape, F32)
    grid_spec = pltpu.PrefetchScalarGridSpec(
        num_scalar_prefetch=3,
        grid=(n_steps,),
        in_specs=[
            pl.BlockSpec(memory_space=pl.ANY),
            pl.BlockSpec((step_tok, D_MODEL), lambda j, *_: (j, 0)),
            pl.BlockSpec((subs, SUBLANES, n_tok), lambda j, *_: (j, 0, 0)),
            mod_spec,
            pl.BlockSpec((1, D_MODEL), lambda j, *_: (0, 0)),
            pl.BlockSpec((1, D_MODEL), lambda j, *_: (0, 0)),
        ],
        out_specs=out_spec,
        scratch_shapes=[pltpu.VMEM((2 * subs, n_tok * TOP_K, ROW_SLABS, LANES), BF16),
                        pltpu.SemaphoreType.DMA((2 * subs,))],
    )
    return pl.pallas_call(
        functools.partial(_combine_kernel, n_tok, tile0, n_steps, subs, seq_steps, alpha_res),
        grid_spec=grid_spec,
        out_shape=out_shape,
        compiler_params=pltpu.CompilerParams(
            dimension_semantics=("arbitrary",), vmem_limit_bytes=VMEM_LIMIT),
        name="combine",
    )(sorted_row, off, cnt, ys, x1, meta, mod, ln2_g, ln2_b)


def _routing_tables(cnt_all):
    total = jnp.sum(cnt_all, axis=0)
    n_tile_e = (total + ROW_TILE - 1) // ROW_TILE
    tile_end = jnp.cumsum(n_tile_e)
    tile_start = tile_end - n_tile_e
    row_start = tile_start * ROW_TILE
    cum = jnp.cumsum(cnt_all, axis=0) - cnt_all
    off = jnp.cumsum(cnt_all, axis=1) - cnt_all
    sorted_row = row_start[None, :] + cum
    n_used = tile_end[-1]
    pad_row = row_start + total
    pad_n = n_tile_e * ROW_TILE - total
    i32 = lambda a: a.astype(jnp.int32)
    return ((i32(sorted_row).reshape(-1), i32(off).reshape(-1), i32(cnt_all).reshape(-1)),
            i32(pad_row), i32(pad_n), i32(tile_start), i32(n_tile_e), i32(n_used).reshape(1))


def kernel(x_prompt, x_sample, state_conv, c_prompt, c_sample, ada_w, ada_b, w_in, sgu_ln_g, sgu_ln_b,
           sgu_w, sgu_b, conv_w, conv_b, w_out, ln1_g, ln1_b, router_w, router_b, w_gu, b_gu,
           w_down, b_down, ln2_g, ln2_b):
    depth = ada_w.shape[0]
    assert depth == 1
    bsz, seq, _ = x_prompt.shape
    n_dec = x_sample.shape[0]
    assert x_sample.shape[1] == 1 and seq % TOK_TILE == 0
    alpha_res = (2.0 * depth) ** 0.25
    l = 0

    mod_s, mod_p = _ada(c_sample, c_prompt, ada_w[l], ada_b[l])

    router_wt =jnp.transpose(router_w[l])
    router_wt_hi = router_wt.astype(BF16)
    router_wt_lo = (router_wt - router_wt_hi.astype(F32)).astype(BF16)
    router_wt2 = jnp.concatenate([router_wt_hi, router_wt_lo], axis=0)
    router_b_col = router_b[l].reshape(N_EXPERTS, 1)
    row = lambda a: a.reshape(1, -1)

    x1_p, h2_p, meta_p, cnt_p, convst_p, w_in_bf, w_out_bf = _mix_prompt(
        x_prompt, mod_p, w_in[l], sgu_ln_g[l], sgu_ln_b[l], sgu_w[l], jnp.transpose(sgu_b[l]),
        conv_w[l], row(conv_b[l]), w_out[l], row(ln1_g[l]), row(ln1_b[l]), router_wt2,
        router_b_col, alpha_res)
    x1_s, h2_s, meta_s, cnt_s, q_s, vn_s = _mix_sample(
        x_sample.reshape(n_dec, D_MODEL), mod_s, state_conv[l, :, 0, :], state_conv[l, :, 1, :],
        w_in_bf, sgu_ln_g[l], sgu_ln_b[l], row(jnp.repeat(sgu_w[l, :, 0, 0], A_HEAD_DIM)),
        row(jnp.repeat(sgu_b[l, :, 0], A_HEAD_DIM)), conv_w[l], row(conv_b[l]), w_out_bf,
        row(ln1_g[l]), row(ln1_b[l]), router_wt2, router_b_col, alpha_res)

    n_ptiles = bsz * seq // TOK_TILE
    cnt_all = jnp.concatenate([cnt_p[:, :, 0], cnt_s[None, :, 0]], axis=0)
    n_assign = (bsz * seq + n_dec) * TOP_K
    n_row_tiles = -(-n_assign // ROW_TILE) + N_EXPERTS
    tables, pad_row, pad_n, tile_start, n_tile_e, n_used = _routing_tables(cnt_all)

    n_sorted_rows = n_row_tiles * ROW_TILE
    meta_s = meta_s[None]
    xs = _sort(tables, pad_row, pad_n, h2_p, meta_p, None, n_sorted_rows, TOK_TILE, 0, SORT_SUB)
    xs = _sort(tables, pad_row, pad_n, h2_s, meta_s, xs, n_sorted_rows, n_dec, n_ptiles, 1)
    ys = _experts(tile_start, n_tile_e, pad_n, n_used, xs, w_gu[l], b_gu[l], w_down[l], b_down[l])
    y_p = _combine(tables, ys, x1_p, meta_p, mod_p, row(ln2_g[l]), row(ln2_b[l]), TOK_TILE, 0,
                   COMBINE_SUB, alpha_res)
    y_s = _combine(tables, ys, x1_s, meta_s, mod_s, row(ln2_g[l]), row(ln2_b[l]), n_dec, n_ptiles,
                   1, alpha_res)

    conv_state_sample = jnp.stack([state_conv[l, :, 1, :], q_s], axis=1)[None]
    return (y_p.reshape(bsz, seq, D_MODEL),
            y_s,
            convst_p[None],
            conv_state_sample,
            vn_s.reshape(1, n_dec, 1, A_HEADS, A_HEAD_DIM))
```

```python
import functools

import jax
import jax.numpy as jnp
from jax import lax
from jax.experimental import pallas as pl
from jax.experimental.pallas import tpu as pltpu

F32 = jnp.float32
BF16 = jnp.bfloat16

D_MODEL = 1024
A_WIDTH = 512
B_WIDTH = 512
A_HEADS = 4
A_HEAD_DIM = 128
CHUNK = 128
PROJ_COLS = 2 * A_WIDTH + 3 * B_WIDTH
N_EXPERTS = 32
TOP_K = 4
D_EXPERT = 1024
SWIGLU_LIMIT = 7.0
SWIGLU_ALPHA = 1.702
LN_EPS = 1e-5
N_MOD = 6

LANES = 128
SUBLANES = 8
ROW_SLABS = D_MODEL // LANES
TOK_TILE = 256
MIX_SUB = 2
SORT_SUB = 4
COMBINE_SUB = 2
ROW_TILE = 256
TILE_GROUP = 4
TILE_AHEAD = TILE_GROUP
TILE_SLOTS = 2 * TILE_GROUP
VMEM_LIMIT = 56 * 1024 * 1024


def _layer_norm(x, g, b):
    mu = jnp.mean(x, axis=-1, keepdims=True)
    xc = x - mu
    var = jnp.mean(xc * xc, axis=-1, keepdims=True)
    return xc * lax.rsqrt(var + LN_EPS) * g + b


def _dot(a, b):
    return jnp.dot(a, b, preferred_element_type=F32)


def _dot_nt(a, b):
    return lax.dot_general(a, b, (((1,), (1,)), ((), ())), preferred_element_type=F32)


def _store_rows(ref, slot, rows):
    n = rows.shape[0]
    ref[slot] = rows.astype(BF16).reshape(n, ROW_SLABS, LANES)


def _load_rows(ref, slot):
    return ref[slot].reshape(ref.shape[1], D_MODEL)


def _split_bf16(a):
    hi = a.astype(BF16)
    lo = (a - hi.astype(F32)).astype(BF16)
    return hi, lo


def _ada_kernel(c_a_ref, c_b_ref, w_ref, b_ref, o_a_ref, o_b_ref):
    n_a = c_a_ref.shape[0]
    c = jnp.concatenate([c_a_ref[...], c_b_ref[...]], axis=0)
    s_hi, s_lo = _split_bf16(c * jax.nn.sigmoid(c))
    w_hi, w_lo = _split_bf16(w_ref[...])
    m = _dot(s_hi, w_hi) + _dot(s_hi, w_lo) + _dot(s_lo, w_hi) + b_ref[...]
    o_a_ref[...] = m[:n_a]
    o_b_ref[...] = m[n_a:]


def _ada(c_a, c_b, ada_w, ada_b):
    cols = 3 * D_MODEL // 2
    rows_spec = lambda c: pl.BlockSpec((c.shape[0], D_MODEL), lambda n: (0, 0))
    out_spec = lambda c: pl.BlockSpec((c.shape[0], cols), lambda n: (0, n))
    return pl.pallas_call(
        _ada_kernel,
        grid=(N_MOD * D_MODEL // cols,),
        in_specs=[
            rows_spec(c_a),
            rows_spec(c_b),
            pl.BlockSpec((D_MODEL, cols), lambda n: (0, n)),
            pl.BlockSpec((1, cols), lambda n: (0, n)),
        ],
        out_specs=[out_spec(c_a), out_spec(c_b)],
        out_shape=[jax.ShapeDtypeStruct((c_a.shape[0], N_MOD * D_MODEL), F32),
                   jax.ShapeDtypeStruct((c_b.shape[0], N_MOD * D_MODEL), F32)],
        name="ada",
    )(c_a, c_b, ada_w, ada_b.reshape(1, -1))


def _route(h2_tiles, router_wt2, router_b):
    n_tiles = len(h2_tiles)
    n_tile_tok = h2_tiles[0].shape[0]
    h2 = h2_tiles[0] if n_tiles == 1 else jnp.concatenate(h2_tiles, axis=0)
    n = h2.shape[0]
    lanes = [slice(i * n_tile_tok, (i + 1) * n_tile_tok) for i in range(n_tiles)]
    h_hi = h2.astype(BF16)
    h_lo = (h2 - h_hi.astype(F32)).astype(BF16)
    l1 = _dot_nt(router_wt2, h_hi)
    l2 = _dot_nt(router_wt2[:N_EXPERTS], h_lo)
    logits = l1[:N_EXPERTS] + l1[N_EXPERTS:] + l2 + router_b

    sub = lax.broadcasted_iota(jnp.int32, (N_EXPERTS, n), 0)
    work = logits
    vals, hots = [], []
    for _ in range(TOP_K):
        m = jnp.max(work, axis=0, keepdims=True)
        idx = jnp.min(jnp.where(work == m, sub, N_EXPERTS), axis=0, keepdims=True)
        hot = sub == idx
        work = jnp.where(hot, -jnp.inf, work)
        vals.append(m)
        hots.append(hot)
    exps = [jnp.exp(v - vals[0]) for v in vals]
    denom = exps[0] + exps[1] + exps[2] + exps[3]
    gates = [e / denom for e in exps]

    onehot = jnp.zeros((N_EXPERTS, n), F32)
    for hot in hots:
        onehot = onehot + hot.astype(F32)
    onehot_bf = onehot.astype(BF16)
    t_r = lax.broadcasted_iota(jnp.int32, (n_tile_tok, n_tile_tok), 0)
    t_c = lax.broadcasted_iota(jnp.int32, (n_tile_tok, n_tile_tok), 1)
    stacked = onehot_bf if n_tiles == 1 else jnp.concatenate([onehot_bf[:, ln] for ln in lanes], axis=0)
    rank_st = _dot(stacked, (t_r < t_c).astype(BF16))
    e_r = lax.broadcasted_iota(jnp.int32, (N_EXPERTS, N_EXPERTS), 0)
    e_c = lax.broadcasted_iota(jnp.int32, (N_EXPERTS, N_EXPERTS), 1)
    below = _dot((e_c < e_r).astype(BF16), onehot_bf)
    cnts, bases = [], []
    for i, ln in enumerate(lanes):
        cnts.append(jnp.sum(onehot[:, ln], axis=1, keepdims=True))
        off = jnp.sum(below[:, ln], axis=1, keepdims=True)
        bases.append(rank_st[i * N_EXPERTS:(i + 1) * N_EXPERTS] + off)
    base = bases[0] if n_tiles == 1 else jnp.concatenate(bases, axis=1)

    row8 = lax.broadcasted_iota(jnp.int32, (SUBLANES, n), 0)
    meta = jnp.zeros((SUBLANES, n), F32)
    for k in range(TOP_K):
        pos_k = jnp.sum(jnp.where(hots[k], base, 0.0), axis=0, keepdims=True)
        meta = jnp.where(row8 == k, pos_k, meta)
        meta = jnp.where(row8 == TOP_K + k, gates[k], meta)
    return [(meta[:, ln], jnp.broadcast_to(c, (N_EXPERTS, LANES))) for ln, c in zip(lanes, cnts)]


def _mix_prompt_kernel(alpha_res, x_ref, mod_ref, w_in_ref, sgu_g_ref, sgu_bln_ref, sgu_w_ref,
                       sgu_bias_ref, conv_w_ref, conv_b_ref, w_out_ref, ln1_g_ref, ln1_b_ref,
                       router_w_ref, router_b_ref,
                       x1_ref, h2_ref, meta_ref, cnt_ref, convst_ref, w_in_bf_ref, w_out_bf_ref,
                       carry_ref):
    t = pl.program_id(1)

    @pl.when(jnp.logical_and(pl.program_id(0) == 0, t == 0))
    def _():
        w_in_bf_ref[...] = w_in_ref[...].astype(BF16)
        w_out_bf_ref[...] = w_out_ref[...].astype(BF16)

    @pl.when(t == 0)
    def _():
        carry_ref[...] = jnp.zeros_like(carry_ref)

    seq_row = pl.ds(pl.program_id(0), 1)
    sh1, sc1, g1, sh2, sc2 = [mod_ref[seq_row, k * D_MODEL:(k + 1) * D_MODEL] for k in range(5)]
    r_i = lax.broadcasted_iota(jnp.int32, (CHUNK, CHUNK), 0)
    c_i = lax.broadcasted_iota(jnp.int32, (CHUNK, CHUNK), 1)
    tril = c_i <= r_i
    tm = TOK_TILE
    prev2 = carry_ref[0:1, :]
    prev1 = carry_ref[1:2, :]

    subs = range(MIX_SUB)
    tile_rows = [slice(sub * tm, (sub + 1) * tm) for sub in subs]
    xs_in = [x_ref[rows, :] for rows in tile_rows]
    zs = [_dot((x * (1.0 + sc1) + sh1).astype(BF16), w_in_bf_ref[...]) for x in xs_in]

    mix_ins = []
    for sub in subs:
        z = zs[sub]
        u = z[:, 0:A_WIDTH]
        v = z[:, A_WIDTH:2 * A_WIDTH]
        gate_b = z[:, 2 * A_WIDTH:2 * A_WIDTH + B_WIDTH]
        gate_c = z[:, 2 * A_WIDTH + B_WIDTH:2 * A_WIDTH + 2 * B_WIDTH]
        hb = z[:, 2 * A_WIDTH + 2 * B_WIDTH:]

        a_parts = []
        for hd in range(A_HEADS):
            sl = slice(hd * A_HEAD_DIM, (hd + 1) * A_HEAD_DIM)
            vn = _layer_norm(v[:, sl], sgu_g_ref[hd:hd + 1, :], sgu_bln_ref[hd:hd + 1, :]).astype(BF16)
            wm = jnp.where(tril, sgu_w_ref[hd], 0.0).astype(BF16)
            bias = sgu_bias_ref[:, hd:hd + 1]
            s_parts = []
            for c in range(tm // CHUNK):
                s_parts.append(_dot(wm, vn[c * CHUNK:(c + 1) * CHUNK, :]) + bias)
            a_parts.append(u[:, sl] * jnp.concatenate(s_parts, axis=0))

        q = gate_c * hb
        row = lax.broadcasted_iota(jnp.int32, q.shape, 0)
        q_m1 = jnp.where(row == 0, prev1, pltpu.roll(q, 1, 0))
        q_m2 = jnp.where(row == 0, prev2, jnp.where(row == 1, prev1, pltpu.roll(q, 2, 0)))
        conv = (conv_b_ref[...] + q_m2 * conv_w_ref[0:1, :] + q_m1 * conv_w_ref[1:2, :]
                + q * conv_w_ref[2:3, :])
        b_out = gate_b * conv
        prev2 = q[tm - 2:tm - 1, :]
        prev1 = q[tm - 1:tm, :]

        mix_ins.append(jnp.concatenate(a_parts + [b_out], axis=-1).astype(BF16))

    mixes = [_dot(mix_in, w_out_bf_ref[...]) for mix_in in mix_ins]
    h2s = []
    for sub in subs:
        x1 = _layer_norm(alpha_res * xs_in[sub] + g1 * mixes[sub], ln1_g_ref[...], ln1_b_ref[...])
        x1_ref[tile_rows[sub], :] = x1
        h2 = x1 * (1.0 + sc2) + sh2
        h2_ref[tile_rows[sub], :] = h2.astype(BF16)
        h2s.append(h2)
    routed = _route(h2s, router_w_ref[...], router_b_ref[...])
    for sub in subs:
        meta, cnt = routed[sub]
        meta_ref[sub] = meta
        cnt_ref[sub] = cnt.astype(jnp.int32)

    last2 = jnp.concatenate([prev2, prev1], axis=0)
    carry_ref[0:2, :] = last2
    convst_ref[...] = last2


def _mix_prompt(x, mod, w_in, sgu_g, sgu_bln, sgu_w, sgu_bias_t, conv_w, conv_b, w_out,
                ln1_g, ln1_b, router_wt2, router_b_col, alpha_res):
    bsz, seq, _ = x.shape
    step_tok = MIX_SUB * TOK_TILE
    steps = seq // step_tok
    n_tok = bsz * seq
    tiles = seq // TOK_TILE
    const2 = lambda b, t: (0, 0)
    tok = lambda b, t: (b * steps + t, 0)
    tile3 = lambda b, t: (b * steps + t, 0, 0)
    return pl.pallas_call(
        functools.partial(_mix_prompt_kernel, alpha_res),
        grid=(bsz, steps),
        in_specs=[
            pl.BlockSpec((None, step_tok, D_MODEL), lambda b, t: (b, t, 0)),
            pl.BlockSpec((bsz, N_MOD * D_MODEL), const2),
            pl.BlockSpec((D_MODEL, PROJ_COLS), const2, pipeline_mode=pl.Buffered(1)),
            pl.BlockSpec((A_HEADS, A_HEAD_DIM), const2),
            pl.BlockSpec((A_HEADS, A_HEAD_DIM), const2),
            pl.BlockSpec((A_HEADS, CHUNK, CHUNK), lambda b, t: (0, 0, 0)),
            pl.BlockSpec((CHUNK, A_HEADS), const2),
            pl.BlockSpec((3, B_WIDTH), const2),
            pl.BlockSpec((1, B_WIDTH), const2),
            pl.BlockSpec((D_MODEL, D_MODEL), const2, pipeline_mode=pl.Buffered(1)),
            pl.BlockSpec((1, D_MODEL), const2),
            pl.BlockSpec((1, D_MODEL), const2),
            pl.BlockSpec((2 * N_EXPERTS, D_MODEL), const2),
            pl.BlockSpec((N_EXPERTS, 1), const2),
        ],
        out_specs=[
            pl.BlockSpec((step_tok, D_MODEL), tok),
            pl.BlockSpec((step_tok, D_MODEL), tok),
            pl.BlockSpec((MIX_SUB, SUBLANES, TOK_TILE), tile3),
            pl.BlockSpec((MIX_SUB, N_EXPERTS, LANES), tile3),
            pl.BlockSpec((None, 2, B_WIDTH), lambda b, t: (b, 0, 0)),
            pl.BlockSpec((D_MODEL, PROJ_COLS), const2),
            pl.BlockSpec((D_MODEL, D_MODEL), const2),
        ],
        out_shape=[
            jax.ShapeDtypeStruct((n_tok, D_MODEL), F32),
            jax.ShapeDtypeStruct((n_tok, D_MODEL), BF16),
            jax.ShapeDtypeStruct((bsz * tiles, SUBLANES, TOK_TILE), F32),
            jax.ShapeDtypeStruct((bsz * tiles, N_EXPERTS, LANES), jnp.int32),
            jax.ShapeDtypeStruct((bsz, 2, B_WIDTH), F32),
            jax.ShapeDtypeStruct((D_MODEL, PROJ_COLS), BF16),
            jax.ShapeDtypeStruct((D_MODEL, D_MODEL), BF16),
        ],
        scratch_shapes=[pltpu.VMEM((SUBLANES, B_WIDTH), F32)],
        compiler_params=pltpu.CompilerParams(
            dimension_semantics=("arbitrary", "arbitrary"), vmem_limit_bytes=VMEM_LIMIT,
            allow_input_fusion=[True] * 14),
        name="mix_prompt",
    )(x, mod, w_in, sgu_g, sgu_bln, sgu_w, sgu_bias_t, conv_w, conv_b, w_out,
      ln1_g, ln1_b, router_wt2, router_b_col)


def _mix_sample_kernel(alpha_res, x_ref, mod_ref, prev0_ref, prev1_ref, w_in_ref, sgu_g_ref,
                       sgu_bln_ref, sgu_w00_ref, sgu_b0_ref, conv_w_ref, conv_b_ref, w_out_ref,
                       ln1_g_ref, ln1_b_ref, router_w_ref, router_b_ref,
                       x1_ref, h2_ref, meta_ref, cnt_ref, q_ref, vn_ref):
    x = x_ref[...]
    sh1 = mod_ref[:, 0:D_MODEL]
    sc1 = mod_ref[:, D_MODEL:2 * D_MODEL]
    g1 = mod_ref[:, 2 * D_MODEL:3 * D_MODEL]
    sh2 = mod_ref[:, 3 * D_MODEL:4 * D_MODEL]
    sc2 = mod_ref[:, 4 * D_MODEL:5 * D_MODEL]
    h = (x * (1.0 + sc1) + sh1).astype(BF16)
    z = _dot(h, w_in_ref[...])
    u = z[:, 0:A_WIDTH]
    v = z[:, A_WIDTH:2 * A_WIDTH]
    gate_b = z[:, 2 * A_WIDTH:2 * A_WIDTH + B_WIDTH]
    gate_c = z[:, 2 * A_WIDTH + B_WIDTH:2 * A_WIDTH + 2 * B_WIDTH]
    hb = z[:, 2 * A_WIDTH + 2 * B_WIDTH:]

    vn_parts = []
    for hd in range(A_HEADS):
        sl = slice(hd * A_HEAD_DIM, (hd + 1) * A_HEAD_DIM)
        vn_parts.append(_layer_norm(v[:, sl], sgu_g_ref[hd:hd + 1, :], sgu_bln_ref[hd:hd + 1, :]))
    vn = jnp.concatenate(vn_parts, axis=-1)
    vn_ref[...] = vn
    a_out = u * (vn * sgu_w00_ref[...] + sgu_b0_ref[...])

    q = gate_c * hb
    q_ref[...] = q
    conv = (conv_b_ref[...] + prev0_ref[...] * conv_w_ref[0:1, :] + prev1_ref[...] * conv_w_ref[1:2, :]
            + q * conv_w_ref[2:3, :])
    b_out = gate_b * conv

    mix_in = jnp.concatenate([a_out, b_out], axis=-1).astype(BF16)
    mix = _dot(mix_in, w_out_ref[...])
    x1 = _layer_norm(alpha_res * x + g1 * mix, ln1_g_ref[...], ln1_b_ref[...])
    x1_ref[...] = x1
    h2 = x1 * (1.0 + sc2) + sh2
    h2_ref[...] = h2.astype(BF16)
    (meta, cnt), = _route([h2], router_w_ref[...], router_b_ref[...])
    meta_ref[...] = meta
    cnt_ref[...] = cnt.astype(jnp.int32)


def _mix_sample(x, mod, prev0, prev1, w_in_bf, sgu_g, sgu_bln, sgu_w00, sgu_b0, conv_w, conv_b,
                w_out_bf, ln1_g, ln1_b, router_wt2, router_b_col, alpha_res):
    n = x.shape[0]
    return pl.pallas_call(
        functools.partial(_mix_sample_kernel, alpha_res),
        out_shape=[
            jax.ShapeDtypeStruct((n, D_MODEL), F32),
            jax.ShapeDtypeStruct((n, D_MODEL), BF16),
            jax.ShapeDtypeStruct((SUBLANES, n), F32),
            jax.ShapeDtypeStruct((N_EXPERTS, LANES), jnp.int32),
            jax.ShapeDtypeStruct((n, B_WIDTH), F32),
            jax.ShapeDtypeStruct((n, A_WIDTH), F32),
        ],
        compiler_params=pltpu.CompilerParams(vmem_limit_bytes=VMEM_LIMIT,
                                             allow_input_fusion=[True] * 16),
        name="mix_sample",
    )(x, mod, prev0, prev1, w_in_bf, sgu_g, sgu_bln, sgu_w00, sgu_b0, conv_w, conv_b, w_out_bf,
      ln1_g, ln1_b, router_wt2, router_b_col)


def _strip_copy(src_ref, src_row, dst_ref, dst_row, n_rows, sem, priority=0):
    @pl.when(n_rows > 0)
    def _():
        pltpu.make_async_copy(src_ref.at[pl.ds(src_row, n_rows)],
                              dst_ref.at[pl.ds(dst_row, n_rows)], sem).start(priority=priority)


def _for_each_expert(strip):
    for e in range(N_EXPERTS):
        strip(e, 0)


def _wait_rows(hbm_ref, vmem_ref, n_rows, sem):
    pltpu.make_async_copy(hbm_ref.at[pl.ds(0, n_rows)], vmem_ref.at[pl.ds(0, n_rows)], sem).wait()


def _sort_kernel(n_tok, tile0, n_steps, subs, first, *refs):
    if first:
        (sorted_row_ref, off_ref, cnt_ref, pad_row_ref, pad_n_ref, h2_ref, meta_ref,
         xs_ref, stage_ref, zero_ref, sem, pad_sem) = refs
    else:
        (sorted_row_ref, off_ref, cnt_ref, pad_row_ref, pad_n_ref, h2_ref, meta_ref, _,
         xs_ref, stage_ref, sem) = refs
    j = pl.program_id(0)
    n_sorted = n_tok * TOP_K
    parity = j % 2

    if first:
        @pl.when(j == 0)
        def _():
            zero_ref[...] = jnp.zeros_like(zero_ref)

            def start(e, carry):
                _strip_copy(zero_ref, 0, xs_ref, pad_row_ref[e], pad_n_ref[e], pad_sem)
                return carry
            lax.fori_loop(0, N_EXPERTS, start, 0)

            def wait(e, carry):
                @pl.when(pad_n_ref[e] > 0)
                def _():
                    _wait_rows(xs_ref, zero_ref, pad_n_ref[e], pad_sem)
                return carry
            lax.fori_loop(0, N_EXPERTS, wait, 0)

    sub_r = lax.broadcasted_iota(jnp.int16, (n_sorted, n_tok), 0)
    for sub in range(subs):
        slot = parity * subs + sub
        pos = meta_ref[sub][0:TOP_K, :].astype(jnp.int16)
        perm = jnp.zeros((n_sorted, n_tok), BF16)
        for k in range(TOP_K):
            perm = jnp.where(sub_r == pos[k:k + 1, :], jnp.ones((), BF16), perm)
        rows = _dot(perm, h2_ref[sub * n_tok:(sub + 1) * n_tok, :])
        _store_rows(stage_ref, slot, rows)

        base = (tile0 + j * subs + sub) * N_EXPERTS

        def strip(e, priority, slot=slot, base=base):
            _strip_copy(stage_ref.at[slot], off_ref[base + e], xs_ref, sorted_row_ref[base + e],
                        cnt_ref[base + e], sem.at[slot], priority)
        _for_each_expert(strip)

    @pl.when(j > 0)
    def _():
        for sub in range(subs):
            _wait_rows(xs_ref, stage_ref.at[0], n_sorted, sem.at[(1 - parity) * subs + sub])

    @pl.when(j == n_steps - 1)
    def _():
        for sub in range(subs):
            _wait_rows(xs_ref, stage_ref.at[0], n_sorted, sem.at[parity * subs + sub])


def _sort(tables, pad_row, pad_n, h2, meta, xs, n_sorted_rows, n_tok, tile0, subs):
    sorted_row, off, cnt = tables
    first = xs is None
    n_steps = h2.shape[0] // (n_tok * subs)
    in_specs = [pl.BlockSpec((subs * n_tok, D_MODEL), lambda j, *_: (j, 0)),
                pl.BlockSpec((subs, SUBLANES, n_tok), lambda j, *_: (j, 0, 0))]
    operands = [sorted_row, off, cnt, pad_row, pad_n, h2, meta]
    scratch = [pltpu.VMEM((2 * subs, n_tok * TOP_K, ROW_SLABS, LANES), BF16)]
    if first:
        scratch.append(pltpu.VMEM((ROW_TILE, ROW_SLABS, LANES), BF16))
        aliases = {}
    else:
        in_specs.append(pl.BlockSpec(memory_space=pl.ANY))
        operands.append(xs)
        aliases = {len(operands) - 1: 0}
    scratch.append(pltpu.SemaphoreType.DMA((2 * subs,)))
    if first:
        scratch.append(pltpu.SemaphoreType.DMA(()))
    grid_spec = pltpu.PrefetchScalarGridSpec(
        num_scalar_prefetch=5,
        grid=(n_steps,),
        in_specs=in_specs,
        out_specs=pl.BlockSpec(memory_space=pl.ANY),
        scratch_shapes=scratch,
    )
    return pl.pallas_call(
        functools.partial(_sort_kernel, n_tok, tile0, n_steps, subs, first),
        grid_spec=grid_spec,
        out_shape=jax.ShapeDtypeStruct((n_sorted_rows, ROW_SLABS, LANES), BF16),
        input_output_aliases=aliases,
        compiler_params=pltpu.CompilerParams(
            dimension_semantics=("arbitrary",), vmem_limit_bytes=VMEM_LIMIT),
        name="sort_first" if first else "sort_more",
    )(*operands)


def _experts_kernel(tile_start_ref, n_tile_ref, short_last_ref, n_used_ref, xs_ref, w_gu_ref, b_gu_ref,
                    w_down_ref, b_down_ref, ys_ref, w_gu_bf_ref, w_down_bf_ref, x_buf, y_buf, x_sem,
                    y_sem):
    e = pl.program_id(0)
    n_used = n_used_ref[0]
    first_tile = tile_start_ref[e]
    n_tile = n_tile_ref[e]

    def x_copy(g):
        slot = g % TILE_SLOTS
        return pltpu.make_async_copy(xs_ref.at[pl.ds(g * ROW_TILE, ROW_TILE)], x_buf.at[slot],
                                     x_sem.at[slot])

    def y_copy(g):
        slot = g % TILE_SLOTS
        return pltpu.make_async_copy(y_buf.at[slot], ys_ref.at[pl.ds(g * ROW_TILE, ROW_TILE)],
                                     y_sem.at[slot])

    def request(g):
        @pl.when(g < n_used)
        def _():
            x_copy(g).start(priority=1)

    @pl.when(e == 0)
    def _():
        for g in range(TILE_AHEAD):
            request(g)

    @pl.when(n_tile > 0)
    def _():
        w_gu_bf_ref[...] = w_gu_ref[...].astype(BF16)
        w_down_bf_ref[...] = w_down_ref[...].astype(BF16)

    def begin(g):
        x_copy(g).wait()
        request(g + TILE_AHEAD)

        @pl.when(g >= TILE_SLOTS)
        def _():
            y_copy(g - TILE_SLOTS).wait()

    def compute(g, rows=ROW_TILE):
        slot = g % TILE_SLOTS
        x = x_buf[slot, 0:rows].reshape(rows, D_MODEL)
        gu = _dot(x, w_gu_bf_ref[...]) + b_gu_ref[...]
        gate = jnp.minimum(gu[:, :D_EXPERT], SWIGLU_LIMIT)
        up = jnp.clip(gu[:, D_EXPERT:], -SWIGLU_LIMIT, SWIGLU_LIMIT)
        act = (up + 1.0) * gate * jax.nn.sigmoid(SWIGLU_ALPHA * gate)
        y = _dot(act.astype(BF16), w_down_bf_ref[...]) + b_down_ref[...]
        y_buf[slot, 0:rows] = y.astype(BF16).reshape(rows, ROW_SLABS, LANES)
        if rows < ROW_TILE:
            y_buf[slot, rows:ROW_TILE] = jnp.zeros((ROW_TILE - rows, ROW_SLABS, LANES), BF16)
        y_copy(g).start(priority=1)

    def group(g, size):
        for k in range(size):
            begin(g + k)
        for k in range(size):
            compute(g + k)

    def group_body(p, carry):
        group(first_tile + TILE_GROUP * p, TILE_GROUP)
        return carry

    short_last = short_last_ref[e]
    n_full = n_tile - short_last
    n_group = n_full // TILE_GROUP
    lax.fori_loop(0, n_group, group_body, 0)
    rest = n_full - n_group * TILE_GROUP
    g_rest = first_tile + n_group * TILE_GROUP
    size = TILE_GROUP // 2
    while size >= 1:
        @pl.when((rest & size) != 0)
        def _(g_rest=g_rest, size=size):
            group(g_rest, size)
        g_rest = g_rest + (rest & size)
        size //= 2

    @pl.when(short_last == 1)
    def _():
        g = first_tile + n_full
        begin(g)
        compute(g, ROW_TILE // 2)

    @pl.when(e == N_EXPERTS - 1)
    def _():
        for back in range(TILE_SLOTS, 0, -1):
            @pl.when(n_used >= back)
            def _(back=back):
                y_copy(n_used - back).wait()


def _experts(tile_start, n_tile_e, pad_n, n_used, xs, w_gu, b_gu, w_down, b_down):
    short_last = (pad_n >= ROW_TILE // 2).astype(jnp.int32)
    w_blk = lambda e, *_: (e, 0, 0)
    grid_spec = pltpu.PrefetchScalarGridSpec(
        num_scalar_prefetch=4,
        grid=(N_EXPERTS,),
        in_specs=[
            pl.BlockSpec(memory_space=pl.ANY),
            pl.BlockSpec((None, D_MODEL, 2 * D_EXPERT), w_blk),
            pl.BlockSpec((None, 1, 2 * D_EXPERT), w_blk),
            pl.BlockSpec((None, D_EXPERT, D_MODEL), w_blk),
            pl.BlockSpec((None, 1, D_MODEL), w_blk),
        ],
        out_specs=pl.BlockSpec(memory_space=pl.ANY),
        scratch_shapes=[pltpu.VMEM((D_MODEL, 2 * D_EXPERT), BF16),
                        pltpu.VMEM((D_EXPERT, D_MODEL), BF16),
                        pltpu.VMEM((TILE_SLOTS, ROW_TILE, ROW_SLABS, LANES), BF16),
                        pltpu.VMEM((TILE_SLOTS, ROW_TILE, ROW_SLABS, LANES), BF16),
                        pltpu.SemaphoreType.DMA((TILE_SLOTS,)),
                        pltpu.SemaphoreType.DMA((TILE_SLOTS,))],
    )
    return pl.pallas_call(
        _experts_kernel,
        grid_spec=grid_spec,
        out_shape=jax.ShapeDtypeStruct(xs.shape, BF16),
        compiler_params=pltpu.CompilerParams(
            dimension_semantics=("arbitrary",), vmem_limit_bytes=VMEM_LIMIT),
        name="experts",
    )(tile_start, n_tile_e, short_last, n_used, xs, w_gu, b_gu.reshape(N_EXPERTS, 1, -1), w_down,
      b_down.reshape(N_EXPERTS, 1, -1))


def _combine_kernel(n_tok, tile0, n_steps, subs, seq_steps, alpha_res, sorted_row_ref, off_ref, cnt_ref,
                    ys_ref, x1_ref, meta_ref, mod_ref, ln2_g_ref, ln2_b_ref, out_ref,
                    stage_ref, sem):
    j = pl.program_id(0)
    n_sorted = n_tok * TOP_K
    parity = j % 2

    def start_step(step, step_parity, live):
        for sub in range(subs):
            base = (tile0 + step * subs + sub) * N_EXPERTS
            to_slot = step_parity * subs + sub

            def strip(e, priority, base=base, to_slot=to_slot):
                _strip_copy(ys_ref, sorted_row_ref[base + e], stage_ref.at[to_slot],
                            off_ref[base + e], jnp.where(live, cnt_ref[base + e], 0),
                            sem.at[to_slot], priority)
            _for_each_expert(strip)

    @pl.when(j == 0)
    def _():
        start_step(j, parity, True)

    start_step(jnp.minimum(j + 1, n_steps - 1), 1 - parity, j + 1 < n_steps)

    sub_r = lax.broadcasted_iota(jnp.int16, (n_sorted, n_tok), 0)
    for sub in range(subs):
        slot = parity * subs + sub
        rows = slice(sub * n_tok, (sub + 1) * n_tok)
        meta = meta_ref[sub]
        pos = meta[0:TOP_K, :].astype(jnp.int16)
        gate = meta[TOP_K:2 * TOP_K, :].astype(BF16)
        comb_t = jnp.zeros((n_sorted, n_tok), BF16)
        for k in range(TOP_K):
            comb_t = jnp.where(sub_r == pos[k:k + 1, :], gate[k:k + 1, :], comb_t)

        _wait_rows(ys_ref, stage_ref.at[0], n_sorted, sem.at[slot])
        ys = _load_rows(stage_ref, slot)
        ffn = lax.dot_general(comb_t, ys, (((0,), (0,)), ((), ())), preferred_element_type=F32)
        if seq_steps is None:
            g2 = mod_ref[rows, 5 * D_MODEL:6 * D_MODEL]
        else:
            g2 = mod_ref[pl.ds(j // seq_steps, 1), 5 * D_MODEL:6 * D_MODEL]
        y = _layer_norm(alpha_res * x1_ref[rows, :] + g2 * ffn, ln2_g_ref[...], ln2_b_ref[...])
        if seq_steps is None:
            out_ref[rows, 0, :] = y
        else:
            out_ref[rows, :] = y


def _combine(tables, ys, x1, meta, mod, ln2_g, ln2_b, n_tok, tile0, subs, alpha_res):
    sorted_row, off, cnt = tables
    step_tok = subs * n_tok
    n_steps = x1.shape[0] // step_tok
    if mod.shape[0] == x1.shape[0]:
        seq_steps = None
        mod_spec = pl.BlockSpec((step_tok, N_MOD * D_MODEL), lambda j, *_: (j, 0))
        out_spec = pl.BlockSpec((step_tok, 1, D_MODEL), lambda j, *_: (j, 0, 0))
        out_shape = jax.ShapeDtypeStruct((x1.shape[0], 1, D_MODEL), F32)
    else:
        seq_steps = n_steps // mod.shape[0]
        mod_spec = pl.BlockSpec(mod.shape, lambda j, *_: (0, 0))
        out_spec = pl.BlockSpec((step_tok, D_MODEL), lambda j, *_: (j, 0))
        out_shape = jax.ShapeDtypeStruct(x1.shape, F32)
    grid_spec = pltpu.PrefetchScalarGridSpec(
        num_scalar_prefetch=3,
        grid=(n_steps,),
        in_specs=[
            pl.BlockSpec(memory_space=pl.ANY),
            pl.BlockSpec((step_tok, D_MODEL), lambda j, *_: (j, 0)),
            pl.BlockSpec((subs, SUBLANES, n_tok), lambda j, *_: (j, 0, 0)),
            mod_spec,
            pl.BlockSpec((1, D_MODEL), lambda j, *_: (0, 0)),
            pl.BlockSpec((1, D_MODEL), lambda j, *_: (0, 0)),
        ],
        out_specs=out_spec,
        scratch_shapes=[pltpu.VMEM((2 * subs, n_tok * TOP_K, ROW_SLABS, LANES), BF16),
                        pltpu.SemaphoreType.DMA((2 * subs,))],
    )
    return pl.pallas_call(
        functools.partial(_combine_kernel, n_tok, tile0, n_steps, subs, seq_steps, alpha_res),
        grid_spec=grid_spec,
        out_shape=out_shape,
        compiler_params=pltpu.CompilerParams(
            dimension_semantics=("arbitrary",), vmem_limit_bytes=VMEM_LIMIT),
        name="combine",
    )(sorted_row, off, cnt, ys, x1, meta, mod, ln2_g, ln2_b)


def _routing_tables(cnt_all):
    total = jnp.sum(cnt_all, axis=0)
    n_tile_e = (total + ROW_TILE - 1) // ROW_TILE
    tile_end = jnp.cumsum(n_tile_e)
    tile_start = tile_end - n_tile_e
    row_start = tile_start * ROW_TILE
    cum = jnp.cumsum(cnt_all, axis=0) - cnt_all
    off = jnp.cumsum(cnt_all, axis=1) - cnt_all
    sorted_row = row_start[None, :] + cum
    n_used = tile_end[-1]
    pad_row = row_start + total
    pad_n = n_tile_e * ROW_TILE - total
    i32 = lambda a: a.astype(jnp.int32)
    return ((i32(sorted_row).reshape(-1), i32(off).reshape(-1), i32(cnt_all).reshape(-1)),
            i32(pad_row), i32(pad_n), i32(tile_start), i32(n_tile_e), i32(n_used).reshape(1))


def kernel(x_prompt, x_sample, state_conv, c_prompt, c_sample, ada_w, ada_b, w_in, sgu_ln_g, sgu_ln_b,
           sgu_w, sgu_b, conv_w, conv_b, w_out, ln1_g, ln1_b, router_w, router_b, w_gu, b_gu,
           w_down, b_down, ln2_g, ln2_b):
    depth = ada_w.shape[0]
    assert depth == 1
    bsz, seq, _ = x_prompt.shape
    n_dec = x_sample.shape[0]
    assert x_sample.shape[1] == 1 and seq % TOK_TILE == 0
    alpha_res = (2.0 * depth) ** 0.25
    l = 0

    mod_s, mod_p = _ada(c_sample, c_prompt, ada_w[l], ada_b[l])

    router_wt =jnp.transpose(router_w[l])
    router_wt_hi = router_wt.astype(BF16)
    router_wt_lo = (router_wt - router_wt_hi.astype(F32)).astype(BF16)
    router_wt2 = jnp.concatenate([router_wt_hi, router_wt_lo], axis=0)
    router_b_col = router_b[l].reshape(N_EXPERTS, 1)
    row = lambda a: a.reshape(1, -1)

    x1_p, h2_p, meta_p, cnt_p, convst_p, w_in_bf, w_out_bf = _mix_prompt(
        x_prompt, mod_p, w_in[l], sgu_ln_g[l], sgu_ln_b[l], sgu_w[l], jnp.transpose(sgu_b[l]),
        conv_w[l], row(conv_b[l]), w_out[l], row(ln1_g[l]), row(ln1_b[l]), router_wt2,
        router_b_col, alpha_res)
    x1_s, h2_s, meta_s, cnt_s, q_s, vn_s = _mix_sample(
        x_sample.reshape(n_dec, D_MODEL), mod_s, state_conv[l, :, 0, :], state_conv[l, :, 1, :],
        w_in_bf, sgu_ln_g[l], sgu_ln_b[l], row(jnp.repeat(sgu_w[l, :, 0, 0], A_HEAD_DIM)),
        row(jnp.repeat(sgu_b[l, :, 0], A_HEAD_DIM)), conv_w[l], row(conv_b[l]), w_out_bf,
        row(ln1_g[l]), row(ln1_b[l]), router_wt2, router_b_col, alpha_res)

    n_ptiles = bsz * seq // TOK_TILE
    cnt_all = jnp.concatenate([cnt_p[:, :, 0], cnt_s[None, :, 0]], axis=0)
    n_assign = (bsz * seq + n_dec) * TOP_K
    n_row_tiles = -(-n_assign // ROW_TILE) + N_EXPERTS
    tables, pad_row, pad_n, tile_start, n_tile_e, n_used = _routing_tables(cnt_all)

    n_sorted_rows = n_row_tiles * ROW_TILE
    meta_s = meta_s[None]
    xs = _sort(tables, pad_row, pad_n, h2_p, meta_p, None, n_sorted_rows, TOK_TILE, 0, SORT_SUB)
    xs = _sort(tables, pad_row, pad_n, h2_s, meta_s, xs, n_sorted_rows, n_dec, n_ptiles, 1)
    ys = _experts(tile_start, n_tile_e, pad_n, n_used, xs, w_gu[l], b_gu[l], w_down[l], b_down[l])
    y_p = _combine(tables, ys, x1_p, meta_p, mod_p, row(ln2_g[l]), row(ln2_b[l]), TOK_TILE, 0,
                   COMBINE_SUB, alpha_res)
    y_s = _combine(tables, ys, x1_s, meta_s, mod_s, row(ln2_g[l]), row(ln2_b[l]), n_dec, n_ptiles,
                   1, alpha_res)

    conv_state_sample = jnp.stack([state_conv[l, :, 1, :], q_s], axis=1)[None]
    return (y_p.reshape(bsz, seq, D_MODEL),
            y_s,
            convst_p[None],
            conv_state_sample,
            vn_s.reshape(1, n_dec, 1, A_HEADS, A_HEAD_DIM))
```

```python
import functools

import jax
import jax.numpy as jnp
from jax import lax
from jax.experimental import pallas as pl
from jax.experimental.pallas import tpu as pltpu

F32 = jnp.float32
BF16 = jnp.bfloat16

D_MODEL = 1024
A_WIDTH = 512
B_WIDTH = 512
A_HEADS = 4
A_HEAD_DIM = 128
CHUNK = 128
PROJ_COLS = 2 * A_WIDTH + 3 * B_WIDTH
N_EXPERTS = 32
TOP_K = 4
D_EXPERT = 1024
SWIGLU_LIMIT = 7.0
SWIGLU_ALPHA = 1.702
LN_EPS = 1e-5
N_MOD = 6

LANES = 128
SUBLANES = 8
ROW_SLABS = D_MODEL // LANES
TOK_TILE = 256
MIX_SUB = 2
SORT_SUB = 4
COMBINE_SUB = 2
ROW_TILE = 256
TILE_GROUP = 4
TILE_AHEAD = TILE_GROUP
TILE_SLOTS = 2 * TILE_GROUP
VMEM_LIMIT = 56 * 1024 * 1024


def _layer_norm(x, g, b):
    mu = jnp.mean(x, axis=-1, keepdims=True)
    xc = x - mu
    var = jnp.mean(xc * xc, axis=-1, keepdims=True)
    return xc * lax.rsqrt(var + LN_EPS) * g + b


def _dot(a, b):
    return jnp.dot(a, b, preferred_element_type=F32)


def _dot_nt(a, b):
    return lax.dot_general(a, b, (((1,), (1,)), ((), ())), preferred_element_type=F32)


def _store_rows(ref, slot, rows):
    n = rows.shape[0]
    ref[slot] = rows.astype(BF16).reshape(n, ROW_SLABS, LANES)


def _load_rows(ref, slot):
    return ref[slot].reshape(ref.shape[1], D_MODEL)


def _split_bf16(a):
    hi = a.astype(BF16)
    lo = (a - hi.astype(F32)).astype(BF16)
    return hi, lo


def _ada_kernel(c_a_ref, c_b_ref, w_ref, b_ref, o_a_ref, o_b_ref):
    n_a = c_a_ref.shape[0]
    c = jnp.concatenate([c_a_ref[...], c_b_ref[...]], axis=0)
    s_hi, s_lo = _split_bf16(c * jax.nn.sigmoid(c))
    w_hi, w_lo = _split_bf16(w_ref[...])
    m = _dot(s_hi, w_hi) + _dot(s_hi, w_lo) + _dot(s_lo, w_hi) + b_ref[...]
    o_a_ref[...] = m[:n_a]
    o_b_ref[...] = m[n_a:]


def _ada(c_a, c_b, ada_w, ada_b):
    cols = 3 * D_MODEL // 2
    rows_spec = lambda c: pl.BlockSpec((c.shape[0], D_MODEL), lambda n: (0, 0))
    out_spec = lambda c: pl.BlockSpec((c.shape[0], cols), lambda n: (0, n))
    return pl.pallas_call(
        _ada_kernel,
        grid=(N_MOD * D_MODEL // cols,),
        in_specs=[
            rows_spec(c_a),
            rows_spec(c_b),
            pl.BlockSpec((D_MODEL, cols), lambda n: (0, n)),
            pl.BlockSpec((1, cols), lambda n: (0, n)),
        ],
        out_specs=[out_spec(c_a), out_spec(c_b)],
        out_shape=[jax.ShapeDtypeStruct((c_a.shape[0], N_MOD * D_MODEL), F32),
                   jax.ShapeDtypeStruct((c_b.shape[0], N_MOD * D_MODEL), F32)],
        name="ada",
    )(c_a, c_b, ada_w, ada_b.reshape(1, -1))


def _route(h2_tiles, router_wt2, router_b):
    n_tiles = len(h2_tiles)
    n_tile_tok = h2_tiles[0].shape[0]
    h2 = h2_tiles[0] if n_tiles == 1 else jnp.concatenate(h2_tiles, axis=0)
    n = h2.shape[0]
    lanes = [slice(i * n_tile_tok, (i + 1) * n_tile_tok) for i in range(n_tiles)]
    h_hi = h2.astype(BF16)
    h_lo = (h2 - h_hi.astype(F32)).astype(BF16)
    l1 = _dot_nt(router_wt2, h_hi)
    l2 = _dot_nt(router_wt2[:N_EXPERTS], h_lo)
    logits = l1[:N_EXPERTS] + l1[N_EXPERTS:] + l2 + router_b

    sub = lax.broadcasted_iota(jnp.int32, (N_EXPERTS, n), 0)
    work = logits
    vals, hots = [], []
    for _ in range(TOP_K):
        m = jnp.max(work, axis=0, keepdims=True)
        idx = jnp.min(jnp.where(work == m, sub, N_EXPERTS), axis=0, keepdims=True)
        hot = sub == idx
        work = jnp.where(hot, -jnp.inf, work)
        vals.append(m)
        hots.append(hot)
    exps = [jnp.exp(v - vals[0]) for v in vals]
    denom = exps[0] + exps[1] + exps[2] + exps[3]
    gates = [e / denom for e in exps]

    onehot = jnp.zeros((N_EXPERTS, n), F32)
    for hot in hots:
        onehot = onehot + hot.astype(F32)
    onehot_bf = onehot.astype(BF16)
    t_r = lax.broadcasted_iota(jnp.int32, (n_tile_tok, n_tile_tok), 0)
    t_c = lax.broadcasted_iota(jnp.int32, (n_tile_tok, n_tile_tok), 1)
    stacked = onehot_bf if n_tiles == 1 else jnp.concatenate([onehot_bf[:, ln] for ln in lanes], axis=0)
    rank_st = _dot(stacked, (t_r < t_c).astype(BF16))
    e_r = lax.broadcasted_iota(jnp.int32, (N_EXPERTS, N_EXPERTS), 0)
    e_c = lax.broadcasted_iota(jnp.int32, (N_EXPERTS, N_EXPERTS), 1)
    below = _dot((e_c < e_r).astype(BF16), onehot_bf)
    cnts, bases = [], []
    for i, ln in enumerate(lanes):
        cnts.append(jnp.sum(onehot[:, ln], axis=1, keepdims=True))
        off = jnp.sum(below[:, ln], axis=1, keepdims=True)
        bases.append(rank_st[i * N_EXPERTS:(i + 1) * N_EXPERTS] + off)
    base = bases[0] if n_tiles == 1 else jnp.concatenate(bases, axis=1)

    row8 = lax.broadcasted_iota(jnp.int32, (SUBLANES, n), 0)
    meta = jnp.zeros((SUBLANES, n), F32)
    for k in range(TOP_K):
        pos_k = jnp.sum(jnp.where(hots[k], base, 0.0), axis=0, keepdims=True)
        meta = jnp.where(row8 == k, pos_k, meta)
        meta = jnp.where(row8 == TOP_K + k, gates[k], meta)
    return [(meta[:, ln], jnp.broadcast_to(c, (N_EXPERTS, LANES))) for ln, c in zip(lanes, cnts)]


def _mix_prompt_kernel(alpha_res, x_ref, mod_ref, w_in_ref, sgu_g_ref, sgu_bln_ref, sgu_w_ref,
                       sgu_bias_ref, conv_w_ref, conv_b_ref, w_out_ref, ln1_g_ref, ln1_b_ref,
                       router_w_ref, router_b_ref,
                       x1_ref, h2_ref, meta_ref, cnt_ref, convst_ref, w_in_bf_ref, w_out_bf_ref,
                       carry_ref):
    t = pl.program_id(1)

    @pl.when(jnp.logical_and(pl.program_id(0) == 0, t == 0))
    def _():
        w_in_bf_ref[...] = w_in_ref[...].astype(BF16)
        w_out_bf_ref[...] = w_out_ref[...].astype(BF16)

    @pl.when(t == 0)
    def _():
        carry_ref[...] = jnp.zeros_like(carry_ref)

    seq_row = pl.ds(pl.program_id(0), 1)
    sh1, sc1, g1, sh2, sc2 = [mod_ref[seq_row, k * D_MODEL:(k + 1) * D_MODEL] for k in range(5)]
    r_i = lax.broadcasted_iota(jnp.int32, (CHUNK, CHUNK), 0)
    c_i = lax.broadcasted_iota(jnp.int32, (CHUNK, CHUNK), 1)
    tril = c_i <= r_i
    tm = TOK_TILE
    prev2 = carry_ref[0:1, :]
    prev1 = carry_ref[1:2, :]

    subs = range(MIX_SUB)
    tile_rows = [slice(sub * tm, (sub + 1) * tm) for sub in subs]
    xs_in = [x_ref[rows, :] for rows in tile_rows]
    zs = [_dot((x * (1.0 + sc1) + sh1).astype(BF16), w_in_bf_ref[...]) for x in xs_in]

    mix_ins = []
    for sub in subs:
        z = zs[sub]
        u = z[:, 0:A_WIDTH]
        v = z[:, A_WIDTH:2 * A_WIDTH]
        gate_b = z[:, 2 * A_WIDTH:2 * A_WIDTH + B_WIDTH]
        gate_c = z[:, 2 * A_WIDTH + B_WIDTH:2 * A_WIDTH + 2 * B_WIDTH]
        hb = z[:, 2 * A_WIDTH + 2 * B_WIDTH:]

        a_parts = []
        for hd in range(A_HEADS):
            sl = slice(hd * A_HEAD_DIM, (hd + 1) * A_HEAD_DIM)
            vn = _layer_norm(v[:, sl], sgu_g_ref[hd:hd + 1, :], sgu_bln_ref[hd:hd + 1, :]).astype(BF16)
            wm = jnp.where(tril, sgu_w_ref[hd], 0.0).astype(BF16)
            bias = sgu_bias_ref[:, hd:hd + 1]
            s_parts = []
            for c in range(tm // CHUNK):
                s_parts.append(_dot(wm, vn[c * CHUNK:(c + 1) * CHUNK, :]) + bias)
            a_parts.append(u[:, sl] * jnp.concatenate(s_parts, axis=0))

        q = gate_c * hb
        row = lax.broadcasted_iota(jnp.int32, q.shape, 0)
        q_m1 = jnp.where(row == 0, prev1, pltpu.roll(q, 1, 0))
        q_m2 = jnp.where(row == 0, prev2, jnp.where(row == 1, prev1, pltpu.roll(q, 2, 0)))
        conv = (conv_b_ref[...] + q_m2 * conv_w_ref[0:1, :] + q_m1 * conv_w_ref[1:2, :]
                + q * conv_w_ref[2:3, :])
        b_out = gate_b * conv
        prev2 = q[tm - 2:tm - 1, :]
        prev1 = q[tm - 1:tm, :]

        mix_ins.append(jnp.concatenate(a_parts + [b_out], axis=-1).astype(BF16))

    mixes = [_dot(mix_in, w_out_bf_ref[...]) for mix_in in mix_ins]
    h2s = []
    for sub in subs:
        x1 = _layer_norm(alpha_res * xs_in[sub] + g1 * mixes[sub], ln1_g_ref[...], ln1_b_ref[...])
        x1_ref[tile_rows[sub], :] = x1
        h2 = x1 * (1.0 + sc2) + sh2
        h2_ref[tile_rows[sub], :] = h2.astype(BF16)
        h2s.append(h2)
    routed = _route(h2s, router_w_ref[...], router_b_ref[...])
    for sub in subs:
        meta, cnt = routed[sub]
        meta_ref[sub] = meta
        cnt_ref[sub] = cnt.astype(jnp.int32)

    last2 = jnp.concatenate([prev2, prev1], axis=0)
    carry_ref[0:2, :] = last2
    convst_ref[...] = last2


def _mix_prompt(x, mod, w_in, sgu_g, sgu_bln, sgu_w, sgu_bias_t, conv_w, conv_b, w_out,
                ln1_g, ln1_b, router_wt2, router_b_col, alpha_res):
    bsz, seq, _ = x.shape
    step_tok = MIX_SUB * TOK_TILE
    steps = seq // step_tok
    n_tok = bsz * seq
    tiles = seq // TOK_TILE
    const2 = lambda b, t: (0, 0)
    tok = lambda b, t: (b * steps + t, 0)
    tile3 = lambda b, t: (b * steps + t, 0, 0)
    return pl.pallas_call(
        functools.partial(_mix_prompt_kernel, alpha_res),
        grid=(bsz, steps),
        in_specs=[
            pl.BlockSpec((None, step_tok, D_MODEL), lambda b, t: (b, t, 0)),
            pl.BlockSpec((bsz, N_MOD * D_MODEL), const2),
            pl.BlockSpec((D_MODEL, PROJ_COLS), const2, pipeline_mode=pl.Buffered(1)),
            pl.BlockSpec((A_HEADS, A_HEAD_DIM), const2),
            pl.BlockSpec((A_HEADS, A_HEAD_DIM), const2),
            pl.BlockSpec((A_HEADS, CHUNK, CHUNK), lambda b, t: (0, 0, 0)),
            pl.BlockSpec((CHUNK, A_HEADS), const2),
            pl.BlockSpec((3, B_WIDTH), const2),
            pl.BlockSpec((1, B_WIDTH), const2),
            pl.BlockSpec((D_MODEL, D_MODEL), const2, pipeline_mode=pl.Buffered(1)),
            pl.BlockSpec((1, D_MODEL), const2),
            pl.BlockSpec((1, D_MODEL), const2),
            pl.BlockSpec((2 * N_EXPERTS, D_MODEL), const2),
            pl.BlockSpec((N_EXPERTS, 1), const2),
        ],
        out_specs=[
            pl.BlockSpec((step_tok, D_MODEL), tok),
            pl.BlockSpec((step_tok, D_MODEL), tok),
            pl.BlockSpec((MIX_SUB, SUBLANES, TOK_TILE), tile3),
            pl.BlockSpec((MIX_SUB, N_EXPERTS, LANES), tile3),
            pl.BlockSpec((None, 2, B_WIDTH), lambda b, t: (b, 0, 0)),
            pl.BlockSpec((D_MODEL, PROJ_COLS), const2),
            pl.BlockSpec((D_MODEL, D_MODEL), const2),
        ],
        out_shape=[
            jax.ShapeDtypeStruct((n_tok, D_MODEL), F32),
            jax.ShapeDtypeStruct((n_tok, D_MODEL), BF16),
            jax.ShapeDtypeStruct((bsz * tiles, SUBLANES, TOK_TILE), F32),
            jax.ShapeDtypeStruct((bsz * tiles, N_EXPERTS, LANES), jnp.int32),
            jax.ShapeDtypeStruct((bsz, 2, B_WIDTH), F32),
            jax.ShapeDtypeStruct((D_MODEL, PROJ_COLS), BF16),
            jax.ShapeDtypeStruct((D_MODEL, D_MODEL), BF16),
        ],
        scratch_shapes=[pltpu.VMEM((SUBLANES, B_WIDTH), F32)],
        compiler_params=pltpu.CompilerParams(
            dimension_semantics=("arbitrary", "arbitrary"), vmem_limit_bytes=VMEM_LIMIT,
            allow_input_fusion=[True] * 14),
        name="mix_prompt",
    )(x, mod, w_in, sgu_g, sgu_bln, sgu_w, sgu_bias_t, conv_w, conv_b, w_out,
      ln1_g, ln1_b, router_wt2, router_b_col)


def _mix_sample_kernel(alpha_res, x_ref, mod_ref, prev0_ref, prev1_ref, w_in_ref, sgu_g_ref,
                       sgu_bln_ref, sgu_w00_ref, sgu_b0_ref, conv_w_ref, conv_b_ref, w_out_ref,
                       ln1_g_ref, ln1_b_ref, router_w_ref, router_b_ref,
                       x1_ref, h2_ref, meta_ref, cnt_ref, q_ref, vn_ref):
    x = x_ref[...]
    sh1 = mod_ref[:, 0:D_MODEL]
    sc1 = mod_ref[:, D_MODEL:2 * D_MODEL]
    g1 = mod_ref[:, 2 * D_MODEL:3 * D_MODEL]
    sh2 = mod_ref[:, 3 * D_MODEL:4 * D_MODEL]
    sc2 = mod_ref[:, 4 * D_MODEL:5 * D_MODEL]
    h = (x * (1.0 + sc1) + sh1).astype(BF16)
    z = _dot(h, w_in_ref[...])
    u = z[:, 0:A_WIDTH]
    v = z[:, A_WIDTH:2 * A_WIDTH]
    gate_b = z[:, 2 * A_WIDTH:2 * A_WIDTH + B_WIDTH]
    gate_c = z[:, 2 * A_WIDTH + B_WIDTH:2 * A_WIDTH + 2 * B_WIDTH]
    hb = z[:, 2 * A_WIDTH + 2 * B_WIDTH:]

    vn_parts = []
    for hd in range(A_HEADS):
        sl = slice(hd * A_HEAD_DIM, (hd + 1) * A_HEAD_DIM)
        vn_parts.append(_layer_norm(v[:, sl], sgu_g_ref[hd:hd + 1, :], sgu_bln_ref[hd:hd + 1, :]))
    vn = jnp.concatenate(vn_parts, axis=-1)
    vn_ref[...] = vn
    a_out = u * (vn * sgu_w00_ref[...] + sgu_b0_ref[...])

    q = gate_c * hb
    q_ref[...] = q
    conv = (conv_b_ref[...] + prev0_ref[...] * conv_w_ref[0:1, :] + prev1_ref[...] * conv_w_ref[1:2, :]
            + q * conv_w_ref[2:3, :])
    b_out = gate_b * conv

    mix_in = jnp.concatenate([a_out, b_out], axis=-1).astype(BF16)
    mix = _dot(mix_in, w_out_ref[...])
    x1 = _layer_norm(alpha_res * x + g1 * mix, ln1_g_ref[...], ln1_b_ref[...])
    x1_ref[...] = x1
    h2 = x1 * (1.0 + sc2) + sh2
    h2_ref[...] = h2.astype(BF16)
    (meta, cnt), = _route([h2], router_w_ref[...], router_b_ref[...])
    meta_ref[...] = meta
    cnt_ref[...] = cnt.astype(jnp.int32)


def _mix_sample(x, mod, prev0, prev1, w_in_bf, sgu_g, sgu_bln, sgu_w00, sgu_b0, conv_w, conv_b,
                w_out_bf, ln1_g, ln1_b, router_wt2, router_b_col, alpha_res):
    n = x.shape[0]
    return pl.pallas_call(
        functools.partial(_mix_sample_kernel, alpha_res),
        out_shape=[
            jax.ShapeDtypeStruct((n, D_MODEL), F32),
            jax.ShapeDtypeStruct((n, D_MODEL), BF16),
            jax.ShapeDtypeStruct((SUBLANES, n), F32),
            jax.ShapeDtypeStruct((N_EXPERTS, LANES), jnp.int32),
            jax.ShapeDtypeStruct((n, B_WIDTH), F32),
            jax.ShapeDtypeStruct((n, A_WIDTH), F32),
        ],
        compiler_params=pltpu.CompilerParams(vmem_limit_bytes=VMEM_LIMIT,
                                             allow_input_fusion=[True] * 16),
        name="mix_sample",
    )(x, mod, prev0, prev1, w_in_bf, sgu_g, sgu_bln, sgu_w00, sgu_b0, conv_w, conv_b, w_out_bf,
      ln1_g, ln1_b, router_wt2, router_b_col)


def _strip_copy(src_ref, src_row, dst_ref, dst_row, n_rows, sem, priority=0):
    @pl.when(n_rows > 0)
    def _():
        pltpu.make_async_copy(src_ref.at[pl.ds(src_row, n_rows)],
                              dst_ref.at[pl.ds(dst_row, n_rows)], sem).start(priority=priority)


def _for_each_expert(strip):
    for e in range(N_EXPERTS):
        strip(e, 0)


def _wait_rows(hbm_ref, vmem_ref, n_rows, sem):
    pltpu.make_async_copy(hbm_ref.at[pl.ds(0, n_rows)], vmem_ref.at[pl.ds(0, n_rows)], sem).wait()


def _sort_kernel(n_tok, tile0, n_steps, subs, first, *refs):
    if first:
        (sorted_row_ref, off_ref, cnt_ref, pad_row_ref, pad_n_ref, h2_ref, meta_ref,
         xs_ref, stage_ref, zero_ref, sem, pad_sem) = refs
    else:
        (sorted_row_ref, off_ref, cnt_ref, pad_row_ref, pad_n_ref, h2_ref, meta_ref, _,
         xs_ref, stage_ref, sem) = refs
    j = pl.program_id(0)
    n_sorted = n_tok * TOP_K
    parity = j % 2

    if first:
        @pl.when(j == 0)
        def _():
            zero_ref[...] = jnp.zeros_like(zero_ref)

            def start(e, carry):
                _strip_copy(zero_ref, 0, xs_ref, pad_row_ref[e], pad_n_ref[e], pad_sem)
                return carry
            lax.fori_loop(0, N_EXPERTS, start, 0)

            def wait(e, carry):
                @pl.when(pad_n_ref[e] > 0)
                def _():
                    _wait_rows(xs_ref, zero_ref, pad_n_ref[e], pad_sem)
                return carry
            lax.fori_loop(0, N_EXPERTS, wait, 0)

    sub_r = lax.broadcasted_iota(jnp.int16, (n_sorted, n_tok), 0)
    for sub in range(subs):
        slot = parity * subs + sub
        pos = meta_ref[sub][0:TOP_K, :].astype(jnp.int16)
        perm = jnp.zeros((n_sorted, n_tok), BF16)
        for k in range(TOP_K):
            perm = jnp.where(sub_r == pos[k:k + 1, :], jnp.ones((), BF16), perm)
        rows = _dot(perm, h2_ref[sub * n_tok:(sub + 1) * n_tok, :])
        _store_rows(stage_ref, slot, rows)

        base = (tile0 + j * subs + sub) * N_EXPERTS

        def strip(e, priority, slot=slot, base=base):
            _strip_copy(stage_ref.at[slot], off_ref[base + e], xs_ref, sorted_row_ref[base + e],
                        cnt_ref[base + e], sem.at[slot], priority)
        _for_each_expert(strip)

    @pl.when(j > 0)
    def _():
        for sub in range(subs):
            _wait_rows(xs_ref, stage_ref.at[0], n_sorted, sem.at[(1 - parity) * subs + sub])

    @pl.when(j == n_steps - 1)
    def _():
        for sub in range(subs):
            _wait_rows(xs_ref, stage_ref.at[0], n_sorted, sem.at[parity * subs + sub])


def _sort(tables, pad_row, pad_n, h2, meta, xs, n_sorted_rows, n_tok, tile0, subs):
    sorted_row, off, cnt = tables
    first = xs is None
    n_steps = h2.shape[0] // (n_tok * subs)
    in_specs = [pl.BlockSpec((subs * n_tok, D_MODEL), lambda j, *_: (j, 0)),
                pl.BlockSpec((subs, SUBLANES, n_tok), lambda j, *_: (j, 0, 0))]
    operands = [sorted_row, off, cnt, pad_row, pad_n, h2, meta]
    scratch = [pltpu.VMEM((2 * subs, n_tok * TOP_K, ROW_SLABS, LANES), BF16)]
    if first:
        scratch.append(pltpu.VMEM((ROW_TILE, ROW_SLABS, LANES), BF16))
        aliases = {}
    else:
        in_specs.append(pl.BlockSpec(memory_space=pl.ANY))
        operands.append(xs)
        aliases = {len(operands) - 1: 0}
    scratch.append(pltpu.SemaphoreType.DMA((2 * subs,)))
    if first:
        scratch.append(pltpu.SemaphoreType.DMA(()))
    grid_spec = pltpu.PrefetchScalarGridSpec(
        num_scalar_prefetch=5,
        grid=(n_steps,),
        in_specs=in_specs,
        out_specs=pl.BlockSpec(memory_space=pl.ANY),
        scratch_shapes=scratch,
    )
    return pl.pallas_call(
        functools.partial(_sort_kernel, n_tok, tile0, n_steps, subs, first),
        grid_spec=grid_spec,
        out_shape=jax.ShapeDtypeStruct((n_sorted_rows, ROW_SLABS, LANES), BF16),
        input_output_aliases=aliases,
        compiler_params=pltpu.CompilerParams(
            dimension_semantics=("arbitrary",), vmem_limit_bytes=VMEM_LIMIT),
        name="sort_first" if first else "sort_more",
    )(*operands)


def _experts_kernel(tile_start_ref, n_tile_ref, short_last_ref, n_used_ref, xs_ref, w_gu_ref, b_gu_ref,
                    w_down_ref, b_down_ref, ys_ref, w_gu_bf_ref, w_down_bf_ref, x_buf, y_buf, x_sem,
                    y_sem):
    e = pl.program_id(0)
    n_used = n_used_ref[0]
    first_tile = tile_start_ref[e]
    n_tile = n_tile_ref[e]

    def x_copy(g):
        slot = g % TILE_SLOTS
        return pltpu.make_async_copy(xs_ref.at[pl.ds(g * ROW_TILE, ROW_TILE)], x_buf.at[slot],
                                     x_sem.at[slot])

    def y_copy(g):
        slot = g % TILE_SLOTS
        return pltpu.make_async_copy(y_buf.at[slot], ys_ref.at[pl.ds(g * ROW_TILE, ROW_TILE)],
                                     y_sem.at[slot])

    def request(g):
        @pl.when(g < n_used)
        def _():
            x_copy(g).start(priority=1)

    @pl.when(e == 0)
    def _():
        for g in range(TILE_AHEAD):
            request(g)

    @pl.when(n_tile > 0)
    def _():
        w_gu_bf_ref[...] = w_gu_ref[...].astype(BF16)
        w_down_bf_ref[...] = w_down_ref[...].astype(BF16)

    def begin(g):
        x_copy(g).wait()
        request(g + TILE_AHEAD)

        @pl.when(g >= TILE_SLOTS)
        def _():
            y_copy(g - TILE_SLOTS).wait()

    def compute(g, rows=ROW_TILE):
        slot = g % TILE_SLOTS
        x = x_buf[slot, 0:rows].reshape(rows, D_MODEL)
        gu = _dot(x, w_gu_bf_ref[...]) + b_gu_ref[...]
        gate = jnp.minimum(gu[:, :D_EXPERT], SWIGLU_LIMIT)
        up = jnp.clip(gu[:, D_EXPERT:], -SWIGLU_LIMIT, SWIGLU_LIMIT)
        act = (up + 1.0) * gate * jax.nn.sigmoid(SWIGLU_ALPHA * gate)
        y = _dot(act.astype(BF16), w_down_bf_ref[...]) + b_down_ref[...]
        y_buf[slot, 0:rows] = y.astype(BF16).reshape(rows, ROW_SLABS, LANES)
        if rows < ROW_TILE:
            y_buf[slot, rows:ROW_TILE] = jnp.zeros((ROW_TILE - rows, ROW_SLABS, LANES), BF16)
        y_copy(g).start(priority=1)

    def group(g, size):
        for k in range(size):
            begin(g + k)
        for k in range(size):
            compute(g + k)

    def group_body(p, carry):
        group(first_tile + TILE_GROUP * p, TILE_GROUP)
        return carry

    short_last = short_last_ref[e]
    n_full = n_tile - short_last
    n_group = n_full // TILE_GROUP
    lax.fori_loop(0, n_group, group_body, 0)
    rest = n_full - n_group * TILE_GROUP
    g_rest = first_tile + n_group * TILE_GROUP
    size = TILE_GROUP // 2
    while size >= 1:
        @pl.when((rest & size) != 0)
        def _(g_rest=g_rest, size=size):
            group(g_rest, size)
        g_rest = g_rest + (rest & size)
        size //= 2

    @pl.when(short_last == 1)
    def _():
        g = first_tile + n_full
        begin(g)
        compute(g, ROW_TILE // 2)

    @pl.when(e == N_EXPERTS - 1)
    def _():
        for back in range(TILE_SLOTS, 0, -1):
            @pl.when(n_used >= back)
            def _(back=back):
                y_copy(n_used - back).wait()


def _experts(tile_start, n_tile_e, pad_n, n_used, xs, w_gu, b_gu, w_down, b_down):
    short_last = (pad_n >= ROW_TILE // 2).astype(jnp.int32)
    w_blk = lambda e, *_: (e, 0, 0)
    grid_spec = pltpu.PrefetchScalarGridSpec(
        num_scalar_prefetch=4,
        grid=(N_EXPERTS,),
        in_specs=[
            pl.BlockSpec(memory_space=pl.ANY),
            pl.BlockSpec((None, D_MODEL, 2 * D_EXPERT), w_blk),
            pl.BlockSpec((None, 1, 2 * D_EXPERT), w_blk),
            pl.BlockSpec((None, D_EXPERT, D_MODEL), w_blk),
            pl.BlockSpec((None, 1, D_MODEL), w_blk),
        ],
        out_specs=pl.BlockSpec(memory_space=pl.ANY),
        scratch_shapes=[pltpu.VMEM((D_MODEL, 2 * D_EXPERT), BF16),
                        pltpu.VMEM((D_EXPERT, D_MODEL), BF16),
                        pltpu.VMEM((TILE_SLOTS, ROW_TILE, ROW_SLABS, LANES), BF16),
                        pltpu.VMEM((TILE_SLOTS, ROW_TILE, ROW_SLABS, LANES), BF16),
                        pltpu.SemaphoreType.DMA((TILE_SLOTS,)),
                        pltpu.SemaphoreType.DMA((TILE_SLOTS,))],
    )
    return pl.pallas_call(
        _experts_kernel,
        grid_spec=grid_spec,
        out_shape=jax.ShapeDtypeStruct(xs.shape, BF16),
        compiler_params=pltpu.CompilerParams(
            dimension_semantics=("arbitrary",), vmem_limit_bytes=VMEM_LIMIT,
            allow_input_fusion=[False] * 6 + [True, False, True]),
        name="experts",
    )(tile_start, n_tile_e, short_last, n_used, xs, w_gu, b_gu.reshape(N_EXPERTS, 1, -1), w_down,
      b_down.reshape(N_EXPERTS, 1, -1))


def _combine_kernel(n_tok, tile0, n_steps, subs, seq_steps, alpha_res, sorted_row_ref, off_ref, cnt_ref,
                    ys_ref, x1_ref, meta_ref, mod_ref, ln2_g_ref, ln2_b_ref, out_ref,
                    stage_ref, sem):
    j = pl.program_id(0)
    n_sorted = n_tok * TOP_K
    parity = j % 2

    def start_step(step, step_parity, live):
        for sub in range(subs):
            base = (tile0 + step * subs + sub) * N_EXPERTS
            to_slot = step_parity * subs + sub

            def strip(e, priority, base=base, to_slot=to_slot):
                _strip_copy(ys_ref, sorted_row_ref[base + e], stage_ref.at[to_slot],
                            off_ref[base + e], jnp.where(live, cnt_ref[base + e], 0),
                            sem.at[to_slot], priority)
            _for_each_expert(strip)

    @pl.when(j == 0)
    def _():
        start_step(j, parity, True)

    start_step(jnp.minimum(j + 1, n_steps - 1), 1 - parity, j + 1 < n_steps)

    sub_r = lax.broadcasted_iota(jnp.int16, (n_sorted, n_tok), 0)
    for sub in range(subs):
        slot = parity * subs + sub
        rows = slice(sub * n_tok, (sub + 1) * n_tok)
        meta = meta_ref[sub]
        pos = meta[0:TOP_K, :].astype(jnp.int16)
        gate = meta[TOP_K:2 * TOP_K, :].astype(BF16)
        comb_t = jnp.zeros((n_sorted, n_tok), BF16)
        for k in range(TOP_K):
            comb_t = jnp.where(sub_r == pos[k:k + 1, :], gate[k:k + 1, :], comb_t)

        _wait_rows(ys_ref, stage_ref.at[0], n_sorted, sem.at[slot])
        ys = _load_rows(stage_ref, slot)
        ffn = lax.dot_general(comb_t, ys, (((0,), (0,)), ((), ())), preferred_element_type=F32)
        if seq_steps is None:
            g2 = mod_ref[rows, 5 * D_MODEL:6 * D_MODEL]
        else:
            g2 = mod_ref[pl.ds(j // seq_steps, 1), 5 * D_MODEL:6 * D_MODEL]
        y = _layer_norm(alpha_res * x1_ref[rows, :] + g2 * ffn, ln2_g_ref[...], ln2_b_ref[...])
        if seq_steps is None:
            out_ref[rows, 0, :] = y
        else:
            out_ref[rows, :] = y


def _combine(tables, ys, x1, meta, mod, ln2_g, ln2_b, n_tok, tile0, subs, alpha_res):
    sorted_row, off, cnt = tables
    step_tok = subs * n_tok
    n_steps = x1.shape[0] // step_tok
    if mod.shape[0] == x1.shape[0]:
        seq_steps = None
        mod_spec = pl.BlockSpec((step_tok, N_MOD * D_MODEL), lambda j, *_: (j, 0))
        out_spec = pl.BlockSpec((step_tok, 1, D_MODEL), lambda j, *_: (j, 0, 0))
        out_shape = jax.ShapeDtypeStruct((x1.shape[0], 1, D_MODEL), F32)
    else:
        seq_steps = n_steps // mod.shape[0]
        mod_spec = pl.BlockSpec(mod.shape, lambda j, *_: (0, 0))
        out_spec = pl.BlockSpec((step_tok, D_MODEL), lambda j, *_: (j, 0))
        out_shape = jax.ShapeDtypeStruct(x1.shape, F32)
    grid_spec = pltpu.PrefetchScalarGridSpec(
        num_scalar_prefetch=3,
        grid=(n_steps,),
        in_specs=[
            pl.BlockSpec(memory_space=pl.ANY),
            pl.BlockSpec((step_tok, D_MODEL), lambda j, *_: (j, 0)),
            pl.BlockSpec((subs, SUBLANES, n_tok), lambda j, *_: (j, 0, 0)),
            mod_spec,
            pl.BlockSpec((1, D_MODEL), lambda j, *_: (0, 0)),
            pl.BlockSpec((1, D_MODEL), lambda j, *_: (0, 0)),
        ],
        out_specs=out_spec,
        scratch_shapes=[pltpu.VMEM((2 * subs, n_tok * TOP_K, ROW_SLABS, LANES), BF16),
                        pltpu.SemaphoreType.DMA((2 * subs,))],
    )
    return pl.pallas_call(
        functools.partial(_combine_kernel, n_tok, tile0, n_steps, subs, seq_steps, alpha_res),
        grid_spec=grid_spec,
        out_shape=out_shape,
        compiler_params=pltpu.CompilerParams(
            dimension_semantics=("arbitrary",), vmem_limit_bytes=VMEM_LIMIT),
        name="combine",
    )(sorted_row, off, cnt, ys, x1, meta, mod, ln2_g, ln2_b)


def _routing_tables(cnt_all):
    total = jnp.sum(cnt_all, axis=0)
    n_tile_e = (total + ROW_TILE - 1) // ROW_TILE
    tile_end = jnp.cumsum(n_tile_e)
    tile_start = tile_end - n_tile_e
    row_start = tile_start * ROW_TILE
    cum = jnp.cumsum(cnt_all, axis=0) - cnt_all
    off = jnp.cumsum(cnt_all, axis=1) - cnt_all
    sorted_row = row_start[None, :] + cum
    n_used = tile_end[-1]
    pad_row = row_start + total
    pad_n = n_tile_e * ROW_TILE - total
    i32 = lambda a: a.astype(jnp.int32)
    return ((i32(sorted_row).reshape(-1), i32(off).reshape(-1), i32(cnt_all).reshape(-1)),
            i32(pad_row), i32(pad_n), i32(tile_start), i32(n_tile_e), i32(n_used).reshape(1))


def kernel(x_prompt, x_sample, state_conv, c_prompt, c_sample, ada_w, ada_b, w_in, sgu_ln_g, sgu_ln_b,
           sgu_w, sgu_b, conv_w, conv_b, w_out, ln1_g, ln1_b, router_w, router_b, w_gu, b_gu,
           w_down, b_down, ln2_g, ln2_b):
    depth = ada_w.shape[0]
    assert depth == 1
    bsz, seq, _ = x_prompt.shape
    n_dec = x_sample.shape[0]
    assert x_sample.shape[1] == 1 and seq % TOK_TILE == 0
    alpha_res = (2.0 * depth) ** 0.25
    l = 0

    mod_s, mod_p = _ada(c_sample, c_prompt, ada_w[l], ada_b[l])

    router_wt =jnp.transpose(router_w[l])
    router_wt_hi = router_wt.astype(BF16)
    router_wt_lo = (router_wt - router_wt_hi.astype(F32)).astype(BF16)
    router_wt2 = jnp.concatenate([router_wt_hi, router_wt_lo], axis=0)
    router_b_col = router_b[l].reshape(N_EXPERTS, 1)
    row = lambda a: a.reshape(1, -1)

    x1_p, h2_p, meta_p, cnt_p, convst_p, w_in_bf, w_out_bf = _mix_prompt(
        x_prompt, mod_p, w_in[l], sgu_ln_g[l], sgu_ln_b[l], sgu_w[l], jnp.transpose(sgu_b[l]),
        conv_w[l], row(conv_b[l]), w_out[l], row(ln1_g[l]), row(ln1_b[l]), router_wt2,
        router_b_col, alpha_res)
    x1_s, h2_s, meta_s, cnt_s, q_s, vn_s = _mix_sample(
        x_sample.reshape(n_dec, D_MODEL), mod_s, state_conv[l, :, 0, :], state_conv[l, :, 1, :],
        w_in_bf, sgu_ln_g[l], sgu_ln_b[l], row(jnp.repeat(sgu_w[l, :, 0, 0], A_HEAD_DIM)),
        row(jnp.repeat(sgu_b[l, :, 0], A_HEAD_DIM)), conv_w[l], row(conv_b[l]), w_out_bf,
        row(ln1_g[l]), row(ln1_b[l]), router_wt2, router_b_col, alpha_res)

    n_ptiles = bsz * seq // TOK_TILE
    cnt_all = jnp.concatenate([cnt_p[:, :, 0], cnt_s[None, :, 0]], axis=0)
    n_assign = (bsz * seq + n_dec) * TOP_K
    n_row_tiles = -(-n_assign // ROW_TILE) + N_EXPERTS
    tables, pad_row, pad_n, tile_start, n_tile_e, n_used = _routing_tables(cnt_all)

    n_sorted_rows = n_row_tiles * ROW_TILE
    meta_s = meta_s[None]
    xs = _sort(tables, pad_row, pad_n, h2_p, meta_p, None, n_sorted_rows, TOK_TILE, 0, SORT_SUB)
    xs = _sort(tables, pad_row, pad_n, h2_s, meta_s, xs, n_sorted_rows, n_dec, n_ptiles, 1)
    ys = _experts(tile_start, n_tile_e, pad_n, n_used, xs, w_gu[l], b_gu[l], w_down[l], b_down[l])
    y_p = _combine(tables, ys, x1_p, meta_p, mod_p, row(ln2_g[l]), row(ln2_b[l]), TOK_TILE, 0,
                   COMBINE_SUB, alpha_res)
    y_s = _combine(tables, ys, x1_s, meta_s, mod_s, row(ln2_g[l]), row(ln2_b[l]), n_dec, n_ptiles,
                   1, alpha_res)

    conv_state_sample = jnp.stack([state_conv[l, :, 1, :], q_s], axis=1)[None]
    return (y_p.reshape(bsz, seq, D_MODEL),
            y_s,
            convst_p[None],
            conv_state_sample,
            vn_s.reshape(1, n_dec, 1, A_HEADS, A_HEAD_DIM))
```
